```python
import jax, jax.numpy as jnp
from jax import lax
import numpy as np

D_MODEL = 2048
BATCH = 4
SEQ = 4096
DEPTH = 2
DEC_BATCH = 16
DEC_SEQ = 16
PAST_LEN = 2048

CHUNK = 64
N_A = DEPTH // 2
N_B = DEPTH - N_A
EPS = 1e-6
ROPE_THETA = 10000.0
NEG_INF = -1e30

RET_HEADS = 8
RET_DK = D_MODEL // RET_HEADS
RET_DV = 2 * RET_DK
RET_QK = RET_HEADS * RET_DK
RET_V = RET_HEADS * RET_DV
RET_IN = 2 * RET_QK + 2 * RET_V

MLA_HEADS = 16
MLA_NOPE = 128
MLA_ROPE = 64
MLA_VDIM = 128
KV_LORA = 512
Q_LORA = 512
MLA_SCALE = (MLA_NOPE + MLA_ROPE) ** -0.5
Q_BLOCK = 128

PEER_HEADS = 8
PEER_DK = 256
N_KEYS = 128
N_EXPERTS = N_KEYS * N_KEYS
PEER_TOPK = 16
PEER_BLOCK = 128

kernel_name = 'yoco_retnet_mla_peer_stream'


def _rmsnorm(x, g):
    xf = x.astype(jnp.float32)
    y = xf * lax.rsqrt(jnp.mean(xf * xf, axis=-1, keepdims=True) + EPS)
    return (y * g.astype(jnp.float32)).astype(x.dtype)


def _modulate(x, g, shift, scale):
    return _rmsnorm(x, g) * (1 + scale[:, None, :]) + shift[:, None, :]


def _rope(x, pos):
    half = x.shape[-1] // 2
    inv = ROPE_THETA ** (-jnp.arange(half, dtype=jnp.float32) / half)
    ang = pos.astype(jnp.float32)[:, None] * inv[None, :]
    cos = jnp.cos(ang)[None, :, None, :]
    sin = jnp.sin(ang)[None, :, None, :]
    xf = x.astype(jnp.float32)
    x1, x2 = xf[..., :half], xf[..., half:]
    return jnp.concatenate([x1 * cos - x2 * sin, x1 * sin + x2 * cos], axis=-1).astype(x.dtype)


def _ret_log_decay():
    return jnp.log1p(-jnp.exp2(-5.0 - jnp.arange(RET_HEADS, dtype=jnp.float32)))


def _retention_chunk(S, q, k, v, log_g):
    L = q.shape[1]
    j = jnp.arange(L, dtype=jnp.float32)
    intra_decay = jnp.exp(jnp.abs(j[:, None] - j[None, :])[None] * log_g[:, None, None])
    scores = jnp.einsum('blhd,bmhd->bhlm', q, k) * intra_decay[None]
    intra = jnp.einsum('bhlm,bmhe->blhe', scores, v)
    q_decay = jnp.exp((j[:, None] + 1.0) * log_g[None, :])
    cross = jnp.einsum('blhd,bhde->blhe', q, S) * q_decay[None, :, :, None]
    k_decay = jnp.exp((L - 1.0 - j)[:, None] * log_g[None, :])
    S_new = jnp.exp(L * log_g)[None, :, None, None] * S + jnp.einsum('blhd,lh,blhe->bhde', k, k_decay, v)
    return S_new, intra + cross


def _retention(h, pos, S0, w_in, gn_g, w_out):
    B, L, _ = h.shape
    proj = (h @ w_in).astype(jnp.float32)
    q = proj[..., :RET_QK].reshape(B, L, RET_HEADS, RET_DK)
    k = proj[..., RET_QK:2 * RET_QK].reshape(B, L, RET_HEADS, RET_DK)
    v = proj[..., 2 * RET_QK:2 * RET_QK + RET_V].reshape(B, L, RET_HEADS, RET_DV)
    g = proj[..., 2 * RET_QK + RET_V:]
    q = _rope(q, pos)
    k = _rope(k, pos) * (RET_DK ** -0.5)
    log_g = _ret_log_decay()
    S0 = S0.astype(jnp.float32)
    if L <= CHUNK:
        S_new, y = _retention_chunk(S0, q, k, v, log_g)
    else:
        nc = L // CHUNK

        def to_chunks(t):
            return t.reshape(B, nc, CHUNK, *t.shape[2:]).swapaxes(0, 1)

        def step(S, qkv):
            qc, kc, vc = qkv
            return _retention_chunk(S, qc, kc, vc, log_g)

        S_new, y = lax.scan(step, S0, (to_chunks(q), to_chunks(k), to_chunks(v)))
        y = y.swapaxes(0, 1).reshape(B, L, RET_HEADS, RET_DV)
    mu = jnp.mean(y, axis=-1, keepdims=True)
    var = jnp.mean(jnp.square(y - mu), axis=-1, keepdims=True)
    y = ((y - mu) * lax.rsqrt(var + EPS)).reshape(B, L, RET_V) * gn_g.astype(jnp.float32)
    out = (jax.nn.silu(g) * y).astype(h.dtype) @ w_out
    return out, S_new


def _mla_kv(x_mid, shift, scale, kv_norm_g, w_dkv, kv_lat_norm_g, pos):
    h = _modulate(x_mid, kv_norm_g, shift, scale)
    ckv = h @ w_dkv
    lat = _rmsnorm(ckv[..., :KV_LORA], kv_lat_norm_g)
    kr = _rope(ckv[..., KV_LORA:][:, :, None, :], pos)[:, :, 0, :]
    return lat, kr


def _mla_query(h, pos, w_dq, q_norm_g, w_uq):
    B, L, _ = h.shape
    cq = _rmsnorm(h @ w_dq, q_norm_g)
    q = (cq @ w_uq).reshape(B, L, MLA_HEADS, MLA_NOPE + MLA_ROPE)
    return q[..., :MLA_NOPE], _rope(q[..., MLA_NOPE:], pos)


def _mla_attend_prompt(q_nope, q_rope, k_nope, v, kr):
    B, L = q_nope.shape[:2]
    nblk = L // Q_BLOCK
    key_chunk = jnp.arange(L) // CHUNK

    def blocks(t):
        return t.reshape(B, nblk, Q_BLOCK, *t.shape[2:]).swapaxes(0, 1)

    def body(args):
        qn, qr, start = args
        s = (jnp.einsum('bqhn,bkhn->bhqk', qn, k_nope)
             + jnp.einsum('bqhp,bkp->bhqk', qr, kr)).astype(jnp.float32) * MLA_SCALE
        q_chunk = (start + jnp.arange(Q_BLOCK)) // CHUNK
        s = jnp.where(key_chunk[None, :] <= q_chunk[:, None], s, NEG_INF)
        p = jax.nn.softmax(s, axis=-1).astype(v.dtype)
        return jnp.einsum('bhqk,bkhv->bqhv', p, v)

    o = lax.map(body, (blocks(q_nope), blocks(q_rope), jnp.arange(nblk) * Q_BLOCK))
    return o.swapaxes(0, 1).reshape(B, L, MLA_HEADS * MLA_VDIM)


def _mla_attend_cached(q_nope, q_rope, lat_all, kr_all, w_uk, w_uv):
    B, L = q_nope.shape[:2]
    q_lat = jnp.einsum('bqhn,rhn->bqhr', q_nope, w_uk)
    s = (jnp.einsum('bqhr,bkr->bhqk', q_lat, lat_all)
         + jnp.einsum('bqhp,bkp->bhqk', q_rope, kr_all)).astype(jnp.float32) * MLA_SCALE
    p = jax.nn.softmax(s, axis=-1).astype(lat_all.dtype)
    o_lat = jnp.einsum('bhqk,bkr->bqhr', p, lat_all)
    return jnp.einsum('bqhr,rhv->bqhv', o_lat, w_uv).reshape(B, L, MLA_HEADS * MLA_VDIM)


def _peer(h, w_q, keys, U, V):
    B, L, D = h.shape
    T = B * L
    x = h.reshape(T, D)
    q = (x @ w_q).reshape(T, PEER_HEADS, 2, PEER_DK // 2)
    s = jnp.einsum('thpd,hpnd->thpn', q, keys).astype(jnp.float32)
    sv, si = lax.top_k(s, PEER_TOPK)
    cand = (sv[:, :, 0, :, None] + sv[:, :, 1, None, :]).reshape(T, PEER_HEADS, PEER_TOPK * PEER_TOPK)
    cidx = (si[:, :, 0, :, None] * N_KEYS + si[:, :, 1, None, :]).reshape(T, PEER_HEADS, PEER_TOPK * PEER_TOPK)
    top_s, top_pos = lax.top_k(cand, PEER_TOPK)
    idx = jnp.take_along_axis(cidx, top_pos, axis=-1)
    gates = jax.nn.softmax(top_s, axis=-1)
    pad = (-T) % PEER_BLOCK
    nb = (T + pad) // PEER_BLOCK
    xp = jnp.pad(x, ((0, pad), (0, 0))).reshape(nb, PEER_BLOCK, D)
    ip = jnp.pad(idx, ((0, pad), (0, 0), (0, 0))).reshape(nb, PEER_BLOCK, PEER_HEADS, PEER_TOPK)
    gp = jnp.pad(gates, ((0, pad), (0, 0), (0, 0))).reshape(nb, PEER_BLOCK, PEER_HEADS, PEER_TOPK)

    def body(args):
        xb, ib, gb = args
        a = jax.nn.gelu(jnp.einsum('td,thkd->thk', xb, U[ib]), approximate=False)
        w = (gb * a).astype(xb.dtype)
        return jnp.einsum('thk,thkd->td', w, V[ib])

    out = lax.map(body, (xp, ip, gp))
    return out.reshape(nb * PEER_BLOCK, D)[:T].reshape(B, L, D)


def _trunk(x, c, pos, ret_state, cache_lat, cache_kr,
           ada_w, ada_b, norm_g, ret_w_in, ret_gn_g, ret_w_out,
           kv_ada_w, kv_ada_b, kv_norm_g, mla_w_dkv, mla_kv_norm_g, mla_w_uk, mla_w_uv,
           mla_w_dq, mla_q_norm_g, mla_w_uq, mla_w_o,
           peer_w_q, peer_keys, peer_u, peer_v, final_g):
    B = x.shape[0]
    cs = jax.nn.silu(c)
    ret_new = []
    lat = kr = None
    for layer in range(DEPTH):
        sh1, sc1, gt1, sh2, sc2, gt2 = jnp.split(cs @ ada_w[layer] + ada_b[layer], 6, axis=-1)
        h = _modulate(x, norm_g[layer, 0], sh1, sc1)
        if layer < N_A:
            S0 = jnp.zeros((B, RET_HEADS, RET_DK, RET_DV), jnp.float32) if ret_state is None else ret_state[layer]
            mix, S_new = _retention(h, pos, S0, ret_w_in[layer], ret_gn_g[layer], ret_w_out[layer])
            ret_new.append(S_new)
        else:
            if layer == N_A:
                kv_sh, kv_sc = jnp.split(cs @ kv_ada_w + kv_ada_b, 2, axis=-1)
                lat, kr = _mla_kv(x, kv_sh, kv_sc, kv_norm_g, mla_w_dkv, mla_kv_norm_g, pos)
                if cache_lat is None:
                    k_nope = jnp.einsum('bkr,rhn->bkhn', lat, mla_w_uk)
                    v_all = jnp.einsum('bkr,rhv->bkhv', lat, mla_w_uv)
                else:
                    lat_all = jnp.concatenate([cache_lat.astype(lat.dtype), lat], axis=1)
                    kr_all = jnp.concatenate([cache_kr.astype(kr.dtype), kr], axis=1)
            b = layer - N_A
            q_nope, q_rope = _mla_query(h, pos, mla_w_dq[b], mla_q_norm_g[b], mla_w_uq[b])
            if cache_lat is None:
                o = _mla_attend_prompt(q_nope, q_rope, k_nope, v_all, kr)
            else:
                o = _mla_attend_cached(q_nope, q_rope, lat_all, kr_all, mla_w_uk, mla_w_uv)
            mix = o @ mla_w_o[b]
        x = x + (1 + gt1[:, None, :]) * mix
        h = _modulate(x, norm_g[layer, 1], sh2, sc2)
        x = x + (1 + gt2[:, None, :]) * _peer(h, peer_w_q[layer], peer_keys[layer], peer_u[layer], peer_v[layer])
    return _rmsnorm(x, final_g), jnp.stack(ret_new), lat, kr


def setup_inputs(seed: int = 0) -> dict:
    key = jax.random.key(seed)
    ks = jax.random.split(key, 29)
    f32 = jnp.float32
    D = D_MODEL

    def nrm(k, shape, scale):
        return jax.random.normal(k, shape, f32) * scale

    def gain(k, shape):
        return 1.0 + 0.05 * jax.random.normal(k, shape, f32)

    return {
        'x_prompt': nrm(ks[0], (BATCH, SEQ, D), 1.0),
        'x_sample': nrm(ks[1], (DEC_BATCH, DEC_SEQ, D), 1.0),
        'c_prompt': nrm(ks[2], (BATCH, D), 1.0),
        'c_sample': nrm(ks[3], (DEC_BATCH, D), 1.0),
        'state_retention': nrm(ks[4], (N_A, DEC_BATCH, RET_HEADS, RET_DK, RET_DV), 0.5),
        'cache_mla_latent': nrm(ks[5], (DEC_BATCH, PAST_LEN, KV_LORA), 1.0),
        'cache_mla_krope': nrm(ks[6], (DEC_BATCH, PAST_LEN, MLA_ROPE), 1.0),
        'ada_w': nrm(ks[7], (DEPTH, D, 6 * D), 0.1 * D ** -0.5),
        'ada_b': nrm(ks[8], (DEPTH, 6 * D), 0.01),
        'norm_g': gain(ks[9], (DEPTH, 2, D)),
        'ret_w_in': nrm(ks[10], (N_A, D, RET_IN), D ** -0.5),
        'ret_gn_g': gain(ks[11], (N_A, RET_V)),
        'ret_w_out': nrm(ks[12], (N_A, RET_V, D), RET_V ** -0.5),
        'kv_ada_w': nrm(ks[13], (D, 2 * D), 0.1 * D ** -0.5),
        'kv_ada_b': nrm(ks[14], (2 * D,), 0.01),
        'kv_norm_g': gain(ks[15], (D,)),
        'mla_w_dkv': nrm(ks[16], (D, KV_LORA + MLA_ROPE), D ** -0.5),
        'mla_kv_norm_g': gain(ks[17], (KV_LORA,)),
        'mla_w_uk': nrm(ks[18], (KV_LORA, MLA_HEADS, MLA_NOPE), KV_LORA ** -0.5),
        'mla_w_uv': nrm(ks[19], (KV_LORA, MLA_HEADS, MLA_VDIM), KV_LORA ** -0.5),
        'mla_w_dq': nrm(ks[20], (N_B, D, Q_LORA), D ** -0.5),
        'mla_q_norm_g': gain(ks[21], (N_B, Q_LORA)),
        'mla_w_uq': nrm(ks[22], (N_B, Q_LORA, MLA_HEADS * (MLA_NOPE + MLA_ROPE)), Q_LORA ** -0.5),
        'mla_w_o': nrm(ks[23], (N_B, MLA_HEADS * MLA_VDIM, D), (MLA_HEADS * MLA_VDIM) ** -0.5),
        'peer_w_q': nrm(ks[24], (DEPTH, D, PEER_HEADS * PEER_DK), D ** -0.5),
        'peer_keys': nrm(ks[25], (DEPTH, PEER_HEADS, 2, N_KEYS, PEER_DK // 2), (PEER_DK // 2) ** -0.5),
        'peer_u': nrm(ks[26], (DEPTH, N_EXPERTS, D), D ** -0.5),
        'peer_v': nrm(ks[27], (DEPTH, N_EXPERTS, D), PEER_HEADS ** -0.5),
        'final_g': gain(ks[28], (D,)),
    }


def reference(x_prompt, x_sample, c_prompt, c_sample, state_retention, cache_mla_latent, cache_mla_krope,
              ada_w, ada_b, norm_g, ret_w_in, ret_gn_g, ret_w_out,
              kv_ada_w, kv_ada_b, kv_norm_g, mla_w_dkv, mla_kv_norm_g, mla_w_uk, mla_w_uv,
              mla_w_dq, mla_q_norm_g, mla_w_uq, mla_w_o,
              peer_w_q, peer_keys, peer_u, peer_v, final_g):
    past = cache_mla_latent.shape[1]
    pos_p = jnp.arange(x_prompt.shape[1], dtype=jnp.int32)
    pos_s = past + jnp.arange(x_sample.shape[1], dtype=jnp.int32)
    y_prompt, ret_p, lat_p, kr_p = _trunk(
        x_prompt, c_prompt, pos_p, None, None, None,
        ada_w, ada_b, norm_g, ret_w_in, ret_gn_g, ret_w_out,
        kv_ada_w, kv_ada_b, kv_norm_g, mla_w_dkv, mla_kv_norm_g, mla_w_uk, mla_w_uv,
        mla_w_dq, mla_q_norm_g, mla_w_uq, mla_w_o,
        peer_w_q, peer_keys, peer_u, peer_v, final_g)
    y_sample, ret_s, lat_s, kr_s = _trunk(
        x_sample, c_sample, pos_s, state_retention, cache_mla_latent, cache_mla_krope,
        ada_w, ada_b, norm_g, ret_w_in, ret_gn_g, ret_w_out,
        kv_ada_w, kv_ada_b, kv_norm_g, mla_w_dkv, mla_kv_norm_g, mla_w_uk, mla_w_uv,
        mla_w_dq, mla_q_norm_g, mla_w_uq, mla_w_o,
        peer_w_q, peer_keys, peer_u, peer_v, final_g)
    return (y_prompt, y_sample, ret_p, ret_s, lat_p, kr_p, lat_s, kr_s)
```

```python
import functools
import math

import jax
import jax.numpy as jnp
from jax import lax
from jax.experimental import pallas as pl
from jax.experimental.pallas import tpu as pltpu

F32 = jnp.float32
BF16 = jnp.bfloat16

EPS = 1e-6
ROPE_THETA = 10000.0
NEG_INF = -1e30
CHUNK = 64

RET_HEADS = 8
RET_DK = 256
RET_DV = 512

MLA_HEADS = 16
MLA_NOPE = 128
MLA_ROPE = 64
MLA_VDIM = 128
KV_LORA = 512
MLA_SCALE = (MLA_NOPE + MLA_ROPE) ** -0.5

PEER_HEADS = 8
N_KEYS = 128
PEER_TOPK = 16

LANES = 128
VMEM_LIMIT = 52 * 1024 * 1024


def _params(sem, vmem=VMEM_LIMIT):
    return pltpu.CompilerParams(dimension_semantics=sem, vmem_limit_bytes=vmem)


def _bdot(a, b):
    return jnp.dot(a.astype(BF16), b.astype(BF16), preferred_element_type=F32)


def _bdot_nt(a, b):
    return lax.dot_general(a.astype(BF16), b.astype(BF16), (((1,), (1,)), ((), ())),
                           preferred_element_type=F32)


def _bdot_tn(a, b):
    return lax.dot_general(a.astype(BF16), b.astype(BF16), (((0,), (0,)), ((), ())),
                           preferred_element_type=F32)


def _silu(x):
    return x * (1.0 / (1.0 + jnp.exp(-x)))


def _gelu(x):
    return 0.5 * x * (1.0 + lax.erf(x * (0.5 ** 0.5)))


def _linear_kernel(*refs, n_w, n_extra, prologue, epilogue):
    x_ref = refs[0]
    w_refs = refs[1:1 + n_w]
    extra_refs = refs[1 + n_w:1 + n_w + n_extra]
    out_refs = refs[1 + n_w + n_extra:]
    xv = x_ref[...]
    if prologue is not None:
        xv = prologue(xv)
    xb = xv.astype(BF16)
    accs = [jnp.dot(xb, w[...].astype(BF16), preferred_element_type=F32) for w in w_refs]
    epilogue(accs, extra_refs, out_refs)


def _linear(x, ws, *, n_cols, tm, tn, epilogue, out_shapes, out_specs, w_lead=None, col_block0=0,
            extras=(), extra_specs=(), prologue=None, name=None):
    m, k = x.shape
    if name is None:
        name = "linear" + getattr(epilogue, "func", epilogue).__name__
    assert m % tm == 0 and n_cols % tn == 0
    if w_lead is None:
        w_spec = pl.BlockSpec((k, tn), lambda i, j: (0, j + col_block0))
    else:
        w_spec = pl.BlockSpec((None, k, tn), lambda i, j: (w_lead, 0, j + col_block0))
    kern = functools.partial(_linear_kernel, n_w=len(ws), n_extra=len(extras), prologue=prologue,
                             epilogue=epilogue)
    return pl.pallas_call(
        kern,
        grid=(m // tm, n_cols // tn),
        in_specs=[pl.BlockSpec((tm, k), lambda i, j: (i, 0))] + [w_spec] * len(ws) + list(extra_specs),
        out_specs=out_specs,
        out_shape=out_shapes,
        compiler_params=_params(("parallel", "arbitrary")),
        name=name,
    )(x, *ws, *extras)


def _ep_plain(accs, extras, outs):
    outs[0][...] = accs[0].astype(outs[0].dtype)


def _ep_heads(accs, extras, outs):
    acc = accs[0]
    for jj in range(acc.shape[1] // LANES):
        outs[0][jj] = acc[:, jj * LANES:(jj + 1) * LANES].astype(outs[0].dtype)


def _ep_bias(accs, extras, outs):
    outs[0][...] = accs[0] + extras[0][...]


def _ep_rope_half128(accs, extras, outs, *, scale):
    acc = accs[0]
    cos = extras[0][...]
    sin = extras[1][...]
    for g in range(acc.shape[1] // 256):
        x1 = acc[:, g * 256:g * 256 + 128]
        x2 = acc[:, g * 256 + 128:(g + 1) * 256]
        outs[0][:, g * 256:g * 256 + 128] = ((x1 * cos - x2 * sin) * scale).astype(outs[0].dtype)
        outs[0][:, g * 256 + 128:(g + 1) * 256] = ((x1 * sin + x2 * cos) * scale).astype(outs[0].dtype)


def _ep_rot_heads(accs, extras, outs):
    cos = extras[0][...]
    sin = extras[1][...]
    a, b = accs
    for jj in range(a.shape[1] // LANES):
        sl = slice(jj * LANES, (jj + 1) * LANES)
        outs[0][jj] = (a[:, sl] * cos + b[:, sl] * sin).astype(outs[0].dtype)


def _ep_rms(accs, extras, outs):
    acc = accs[0]
    g = extras[0][...]
    y = acc * lax.rsqrt(jnp.mean(acc * acc, axis=-1, keepdims=True) + EPS) * g
    outs[0][...] = y.astype(outs[0].dtype)


def _ep_kv(accs, extras, outs):
    acc = accs[0]
    g = extras[0][...]
    cos = extras[1][...]
    sin = extras[2][...]
    c = acc[:, :KV_LORA]
    outs[0][...] = c * lax.rsqrt(jnp.mean(c * c, axis=-1, keepdims=True) + EPS) * g
    kr = acc[:, KV_LORA:KV_LORA + LANES] * cos + acc[:, KV_LORA + LANES:KV_LORA + 2 * LANES] * sin
    outs[1][...] = kr[:, :MLA_ROPE]
    outs[2][...] = kr.astype(BF16)


def _row_tile(m, k):
    cap = 1024 if k <= 2048 else 512
    return min(m, cap)


def _table_spec(tm, table_rows):
    nblk = table_rows // tm
    return pl.BlockSpec((tm, LANES), lambda i, j: (i % nblk, 0))


def _ew_kernel(*refs, has_resid, n_mod, final):
    idx = 0
    x = refs[idx][...]; idx += 1
    if has_resid:
        mix = refs[idx][...]; idx += 1
        gate = refs[idx][...]; idx += 1
        x = x + (1.0 + gate) * mix
    mods = []
    for _ in range(n_mod):
        mods.append((refs[idx][...], refs[idx + 1][...], refs[idx + 2][...]))
        idx += 3
    if final:
        fg = refs[idx][...]; idx += 1
    outs = refs[idx:]
    o = 0
    xn = x * lax.rsqrt(jnp.mean(x * x, axis=-1, keepdims=True) + EPS)
    if final:
        outs[o][...] = xn * fg
        return
    if has_resid:
        outs[o][...] = x
        o += 1
    for g, sh, sc in mods:
        outs[o][...] = ((xn * g) * (1.0 + sc) + sh).astype(outs[o].dtype)
        o += 1


def _resid_mod(x, mix=None, gate=None, mods=(), final_g=None):
    nb, l, d = x.shape
    tl = min(l, 256)
    tok = pl.BlockSpec((None, tl, d), lambda b, i: (b, i, 0))
    per_b = pl.BlockSpec((None, 1, d), lambda b, i: (b, 0, 0))
    gain = pl.BlockSpec((1, d), lambda b, i: (0, 0))
    args, specs = [x], [tok]
    if mix is not None:
        args += [mix, gate]
        specs += [tok, per_b]
    for g, sh, sc in mods:
        args += [g, sh, sc]
        specs += [gain, per_b, per_b]
    out_shapes, out_specs = [], []
    if final_g is not None:
        args.append(final_g)
        specs.append(gain)
        out_shapes.append(jax.ShapeDtypeStruct((nb, l, d), F32))
        out_specs.append(tok)
    else:
        if mix is not None:
            out_shapes.append(jax.ShapeDtypeStruct((nb, l, d), F32))
            out_specs.append(tok)
        for _ in mods:
            out_shapes.append(jax.ShapeDtypeStruct((nb, l, d), BF16))
            out_specs.append(tok)
    kern = functools.partial(_ew_kernel, has_resid=mix is not None, n_mod=len(mods),
                             final=final_g is not None)
    return pl.pallas_call(
        kern, grid=(nb, l // tl), in_specs=specs, out_specs=out_specs, out_shape=out_shapes,
        compiler_params=_params(("parallel", "parallel")),
        name="resid_mod",
    )(*args)


def _ret_log_decay():
    return jnp.log1p(-jnp.exp2(-5.0 - jnp.arange(RET_HEADS, dtype=F32)))


def _retention_tables(c):
    log_g = _ret_log_decay()[:, None, None]
    n = jnp.arange(c, dtype=F32)
    dist = n[:, None] - n[None, :]
    same = (jnp.arange(c)[:, None] // CHUNK) == (jnp.arange(c)[None, :] // CHUNK)
    earlier = (jnp.arange(c)[None, :] // CHUNK) < (jnp.arange(c)[:, None] // CHUNK)
    mask = jnp.where(same[None], jnp.exp(jnp.abs(dist)[None] * log_g),
                     jnp.where(earlier[None], jnp.exp(dist[None] * log_g), 0.0))
    q_decay = jnp.exp((n[None, :, None] + 1.0) * log_g)
    k_decay = jnp.exp((c - 1.0 - n)[None, :, None] * log_g)
    blk_decay = jnp.exp(c * log_g)
    return mask, q_decay, k_decay, blk_decay


def _retention_kernel(*refs, has_s0):
    if has_s0:
        (q_ref, k_ref, v_ref, g_ref, mask_ref, qd_ref, kd_ref, bd_ref, gn_ref, s0_ref,
         y_ref, s_out_ref, s_ref) = refs
    else:
        (q_ref, k_ref, v_ref, g_ref, mask_ref, qd_ref, kd_ref, bd_ref, gn_ref,
         y_ref, s_out_ref, s_ref) = refs
    c = pl.program_id(2)

    @pl.when(c == 0)
    def _():
        if has_s0:
            s_ref[...] = s0_ref[...]
        else:
            s_ref[...] = jnp.zeros_like(s_ref)

    q = q_ref[...]
    k = k_ref[...]
    v = v_ref[...]
    s_prev = s_ref[...]
    scores = _bdot_nt(q, k) * mask_ref[...]
    y = _bdot(scores, v) + _bdot(q, s_prev) * qd_ref[...]
    k_scaled = k.astype(F32) * kd_ref[...]
    s_new = bd_ref[...] * s_prev + _bdot_tn(k_scaled, v)
    s_ref[...] = s_new

    mu = jnp.mean(y, axis=-1, keepdims=True)
    yc = y - mu
    var = jnp.mean(yc * yc, axis=-1, keepdims=True)
    yn = yc * lax.rsqrt(var + EPS) * gn_ref[...]
    y_ref[...] = (_silu(g_ref[...].astype(F32)) * yn).astype(y_ref.dtype)

    @pl.when(c == pl.num_programs(2) - 1)
    def _():
        s_out_ref[...] = s_new


def _retention(q, k, v, g, gn_g, s0, nb, l):
    cb = min(l, 256)
    nc = l // cb
    mask, qd, kd, bd = _retention_tables(cb)
    row = lambda b, h, c: (b * nc + c, h)
    per_h3 = lambda b, h, c: (h, 0, 0)
    in_specs = [
        pl.BlockSpec((cb, RET_DK), row), pl.BlockSpec((cb, RET_DK), row),
        pl.BlockSpec((cb, RET_DV), row), pl.BlockSpec((cb, RET_DV), row),
        pl.BlockSpec((None, cb, cb), per_h3), pl.BlockSpec((None, cb, 1), per_h3),
        pl.BlockSpec((None, cb, 1), per_h3), pl.BlockSpec((None, 1, 1), per_h3),
        pl.BlockSpec((1, RET_DV), lambda b, h, c: (0, h)),
    ]
    args = [q, k, v, g, mask, qd, kd, bd, gn_g]
    state_spec = pl.BlockSpec((None, None, RET_DK, RET_DV), lambda b, h, c: (b, h, 0, 0))
    if s0 is not None:
        in_specs.append(state_spec)
        args.append(s0)
    y, s_new = pl.pallas_call(
        functools.partial(_retention_kernel, has_s0=s0 is not None),
        grid=(nb, RET_HEADS, nc),
        in_specs=in_specs,
        out_specs=[pl.BlockSpec((cb, RET_DV), row), state_spec],
        out_shape=[jax.ShapeDtypeStruct((nb * l, RET_HEADS * RET_DV), BF16),
                   jax.ShapeDtypeStruct((nb, RET_HEADS, RET_DK, RET_DV), F32)],
        scratch_shapes=[pltpu.VMEM((RET_DK, RET_DV), F32)],
        compiler_params=_params(("parallel", "parallel", "arbitrary")),
        name="retention",
    )(*args)
    return y, s_new


def _attn_prompt_kernel(qn_ref, qr_ref, kn_ref, v_ref, kr_ref, o_ref, m_ref, l_ref, acc_ref, *, tq, tk):
    qi = pl.program_id(1)
    ki = pl.program_id(2)

    @pl.when(ki == 0)
    def _():
        m_ref[...] = jnp.full_like(m_ref, NEG_INF)
        l_ref[...] = jnp.zeros_like(l_ref)
        acc_ref[...] = jnp.zeros_like(acc_ref)

    @pl.when(ki <= qi)
    def _():
        q_chunk = (qi * tq + lax.broadcasted_iota(jnp.int32, (tq, tk), 0)) // CHUNK
        k_chunk = (ki * tk + lax.broadcasted_iota(jnp.int32, (tq, tk), 1)) // CHUNK
        visible = k_chunk <= q_chunk
        kr = kr_ref[...]

        def head(h, carry):
            q = jnp.concatenate([qn_ref[h], qr_ref[h]], axis=1)
            k = jnp.concatenate([kn_ref[h], kr], axis=1)
            s = _bdot_nt(q, k) * MLA_SCALE
            s = jnp.where(visible, s, NEG_INF)
            m_prev = m_ref[h]
            m_new = jnp.maximum(m_prev, jnp.max(s, axis=-1, keepdims=True))
            alpha = jnp.exp(m_prev - m_new)
            p = jnp.exp(s - m_new)
            l_ref[h] = alpha * l_ref[h] + jnp.sum(p, axis=-1, keepdims=True)
            acc_ref[h] = alpha * acc_ref[h] + _bdot(p, v_ref[h])
            m_ref[h] = m_new
            return carry

        lax.fori_loop(0, MLA_HEADS, head, 0)

    @pl.when(ki == qi)
    def _():
        for h in range(MLA_HEADS):
            o_ref[:, h * MLA_VDIM:(h + 1) * MLA_VDIM] = (acc_ref[h] / l_ref[h]).astype(o_ref.dtype)


def _attn_prompt(qn, qr, kn, v, kr, nb, l):
    tq = tk = min(l, 512)
    nq = l // tq
    q_spec = pl.BlockSpec((MLA_HEADS, tq, LANES), lambda b, qi, ki: (0, b * nq + qi, 0))
    k_spec = pl.BlockSpec((MLA_HEADS, tk, LANES), lambda b, qi, ki: (0, b * nq + jnp.minimum(ki, qi), 0))
    return pl.pallas_call(
        functools.partial(_attn_prompt_kernel, tq=tq, tk=tk),
        grid=(nb, nq, nq),
        in_specs=[q_spec, q_spec, k_spec, k_spec,
                  pl.BlockSpec((tk, LANES), lambda b, qi, ki: (b * nq + jnp.minimum(ki, qi), 0))],
        out_specs=pl.BlockSpec((tq, MLA_HEADS * MLA_VDIM), lambda b, qi, ki: (b * nq + qi, 0)),
        out_shape=jax.ShapeDtypeStruct((nb * l, MLA_HEADS * MLA_VDIM), BF16),
        scratch_shapes=[pltpu.VMEM((MLA_HEADS, tq, 1), F32), pltpu.VMEM((MLA_HEADS, tq, 1), F32),
                        pltpu.VMEM((MLA_HEADS, tq, MLA_VDIM), F32)],
        compiler_params=_params(("parallel", "parallel", "arbitrary")),
        name="attn_prompt",
    )(qn, qr, kn, v, kr)


def _attn_cached_kernel(qn_ref, qr_ref, clat_ref, ckr_ref, nlat_ref, nkr_ref, wuk_ref, wuv_ref,
                        o_ref, ql_ref, qrs_ref, *, lq):
    for h in range(MLA_HEADS):
        ql_ref[h * lq:(h + 1) * lq, :] = _bdot_nt(qn_ref[h], wuk_ref[h]).astype(BF16)
        qrs_ref[h * lq:(h + 1) * lq, :] = qr_ref[h]
    ql = ql_ref[...]
    qr = qrs_ref[...]
    clat = clat_ref[...].astype(BF16)
    nlat = nlat_ref[...].astype(BF16)
    s_c = (_bdot_nt(ql, clat) + _bdot_nt(qr, ckr_ref[...])) * MLA_SCALE
    s_n = (_bdot_nt(ql, nlat) + _bdot_nt(qr, nkr_ref[...])) * MLA_SCALE
    m = jnp.maximum(jnp.max(s_c, axis=-1, keepdims=True), jnp.max(s_n, axis=-1, keepdims=True))
    p_c = jnp.exp(s_c - m)
    p_n = jnp.exp(s_n - m)
    denom = jnp.sum(p_c, axis=-1, keepdims=True) + jnp.sum(p_n, axis=-1, keepdims=True)
    o_lat = (_bdot(p_c, clat) + _bdot(p_n, nlat)) / denom
    for h in range(MLA_HEADS):
        o_ref[:, h * MLA_VDIM:(h + 1) * MLA_VDIM] = _bdot(
            o_lat[h * lq:(h + 1) * lq, :], wuv_ref[h]).astype(o_ref.dtype)


def _attn_cached(qn, qr, cache_lat, cache_kr_pad, new_lat, new_kr_pad, wuk_h, wuv_h, nb, lq):
    past = cache_lat.shape[1]
    whole = lambda b: (0, 0, 0)
    return pl.pallas_call(
        functools.partial(_attn_cached_kernel, lq=lq),
        grid=(nb,),
        in_specs=[
            pl.BlockSpec((MLA_HEADS, lq, LANES), lambda b: (0, b, 0)),
            pl.BlockSpec((MLA_HEADS, lq, LANES), lambda b: (0, b, 0)),
            pl.BlockSpec((None, past, KV_LORA), lambda b: (b, 0, 0)),
            pl.BlockSpec((None, past, LANES), lambda b: (b, 0, 0)),
            pl.BlockSpec((lq, KV_LORA), lambda b: (b, 0)),
            pl.BlockSpec((lq, LANES), lambda b: (b, 0)),
            pl.BlockSpec((MLA_HEADS, KV_LORA, MLA_NOPE), whole),
            pl.BlockSpec((MLA_HEADS, KV_LORA, MLA_VDIM), whole),
        ],
        out_specs=pl.BlockSpec((lq, MLA_HEADS * MLA_VDIM), lambda b: (b, 0)),
        out_shape=jax.ShapeDtypeStruct((nb * lq, MLA_HEADS * MLA_VDIM), BF16),
        scratch_shapes=[pltpu.VMEM((MLA_HEADS * lq, KV_LORA), BF16),
                        pltpu.VMEM((MLA_HEADS * lq, LANES), BF16)],
        compiler_params=_params(("parallel",)),
        name="attn_cached",
    )(qn, qr, cache_lat, cache_kr_pad, new_lat, new_kr_pad, wuk_h, wuv_h)


def _top_values(s, top_ref):
    for r in range(PEER_TOPK):
        m = jnp.max(s, axis=0, keepdims=True)
        top_ref[r:r + 1, :] = m
        s = jnp.where(s == m, -jnp.inf, s)


def _peer_select_kernel(h_ref, wq_ref, keys_ref, s1_ref, e1_ref, thr_ref, e0_ref, a0_ref, a1_ref):
    q_t = _bdot_nt(wq_ref[...], h_ref[...])
    s0 = _bdot(keys_ref[0], q_t[:N_KEYS])
    s1 = _bdot(keys_ref[1], q_t[N_KEYS:])
    _top_values(s0, a0_ref)
    _top_values(s1, a1_ref)
    a0 = a0_ref[...]
    a1 = a1_ref[...]
    cand = jnp.concatenate([a1[b:b + 1, :] + a0 for b in range(PEER_TOPK)], axis=0)
    c = cand
    for r in range(PEER_TOPK):
        tau = jnp.max(c, axis=0, keepdims=True)
        if r + 1 < PEER_TOPK:
            c = jnp.where(c == tau, -jnp.inf, c)
    sel = cand >= tau
    cmax = a0[0:1, :] + a1[0:1, :]
    z = jnp.sum(jnp.where(sel, jnp.exp(cand - cmax), 0.0), axis=0, keepdims=True)
    thr_rank = jnp.full(a0.shape, jnp.inf, F32)
    for b in range(PEER_TOPK):
        thr_rank = jnp.where(sel[b * PEER_TOPK:(b + 1) * PEER_TOPK, :], a1[b:b + 1, :], thr_rank)
    thr = jnp.full(s0.shape, jnp.inf, F32)
    for a in range(PEER_TOPK):
        thr = jnp.where(s0 == a0[a:a + 1, :], thr_rank[a:a + 1, :], thr)
    s1_ref[...] = s1
    thr_ref[...] = thr
    e0_ref[...] = jnp.exp(s0 - a0[0:1, :]) / z
    e1_ref[...] = jnp.exp(s1 - a1[0:1, :])


def _peer_select(hx, wq_t, keys, layer, tt):
    t, d = hx.shape
    out = jax.ShapeDtypeStruct((PEER_HEADS, N_KEYS, t), F32)
    o_spec = pl.BlockSpec((None, N_KEYS, tt), lambda i, h: (h, 0, i))
    return pl.pallas_call(
        _peer_select_kernel,
        grid=(t // tt, PEER_HEADS),
        in_specs=[pl.BlockSpec((tt, d), lambda i, h: (i, 0)),
                  pl.BlockSpec((2 * N_KEYS, d), lambda i, h: (h, 0)),
                  pl.BlockSpec((None, None, 2, N_KEYS, N_KEYS), lambda i, h: (layer, h, 0, 0, 0))],
        out_specs=[o_spec] * 4,
        out_shape=[out] * 4,
        scratch_shapes=[pltpu.VMEM((PEER_TOPK, tt), F32), pltpu.VMEM((PEER_TOPK, tt), F32)],
        compiler_params=_params(("parallel", "arbitrary")),
        name="peer_select",
    )(hx, wq_t, keys)


def _peer_mix_kernel(x_ref, u_ref, vt_ref, s1_ref, e1_ref, thr_ref, e0_ref, o_ref, acc_ref, *, n_i):
    j = pl.program_id(1)

    @pl.when(j == 0)
    def _():
        acc_ref[...] = jnp.zeros_like(acc_ref)

    a_t = _bdot_nt(u_ref[...], x_ref[...])
    rows = []
    for ii in range(n_i):
        w = None
        for h in range(PEER_HEADS):
            hit = s1_ref[h] >= thr_ref[h, ii:ii + 1, :]
            term = jnp.where(hit, e1_ref[h] * e0_ref[h, ii:ii + 1, :], 0.0)
            w = term if w is None else w + term
        rows.append(w)
    gates = jnp.concatenate(rows, axis=0)
    p = (gates * _gelu(a_t)).astype(BF16)
    acc_ref[...] += jnp.dot(vt_ref[...], p, preferred_element_type=F32)

    @pl.when(j == pl.num_programs(1) - 1)
    def _():
        o_ref[...] = acc_ref[...].T


def _peer_mix(hx, u_b, vt_b, s1, e1, thr, e0, tt, et):
    t, d = hx.shape
    e = u_b.shape[0]
    n_i = et // N_KEYS
    tok_all = pl.BlockSpec((PEER_HEADS, N_KEYS, tt), lambda i, j: (0, 0, i))
    tok_i = pl.BlockSpec((PEER_HEADS, n_i, tt), lambda i, j: (0, j, i))
    return pl.pallas_call(
        functools.partial(_peer_mix_kernel, n_i=n_i),
        grid=(t // tt, e // et),
        in_specs=[pl.BlockSpec((tt, d), lambda i, j: (i, 0)),
                  pl.BlockSpec((et, d), lambda i, j: (j, 0)),
                  pl.BlockSpec((d, et), lambda i, j: (0, j)),
                  tok_all, tok_all, tok_i, tok_i],
        out_specs=pl.BlockSpec((tt, d), lambda i, j: (i, 0)),
        out_shape=jax.ShapeDtypeStruct((t, d), F32),
        scratch_shapes=[pltpu.VMEM((d, tt), F32)],
        compiler_params=_params(("parallel", "arbitrary")),
        name="peer_mix",
    )(hx, u_b, vt_b, s1, e1, thr, e0)


def _peer(hx, wq_t, keys, layer, u_b, vt_b):
    t = hx.shape[0]
    tt = min(t, 512)
    s1, e1, thr, e0 = _peer_select(hx, wq_t, keys, layer, tt)
    return _peer_mix(hx, u_b, vt_b, s1, e1, thr, e0, tt, 1024)


def _rope_tables(pos, half, reps):
    inv = ROPE_THETA ** (-jnp.arange(half, dtype=F32) / half)
    ang = pos.astype(F32)[:, None] * inv[None, :]
    return jnp.tile(jnp.cos(ang), (1, reps)), jnp.tile(jnp.sin(ang), (1, reps))


def _rotate_half_cols(w):
    half = w.shape[-1] // 2
    return jnp.concatenate([-w[..., half:], w[..., :half]], axis=-1)


def _pad_last(w, width):
    return jnp.pad(w, [(0, 0)] * (w.ndim - 1) + [(0, width - w.shape[-1])])


def _shared_weights(p):
    out = {}
    w_rope = p["mla_w_dkv"][:, KV_LORA:]
    out["w_dkv_ext"] = jnp.concatenate(
        [p["mla_w_dkv"][:, :KV_LORA], _pad_last(w_rope, LANES), _pad_last(_rotate_half_cols(w_rope), LANES)],
        axis=1)
    w_uq = p["mla_w_uq"][0].reshape(-1, MLA_HEADS, MLA_NOPE + MLA_ROPE)
    q_lora = w_uq.shape[0]
    out["w_uq_nope"] = w_uq[:, :, :MLA_NOPE].reshape(q_lora, MLA_HEADS * MLA_NOPE)
    w_qr = w_uq[:, :, MLA_NOPE:]
    out["w_uq_rope"] = _pad_last(w_qr, LANES).reshape(q_lora, MLA_HEADS * LANES)
    out["w_uq_rot"] = _pad_last(_rotate_half_cols(w_qr), LANES).reshape(q_lora, MLA_HEADS * LANES)
    out["w_uk_flat"] = p["mla_w_uk"].reshape(KV_LORA, MLA_HEADS * MLA_NOPE)
    out["w_uv_flat"] = p["mla_w_uv"].reshape(KV_LORA, MLA_HEADS * MLA_VDIM)
    out["w_uk_h"] = jnp.transpose(p["mla_w_uk"], (1, 0, 2))
    out["w_uv_h"] = jnp.transpose(p["mla_w_uv"], (1, 0, 2))
    out["peer_wq_t"] = [p["peer_w_q"][l].T for l in range(2)]
    out["peer_u_b"] = [p["peer_u"][l].astype(BF16) for l in range(2)]
    out["peer_vt_b"] = [p["peer_v"][l].astype(BF16).T for l in range(2)]
    return out


def _adaln(c_all, w, b, lead, n_out):
    m, d = c_all.shape
    tn = 512
    if w.ndim == 2:
        w = w[None]
        b = b[None]
    b3 = b.reshape(b.shape[0], 1, n_out)
    return _linear(
        c_all, [w], n_cols=n_out, tm=m, tn=tn, epilogue=_ep_bias, prologue=_silu, w_lead=lead,
        extras=[b3], extra_specs=[pl.BlockSpec((None, 1, tn), lambda i, j: (lead, 0, j))],
        out_shapes=[jax.ShapeDtypeStruct((m, n_out), F32)],
        out_specs=[pl.BlockSpec((m, tn), lambda i, j: (0, j))])[0]


def _trunk(x, pos, mods, kv_mod, p, sw, ret_state, cache_lat, cache_kr):
    nb, l, d = x.shape
    t = nb * l
    pos_rows = jnp.tile(pos, nb) if l < 256 else pos
    table_rows = pos_rows.shape[0]

    def vec(a):
        return a.reshape(nb, 1, d)

    def flat(a):
        return a.reshape(t, a.shape[-1])

    sh1, sc1, gt1, sh2, sc2, gt2 = [vec(m) for m in mods[0]]
    (h,) = _resid_mod(x, mods=[(p["norm_g"][0, 0][None], sh1, sc1)])
    h = flat(h)
    tm = _row_tile(t, d)
    cos_r, sin_r = _rope_tables(pos_rows, RET_DK // 2, 1)
    tspec = _table_spec(tm, table_rows)
    w_in = p["ret_w_in"]
    qk_cols = RET_HEADS * RET_DK
    v_cols = RET_HEADS * RET_DV
    tn = 512

    def rope_proj(col0, scale):
        return _linear(
            h, [w_in], n_cols=qk_cols, tm=tm, tn=tn, w_lead=0, col_block0=col0 // tn,
            epilogue=functools.partial(_ep_rope_half128, scale=scale),
            extras=[cos_r, sin_r], extra_specs=[tspec, tspec],
            out_shapes=[jax.ShapeDtypeStruct((t, qk_cols), BF16)],
            out_specs=[pl.BlockSpec((tm, tn), lambda i, j: (i, j))])[0]

    def plain_proj(x_in, w, n_cols, col0=0, lead=None, dtype=BF16, tm_=None):
        tm_ = tm_ or _row_tile(x_in.shape[0], x_in.shape[1])
        tn_ = min(n_cols, 512)
        return _linear(
            x_in, [w], n_cols=n_cols, tm=tm_, tn=tn_, w_lead=lead, col_block0=col0 // tn_,
            epilogue=_ep_plain,
            out_shapes=[jax.ShapeDtypeStruct((x_in.shape[0], n_cols), dtype)],
            out_specs=[pl.BlockSpec((tm_, tn_), lambda i, j: (i, j))])[0]

    def heads_proj(x_in, ws, n_cols, epilogue=_ep_heads, extras=(), extra_specs=()):
        tm_ = _row_tile(x_in.shape[0], x_in.shape[1])
        tn_ = 512
        return _linear(
            x_in, ws, n_cols=n_cols, tm=tm_, tn=tn_, epilogue=epilogue, extras=extras, extra_specs=extra_specs,
            out_shapes=[jax.ShapeDtypeStruct((n_cols // LANES, x_in.shape[0], LANES), BF16)],
            out_specs=[pl.BlockSpec((tn_ // LANES, tm_, LANES), lambda i, j: (j, i, 0))])[0]

    q_r = rope_proj(0, 1.0)
    k_r = rope_proj(qk_cols, RET_DK ** -0.5)
    v_r = plain_proj(h, w_in, v_cols, col0=2 * qk_cols, lead=0)
    g_r = plain_proj(h, w_in, v_cols, col0=2 * qk_cols + v_cols, lead=0)
    s0 = None if ret_state is None else ret_state[0]
    y_r, s_new = _retention(q_r, k_r, v_r, g_r, p["ret_gn_g"], s0, nb, l)
    mix = plain_proj(y_r, p["ret_w_out"], d, lead=0, dtype=F32)
    x, h = _resid_mod(x, mix.reshape(nb, l, d), gt1, mods=[(p["norm_g"][0, 1][None], sh2, sc2)])
    ff = _peer(flat(h), sw["peer_wq_t"][0], p["peer_keys"], 0, sw["peer_u_b"][0], sw["peer_vt_b"][0])

    sh1, sc1, gt1b, sh2, sc2, gt2b = [vec(m) for m in mods[1]]
    kv_sh, kv_sc = [vec(m) for m in kv_mod]
    x, h, h_kv = _resid_mod(x, ff.reshape(nb, l, d), gt2,
                            mods=[(p["norm_g"][1, 0][None], sh1, sc1), (p["kv_norm_g"][None], kv_sh, kv_sc)])
    h = flat(h)
    h_kv = flat(h_kv)
    cos_m, sin_m = _rope_tables(pos_rows, MLA_ROPE // 2, LANES // (MLA_ROPE // 2))
    n_ext = sw["w_dkv_ext"].shape[1]
    lat, kr, kr_pad = _linear(
        h_kv, [sw["w_dkv_ext"]], n_cols=n_ext, tm=tm, tn=n_ext, epilogue=_ep_kv,
        extras=[p["mla_kv_norm_g"][None], cos_m, sin_m],
        extra_specs=[pl.BlockSpec((1, KV_LORA), lambda i, j: (0, 0)), tspec, tspec],
        out_shapes=[jax.ShapeDtypeStruct((t, KV_LORA), F32), jax.ShapeDtypeStruct((t, MLA_ROPE), F32),
                    jax.ShapeDtypeStruct((t, LANES), BF16)],
        out_specs=[pl.BlockSpec((tm, KV_LORA), lambda i, j: (i, 0)),
                   pl.BlockSpec((tm, MLA_ROPE), lambda i, j: (i, 0)),
                   pl.BlockSpec((tm, LANES), lambda i, j: (i, 0))])
    q_lora = p["mla_w_dq"].shape[-1]
    cq = _linear(
        h, [p["mla_w_dq"]], n_cols=q_lora, tm=tm, tn=q_lora, w_lead=0, epilogue=_ep_rms,
        extras=[p["mla_q_norm_g"]], extra_specs=[pl.BlockSpec((1, q_lora), lambda i, j: (0, 0))],
        out_shapes=[jax.ShapeDtypeStruct((t, q_lora), BF16)],
        out_specs=[pl.BlockSpec((tm, q_lora), lambda i, j: (i, 0))])[0]
    hl = MLA_HEADS * LANES
    qn = heads_proj(cq, [sw["w_uq_nope"]], hl)
    qr = heads_proj(cq, [sw["w_uq_rope"], sw["w_uq_rot"]], hl, epilogue=_ep_rot_heads,
                    extras=[cos_m, sin_m], extra_specs=[tspec, tspec])
    if cache_lat is None:
        kn = heads_proj(lat, [sw["w_uk_flat"]], hl)
        vv = heads_proj(lat, [sw["w_uv_flat"]], hl)
        o = _attn_prompt(qn, qr, kn, vv, kr_pad, nb, l)
    else:
        ckr = _pad_last(cache_kr, LANES).astype(BF16)
        o = _attn_cached(qn, qr, cache_lat, ckr, lat, kr_pad, sw["w_uk_h"], sw["w_uv_h"], nb, l)
    mix = plain_proj(o, p["mla_w_o"], d, lead=0, dtype=F32)
    x, h = _resid_mod(x, mix.reshape(nb, l, d), gt1b, mods=[(p["norm_g"][1, 1][None], sh2, sc2)])
    ff = _peer(flat(h), sw["peer_wq_t"][1], p["peer_keys"], 1, sw["peer_u_b"][1], sw["peer_vt_b"][1])
    (y,) = _resid_mod(x, ff.reshape(nb, l, d), gt2b, final_g=p["final_g"][None])
    return y, s_new[None], lat.reshape(nb, l, KV_LORA), kr.reshape(nb, l, MLA_ROPE)


def kernel(x_prompt, x_sample, c_prompt, c_sample, state_retention, cache_mla_latent, cache_mla_krope,
           ada_w, ada_b, norm_g, ret_w_in, ret_gn_g, ret_w_out,
           kv_ada_w, kv_ada_b, kv_norm_g, mla_w_dkv, mla_kv_norm_g, mla_w_uk, mla_w_uv,
           mla_w_dq, mla_q_norm_g, mla_w_uq, mla_w_o,
           peer_w_q, peer_keys, peer_u, peer_v, final_g):
    p = dict(norm_g=norm_g, ret_w_in=ret_w_in, ret_gn_g=ret_gn_g[0][None], ret_w_out=ret_w_out,
             kv_norm_g=kv_norm_g, mla_w_dkv=mla_w_dkv, mla_kv_norm_g=mla_kv_norm_g,
             mla_w_uk=mla_w_uk, mla_w_uv=mla_w_uv, mla_w_dq=mla_w_dq, mla_q_norm_g=mla_q_norm_g,
             mla_w_uq=mla_w_uq, mla_w_o=mla_w_o, peer_w_q=peer_w_q, peer_keys=peer_keys,
             peer_u=peer_u, peer_v=peer_v, final_g=final_g)
    sw = _shared_weights(p)
    nbp, lp, d = x_prompt.shape
    nbs, ls, _ = x_sample.shape
    past = cache_mla_latent.shape[1]

    n_c = nbp + nbs
    c_all = jnp.pad(jnp.concatenate([c_prompt, c_sample], axis=0), ((0, (-n_c) % 8), (0, 0)))
    layer_mods = [_adaln(c_all, ada_w, ada_b, layer, 6 * d) for layer in range(2)]
    kv_mods = _adaln(c_all, kv_ada_w, kv_ada_b, 0, 2 * d)

    def stream_mods(lo, hi):
        return ([jnp.split(m[lo:hi], 6, axis=-1) for m in layer_mods], jnp.split(kv_mods[lo:hi], 2, axis=-1))

    mods_p, kv_p = stream_mods(0, nbp)
    mods_s, kv_s = stream_mods(nbp, n_c)
    pos_p = jnp.arange(lp, dtype=jnp.int32)
    pos_s = past + jnp.arange(ls, dtype=jnp.int32)
    y_p, ret_p, lat_p, kr_p = _trunk(x_prompt, pos_p, mods_p, kv_p, p, sw, None, None, None)
    y_s, ret_s, lat_s, kr_s = _trunk(x_sample, pos_s, mods_s, kv_s, p, sw, state_retention,
                                     cache_mla_latent, cache_mla_krope)
    return (y_p, y_s, ret_p, ret_s, lat_p, kr_p, lat_s, kr_s)
```

```python
import functools
import math

import jax
import jax.numpy as jnp
from jax import lax
from jax.experimental import pallas as pl
from jax.experimental.pallas import tpu as pltpu

F32 = jnp.float32
BF16 = jnp.bfloat16

EPS = 1e-6
ROPE_THETA = 10000.0
NEG_INF = -1e30
CHUNK = 64

RET_HEADS = 8
RET_DK = 256
RET_DV = 512

MLA_HEADS = 16
MLA_NOPE = 128
MLA_ROPE = 64
MLA_VDIM = 128
KV_LORA = 512
MLA_SCALE = (MLA_NOPE + MLA_ROPE) ** -0.5

PEER_HEADS = 8
N_KEYS = 128
PEER_TOPK = 16

LANES = 128
VMEM_LIMIT = 52 * 1024 * 1024
PEER_MIX_VMEM_LIMIT = 58 * 1024 * 1024


def _params(sem, vmem=VMEM_LIMIT, flags=None):
    return pltpu.CompilerParams(dimension_semantics=sem, vmem_limit_bytes=vmem, flags=flags)


def _bdot(a, b):
    return jnp.dot(a.astype(BF16), b.astype(BF16), preferred_element_type=F32)


def _bdot_nt(a, b):
    return lax.dot_general(a.astype(BF16), b.astype(BF16), (((1,), (1,)), ((), ())),
                           preferred_element_type=F32)


def _bdot_tn(a, b):
    return lax.dot_general(a.astype(BF16), b.astype(BF16), (((0,), (0,)), ((), ())),
                           preferred_element_type=F32)


def _silu(x):
    return x * (1.0 / (1.0 + jnp.exp(-x)))


def _gelu(x):
    return 0.5 * x * (1.0 + lax.erf(x * (0.5 ** 0.5)))


def _linear_kernel(*refs, n_w, n_extra, prologue, epilogue):
    x_ref = refs[0]
    w_refs = refs[1:1 + n_w]
    extra_refs = refs[1 + n_w:1 + n_w + n_extra]
    out_refs = refs[1 + n_w + n_extra:]
    xv = x_ref[...]
    if prologue is not None:
        xv = prologue(xv)
    xb = xv.astype(BF16)
    accs = [jnp.dot(xb, w[...].astype(BF16), preferred_element_type=F32) for w in w_refs]
    epilogue(accs, extra_refs, out_refs)


def _linear(x, ws, *, n_cols, tm, tn, epilogue, out_shapes, out_specs, w_lead=None, col_block0=0,
            extras=(), extra_specs=(), prologue=None, name=None):
    m, k = x.shape
    if name is None:
        name = "linear" + getattr(epilogue, "func", epilogue).__name__
    assert m % tm == 0 and n_cols % tn == 0
    if w_lead is None:
        w_spec = pl.BlockSpec((k, tn), lambda i, j: (0, j + col_block0))
    else:
        w_spec = pl.BlockSpec((None, k, tn), lambda i, j: (w_lead, 0, j + col_block0))
    kern = functools.partial(_linear_kernel, n_w=len(ws), n_extra=len(extras), prologue=prologue,
                             epilogue=epilogue)
    return pl.pallas_call(
        kern,
        grid=(m // tm, n_cols // tn),
        in_specs=[pl.BlockSpec((tm, k), lambda i, j: (i, 0))] + [w_spec] * len(ws) + list(extra_specs),
        out_specs=out_specs,
        out_shape=out_shapes,
        compiler_params=_params(("parallel", "arbitrary")),
        name=name,
    )(x, *ws, *extras)


def _ep_plain(accs, extras, outs):
    outs[0][...] = accs[0].astype(outs[0].dtype)


def _ep_heads(accs, extras, outs):
    acc = accs[0]
    for jj in range(acc.shape[1] // LANES):
        outs[0][jj] = acc[:, jj * LANES:(jj + 1) * LANES].astype(outs[0].dtype)


def _ep_bias(accs, extras, outs):
    outs[0][...] = accs[0] + extras[0][...]


def _ep_rope_half128(accs, extras, outs, *, scale):
    acc = accs[0]
    cos = extras[0][...]
    sin = extras[1][...]
    for g in range(acc.shape[1] // 256):
        x1 = acc[:, g * 256:g * 256 + 128]
        x2 = acc[:, g * 256 + 128:(g + 1) * 256]
        outs[0][:, g * 256:g * 256 + 128] = ((x1 * cos - x2 * sin) * scale).astype(outs[0].dtype)
        outs[0][:, g * 256 + 128:(g + 1) * 256] = ((x1 * sin + x2 * cos) * scale).astype(outs[0].dtype)


def _ep_rot_heads(accs, extras, outs):
    cos = extras[0][...]
    sin = extras[1][...]
    a, b = accs
    for jj in range(a.shape[1] // LANES):
        sl = slice(jj * LANES, (jj + 1) * LANES)
        outs[0][jj] = (a[:, sl] * cos + b[:, sl] * sin).astype(outs[0].dtype)


def _ep_rms(accs, extras, outs):
    acc = accs[0]
    g = extras[0][...]
    y = acc * lax.rsqrt(jnp.mean(acc * acc, axis=-1, keepdims=True) + EPS) * g
    outs[0][...] = y.astype(outs[0].dtype)


def _ep_kv(accs, extras, outs):
    acc = accs[0]
    g = extras[0][...]
    cos = extras[1][...]
    sin = extras[2][...]
    c = acc[:, :KV_LORA]
    outs[0][...] = c * lax.rsqrt(jnp.mean(c * c, axis=-1, keepdims=True) + EPS) * g
    kr = acc[:, KV_LORA:KV_LORA + LANES] * cos + acc[:, KV_LORA + LANES:KV_LORA + 2 * LANES] * sin
    outs[1][...] = kr[:, :MLA_ROPE]
    outs[2][...] = kr.astype(BF16)


def _row_tile(m, k):
    cap = 1024 if k <= 2048 else 512
    return min(m, cap)


def _table_spec(tm, table_rows):
    nblk = table_rows // tm
    return pl.BlockSpec((tm, LANES), lambda i, j: (i % nblk, 0))


def _ew_kernel(*refs, has_resid, n_mod, final):
    idx = 0
    x = refs[idx][...]; idx += 1
    if has_resid:
        mix = refs[idx][...]; idx += 1
        gate = refs[idx][...]; idx += 1
        x = x + (1.0 + gate) * mix
    mods = []
    for _ in range(n_mod):
        mods.append((refs[idx][...], refs[idx + 1][...], refs[idx + 2][...]))
        idx += 3
    if final:
        fg = refs[idx][...]; idx += 1
    outs = refs[idx:]
    o = 0
    xn = x * lax.rsqrt(jnp.mean(x * x, axis=-1, keepdims=True) + EPS)
    if final:
        outs[o][...] = xn * fg
        return
    if has_resid:
        outs[o][...] = x
        o += 1
    for g, sh, sc in mods:
        outs[o][...] = ((xn * g) * (1.0 + sc) + sh).astype(outs[o].dtype)
        o += 1


def _resid_mod(x, mix=None, gate=None, mods=(), final_g=None):
    nb, l, d = x.shape
    tl = min(l, 256)
    tok = pl.BlockSpec((None, tl, d), lambda b, i: (b, i, 0))
    per_b = pl.BlockSpec((None, 1, d), lambda b, i: (b, 0, 0))
    gain = pl.BlockSpec((1, d), lambda b, i: (0, 0))
    args, specs = [x], [tok]
    if mix is not None:
        args += [mix, gate]
        specs += [tok, per_b]
    for g, sh, sc in mods:
        args += [g, sh, sc]
        specs += [gain, per_b, per_b]
    out_shapes, out_specs = [], []
    if final_g is not None:
        args.append(final_g)
        specs.append(gain)
        out_shapes.append(jax.ShapeDtypeStruct((nb, l, d), F32))
        out_specs.append(tok)
    else:
        if mix is not None:
            out_shapes.append(jax.ShapeDtypeStruct((nb, l, d), F32))
            out_specs.append(tok)
        for _ in mods:
            out_shapes.append(jax.ShapeDtypeStruct((nb, l, d), BF16))
            out_specs.append(tok)
    kern = functools.partial(_ew_kernel, has_resid=mix is not None, n_mod=len(mods),
                             final=final_g is not None)
    return pl.pallas_call(
        kern, grid=(nb, l // tl), in_specs=specs, out_specs=out_specs, out_shape=out_shapes,
        compiler_params=_params(("parallel", "parallel")),
        name="resid_mod",
    )(*args)


def _ret_log_decay():
    return jnp.log1p(-jnp.exp2(-5.0 - jnp.arange(RET_HEADS, dtype=F32)))


def _retention_tables(c):
    log_g = _ret_log_decay()[:, None, None]
    n = jnp.arange(c, dtype=F32)
    dist = n[:, None] - n[None, :]
    same = (jnp.arange(c)[:, None] // CHUNK) == (jnp.arange(c)[None, :] // CHUNK)
    earlier = (jnp.arange(c)[None, :] // CHUNK) < (jnp.arange(c)[:, None] // CHUNK)
    mask = jnp.where(same[None], jnp.exp(jnp.abs(dist)[None] * log_g),
                     jnp.where(earlier[None], jnp.exp(dist[None] * log_g), 0.0))
    q_decay = jnp.exp((n[None, :, None] + 1.0) * log_g)
    k_decay = jnp.exp((c - 1.0 - n)[None, :, None] * log_g)
    blk_decay = jnp.exp(c * log_g)
    return mask, q_decay, k_decay, blk_decay


def _retention_kernel(*refs, has_s0):
    if has_s0:
        (q_ref, k_ref, v_ref, g_ref, mask_ref, qd_ref, kd_ref, bd_ref, gn_ref, s0_ref,
         y_ref, s_out_ref, s_ref) = refs
    else:
        (q_ref, k_ref, v_ref, g_ref, mask_ref, qd_ref, kd_ref, bd_ref, gn_ref,
         y_ref, s_out_ref, s_ref) = refs
    c = pl.program_id(2)

    @pl.when(c == 0)
    def _():
        if has_s0:
            s_ref[...] = s0_ref[...]
        else:
            s_ref[...] = jnp.zeros_like(s_ref)

    q = q_ref[...]
    k = k_ref[...]
    v = v_ref[...]
    s_prev = s_ref[...]
    scores = _bdot_nt(q, k) * mask_ref[...]
    y = _bdot(scores, v) + _bdot(q, s_prev) * qd_ref[...]
    k_scaled = k.astype(F32) * kd_ref[...]
    s_new = bd_ref[...] * s_prev + _bdot_tn(k_scaled, v)
    s_ref[...] = s_new

    mu = jnp.mean(y, axis=-1, keepdims=True)
    yc = y - mu
    var = jnp.mean(yc * yc, axis=-1, keepdims=True)
    yn = yc * lax.rsqrt(var + EPS) * gn_ref[...]
    y_ref[...] = (_silu(g_ref[...].astype(F32)) * yn).astype(y_ref.dtype)

    @pl.when(c == pl.num_programs(2) - 1)
    def _():
        s_out_ref[...] = s_new


def _retention(q, k, v, g, gn_g, s0, nb, l):
    cb = min(l, 256)
    nc = l // cb
    mask, qd, kd, bd = _retention_tables(cb)
    row = lambda b, h, c: (b * nc + c, h)
    per_h3 = lambda b, h, c: (h, 0, 0)
    in_specs = [
        pl.BlockSpec((cb, RET_DK), row), pl.BlockSpec((cb, RET_DK), row),
        pl.BlockSpec((cb, RET_DV), row), pl.BlockSpec((cb, RET_DV), row),
        pl.BlockSpec((None, cb, cb), per_h3), pl.BlockSpec((None, cb, 1), per_h3),
        pl.BlockSpec((None, cb, 1), per_h3), pl.BlockSpec((None, 1, 1), per_h3),
        pl.BlockSpec((1, RET_DV), lambda b, h, c: (0, h)),
    ]
    args = [q, k, v, g, mask, qd, kd, bd, gn_g]
    state_spec = pl.BlockSpec((None, None, RET_DK, RET_DV), lambda b, h, c: (b, h, 0, 0))
    if s0 is not None:
        in_specs.append(state_spec)
        args.append(s0)
    y, s_new = pl.pallas_call(
        functools.partial(_retention_kernel, has_s0=s0 is not None),
        grid=(nb, RET_HEADS, nc),
        in_specs=in_specs,
        out_specs=[pl.BlockSpec((cb, RET_DV), row), state_spec],
        out_shape=[jax.ShapeDtypeStruct((nb * l, RET_HEADS * RET_DV), BF16),
                   jax.ShapeDtypeStruct((nb, RET_HEADS, RET_DK, RET_DV), F32)],
        scratch_shapes=[pltpu.VMEM((RET_DK, RET_DV), F32)],
        compiler_params=_params(("parallel", "parallel", "arbitrary")),
        name="retention",
    )(*args)
    return y, s_new


def _attn_prompt_kernel(qn_ref, qr_ref, kn_ref, v_ref, kr_ref, o_ref, m_ref, acc_ref, *, tq, tk):
    qi = pl.program_id(1)
    ki = pl.program_id(2)
    exp2_scale = MLA_SCALE * math.log2(math.e)

    @pl.when(ki == 0)
    def _():
        m_ref[...] = jnp.full_like(m_ref, NEG_INF)
        acc_ref[...] = jnp.zeros_like(acc_ref)

    def block(masked):
        if masked:
            q_chunk = (qi * tq + lax.broadcasted_iota(jnp.int32, (tq, tk), 0)) // CHUNK
            k_chunk = (ki * tk + lax.broadcasted_iota(jnp.int32, (tq, tk), 1)) // CHUNK
            visible = k_chunk <= q_chunk
        kr = kr_ref[...]
        ones = jnp.ones((tk, LANES), BF16)

        def head(h, carry):
            q = jnp.concatenate([qn_ref[h], qr_ref[h]], axis=1)
            k = jnp.concatenate([kn_ref[h], kr], axis=1)
            s = _bdot_nt(q, k)
            if masked:
                s = jnp.where(visible, s, NEG_INF)
            m_prev = m_ref[h]
            m_new = jnp.maximum(m_prev, jnp.max(s, axis=-1, keepdims=True))
            alpha = jnp.exp2((m_prev - m_new) * exp2_scale)
            p = jnp.exp2((s - jnp.tile(m_new, (1, tk // LANES))) * exp2_scale)
            v_ext = jnp.concatenate([v_ref[h], ones], axis=1)
            acc_ref[h] = jnp.tile(alpha, (1, 2)) * acc_ref[h] + _bdot(p, v_ext)
            m_ref[h] = m_new
            return carry

        lax.fori_loop(0, MLA_HEADS, head, 0, unroll=8)

    @pl.when(ki < qi)
    def _():
        block(False)

    @pl.when(ki == qi)
    def _():
        block(True)
        for h in range(MLA_HEADS):
            acc = acc_ref[h]
            o_ref[:, h * MLA_VDIM:(h + 1) * MLA_VDIM] = (
                acc[:, :MLA_VDIM] / acc[:, MLA_VDIM:]).astype(o_ref.dtype)


def _attn_prompt(qn, qr, kn, v, kr, nb, l):
    tq = tk = min(l, 512)
    nq = l // tq
    q_spec = pl.BlockSpec((MLA_HEADS, tq, LANES), lambda b, qi, ki: (0, b * nq + qi, 0))
    k_spec = pl.BlockSpec((MLA_HEADS, tk, LANES), lambda b, qi, ki: (0, b * nq + jnp.minimum(ki, qi), 0))
    return pl.pallas_call(
        functools.partial(_attn_prompt_kernel, tq=tq, tk=tk),
        grid=(nb, nq, nq),
        in_specs=[q_spec, q_spec, k_spec, k_spec,
                  pl.BlockSpec((tk, LANES), lambda b, qi, ki: (b * nq + jnp.minimum(ki, qi), 0))],
        out_specs=pl.BlockSpec((tq, MLA_HEADS * MLA_VDIM), lambda b, qi, ki: (b * nq + qi, 0)),
        out_shape=jax.ShapeDtypeStruct((nb * l, MLA_HEADS * MLA_VDIM), BF16),
        scratch_shapes=[pltpu.VMEM((MLA_HEADS, tq, LANES), F32),
                        pltpu.VMEM((MLA_HEADS, tq, MLA_VDIM + LANES), F32)],
        compiler_params=_params(("parallel", "parallel", "arbitrary")),
        name="attn_prompt",
    )(qn, qr, kn, v, kr)


def _attn_cached_kernel(qn_ref, qr_ref, clat_ref, ckr_ref, nlat_ref, nkr_ref, wuk_ref, wuv_ref,
                        o_ref, ql_ref, qrs_ref, *, lq):
    for h in range(MLA_HEADS):
        ql_ref[h * lq:(h + 1) * lq, :] = _bdot_nt(qn_ref[h], wuk_ref[h]).astype(BF16)
        qrs_ref[h * lq:(h + 1) * lq, :] = qr_ref[h]
    ql = ql_ref[...]
    qr = qrs_ref[...]
    clat = clat_ref[...].astype(BF16)
    nlat = nlat_ref[...].astype(BF16)
    s_c = (_bdot_nt(ql, clat) + _bdot_nt(qr, ckr_ref[...])) * MLA_SCALE
    s_n = (_bdot_nt(ql, nlat) + _bdot_nt(qr, nkr_ref[...])) * MLA_SCALE
    m = jnp.maximum(jnp.max(s_c, axis=-1, keepdims=True), jnp.max(s_n, axis=-1, keepdims=True))
    p_c = jnp.exp(s_c - m)
    p_n = jnp.exp(s_n - m)
    denom = jnp.sum(p_c, axis=-1, keepdims=True) + jnp.sum(p_n, axis=-1, keepdims=True)
    o_lat = (_bdot(p_c, clat) + _bdot(p_n, nlat)) / denom
    for h in range(MLA_HEADS):
        o_ref[:, h * MLA_VDIM:(h + 1) * MLA_VDIM] = _bdot(
            o_lat[h * lq:(h + 1) * lq, :], wuv_ref[h]).astype(o_ref.dtype)


def _attn_cached(qn, qr, cache_lat, cache_kr_pad, new_lat, new_kr_pad, wuk_h, wuv_h, nb, lq):
    past = cache_lat.shape[1]
    whole = lambda b: (0, 0, 0)
    return pl.pallas_call(
        functools.partial(_attn_cached_kernel, lq=lq),
        grid=(nb,),
        in_specs=[
            pl.BlockSpec((MLA_HEADS, lq, LANES), lambda b: (0, b, 0)),
            pl.BlockSpec((MLA_HEADS, lq, LANES), lambda b: (0, b, 0)),
            pl.BlockSpec((None, past, KV_LORA), lambda b: (b, 0, 0)),
            pl.BlockSpec((None, past, LANES), lambda b: (b, 0, 0)),
            pl.BlockSpec((lq, KV_LORA), lambda b: (b, 0)),
            pl.BlockSpec((lq, LANES), lambda b: (b, 0)),
            pl.BlockSpec((MLA_HEADS, KV_LORA, MLA_NOPE), whole),
            pl.BlockSpec((MLA_HEADS, KV_LORA, MLA_VDIM), whole),
        ],
        out_specs=pl.BlockSpec((lq, MLA_HEADS * MLA_VDIM), lambda b: (b, 0)),
        out_shape=jax.ShapeDtypeStruct((nb * lq, MLA_HEADS * MLA_VDIM), BF16),
        scratch_shapes=[pltpu.VMEM((MLA_HEADS * lq, KV_LORA), BF16),
                        pltpu.VMEM((MLA_HEADS * lq, LANES), BF16)],
        compiler_params=_params(("parallel",)),
        name="attn_cached",
    )(qn, qr, cache_lat, cache_kr_pad, new_lat, new_kr_pad, wuk_h, wuv_h)


def _top_values(s, top_ref):
    for r in range(PEER_TOPK):
        m = jnp.max(s, axis=0, keepdims=True)
        top_ref[r:r + 1, :] = m
        s = jnp.where(s == m, -jnp.inf, s)


def _peer_select_kernel(h_ref, wq_ref, keys_ref, s1_ref, e1_ref, thr_ref, e0_ref, a0_ref, a1_ref):
    q_t = _bdot_nt(wq_ref[...], h_ref[...])
    s0 = _bdot(keys_ref[0], q_t[:N_KEYS])
    s1 = _bdot(keys_ref[1], q_t[N_KEYS:])
    _top_values(s0, a0_ref)
    _top_values(s1, a1_ref)
    a0 = a0_ref[...]
    a1 = a1_ref[...]
    cand = jnp.concatenate([a1[b:b + 1, :] + a0 for b in range(PEER_TOPK)], axis=0)
    c = cand
    for r in range(PEER_TOPK):
        tau = jnp.max(c, axis=0, keepdims=True)
        if r + 1 < PEER_TOPK:
            c = jnp.where(c == tau, -jnp.inf, c)
    sel = cand >= tau
    cmax = a0[0:1, :] + a1[0:1, :]
    z = jnp.sum(jnp.where(sel, jnp.exp(cand - cmax), 0.0), axis=0, keepdims=True)
    thr_rank = jnp.full(a0.shape, jnp.inf, F32)
    for b in range(PEER_TOPK):
        thr_rank = jnp.where(sel[b * PEER_TOPK:(b + 1) * PEER_TOPK, :], a1[b:b + 1, :], thr_rank)
    thr = jnp.full(s0.shape, jnp.inf, F32)
    for a in range(PEER_TOPK):
        thr = jnp.where(s0 == a0[a:a + 1, :], thr_rank[a:a + 1, :], thr)
    s1_ref[...] = s1
    thr_ref[...] = thr
    e0_ref[...] = jnp.exp(s0 - a0[0:1, :]) / z
    e1_ref[...] = jnp.exp(s1 - a1[0:1, :])


def _peer_select(hx, wq_t, keys, layer, tt):
    t, d = hx.shape
    out = jax.ShapeDtypeStruct((PEER_HEADS, N_KEYS, t), F32)
    o_spec = pl.BlockSpec((None, N_KEYS, tt), lambda i, h: (h, 0, i))
    return pl.pallas_call(
        _peer_select_kernel,
        grid=(t // tt, PEER_HEADS),
        in_specs=[pl.BlockSpec((tt, d), lambda i, h: (i, 0)),
                  pl.BlockSpec((2 * N_KEYS, d), lambda i, h: (h, 0)),
                  pl.BlockSpec((None, None, 2, N_KEYS, N_KEYS), lambda i, h: (layer, h, 0, 0, 0))],
        out_specs=[o_spec] * 4,
        out_shape=[out] * 4,
        scratch_shapes=[pltpu.VMEM((PEER_TOPK, tt), F32), pltpu.VMEM((PEER_TOPK, tt), F32)],
        compiler_params=_params(("parallel", "arbitrary")),
        name="peer_select",
    )(hx, wq_t, keys)


def _peer_mix_kernel(x_ref, u_ref, vt_ref, s1_ref, e1_ref, thr_ref, e0_ref, o_ref,
                     acc_ref, a_ref, g_ref, p_ref, *, n_i):
    j = pl.program_id(1)

    @pl.when(j == 0)
    def _():
        acc_ref[...] = jnp.zeros_like(acc_ref)

    x = x_ref[...]
    rows_per_chunk = 2 * N_KEYS
    for c in range(n_i // 2):
        lo = c * rows_per_chunk
        a_ref[c] = _bdot_nt(u_ref[lo:lo + rows_per_chunk, :], x)
    for ii in range(n_i):
        w = None
        for h in range(PEER_HEADS):
            hit = s1_ref[h] >= thr_ref[h, ii:ii + 1, :]
            term = jnp.where(hit, e1_ref[h] * e0_ref[h, ii:ii + 1, :], 0.0)
            w = term if w is None else w + term
        g_ref[ii] = w
    for c in range(n_i // 2):
        lo = c * rows_per_chunk
        gates = jnp.concatenate([g_ref[2 * c], g_ref[2 * c + 1]], axis=0)
        p_ref[lo:lo + rows_per_chunk, :] = (gates * _gelu(a_ref[c])).astype(BF16)
    acc_ref[...] += jnp.dot(vt_ref[...], p_ref[...], preferred_element_type=F32)

    @pl.when(j == pl.num_programs(1) - 1)
    def _():
        o_ref[...] = acc_ref[...].T


def _peer_mix(hx, u_b, vt_b, s1, e1, thr, e0, tt, et):
    t, d = hx.shape
    e = u_b.shape[0]
    n_i = et // N_KEYS
    tok_all = pl.BlockSpec((PEER_HEADS, N_KEYS, tt), lambda i, j: (0, 0, i))
    tok_i = pl.BlockSpec((PEER_HEADS, n_i, tt), lambda i, j: (0, j, i))
    return pl.pallas_call(
        functools.partial(_peer_mix_kernel, n_i=n_i),
        grid=(t // tt, e // et),
        in_specs=[pl.BlockSpec((tt, d), lambda i, j: (i, 0)),
                  pl.BlockSpec((et, d), lambda i, j: (j, 0)),
                  pl.BlockSpec((d, et), lambda i, j: (0, j)),
                  tok_all, tok_all, tok_i, tok_i],
        out_specs=pl.BlockSpec((tt, d), lambda i, j: (i, 0)),
        out_shape=jax.ShapeDtypeStruct((t, d), F32),
        scratch_shapes=[pltpu.VMEM((d, tt), F32),
                        pltpu.VMEM((n_i // 2, 2 * N_KEYS, tt), F32),
                        pltpu.VMEM((n_i, N_KEYS, tt), F32),
                        pltpu.VMEM((et, tt), BF16)],
        compiler_params=_params(("parallel", "arbitrary"), vmem=PEER_MIX_VMEM_LIMIT),
        name="peer_mix",
    )(hx, u_b, vt_b, s1, e1, thr, e0)


def _peer(hx, wq_t, keys, layer, u_b, vt_b):
    t = hx.shape[0]
    tt = min(t, 512)
    s1, e1, thr, e0 = _peer_select(hx, wq_t, keys, layer, tt)
    return _peer_mix(hx, u_b, vt_b, s1, e1, thr, e0, tt, 1024)


def _rope_tables(pos, half, reps):
    inv = ROPE_THETA ** (-jnp.arange(half, dtype=F32) / half)
    ang = pos.astype(F32)[:, None] * inv[None, :]
    return jnp.tile(jnp.cos(ang), (1, reps)), jnp.tile(jnp.sin(ang), (1, reps))


def _rotate_half_cols(w):
    half = w.shape[-1] // 2
    return jnp.concatenate([-w[..., half:], w[..., :half]], axis=-1)


def _pad_last(w, width):
    return jnp.pad(w, [(0, 0)] * (w.ndim - 1) + [(0, width - w.shape[-1])])


def _shared_weights(p):
    out = {}
    w_rope = p["mla_w_dkv"][:, KV_LORA:]
    out["w_dkv_ext"] = jnp.concatenate(
        [p["mla_w_dkv"][:, :KV_LORA], _pad_last(w_rope, LANES), _pad_last(_rotate_half_cols(w_rope), LANES)],
        axis=1)
    w_uq = p["mla_w_uq"][0].reshape(-1, MLA_HEADS, MLA_NOPE + MLA_ROPE)
    q_lora = w_uq.shape[0]
    out["w_uq_nope"] = w_uq[:, :, :MLA_NOPE].reshape(q_lora, MLA_HEADS * MLA_NOPE)
    w_qr = w_uq[:, :, MLA_NOPE:]
    out["w_uq_rope"] = _pad_last(w_qr, LANES).reshape(q_lora, MLA_HEADS * LANES)
    out["w_uq_rot"] = _pad_last(_rotate_half_cols(w_qr), LANES).reshape(q_lora, MLA_HEADS * LANES)
    out["w_uk_flat"] = p["mla_w_uk"].reshape(KV_LORA, MLA_HEADS * MLA_NOPE)
    out["w_uv_flat"] = p["mla_w_uv"].reshape(KV_LORA, MLA_HEADS * MLA_VDIM)
    out["w_uk_h"] = jnp.transpose(p["mla_w_uk"], (1, 0, 2))
    out["w_uv_h"] = jnp.transpose(p["mla_w_uv"], (1, 0, 2))
    out["peer_wq_t"] = [p["peer_w_q"][l].T for l in range(2)]
    out["peer_u_b"] = [p["peer_u"][l].astype(BF16) for l in range(2)]
    out["peer_vt_b"] = [p["peer_v"][l].astype(BF16).T for l in range(2)]
    return out


def _adaln(c_all, w, b, lead, n_out):
    m, d = c_all.shape
    tn = 512
    if w.ndim == 2:
        w = w[None]
        b = b[None]
    b3 = b.reshape(b.shape[0], 1, n_out)
    return _linear(
        c_all, [w], n_cols=n_out, tm=m, tn=tn, epilogue=_ep_bias, prologue=_silu, w_lead=lead,
        extras=[b3], extra_specs=[pl.BlockSpec((None, 1, tn), lambda i, j: (lead, 0, j))],
        out_shapes=[jax.ShapeDtypeStruct((m, n_out), F32)],
        out_specs=[pl.BlockSpec((m, tn), lambda i, j: (0, j))])[0]


def _trunk(x, pos, mods, kv_mod, p, sw, ret_state, cache_lat, cache_kr):
    nb, l, d = x.shape
    t = nb * l
    pos_rows = jnp.tile(pos, nb) if l < 256 else pos
    table_rows = pos_rows.shape[0]

    def vec(a):
        return a.reshape(nb, 1, d)

    def flat(a):
        return a.reshape(t, a.shape[-1])

    sh1, sc1, gt1, sh2, sc2, gt2 = [vec(m) for m in mods[0]]
    (h,) = _resid_mod(x, mods=[(p["norm_g"][0, 0][None], sh1, sc1)])
    h = flat(h)
    tm = _row_tile(t, d)
    cos_r, sin_r = _rope_tables(pos_rows, RET_DK // 2, 1)
    tspec = _table_spec(tm, table_rows)
    w_in = p["ret_w_in"]
    qk_cols = RET_HEADS * RET_DK
    v_cols = RET_HEADS * RET_DV
    tn = 512

    def rope_proj(col0, scale):
        return _linear(
            h, [w_in], n_cols=qk_cols, tm=tm, tn=tn, w_lead=0, col_block0=col0 // tn,
            epilogue=functools.partial(_ep_rope_half128, scale=scale),
            extras=[cos_r, sin_r], extra_specs=[tspec, tspec],
            out_shapes=[jax.ShapeDtypeStruct((t, qk_cols), BF16)],
            out_specs=[pl.BlockSpec((tm, tn), lambda i, j: (i, j))])[0]

    def plain_proj(x_in, w, n_cols, col0=0, lead=None, dtype=BF16, tm_=None):
        tm_ = tm_ or _row_tile(x_in.shape[0], x_in.shape[1])
        tn_ = min(n_cols, 512)
        return _linear(
            x_in, [w], n_cols=n_cols, tm=tm_, tn=tn_, w_lead=lead, col_block0=col0 // tn_,
            epilogue=_ep_plain,
            out_shapes=[jax.ShapeDtypeStruct((x_in.shape[0], n_cols), dtype)],
            out_specs=[pl.BlockSpec((tm_, tn_), lambda i, j: (i, j))])[0]

    def heads_proj(x_in, ws, n_cols, epilogue=_ep_heads, extras=(), extra_specs=()):
        tm_ = _row_tile(x_in.shape[0], x_in.shape[1])
        tn_ = 512
        return _linear(
            x_in, ws, n_cols=n_cols, tm=tm_, tn=tn_, epilogue=epilogue, extras=extras, extra_specs=extra_specs,
            out_shapes=[jax.ShapeDtypeStruct((n_cols // LANES, x_in.shape[0], LANES), BF16)],
            out_specs=[pl.BlockSpec((tn_ // LANES, tm_, LANES), lambda i, j: (j, i, 0))])[0]

    q_r = rope_proj(0, 1.0)
    k_r = rope_proj(qk_cols, RET_DK ** -0.5)
    v_r = plain_proj(h, w_in, v_cols, col0=2 * qk_cols, lead=0)
    g_r = plain_proj(h, w_in, v_cols, col0=2 * qk_cols + v_cols, lead=0)
    s0 = None if ret_state is None else ret_state[0]
    y_r, s_new = _retention(q_r, k_r, v_r, g_r, p["ret_gn_g"], s0, nb, l)
    mix = plain_proj(y_r, p["ret_w_out"], d, lead=0, dtype=F32)
    x, h = _resid_mod(x, mix.reshape(nb, l, d), gt1, mods=[(p["norm_g"][0, 1][None], sh2, sc2)])
    ff = _peer(flat(h), sw["peer_wq_t"][0], p["peer_keys"], 0, sw["peer_u_b"][0], sw["peer_vt_b"][0])

    sh1, sc1, gt1b, sh2, sc2, gt2b = [vec(m) for m in mods[1]]
    kv_sh, kv_sc = [vec(m) for m in kv_mod]
    x, h, h_kv = _resid_mod(x, ff.reshape(nb, l, d), gt2,
                            mods=[(p["norm_g"][1, 0][None], sh1, sc1), (p["kv_norm_g"][None], kv_sh, kv_sc)])
    h = flat(h)
    h_kv = flat(h_kv)
    cos_m, sin_m = _rope_tables(pos_rows, MLA_ROPE // 2, LANES // (MLA_ROPE // 2))
    n_ext = sw["w_dkv_ext"].shape[1]
    lat, kr, kr_pad = _linear(
        h_kv, [sw["w_dkv_ext"]], n_cols=n_ext, tm=tm, tn=n_ext, epilogue=_ep_kv,
        extras=[p["mla_kv_norm_g"][None], cos_m, sin_m],
        extra_specs=[pl.BlockSpec((1, KV_LORA), lambda i, j: (0, 0)), tspec, tspec],
        out_shapes=[jax.ShapeDtypeStruct((t, KV_LORA), F32), jax.ShapeDtypeStruct((t, MLA_ROPE), F32),
                    jax.ShapeDtypeStruct((t, LANES), BF16)],
        out_specs=[pl.BlockSpec((tm, KV_LORA), lambda i, j: (i, 0)),
                   pl.BlockSpec((tm, MLA_ROPE), lambda i, j: (i, 0)),
                   pl.BlockSpec((tm, LANES), lambda i, j: (i, 0))])
    q_lora = p["mla_w_dq"].shape[-1]
    cq = _linear(
        h, [p["mla_w_dq"]], n_cols=q_lora, tm=tm, tn=q_lora, w_lead=0, epilogue=_ep_rms,
        extras=[p["mla_q_norm_g"]], extra_specs=[pl.BlockSpec((1, q_lora), lambda i, j: (0, 0))],
        out_shapes=[jax.ShapeDtypeStruct((t, q_lora), BF16)],
        out_specs=[pl.BlockSpec((tm, q_lora), lambda i, j: (i, 0))])[0]
    hl = MLA_HEADS * LANES
    qn = heads_proj(cq, [sw["w_uq_nope"]], hl)
    qr = heads_proj(cq, [sw["w_uq_rope"], sw["w_uq_rot"]], hl, epilogue=_ep_rot_heads,
                    extras=[cos_m, sin_m], extra_specs=[tspec, tspec])
    if cache_lat is None:
        kn = heads_proj(lat, [sw["w_uk_flat"]], hl)
        vv = heads_proj(lat, [sw["w_uv_flat"]], hl)
        o = _attn_prompt(qn, qr, kn, vv, kr_pad, nb, l)
    else:
        ckr = _pad_last(cache_kr, LANES).astype(BF16)
        o = _attn_cached(qn, qr, cache_lat, ckr, lat, kr_pad, sw["w_uk_h"], sw["w_uv_h"], nb, l)
    mix = plain_proj(o, p["mla_w_o"], d, lead=0, dtype=F32)
    x, h = _resid_mod(x, mix.reshape(nb, l, d), gt1b, mods=[(p["norm_g"][1, 1][None], sh2, sc2)])
    ff = _peer(flat(h), sw["peer_wq_t"][1], p["peer_keys"], 1, sw["peer_u_b"][1], sw["peer_vt_b"][1])
    (y,) = _resid_mod(x, ff.reshape(nb, l, d), gt2b, final_g=p["final_g"][None])
    return y, s_new[None], lat.reshape(nb, l, KV_LORA), kr.reshape(nb, l, MLA_ROPE)


def kernel(x_prompt, x_sample, c_prompt, c_sample, state_retention, cache_mla_latent, cache_mla_krope,
           ada_w, ada_b, norm_g, ret_w_in, ret_gn_g, ret_w_out,
           kv_ada_w, kv_ada_b, kv_norm_g, mla_w_dkv, mla_kv_norm_g, mla_w_uk, mla_w_uv,
           mla_w_dq, mla_q_norm_g, mla_w_uq, mla_w_o,
           peer_w_q, peer_keys, peer_u, peer_v, final_g):
    p = dict(norm_g=norm_g, ret_w_in=ret_w_in, ret_gn_g=ret_gn_g[0][None], ret_w_out=ret_w_out,
             kv_norm_g=kv_norm_g, mla_w_dkv=mla_w_dkv, mla_kv_norm_g=mla_kv_norm_g,
             mla_w_uk=mla_w_uk, mla_w_uv=mla_w_uv, mla_w_dq=mla_w_dq, mla_q_norm_g=mla_q_norm_g,
             mla_w_uq=mla_w_uq, mla_w_o=mla_w_o, peer_w_q=peer_w_q, peer_keys=peer_keys,
             peer_u=peer_u, peer_v=peer_v, final_g=final_g)
    sw = _shared_weights(p)
    nbp, lp, d = x_prompt.shape
    nbs, ls, _ = x_sample.shape
    past = cache_mla_latent.shape[1]

    n_c = nbp + nbs
    c_all = jnp.pad(jnp.concatenate([c_prompt, c_sample], axis=0), ((0, (-n_c) % 8), (0, 0)))
    layer_mods = [_adaln(c_all, ada_w, ada_b, layer, 6 * d) for layer in range(2)]
    kv_mods = _adaln(c_all, kv_ada_w, kv_ada_b, 0, 2 * d)

    def stream_mods(lo, hi):
        return ([jnp.split(m[lo:hi], 6, axis=-1) for m in layer_mods], jnp.split(kv_mods[lo:hi], 2, axis=-1))

    mods_p, kv_p = stream_mods(0, nbp)
    mods_s, kv_s = stream_mods(nbp, n_c)
    pos_p = jnp.arange(lp, dtype=jnp.int32)
    pos_s = past + jnp.arange(ls, dtype=jnp.int32)
    y_p, ret_p, lat_p, kr_p = _trunk(x_prompt, pos_p, mods_p, kv_p, p, sw, None, None, None)
    y_s, ret_s, lat_s, kr_s = _trunk(x_sample, pos_s, mods_s, kv_s, p, sw, state_retention,
                                     cache_mla_latent, cache_mla_krope)
    return (y_p, y_s, ret_p, ret_s, lat_p, kr_p, lat_s, kr_s)
```

```python
import functools
import math

import jax
import jax.numpy as jnp
from jax import lax
from jax.experimental import pallas as pl
from jax.experimental.pallas import tpu as pltpu

F32 = jnp.float32
BF16 = jnp.bfloat16

EPS = 1e-6
ROPE_THETA = 10000.0
NEG_INF = -1e30
CHUNK = 64

RET_HEADS = 8
RET_DK = 256
RET_DV = 512

MLA_HEADS = 16
MLA_NOPE = 128
MLA_ROPE = 64
MLA_VDIM = 128
KV_LORA = 512
MLA_SCALE = (MLA_NOPE + MLA_ROPE) ** -0.5

PEER_HEADS = 8
N_KEYS = 128
PEER_TOPK = 16

LANES = 128
VMEM_LIMIT = 52 * 1024 * 1024
PEER_MIX_VMEM_LIMIT = 58 * 1024 * 1024


def _params(sem, vmem=VMEM_LIMIT, flags=None):
    return pltpu.CompilerParams(dimension_semantics=sem, vmem_limit_bytes=vmem, flags=flags)


def _bdot(a, b):
    return jnp.dot(a.astype(BF16), b.astype(BF16), preferred_element_type=F32)


def _bdot_nt(a, b):
    return lax.dot_general(a.astype(BF16), b.astype(BF16), (((1,), (1,)), ((), ())),
                           preferred_element_type=F32)


def _bdot_tn(a, b):
    return lax.dot_general(a.astype(BF16), b.astype(BF16), (((0,), (0,)), ((), ())),
                           preferred_element_type=F32)


def _silu(x):
    return x * (1.0 / (1.0 + jnp.exp(-x)))


def _gelu(x):
    return 0.5 * x * (1.0 + lax.erf(x * (0.5 ** 0.5)))


def _linear_kernel(*refs, n_w, n_extra, prologue, epilogue):
    x_ref = refs[0]
    w_refs = refs[1:1 + n_w]
    extra_refs = refs[1 + n_w:1 + n_w + n_extra]
    out_refs = refs[1 + n_w + n_extra:]
    xv = x_ref[...]
    if prologue is not None:
        xv = prologue(xv)
    xb = xv.astype(BF16)
    accs = [jnp.dot(xb, w[...].astype(BF16), preferred_element_type=F32) for w in w_refs]
    epilogue(accs, extra_refs, out_refs)


def _linear(x, ws, *, n_cols, tm, tn, epilogue, out_shapes, out_specs, w_lead=None, col_block0=0,
            extras=(), extra_specs=(), prologue=None, name=None):
    m, k = x.shape
    if name is None:
        name = "linear" + getattr(epilogue, "func", epilogue).__name__
    assert m % tm == 0 and n_cols % tn == 0
    if w_lead is None:
        w_spec = pl.BlockSpec((k, tn), lambda i, j: (0, j + col_block0))
    else:
        w_spec = pl.BlockSpec((None, k, tn), lambda i, j: (w_lead, 0, j + col_block0))
    kern = functools.partial(_linear_kernel, n_w=len(ws), n_extra=len(extras), prologue=prologue,
                             epilogue=epilogue)
    return pl.pallas_call(
        kern,
        grid=(m // tm, n_cols // tn),
        in_specs=[pl.BlockSpec((tm, k), lambda i, j: (i, 0))] + [w_spec] * len(ws) + list(extra_specs),
        out_specs=out_specs,
        out_shape=out_shapes,
        compiler_params=_params(("parallel", "arbitrary")),
        name=name,
    )(x, *ws, *extras)


def _ep_plain(accs, extras, outs):
    outs[0][...] = accs[0].astype(outs[0].dtype)


def _ep_heads(accs, extras, outs):
    acc = accs[0]
    for jj in range(acc.shape[1] // LANES):
        outs[0][jj] = acc[:, jj * LANES:(jj + 1) * LANES].astype(outs[0].dtype)


def _ep_bias(accs, extras, outs):
    outs[0][...] = accs[0] + extras[0][...]


def _ep_rope_half128(accs, extras, outs, *, scale):
    acc = accs[0]
    cos = extras[0][...]
    sin = extras[1][...]
    for g in range(acc.shape[1] // 256):
        x1 = acc[:, g * 256:g * 256 + 128]
        x2 = acc[:, g * 256 + 128:(g + 1) * 256]
        outs[0][:, g * 256:g * 256 + 128] = ((x1 * cos - x2 * sin) * scale).astype(outs[0].dtype)
        outs[0][:, g * 256 + 128:(g + 1) * 256] = ((x1 * sin + x2 * cos) * scale).astype(outs[0].dtype)


def _ep_rot_heads(accs, extras, outs):
    cos = extras[0][...]
    sin = extras[1][...]
    a, b = accs
    for jj in range(a.shape[1] // LANES):
        sl = slice(jj * LANES, (jj + 1) * LANES)
        outs[0][jj] = (a[:, sl] * cos + b[:, sl] * sin).astype(outs[0].dtype)


def _ep_rms(accs, extras, outs):
    acc = accs[0]
    g = extras[0][...]
    y = acc * lax.rsqrt(jnp.mean(acc * acc, axis=-1, keepdims=True) + EPS) * g
    outs[0][...] = y.astype(outs[0].dtype)


def _ep_kv(accs, extras, outs):
    acc = accs[0]
    g = extras[0][...]
    cos = extras[1][...]
    sin = extras[2][...]
    c = acc[:, :KV_LORA]
    outs[0][...] = c * lax.rsqrt(jnp.mean(c * c, axis=-1, keepdims=True) + EPS) * g
    kr = acc[:, KV_LORA:KV_LORA + LANES] * cos + acc[:, KV_LORA + LANES:KV_LORA + 2 * LANES] * sin
    outs[1][...] = kr[:, :MLA_ROPE]
    outs[2][...] = kr.astype(BF16)


def _row_tile(m, k):
    cap = 1024 if k <= 2048 else 512
    return min(m, cap)


def _table_spec(tm, table_rows):
    nblk = table_rows // tm
    return pl.BlockSpec((tm, LANES), lambda i, j: (i % nblk, 0))


def _ew_kernel(*refs, has_resid, n_mod, final):
    idx = 0
    x = refs[idx][...]; idx += 1
    if has_resid:
        mix = refs[idx][...]; idx += 1
        gate = refs[idx][...]; idx += 1
        x = x + (1.0 + gate) * mix
    mods = []
    for _ in range(n_mod):
        mods.append((refs[idx][...], refs[idx + 1][...], refs[idx + 2][...]))
        idx += 3
    if final:
        fg = refs[idx][...]; idx += 1
    outs = refs[idx:]
    o = 0
    xn = x * lax.rsqrt(jnp.mean(x * x, axis=-1, keepdims=True) + EPS)
    if final:
        outs[o][...] = xn * fg
        return
    if has_resid:
        outs[o][...] = x
        o += 1
    for g, sh, sc in mods:
        outs[o][...] = ((xn * g) * (1.0 + sc) + sh).astype(outs[o].dtype)
        o += 1


def _resid_mod(x, mix=None, gate=None, mods=(), final_g=None):
    nb, l, d = x.shape
    tl = min(l, 256)
    tok = pl.BlockSpec((None, tl, d), lambda b, i: (b, i, 0))
    per_b = pl.BlockSpec((None, 1, d), lambda b, i: (b, 0, 0))
    gain = pl.BlockSpec((1, d), lambda b, i: (0, 0))
    args, specs = [x], [tok]
    if mix is not None:
        args += [mix, gate]
        specs += [tok, per_b]
    for g, sh, sc in mods:
        args += [g, sh, sc]
        specs += [gain, per_b, per_b]
    out_shapes, out_specs = [], []
    if final_g is not None:
        args.append(final_g)
        specs.append(gain)
        out_shapes.append(jax.ShapeDtypeStruct((nb, l, d), F32))
        out_specs.append(tok)
    else:
        if mix is not None:
            out_shapes.append(jax.ShapeDtypeStruct((nb, l, d), F32))
            out_specs.append(tok)
        for _ in mods:
            out_shapes.append(jax.ShapeDtypeStruct((nb, l, d), BF16))
            out_specs.append(tok)
    kern = functools.partial(_ew_kernel, has_resid=mix is not None, n_mod=len(mods),
                             final=final_g is not None)
    return pl.pallas_call(
        kern, grid=(nb, l // tl), in_specs=specs, out_specs=out_specs, out_shape=out_shapes,
        compiler_params=_params(("parallel", "parallel")),
        name="resid_mod",
    )(*args)


def _ret_log_decay():
    return jnp.log1p(-jnp.exp2(-5.0 - jnp.arange(RET_HEADS, dtype=F32)))


def _retention_tables(c):
    log_g = _ret_log_decay()[:, None, None]
    n = jnp.arange(c, dtype=F32)
    dist = n[:, None] - n[None, :]
    same = (jnp.arange(c)[:, None] // CHUNK) == (jnp.arange(c)[None, :] // CHUNK)
    earlier = (jnp.arange(c)[None, :] // CHUNK) < (jnp.arange(c)[:, None] // CHUNK)
    mask = jnp.where(same[None], jnp.exp(jnp.abs(dist)[None] * log_g),
                     jnp.where(earlier[None], jnp.exp(dist[None] * log_g), 0.0))
    q_decay = jnp.exp((n[None, :, None] + 1.0) * log_g)
    k_decay = jnp.exp((c - 1.0 - n)[None, :, None] * log_g)
    blk_decay = jnp.exp(c * log_g)
    return mask, q_decay, k_decay, blk_decay


def _retention_kernel(*refs, has_s0):
    if has_s0:
        (q_ref, k_ref, v_ref, g_ref, mask_ref, qd_ref, kd_ref, bd_ref, gn_ref, s0_ref,
         y_ref, s_out_ref, s_ref) = refs
    else:
        (q_ref, k_ref, v_ref, g_ref, mask_ref, qd_ref, kd_ref, bd_ref, gn_ref,
         y_ref, s_out_ref, s_ref) = refs
    c = pl.program_id(2)

    @pl.when(c == 0)
    def _():
        if has_s0:
            s_ref[...] = s0_ref[...]
        else:
            s_ref[...] = jnp.zeros_like(s_ref)

    q = q_ref[...]
    k = k_ref[...]
    v = v_ref[...]
    s_prev = s_ref[...]
    scores = _bdot_nt(q, k) * mask_ref[...]
    y = _bdot(scores, v) + _bdot(q, s_prev) * qd_ref[...]
    k_scaled = k.astype(F32) * kd_ref[...]
    s_new = bd_ref[...] * s_prev + _bdot_tn(k_scaled, v)
    s_ref[...] = s_new

    mu = jnp.mean(y, axis=-1, keepdims=True)
    yc = y - mu
    var = jnp.mean(yc * yc, axis=-1, keepdims=True)
    yn = yc * lax.rsqrt(var + EPS) * gn_ref[...]
    y_ref[...] = (_silu(g_ref[...].astype(F32)) * yn).astype(y_ref.dtype)

    @pl.when(c == pl.num_programs(2) - 1)
    def _():
        s_out_ref[...] = s_new


def _retention(q, k, v, g, gn_g, s0, nb, l):
    cb = min(l, 256)
    nc = l // cb
    mask, qd, kd, bd = _retention_tables(cb)
    row = lambda b, h, c: (b * nc + c, h)
    per_h3 = lambda b, h, c: (h, 0, 0)
    in_specs = [
        pl.BlockSpec((cb, RET_DK), row), pl.BlockSpec((cb, RET_DK), row),
        pl.BlockSpec((cb, RET_DV), row), pl.BlockSpec((cb, RET_DV), row),
        pl.BlockSpec((None, cb, cb), per_h3), pl.BlockSpec((None, cb, 1), per_h3),
        pl.BlockSpec((None, cb, 1), per_h3), pl.BlockSpec((None, 1, 1), per_h3),
        pl.BlockSpec((1, RET_DV), lambda b, h, c: (0, h)),
    ]
    args = [q, k, v, g, mask, qd, kd, bd, gn_g]
    state_spec = pl.BlockSpec((None, None, RET_DK, RET_DV), lambda b, h, c: (b, h, 0, 0))
    if s0 is not None:
        in_specs.append(state_spec)
        args.append(s0)
    y, s_new = pl.pallas_call(
        functools.partial(_retention_kernel, has_s0=s0 is not None),
        grid=(nb, RET_HEADS, nc),
        in_specs=in_specs,
        out_specs=[pl.BlockSpec((cb, RET_DV), row), state_spec],
        out_shape=[jax.ShapeDtypeStruct((nb * l, RET_HEADS * RET_DV), BF16),
                   jax.ShapeDtypeStruct((nb, RET_HEADS, RET_DK, RET_DV), F32)],
        scratch_shapes=[pltpu.VMEM((RET_DK, RET_DV), F32)],
        compiler_params=_params(("parallel", "parallel", "arbitrary")),
        name="retention",
    )(*args)
    return y, s_new


def _attn_prompt_kernel(qn_ref, qr_ref, kn_ref, v_ref, kr_ref, o_ref, m_ref, acc_ref, *, tq, tk):
    qi = pl.program_id(1)
    ki = pl.program_id(2)
    exp2_scale = MLA_SCALE * math.log2(math.e)

    @pl.when(ki == 0)
    def _():
        m_ref[...] = jnp.full_like(m_ref, NEG_INF)
        acc_ref[...] = jnp.zeros_like(acc_ref)

    def block(masked):
        if masked:
            q_chunk = (qi * tq + lax.broadcasted_iota(jnp.int32, (tq, tk), 0)) // CHUNK
            k_chunk = (ki * tk + lax.broadcasted_iota(jnp.int32, (tq, tk), 1)) // CHUNK
            visible = k_chunk <= q_chunk
        kr = kr_ref[...]
        ones = jnp.ones((tk, LANES), BF16)

        def head(h, carry):
            q = jnp.concatenate([qn_ref[h], qr_ref[h]], axis=1)
            k = jnp.concatenate([kn_ref[h], kr], axis=1)
            s = _bdot_nt(q, k)
            if masked:
                s = jnp.where(visible, s, NEG_INF)
            m_prev = m_ref[h]
            m_new = jnp.maximum(m_prev, jnp.max(s, axis=-1, keepdims=True))
            alpha = jnp.exp2((m_prev - m_new) * exp2_scale)
            p = jnp.exp2((s - jnp.tile(m_new, (1, tk // LANES))) * exp2_scale)
            v_ext = jnp.concatenate([v_ref[h], ones], axis=1)
            acc_ref[h] = jnp.tile(alpha, (1, 2)) * acc_ref[h] + _bdot(p, v_ext)
            m_ref[h] = m_new
            return carry

        lax.fori_loop(0, MLA_HEADS, head, 0, unroll=8)

    @pl.when(ki < qi)
    def _():
        block(False)

    @pl.when(ki == qi)
    def _():
        block(True)
        for h in range(MLA_HEADS):
            acc = acc_ref[h]
            o_ref[:, h * MLA_VDIM:(h + 1) * MLA_VDIM] = (
                acc[:, :MLA_VDIM] / acc[:, MLA_VDIM:]).astype(o_ref.dtype)


def _attn_prompt(qn, qr, kn, v, kr, nb, l):
    tq = tk = min(l, 512)
    nq = l // tq
    q_spec = pl.BlockSpec((MLA_HEADS, tq, LANES), lambda b, qi, ki: (0, b * nq + qi, 0))
    k_spec = pl.BlockSpec((MLA_HEADS, tk, LANES), lambda b, qi, ki: (0, b * nq + jnp.minimum(ki, qi), 0))
    return pl.pallas_call(
        functools.partial(_attn_prompt_kernel, tq=tq, tk=tk),
        grid=(nb, nq, nq),
        in_specs=[q_spec, q_spec, k_spec, k_spec,
                  pl.BlockSpec((tk, LANES), lambda b, qi, ki: (b * nq + jnp.minimum(ki, qi), 0))],
        out_specs=pl.BlockSpec((tq, MLA_HEADS * MLA_VDIM), lambda b, qi, ki: (b * nq + qi, 0)),
        out_shape=jax.ShapeDtypeStruct((nb * l, MLA_HEADS * MLA_VDIM), BF16),
        scratch_shapes=[pltpu.VMEM((MLA_HEADS, tq, LANES), F32),
                        pltpu.VMEM((MLA_HEADS, tq, MLA_VDIM + LANES), F32)],
        compiler_params=_params(("parallel", "parallel", "arbitrary")),
        name="attn_prompt",
    )(qn, qr, kn, v, kr)


def _attn_cached_kernel(qn_ref, qr_ref, clat_ref, ckr_ref, nlat_ref, nkr_ref, wuk_ref, wuv_ref,
                        o_ref, ql_ref, qrs_ref, *, lq):
    for h in range(MLA_HEADS):
        ql_ref[h * lq:(h + 1) * lq, :] = _bdot_nt(qn_ref[h], wuk_ref[h]).astype(BF16)
        qrs_ref[h * lq:(h + 1) * lq, :] = qr_ref[h]
    ql = ql_ref[...]
    qr = qrs_ref[...]
    clat = clat_ref[...].astype(BF16)
    nlat = nlat_ref[...].astype(BF16)
    s_c = (_bdot_nt(ql, clat) + _bdot_nt(qr, ckr_ref[...])) * MLA_SCALE
    s_n = (_bdot_nt(ql, nlat) + _bdot_nt(qr, nkr_ref[...])) * MLA_SCALE
    m = jnp.maximum(jnp.max(s_c, axis=-1, keepdims=True), jnp.max(s_n, axis=-1, keepdims=True))
    p_c = jnp.exp(s_c - m)
    p_n = jnp.exp(s_n - m)
    denom = jnp.sum(p_c, axis=-1, keepdims=True) + jnp.sum(p_n, axis=-1, keepdims=True)
    o_lat = (_bdot(p_c, clat) + _bdot(p_n, nlat)) / denom
    for h in range(MLA_HEADS):
        o_ref[:, h * MLA_VDIM:(h + 1) * MLA_VDIM] = _bdot(
            o_lat[h * lq:(h + 1) * lq, :], wuv_ref[h]).astype(o_ref.dtype)


def _attn_cached(qn, qr, cache_lat, cache_kr_pad, new_lat, new_kr_pad, wuk_h, wuv_h, nb, lq):
    past = cache_lat.shape[1]
    whole = lambda b: (0, 0, 0)
    return pl.pallas_call(
        functools.partial(_attn_cached_kernel, lq=lq),
        grid=(nb,),
        in_specs=[
            pl.BlockSpec((MLA_HEADS, lq, LANES), lambda b: (0, b, 0)),
            pl.BlockSpec((MLA_HEADS, lq, LANES), lambda b: (0, b, 0)),
            pl.BlockSpec((None, past, KV_LORA), lambda b: (b, 0, 0)),
            pl.BlockSpec((None, past, LANES), lambda b: (b, 0, 0)),
            pl.BlockSpec((lq, KV_LORA), lambda b: (b, 0)),
            pl.BlockSpec((lq, LANES), lambda b: (b, 0)),
            pl.BlockSpec((MLA_HEADS, KV_LORA, MLA_NOPE), whole),
            pl.BlockSpec((MLA_HEADS, KV_LORA, MLA_VDIM), whole),
        ],
        out_specs=pl.BlockSpec((lq, MLA_HEADS * MLA_VDIM), lambda b: (b, 0)),
        out_shape=jax.ShapeDtypeStruct((nb * lq, MLA_HEADS * MLA_VDIM), BF16),
        scratch_shapes=[pltpu.VMEM((MLA_HEADS * lq, KV_LORA), BF16),
                        pltpu.VMEM((MLA_HEADS * lq, LANES), BF16)],
        compiler_params=_params(("parallel",)),
        name="attn_cached",
    )(qn, qr, cache_lat, cache_kr_pad, new_lat, new_kr_pad, wuk_h, wuv_h)


def _top_values(s, top_ref, want_rank):
    rank = jnp.full(s.shape, float(PEER_TOPK), F32) if want_rank else None
    for r in range(PEER_TOPK):
        m = jnp.max(s, axis=0, keepdims=True)
        top_ref[r:r + 1, :] = m
        hit = s == m
        if want_rank:
            rank = jnp.where(hit, float(r), rank)
        s = jnp.where(hit, -jnp.inf, s)
    return rank


_CAND_LIMITS = tuple(PEER_TOPK // (b + 1) for b in range(1, 8))
_HALF_TOPK = PEER_TOPK // 2


def _peer_select_kernel(h_ref, wq_ref, keys_ref, r1_ref, e1_ref, c0_ref, e0_ref, a0_ref, a1_ref):
    q_t = _bdot_nt(wq_ref[...], h_ref[...])
    s0 = _bdot(keys_ref[0], q_t[:N_KEYS])
    s1 = _bdot(keys_ref[1], q_t[N_KEYS:])
    _top_values(s0, a0_ref, False)
    rank1 = _top_values(s1, a1_ref, True)
    a0 = a0_ref[...]
    a1 = a1_ref[...]
    a0_lo = a0[:_HALF_TOPK, :]
    row = lax.broadcasted_iota(jnp.int32, a0_lo.shape, 0)
    pieces = [a1[0:1, :] + a0]
    for b, limit in enumerate(_CAND_LIMITS, start=1):
        pieces.append(jnp.where(row < limit, a1[b:b + 1, :] + a0_lo, -jnp.inf))
    pieces.append(a1[_HALF_TOPK:, :] + a0[0:1, :])
    cand = jnp.concatenate(pieces, axis=0)
    c = cand
    for r in range(PEER_TOPK):
        tau = jnp.max(c, axis=0, keepdims=True)
        if r + 1 < PEER_TOPK:
            c = jnp.where(c == tau, -jnp.inf, c)
    sel = cand >= tau
    cmax = a0[0:1, :] + a1[0:1, :]
    z = jnp.sum(jnp.where(sel, jnp.exp(cand - cmax), 0.0), axis=0, keepdims=True)
    self32 = sel.astype(F32)
    n_hi = self32[_HALF_TOPK:PEER_TOPK, :]
    n_lo = self32[:_HALF_TOPK, :]
    for b in range(1, 8):
        lo = PEER_TOPK + (b - 1) * _HALF_TOPK
        n_lo = n_lo + self32[lo:lo + _HALF_TOPK, :]
    tail = jnp.sum(self32[PEER_TOPK + 7 * _HALF_TOPK:, :], axis=0, keepdims=True)
    n_lo = n_lo + jnp.where(row == 0, tail, 0.0)
    count0 = jnp.zeros(s0.shape, F32)
    for a in range(PEER_TOPK):
        n_a = n_lo[a:a + 1, :] if a < _HALF_TOPK else n_hi[a - _HALF_TOPK:a - _HALF_TOPK + 1, :]
        count0 = jnp.where(s0 == a0[a:a + 1, :], n_a, count0)
    r1_ref[...] = rank1.astype(r1_ref.dtype)
    c0_ref[...] = count0
    e0_ref[...] = jnp.exp(s0 - a0[0:1, :]) / z
    e1_ref[...] = jnp.exp(s1 - a1[0:1, :]).astype(e1_ref.dtype)


def _peer_select(hx, wq_t, keys, layer, tt):
    t, d = hx.shape
    o_spec = pl.BlockSpec((None, N_KEYS, tt), lambda i, h: (h, 0, i))
    return pl.pallas_call(
        _peer_select_kernel,
        grid=(t // tt, PEER_HEADS),
        in_specs=[pl.BlockSpec((tt, d), lambda i, h: (i, 0)),
                  pl.BlockSpec((2 * N_KEYS, d), lambda i, h: (h, 0)),
                  pl.BlockSpec((None, None, 2, N_KEYS, N_KEYS), lambda i, h: (layer, h, 0, 0, 0))],
        out_specs=[o_spec] * 4,
        out_shape=[jax.ShapeDtypeStruct((PEER_HEADS, N_KEYS, t), dt) for dt in (BF16, BF16, F32, F32)],
        scratch_shapes=[pltpu.VMEM((PEER_TOPK, tt), F32), pltpu.VMEM((PEER_TOPK, tt), F32)],
        compiler_params=_params(("parallel", "arbitrary")),
        name="peer_select",
    )(hx, wq_t, keys)


def _sublane_bcast_bf16(row, rows):
    tile_rows = 16
    packed = jnp.broadcast_to(row, (tile_rows, row.shape[1])).astype(BF16)
    return jnp.tile(packed, (rows // tile_rows, 1))


def _peer_mix_kernel(x_ref, u_ref, vt_ref, r1_ref, e1_ref, c0_ref, e0_ref, o_ref,
                     acc_ref, a_ref, g_ref, p_ref, *, n_i):
    j = pl.program_id(1)

    @pl.when(j == 0)
    def _():
        acc_ref[...] = jnp.zeros_like(acc_ref)

    x = x_ref[...]
    rows_per_chunk = 2 * N_KEYS
    for c in range(n_i // 2):
        lo = c * rows_per_chunk
        a_ref[c] = _bdot_nt(u_ref[lo:lo + rows_per_chunk, :], x)
    zero = jnp.zeros((), BF16)
    for ii in range(n_i):
        w = None
        for h in range(PEER_HEADS):
            count = _sublane_bcast_bf16(c0_ref[h, ii:ii + 1, :], N_KEYS)
            e0 = _sublane_bcast_bf16(e0_ref[h, ii:ii + 1, :], N_KEYS)
            term = jnp.where(r1_ref[h] < count, e1_ref[h] * e0, zero)
            w = term if w is None else w + term
        g_ref[ii] = w
    for c in range(n_i // 2):
        lo = c * rows_per_chunk
        gates = jnp.concatenate([g_ref[2 * c], g_ref[2 * c + 1]], axis=0)
        p_ref[lo:lo + rows_per_chunk, :] = gates * _gelu(a_ref[c]).astype(BF16)
    acc_ref[...] += jnp.dot(vt_ref[...], p_ref[...], preferred_element_type=F32)

    @pl.when(j == pl.num_programs(1) - 1)
    def _():
        o_ref[...] = acc_ref[...].T


def _peer_mix(hx, u_b, vt_b, r1, e1, c0, e0, tt, et):
    t, d = hx.shape
    e = u_b.shape[0]
    n_i = et // N_KEYS
    tok_all = pl.BlockSpec((PEER_HEADS, N_KEYS, tt), lambda i, j: (0, 0, i))
    tok_i = pl.BlockSpec((PEER_HEADS, n_i, tt), lambda i, j: (0, j, i))
    return pl.pallas_call(
        functools.partial(_peer_mix_kernel, n_i=n_i),
        grid=(t // tt, e // et),
        in_specs=[pl.BlockSpec((tt, d), lambda i, j: (i, 0)),
                  pl.BlockSpec((et, d), lambda i, j: (j, 0)),
                  pl.BlockSpec((d, et), lambda i, j: (0, j)),
                  tok_all, tok_all, tok_i, tok_i],
        out_specs=pl.BlockSpec((tt, d), lambda i, j: (i, 0)),
        out_shape=jax.ShapeDtypeStruct((t, d), F32),
        scratch_shapes=[pltpu.VMEM((d, tt), F32),
                        pltpu.VMEM((n_i // 2, 2 * N_KEYS, tt), F32),
                        pltpu.VMEM((n_i, N_KEYS, tt), BF16),
                        pltpu.VMEM((et, tt), BF16)],
        compiler_params=_params(("parallel", "arbitrary"), vmem=PEER_MIX_VMEM_LIMIT),
        name="peer_mix",
    )(hx, u_b, vt_b, r1, e1, c0, e0)


def _peer(hx, wq_t, keys, layer, u_b, vt_b):
    t = hx.shape[0]
    tt = min(t, 512)
    r1, e1, c0, e0 = _peer_select(hx, wq_t, keys, layer, tt)
    return _peer_mix(hx, u_b, vt_b, r1, e1, c0, e0, tt, 1024)


def _rope_tables(pos, half, reps):
    inv = ROPE_THETA ** (-jnp.arange(half, dtype=F32) / half)
    ang = pos.astype(F32)[:, None] * inv[None, :]
    return jnp.tile(jnp.cos(ang), (1, reps)), jnp.tile(jnp.sin(ang), (1, reps))


def _rotate_half_cols(w):
    half = w.shape[-1] // 2
    return jnp.concatenate([-w[..., half:], w[..., :half]], axis=-1)


def _pad_last(w, width):
    return jnp.pad(w, [(0, 0)] * (w.ndim - 1) + [(0, width - w.shape[-1])])


def _shared_weights(p):
    out = {}
    w_rope = p["mla_w_dkv"][:, KV_LORA:]
    out["w_dkv_ext"] = jnp.concatenate(
        [p["mla_w_dkv"][:, :KV_LORA], _pad_last(w_rope, LANES), _pad_last(_rotate_half_cols(w_rope), LANES)],
        axis=1)
    w_uq = p["mla_w_uq"][0].reshape(-1, MLA_HEADS, MLA_NOPE + MLA_ROPE)
    q_lora = w_uq.shape[0]
    out["w_uq_nope"] = w_uq[:, :, :MLA_NOPE].reshape(q_lora, MLA_HEADS * MLA_NOPE)
    w_qr = w_uq[:, :, MLA_NOPE:]
    out["w_uq_rope"] = _pad_last(w_qr, LANES).reshape(q_lora, MLA_HEADS * LANES)
    out["w_uq_rot"] = _pad_last(_rotate_half_cols(w_qr), LANES).reshape(q_lora, MLA_HEADS * LANES)
    out["w_uk_flat"] = p["mla_w_uk"].reshape(KV_LORA, MLA_HEADS * MLA_NOPE)
    out["w_uv_flat"] = p["mla_w_uv"].reshape(KV_LORA, MLA_HEADS * MLA_VDIM)
    out["w_uk_h"] = jnp.transpose(p["mla_w_uk"], (1, 0, 2))
    out["w_uv_h"] = jnp.transpose(p["mla_w_uv"], (1, 0, 2))
    out["peer_wq_t"] = [p["peer_w_q"][l].T for l in range(2)]
    out["peer_u_b"] = [p["peer_u"][l].astype(BF16) for l in range(2)]
    out["peer_vt_b"] = [p["peer_v"][l].astype(BF16).T for l in range(2)]
    return out


def _adaln(c_all, w, b, lead, n_out):
    m, d = c_all.shape
    tn = 512
    if w.ndim == 2:
        w = w[None]
        b = b[None]
    b3 = b.reshape(b.shape[0], 1, n_out)
    return _linear(
        c_all, [w], n_cols=n_out, tm=m, tn=tn, epilogue=_ep_bias, prologue=_silu, w_lead=lead,
        extras=[b3], extra_specs=[pl.BlockSpec((None, 1, tn), lambda i, j: (lead, 0, j))],
        out_shapes=[jax.ShapeDtypeStruct((m, n_out), F32)],
        out_specs=[pl.BlockSpec((m, tn), lambda i, j: (0, j))])[0]


def _trunk(x, pos, mods, kv_mod, p, sw, ret_state, cache_lat, cache_kr):
    nb, l, d = x.shape
    t = nb * l
    pos_rows = jnp.tile(pos, nb) if l < 256 else pos
    table_rows = pos_rows.shape[0]

    def vec(a):
        return a.reshape(nb, 1, d)

    def flat(a):
        return a.reshape(t, a.shape[-1])

    sh1, sc1, gt1, sh2, sc2, gt2 = [vec(m) for m in mods[0]]
    (h,) = _resid_mod(x, mods=[(p["norm_g"][0, 0][None], sh1, sc1)])
    h = flat(h)
    tm = _row_tile(t, d)
    cos_r, sin_r = _rope_tables(pos_rows, RET_DK // 2, 1)
    tspec = _table_spec(tm, table_rows)
    w_in = p["ret_w_in"]
    qk_cols = RET_HEADS * RET_DK
    v_cols = RET_HEADS * RET_DV
    tn = 512

    def rope_proj(col0, scale):
        return _linear(
            h, [w_in], n_cols=qk_cols, tm=tm, tn=tn, w_lead=0, col_block0=col0 // tn,
            epilogue=functools.partial(_ep_rope_half128, scale=scale),
            extras=[cos_r, sin_r], extra_specs=[tspec, tspec],
            out_shapes=[jax.ShapeDtypeStruct((t, qk_cols), BF16)],
            out_specs=[pl.BlockSpec((tm, tn), lambda i, j: (i, j))])[0]

    def plain_proj(x_in, w, n_cols, col0=0, lead=None, dtype=BF16, tm_=None):
        tm_ = tm_ or _row_tile(x_in.shape[0], x_in.shape[1])
        tn_ = min(n_cols, 512)
        return _linear(
            x_in, [w], n_cols=n_cols, tm=tm_, tn=tn_, w_lead=lead, col_block0=col0 // tn_,
            epilogue=_ep_plain,
            out_shapes=[jax.ShapeDtypeStruct((x_in.shape[0], n_cols), dtype)],
            out_specs=[pl.BlockSpec((tm_, tn_), lambda i, j: (i, j))])[0]

    def heads_proj(x_in, ws, n_cols, epilogue=_ep_heads, extras=(), extra_specs=()):
        tm_ = _row_tile(x_in.shape[0], x_in.shape[1])
        tn_ = 512
        return _linear(
            x_in, ws, n_cols=n_cols, tm=tm_, tn=tn_, epilogue=epilogue, extras=extras, extra_specs=extra_specs,
            out_shapes=[jax.ShapeDtypeStruct((n_cols // LANES, x_in.shape[0], LANES), BF16)],
            out_specs=[pl.BlockSpec((tn_ // LANES, tm_, LANES), lambda i, j: (j, i, 0))])[0]

    q_r = rope_proj(0, 1.0)
    k_r = rope_proj(qk_cols, RET_DK ** -0.5)
    v_r = plain_proj(h, w_in, v_cols, col0=2 * qk_cols, lead=0)
    g_r = plain_proj(h, w_in, v_cols, col0=2 * qk_cols + v_cols, lead=0)
    s0 = None if ret_state is None else ret_state[0]
    y_r, s_new = _retention(q_r, k_r, v_r, g_r, p["ret_gn_g"], s0, nb, l)
    mix = plain_proj(y_r, p["ret_w_out"], d, lead=0, dtype=F32)
    x, h = _resid_mod(x, mix.reshape(nb, l, d), gt1, mods=[(p["norm_g"][0, 1][None], sh2, sc2)])
    ff = _peer(flat(h), sw["peer_wq_t"][0], p["peer_keys"], 0, sw["peer_u_b"][0], sw["peer_vt_b"][0])

    sh1, sc1, gt1b, sh2, sc2, gt2b = [vec(m) for m in mods[1]]
    kv_sh, kv_sc = [vec(m) for m in kv_mod]
    x, h, h_kv = _resid_mod(x, ff.reshape(nb, l, d), gt2,
                            mods=[(p["norm_g"][1, 0][None], sh1, sc1), (p["kv_norm_g"][None], kv_sh, kv_sc)])
    h = flat(h)
    h_kv = flat(h_kv)
    cos_m, sin_m = _rope_tables(pos_rows, MLA_ROPE // 2, LANES // (MLA_ROPE // 2))
    n_ext = sw["w_dkv_ext"].shape[1]
    lat, kr, kr_pad = _linear(
        h_kv, [sw["w_dkv_ext"]], n_cols=n_ext, tm=tm, tn=n_ext, epilogue=_ep_kv,
        extras=[p["mla_kv_norm_g"][None], cos_m, sin_m],
        extra_specs=[pl.BlockSpec((1, KV_LORA), lambda i, j: (0, 0)), tspec, tspec],
        out_shapes=[jax.ShapeDtypeStruct((t, KV_LORA), F32), jax.ShapeDtypeStruct((t, MLA_ROPE), F32),
                    jax.ShapeDtypeStruct((t, LANES), BF16)],
        out_specs=[pl.BlockSpec((tm, KV_LORA), lambda i, j: (i, 0)),
                   pl.BlockSpec((tm, MLA_ROPE), lambda i, j: (i, 0)),
                   pl.BlockSpec((tm, LANES), lambda i, j: (i, 0))])
    q_lora = p["mla_w_dq"].shape[-1]
    cq = _linear(
        h, [p["mla_w_dq"]], n_cols=q_lora, tm=tm, tn=q_lora, w_lead=0, epilogue=_ep_rms,
        extras=[p["mla_q_norm_g"]], extra_specs=[pl.BlockSpec((1, q_lora), lambda i, j: (0, 0))],
        out_shapes=[jax.ShapeDtypeStruct((t, q_lora), BF16)],
        out_specs=[pl.BlockSpec((tm, q_lora), lambda i, j: (i, 0))])[0]
    hl = MLA_HEADS * LANES
    qn = heads_proj(cq, [sw["w_uq_nope"]], hl)
    qr = heads_proj(cq, [sw["w_uq_rope"], sw["w_uq_rot"]], hl, epilogue=_ep_rot_heads,
                    extras=[cos_m, sin_m], extra_specs=[tspec, tspec])
    if cache_lat is None:
        kn = heads_proj(lat, [sw["w_uk_flat"]], hl)
        vv = heads_proj(lat, [sw["w_uv_flat"]], hl)
        o = _attn_prompt(qn, qr, kn, vv, kr_pad, nb, l)
    else:
        ckr = _pad_last(cache_kr, LANES).astype(BF16)
        o = _attn_cached(qn, qr, cache_lat, ckr, lat, kr_pad, sw["w_uk_h"], sw["w_uv_h"], nb, l)
    mix = plain_proj(o, p["mla_w_o"], d, lead=0, dtype=F32)
    x, h = _resid_mod(x, mix.reshape(nb, l, d), gt1b, mods=[(p["norm_g"][1, 1][None], sh2, sc2)])
    ff = _peer(flat(h), sw["peer_wq_t"][1], p["peer_keys"], 1, sw["peer_u_b"][1], sw["peer_vt_b"][1])
    (y,) = _resid_mod(x, ff.reshape(nb, l, d), gt2b, final_g=p["final_g"][None])
    return y, s_new[None], lat.reshape(nb, l, KV_LORA), kr.reshape(nb, l, MLA_ROPE)


def kernel(x_prompt, x_sample, c_prompt, c_sample, state_retention, cache_mla_latent, cache_mla_krope,
           ada_w, ada_b, norm_g, ret_w_in, ret_gn_g, ret_w_out,
           kv_ada_w, kv_ada_b, kv_norm_g, mla_w_dkv, mla_kv_norm_g, mla_w_uk, mla_w_uv,
           mla_w_dq, mla_q_norm_g, mla_w_uq, mla_w_o,
           peer_w_q, peer_keys, peer_u, peer_v, final_g):
    p = dict(norm_g=norm_g, ret_w_in=ret_w_in, ret_gn_g=ret_gn_g[0][None], ret_w_out=ret_w_out,
             kv_norm_g=kv_norm_g, mla_w_dkv=mla_w_dkv, mla_kv_norm_g=mla_kv_norm_g,
             mla_w_uk=mla_w_uk, mla_w_uv=mla_w_uv, mla_w_dq=mla_w_dq, mla_q_norm_g=mla_q_norm_g,
             mla_w_uq=mla_w_uq, mla_w_o=mla_w_o, peer_w_q=peer_w_q, peer_keys=peer_keys,
             peer_u=peer_u, peer_v=peer_v, final_g=final_g)
    sw = _shared_weights(p)
    nbp, lp, d = x_prompt.shape
    nbs, ls, _ = x_sample.shape
    past = cache_mla_latent.shape[1]

    n_c = nbp + nbs
    c_all = jnp.pad(jnp.concatenate([c_prompt, c_sample], axis=0), ((0, (-n_c) % 8), (0, 0)))
    layer_mods = [_adaln(c_all, ada_w, ada_b, layer, 6 * d) for layer in range(2)]
    kv_mods = _adaln(c_all, kv_ada_w, kv_ada_b, 0, 2 * d)

    def stream_mods(lo, hi):
        return ([jnp.split(m[lo:hi], 6, axis=-1) for m in layer_mods], jnp.split(kv_mods[lo:hi], 2, axis=-1))

    mods_p, kv_p = stream_mods(0, nbp)
    mods_s, kv_s = stream_mods(nbp, n_c)
    pos_p = jnp.arange(lp, dtype=jnp.int32)
    pos_s = past + jnp.arange(ls, dtype=jnp.int32)
    y_p, ret_p, lat_p, kr_p = _trunk(x_prompt, pos_p, mods_p, kv_p, p, sw, None, None, None)
    y_s, ret_s, lat_s, kr_s = _trunk(x_sample, pos_s, mods_s, kv_s, p, sw, state_retention,
                                     cache_mla_latent, cache_mla_krope)
    return (y_p, y_s, ret_p, ret_s, lat_p, kr_p, lat_s, kr_s)
```

```python
import functools
import math

import jax
import jax.numpy as jnp
from jax import lax
from jax.experimental import pallas as pl
from jax.experimental.pallas import tpu as pltpu

F32 = jnp.float32
BF16 = jnp.bfloat16

EPS = 1e-6
ROPE_THETA = 10000.0
NEG_INF = -1e30
CHUNK = 64

RET_HEADS = 8
RET_DK = 256
RET_DV = 512

MLA_HEADS = 16
MLA_NOPE = 128
MLA_ROPE = 64
MLA_VDIM = 128
KV_LORA = 512
MLA_SCALE = (MLA_NOPE + MLA_ROPE) ** -0.5

PEER_HEADS = 8
N_KEYS = 128
PEER_TOPK = 16

LANES = 128
VMEM_LIMIT = 52 * 1024 * 1024
PEER_MIX_VMEM_LIMIT = 58 * 1024 * 1024
SELECT_HEADS_PER_STEP = 2


def _params(sem, vmem=VMEM_LIMIT, flags=None):
    return pltpu.CompilerParams(dimension_semantics=sem, vmem_limit_bytes=vmem, flags=flags)


def _bdot(a, b):
    return jnp.dot(a.astype(BF16), b.astype(BF16), preferred_element_type=F32)


def _bdot_nt(a, b):
    return lax.dot_general(a.astype(BF16), b.astype(BF16), (((1,), (1,)), ((), ())),
                           preferred_element_type=F32)


def _bdot_tn(a, b):
    return lax.dot_general(a.astype(BF16), b.astype(BF16), (((0,), (0,)), ((), ())),
                           preferred_element_type=F32)


def _silu(x):
    return x * (1.0 / (1.0 + jnp.exp(-x)))


def _gelu(x):
    return 0.5 * x * (1.0 + lax.erf(x * (0.5 ** 0.5)))


def _linear_kernel(*refs, n_w, n_extra, prologue, epilogue):
    x_ref = refs[0]
    w_refs = refs[1:1 + n_w]
    extra_refs = refs[1 + n_w:1 + n_w + n_extra]
    out_refs = refs[1 + n_w + n_extra:]
    xv = x_ref[...]
    if prologue is not None:
        xv = prologue(xv)
    xb = xv.astype(BF16)
    accs = [jnp.dot(xb, w[...].astype(BF16), preferred_element_type=F32) for w in w_refs]
    epilogue(accs, extra_refs, out_refs)


def _linear(x, ws, *, n_cols, tm, tn, epilogue, out_shapes, out_specs, w_lead=None, col_block0=0,
            extras=(), extra_specs=(), prologue=None, name=None):
    m, k = x.shape
    if name is None:
        name = "linear" + getattr(epilogue, "func", epilogue).__name__
    assert m % tm == 0 and n_cols % tn == 0
    if w_lead is None:
        w_spec = pl.BlockSpec((k, tn), lambda i, j: (0, j + col_block0))
    else:
        w_spec = pl.BlockSpec((None, k, tn), lambda i, j: (w_lead, 0, j + col_block0))
    kern = functools.partial(_linear_kernel, n_w=len(ws), n_extra=len(extras), prologue=prologue,
                             epilogue=epilogue)
    return pl.pallas_call(
        kern,
        grid=(m // tm, n_cols // tn),
        in_specs=[pl.BlockSpec((tm, k), lambda i, j: (i, 0))] + [w_spec] * len(ws) + list(extra_specs),
        out_specs=out_specs,
        out_shape=out_shapes,
        compiler_params=_params(("parallel", "arbitrary")),
        name=name,
    )(x, *ws, *extras)


def _ep_plain(accs, extras, outs):
    outs[0][...] = accs[0].astype(outs[0].dtype)


def _ep_heads(accs, extras, outs):
    acc = accs[0]
    for jj in range(acc.shape[1] // LANES):
        outs[0][jj] = acc[:, jj * LANES:(jj + 1) * LANES].astype(outs[0].dtype)


def _ep_bias(accs, extras, outs):
    outs[0][...] = accs[0] + extras[0][...]


def _ep_rope_half128(accs, extras, outs, *, scale):
    acc = accs[0]
    cos = extras[0][...]
    sin = extras[1][...]
    for g in range(acc.shape[1] // 256):
        x1 = acc[:, g * 256:g * 256 + 128]
        x2 = acc[:, g * 256 + 128:(g + 1) * 256]
        outs[0][:, g * 256:g * 256 + 128] = ((x1 * cos - x2 * sin) * scale).astype(outs[0].dtype)
        outs[0][:, g * 256 + 128:(g + 1) * 256] = ((x1 * sin + x2 * cos) * scale).astype(outs[0].dtype)


def _ep_rot_heads(accs, extras, outs):
    cos = extras[0][...]
    sin = extras[1][...]
    a, b = accs
    for jj in range(a.shape[1] // LANES):
        sl = slice(jj * LANES, (jj + 1) * LANES)
        outs[0][jj] = (a[:, sl] * cos + b[:, sl] * sin).astype(outs[0].dtype)


def _ep_rms(accs, extras, outs):
    acc = accs[0]
    g = extras[0][...]
    y = acc * lax.rsqrt(jnp.mean(acc * acc, axis=-1, keepdims=True) + EPS) * g
    outs[0][...] = y.astype(outs[0].dtype)


def _ep_kv(accs, extras, outs):
    acc = accs[0]
    g = extras[0][...]
    cos = extras[1][...]
    sin = extras[2][...]
    c = acc[:, :KV_LORA]
    outs[0][...] = c * lax.rsqrt(jnp.mean(c * c, axis=-1, keepdims=True) + EPS) * g
    kr = acc[:, KV_LORA:KV_LORA + LANES] * cos + acc[:, KV_LORA + LANES:KV_LORA + 2 * LANES] * sin
    outs[1][...] = kr[:, :MLA_ROPE]
    outs[2][...] = kr.astype(BF16)


def _row_tile(m, k):
    return min(m, 1024)


def _col_tile(k, n, n_weights=1):
    for tn in (1024, 512):
        if n % tn == 0 and 2 * n_weights * k * tn * 4 <= 16 * 1024 * 1024:
            return tn
    return min(n, 512)


def _table_spec(tm, table_rows):
    nblk = table_rows // tm
    return pl.BlockSpec((tm, LANES), lambda i, j: (i % nblk, 0))


def _ew_kernel(*refs, has_resid, n_mod, final):
    idx = 0
    x = refs[idx][...]; idx += 1
    if has_resid:
        mix = refs[idx][...]; idx += 1
        gate = refs[idx][...]; idx += 1
        x = x + (1.0 + gate) * mix
    mods = []
    for _ in range(n_mod):
        mods.append((refs[idx][...], refs[idx + 1][...], refs[idx + 2][...]))
        idx += 3
    if final:
        fg = refs[idx][...]; idx += 1
    outs = refs[idx:]
    o = 0
    xn = x * lax.rsqrt(jnp.mean(x * x, axis=-1, keepdims=True) + EPS)
    if final:
        outs[o][...] = xn * fg
        return
    if has_resid:
        outs[o][...] = x
        o += 1
    for g, sh, sc in mods:
        outs[o][...] = ((xn * g) * (1.0 + sc) + sh).astype(outs[o].dtype)
        o += 1


def _resid_mod(x, mix=None, gate=None, mods=(), final_g=None):
    nb, l, d = x.shape
    tl = min(l, 256)
    tok = pl.BlockSpec((None, tl, d), lambda b, i: (b, i, 0))
    per_b = pl.BlockSpec((None, 1, d), lambda b, i: (b, 0, 0))
    gain = pl.BlockSpec((1, d), lambda b, i: (0, 0))
    args, specs = [x], [tok]
    if mix is not None:
        args += [mix, gate]
        specs += [tok, per_b]
    for g, sh, sc in mods:
        args += [g, sh, sc]
        specs += [gain, per_b, per_b]
    out_shapes, out_specs = [], []
    if final_g is not None:
        args.append(final_g)
        specs.append(gain)
        out_shapes.append(jax.ShapeDtypeStruct((nb, l, d), F32))
        out_specs.append(tok)
    else:
        if mix is not None:
            out_shapes.append(jax.ShapeDtypeStruct((nb, l, d), F32))
            out_specs.append(tok)
        for _ in mods:
            out_shapes.append(jax.ShapeDtypeStruct((nb, l, d), BF16))
            out_specs.append(tok)
    kern = functools.partial(_ew_kernel, has_resid=mix is not None, n_mod=len(mods),
                             final=final_g is not None)
    return pl.pallas_call(
        kern, grid=(nb, l // tl), in_specs=specs, out_specs=out_specs, out_shape=out_shapes,
        compiler_params=_params(("parallel", "parallel")),
        name="resid_mod",
    )(*args)


def _ret_log_decay():
    return jnp.log1p(-jnp.exp2(-5.0 - jnp.arange(RET_HEADS, dtype=F32)))


def _retention_tables(c):
    log_g = _ret_log_decay()[:, None, None]
    n = jnp.arange(c, dtype=F32)
    dist = n[:, None] - n[None, :]
    same = (jnp.arange(c)[:, None] // CHUNK) == (jnp.arange(c)[None, :] // CHUNK)
    earlier = (jnp.arange(c)[None, :] // CHUNK) < (jnp.arange(c)[:, None] // CHUNK)
    mask = jnp.where(same[None], jnp.exp(jnp.abs(dist)[None] * log_g),
                     jnp.where(earlier[None], jnp.exp(dist[None] * log_g), 0.0))
    q_decay = jnp.exp((n[None, :, None] + 1.0) * log_g)
    k_decay = jnp.exp((c - 1.0 - n)[None, :, None] * log_g)
    blk_decay = jnp.exp(c * log_g)
    return mask, q_decay, k_decay, blk_decay


def _retention_kernel(*refs, has_s0):
    if has_s0:
        (q_ref, k_ref, v_ref, g_ref, mask_ref, qd_ref, kd_ref, bd_ref, gn_ref, s0_ref,
         y_ref, s_out_ref, s_ref) = refs
    else:
        (q_ref, k_ref, v_ref, g_ref, mask_ref, qd_ref, kd_ref, bd_ref, gn_ref,
         y_ref, s_out_ref, s_ref) = refs
    c = pl.program_id(2)

    @pl.when(c == 0)
    def _():
        if has_s0:
            s_ref[...] = s0_ref[...]
        else:
            s_ref[...] = jnp.zeros_like(s_ref)

    q = q_ref[...]
    k = k_ref[...]
    v = v_ref[...]
    s_prev = s_ref[...]
    scores = _bdot_nt(q, k) * mask_ref[...]
    y = _bdot(scores, v) + _bdot(q, s_prev) * qd_ref[...]
    k_scaled = k.astype(F32) * kd_ref[...]
    s_new = bd_ref[...] * s_prev + _bdot_tn(k_scaled, v)
    s_ref[...] = s_new

    mu = jnp.mean(y, axis=-1, keepdims=True)
    yc = y - mu
    var = jnp.mean(yc * yc, axis=-1, keepdims=True)
    yn = yc * lax.rsqrt(var + EPS) * gn_ref[...]
    y_ref[...] = (_silu(g_ref[...].astype(F32)) * yn).astype(y_ref.dtype)

    @pl.when(c == pl.num_programs(2) - 1)
    def _():
        s_out_ref[...] = s_new


def _retention(q, k, v, g, gn_g, s0, nb, l):
    cb = min(l, 256)
    nc = l // cb
    mask, qd, kd, bd = _retention_tables(cb)
    row = lambda b, h, c: (b * nc + c, h)
    per_h3 = lambda b, h, c: (h, 0, 0)
    in_specs = [
        pl.BlockSpec((cb, RET_DK), row), pl.BlockSpec((cb, RET_DK), row),
        pl.BlockSpec((cb, RET_DV), row), pl.BlockSpec((cb, RET_DV), row),
        pl.BlockSpec((None, cb, cb), per_h3), pl.BlockSpec((None, cb, 1), per_h3),
        pl.BlockSpec((None, cb, 1), per_h3), pl.BlockSpec((None, 1, 1), per_h3),
        pl.BlockSpec((1, RET_DV), lambda b, h, c: (0, h)),
    ]
    args = [q, k, v, g, mask, qd, kd, bd, gn_g]
    state_spec = pl.BlockSpec((None, None, RET_DK, RET_DV), lambda b, h, c: (b, h, 0, 0))
    if s0 is not None:
        in_specs.append(state_spec)
        args.append(s0)
    y, s_new = pl.pallas_call(
        functools.partial(_retention_kernel, has_s0=s0 is not None),
        grid=(nb, RET_HEADS, nc),
        in_specs=in_specs,
        out_specs=[pl.BlockSpec((cb, RET_DV), row), state_spec],
        out_shape=[jax.ShapeDtypeStruct((nb * l, RET_HEADS * RET_DV), BF16),
                   jax.ShapeDtypeStruct((nb, RET_HEADS, RET_DK, RET_DV), F32)],
        scratch_shapes=[pltpu.VMEM((RET_DK, RET_DV), F32)],
        compiler_params=_params(("parallel", "parallel", "arbitrary")),
        name="retention",
    )(*args)
    return y, s_new


def _attn_prompt_kernel(qn_ref, qr_ref, kn_ref, v_ref, kr_ref, o_ref, m_ref, acc_ref, *, tq, tk):
    qi = pl.program_id(1)
    ki = pl.program_id(2)
    exp2_scale = MLA_SCALE * math.log2(math.e)

    @pl.when(ki == 0)
    def _():
        m_ref[...] = jnp.full_like(m_ref, NEG_INF)
        acc_ref[...] = jnp.zeros_like(acc_ref)

    def block(masked):
        if masked:
            q_chunk = (qi * tq + lax.broadcasted_iota(jnp.int32, (tq, tk), 0)) // CHUNK
            k_chunk = (ki * tk + lax.broadcasted_iota(jnp.int32, (tq, tk), 1)) // CHUNK
            visible = k_chunk <= q_chunk
        kr = kr_ref[...]
        ones = jnp.ones((tk, LANES), BF16)

        def head(h, carry):
            q = jnp.concatenate([qn_ref[h], qr_ref[h]], axis=1)
            k = jnp.concatenate([kn_ref[h], kr], axis=1)
            s = _bdot_nt(q, k)
            if masked:
                s = jnp.where(visible, s, NEG_INF)
            m_prev = m_ref[h]
            m_new = jnp.maximum(m_prev, jnp.max(s, axis=-1, keepdims=True))
            alpha = jnp.exp2((m_prev - m_new) * exp2_scale)
            p = jnp.exp2((s - jnp.tile(m_new, (1, tk // LANES))) * exp2_scale)
            v_ext = jnp.concatenate([v_ref[h], ones], axis=1)
            acc_ref[h] = jnp.tile(alpha, (1, 2)) * acc_ref[h] + _bdot(p, v_ext)
            m_ref[h] = m_new
            return carry

        lax.fori_loop(0, MLA_HEADS, head, 0, unroll=True)

    @pl.when(ki < qi)
    def _():
        block(False)

    @pl.when(ki == qi)
    def _():
        block(True)
        for h in range(MLA_HEADS):
            acc = acc_ref[h]
            o_ref[:, h * MLA_VDIM:(h + 1) * MLA_VDIM] = (
                acc[:, :MLA_VDIM] / acc[:, MLA_VDIM:]).astype(o_ref.dtype)


def _attn_prompt(qn, qr, kn, v, kr, nb, l):
    tq = tk = min(l, 512)
    nq = l // tq
    q_spec = pl.BlockSpec((MLA_HEADS, tq, LANES), lambda b, qi, ki: (0, b * nq + qi, 0))
    k_spec = pl.BlockSpec((MLA_HEADS, tk, LANES), lambda b, qi, ki: (0, b * nq + jnp.minimum(ki, qi), 0))
    return pl.pallas_call(
        functools.partial(_attn_prompt_kernel, tq=tq, tk=tk),
        grid=(nb, nq, nq),
        in_specs=[q_spec, q_spec, k_spec, k_spec,
                  pl.BlockSpec((tk, LANES), lambda b, qi, ki: (b * nq + jnp.minimum(ki, qi), 0))],
        out_specs=pl.BlockSpec((tq, MLA_HEADS * MLA_VDIM), lambda b, qi, ki: (b * nq + qi, 0)),
        out_shape=jax.ShapeDtypeStruct((nb * l, MLA_HEADS * MLA_VDIM), BF16),
        scratch_shapes=[pltpu.VMEM((MLA_HEADS, tq, LANES), F32),
                        pltpu.VMEM((MLA_HEADS, tq, MLA_VDIM + LANES), F32)],
        compiler_params=_params(("parallel", "parallel", "arbitrary")),
        name="attn_prompt",
    )(qn, qr, kn, v, kr)


def _attn_cached_kernel(qn_ref, qr_ref, clat_ref, ckr_ref, nlat_ref, nkr_ref, wuk_ref, wuv_ref,
                        o_ref, ql_ref, qrs_ref, *, lq):
    for h in range(MLA_HEADS):
        ql_ref[h * lq:(h + 1) * lq, :] = _bdot_nt(qn_ref[h], wuk_ref[h]).astype(BF16)
        qrs_ref[h * lq:(h + 1) * lq, :] = qr_ref[h]
    ql = ql_ref[...]
    qr = qrs_ref[...]
    clat = clat_ref[...].astype(BF16)
    nlat = nlat_ref[...].astype(BF16)
    s_c = (_bdot_nt(ql, clat) + _bdot_nt(qr, ckr_ref[...])) * MLA_SCALE
    s_n = (_bdot_nt(ql, nlat) + _bdot_nt(qr, nkr_ref[...])) * MLA_SCALE
    m = jnp.maximum(jnp.max(s_c, axis=-1, keepdims=True), jnp.max(s_n, axis=-1, keepdims=True))
    p_c = jnp.exp(s_c - m)
    p_n = jnp.exp(s_n - m)
    denom = jnp.sum(p_c, axis=-1, keepdims=True) + jnp.sum(p_n, axis=-1, keepdims=True)
    o_lat = (_bdot(p_c, clat) + _bdot(p_n, nlat)) / denom
    for h in range(MLA_HEADS):
        o_ref[:, h * MLA_VDIM:(h + 1) * MLA_VDIM] = _bdot(
            o_lat[h * lq:(h + 1) * lq, :], wuv_ref[h]).astype(o_ref.dtype)


def _attn_cached(qn, qr, cache_lat, cache_kr_pad, new_lat, new_kr_pad, wuk_h, wuv_h, nb, lq):
    past = cache_lat.shape[1]
    whole = lambda b: (0, 0, 0)
    return pl.pallas_call(
        functools.partial(_attn_cached_kernel, lq=lq),
        grid=(nb,),
        in_specs=[
            pl.BlockSpec((MLA_HEADS, lq, LANES), lambda b: (0, b, 0)),
            pl.BlockSpec((MLA_HEADS, lq, LANES), lambda b: (0, b, 0)),
            pl.BlockSpec((None, past, KV_LORA), lambda b: (b, 0, 0)),
            pl.BlockSpec((None, past, LANES), lambda b: (b, 0, 0)),
            pl.BlockSpec((lq, KV_LORA), lambda b: (b, 0)),
            pl.BlockSpec((lq, LANES), lambda b: (b, 0)),
            pl.BlockSpec((MLA_HEADS, KV_LORA, MLA_NOPE), whole),
            pl.BlockSpec((MLA_HEADS, KV_LORA, MLA_VDIM), whole),
        ],
        out_specs=pl.BlockSpec((lq, MLA_HEADS * MLA_VDIM), lambda b: (b, 0)),
        out_shape=jax.ShapeDtypeStruct((nb * lq, MLA_HEADS * MLA_VDIM), BF16),
        scratch_shapes=[pltpu.VMEM((MLA_HEADS * lq, KV_LORA), BF16),
                        pltpu.VMEM((MLA_HEADS * lq, LANES), BF16)],
        compiler_params=_params(("parallel",)),
        name="attn_cached",
    )(qn, qr, cache_lat, cache_kr_pad, new_lat, new_kr_pad, wuk_h, wuv_h)


def _top_values(s, top_ref, want_rank):
    rank = jnp.full(s.shape, float(PEER_TOPK), F32) if want_rank else None
    for r in range(PEER_TOPK):
        m = jnp.max(s, axis=0, keepdims=True)
        top_ref[r:r + 1, :] = m
        hit = s == m
        if want_rank:
            rank = jnp.where(hit, float(r), rank)
        s = jnp.where(hit, -jnp.inf, s)
    return rank


_CAND_LIMITS = tuple(PEER_TOPK // (b + 1) for b in range(1, 8))
_HALF_TOPK = PEER_TOPK // 2


def _peer_select_kernel(h_ref, wq_ref, keys_ref, r1_ref, e1_ref, c0_ref, e0_ref, s_ref, a0_ref, a1_ref,
                        *, heads_per_step):
    hx = h_ref[...]
    for hh in range(heads_per_step):
        q_t = _bdot_nt(wq_ref[hh * 2 * N_KEYS:(hh + 1) * 2 * N_KEYS, :], hx)
        s_ref[2 * hh] = _bdot(keys_ref[hh, 0], q_t[:N_KEYS])
        s_ref[2 * hh + 1] = _bdot(keys_ref[hh, 1], q_t[N_KEYS:])
    for hh in range(heads_per_step):
        _peer_select_head(s_ref[2 * hh], s_ref[2 * hh + 1], r1_ref.at[hh], e1_ref.at[hh], c0_ref.at[hh],
                          e0_ref.at[hh], a0_ref.at[hh], a1_ref.at[hh])


def _peer_select_head(s0, s1, r1_ref, e1_ref, c0_ref, e0_ref, a0_ref, a1_ref):
    _top_values(s0, a0_ref, False)
    rank1 = _top_values(s1, a1_ref, True)
    a0 = a0_ref[...]
    a1 = a1_ref[...]
    a0_lo = a0[:_HALF_TOPK, :]
    row = lax.broadcasted_iota(jnp.int32, a0_lo.shape, 0)
    pieces = [a1[0:1, :] + a0]
    for b, limit in enumerate(_CAND_LIMITS, start=1):
        pieces.append(jnp.where(row < limit, a1[b:b + 1, :] + a0_lo, -jnp.inf))
    pieces.append(a1[_HALF_TOPK:, :] + a0[0:1, :])
    cand = jnp.concatenate(pieces, axis=0)
    c = cand
    for r in range(PEER_TOPK):
        tau = jnp.max(c, axis=0, keepdims=True)
        if r + 1 < PEER_TOPK:
            c = jnp.where(c == tau, -jnp.inf, c)
    sel = cand >= tau
    cmax = a0[0:1, :] + a1[0:1, :]
    z = jnp.sum(jnp.where(sel, jnp.exp(cand - cmax), 0.0), axis=0, keepdims=True)
    self32 = sel.astype(F32)
    n_hi = self32[_HALF_TOPK:PEER_TOPK, :]
    n_lo = self32[:_HALF_TOPK, :]
    for b in range(1, 8):
        lo = PEER_TOPK + (b - 1) * _HALF_TOPK
        n_lo = n_lo + self32[lo:lo + _HALF_TOPK, :]
    tail = jnp.sum(self32[PEER_TOPK + 7 * _HALF_TOPK:, :], axis=0, keepdims=True)
    n_lo = n_lo + jnp.where(row == 0, tail, 0.0)
    count0 = jnp.zeros(s0.shape, F32)
    for a in range(PEER_TOPK):
        n_a = n_lo[a:a + 1, :] if a < _HALF_TOPK else n_hi[a - _HALF_TOPK:a - _HALF_TOPK + 1, :]
        count0 = jnp.where(s0 == a0[a:a + 1, :], n_a, count0)
    r1_ref[...] = rank1.astype(r1_ref.dtype)
    c0_ref[...] = count0
    e0_ref[...] = jnp.exp(s0 - a0[0:1, :]) / z
    e1_ref[...] = jnp.exp(s1 - a1[0:1, :]).astype(e1_ref.dtype)


def _peer_select(hx, wq_t, keys, layer, tt):
    t, d = hx.shape
    hps = SELECT_HEADS_PER_STEP
    o_spec = pl.BlockSpec((hps, N_KEYS, tt), lambda i, g: (g, 0, i))
    return pl.pallas_call(
        functools.partial(_peer_select_kernel, heads_per_step=hps),
        grid=(t // tt, PEER_HEADS // hps),
        in_specs=[pl.BlockSpec((tt, d), lambda i, g: (i, 0)),
                  pl.BlockSpec((hps * 2 * N_KEYS, d), lambda i, g: (g, 0)),
                  pl.BlockSpec((None, hps, 2, N_KEYS, N_KEYS), lambda i, g: (layer, g, 0, 0, 0))],
        out_specs=[o_spec] * 4,
        out_shape=[jax.ShapeDtypeStruct((PEER_HEADS, N_KEYS, t), dt) for dt in (BF16, BF16, F32, F32)],
        scratch_shapes=[pltpu.VMEM((2 * hps, N_KEYS, tt), F32),
                        pltpu.VMEM((hps, PEER_TOPK, tt), F32), pltpu.VMEM((hps, PEER_TOPK, tt), F32)],
        compiler_params=_params(("parallel", "arbitrary")),
        name="peer_select",
    )(hx, wq_t, keys)


def _sublane_bcast_bf16(row, rows):
    tile_rows = 16
    packed = jnp.broadcast_to(row, (tile_rows, row.shape[1])).astype(BF16)
    return jnp.tile(packed, (rows // tile_rows, 1))


def _peer_mix_kernel(x_ref, u_ref, vt_ref, r1_ref, e1_ref, c0_ref, e0_ref, o_ref,
                     acc_ref, a_ref, g_ref, p_ref, *, n_i):
    j = pl.program_id(1)

    @pl.when(j == 0)
    def _():
        acc_ref[...] = jnp.zeros_like(acc_ref)

    x = x_ref[...]
    rows_per_chunk = 2 * N_KEYS
    for c in range(n_i // 2):
        lo = c * rows_per_chunk
        a_ref[c] = _bdot_nt(u_ref[lo:lo + rows_per_chunk, :], x)
    zero = jnp.zeros((), BF16)
    for ii in range(n_i):
        w = None
        for h in range(PEER_HEADS):
            count = _sublane_bcast_bf16(c0_ref[h, ii:ii + 1, :], N_KEYS)
            e0 = _sublane_bcast_bf16(e0_ref[h, ii:ii + 1, :], N_KEYS)
            term = jnp.where(r1_ref[h] < count, e1_ref[h] * e0, zero)
            w = term if w is None else w + term
        g_ref[ii] = w
    for c in range(n_i // 2):
        lo = c * rows_per_chunk
        gates = jnp.concatenate([g_ref[2 * c], g_ref[2 * c + 1]], axis=0)
        p_ref[lo:lo + rows_per_chunk, :] = gates * _gelu(a_ref[c]).astype(BF16)
    acc_ref[...] += jnp.dot(vt_ref[...], p_ref[...], preferred_element_type=F32)

    @pl.when(j == pl.num_programs(1) - 1)
    def _():
        o_ref[...] = acc_ref[...].T


def _peer_mix(hx, u_b, vt_b, r1, e1, c0, e0, tt, et):
    t, d = hx.shape
    e = u_b.shape[0]
    n_i = et // N_KEYS
    tok_all = pl.BlockSpec((PEER_HEADS, N_KEYS, tt), lambda i, j: (0, 0, i))
    tok_i = pl.BlockSpec((PEER_HEADS, n_i, tt), lambda i, j: (0, j, i))
    return pl.pallas_call(
        functools.partial(_peer_mix_kernel, n_i=n_i),
        grid=(t // tt, e // et),
        in_specs=[pl.BlockSpec((tt, d), lambda i, j: (i, 0)),
                  pl.BlockSpec((et, d), lambda i, j: (j, 0)),
                  pl.BlockSpec((d, et), lambda i, j: (0, j)),
                  tok_all, tok_all, tok_i, tok_i],
        out_specs=pl.BlockSpec((tt, d), lambda i, j: (i, 0)),
        out_shape=jax.ShapeDtypeStruct((t, d), F32),
        scratch_shapes=[pltpu.VMEM((d, tt), F32),
                        pltpu.VMEM((n_i // 2, 2 * N_KEYS, tt), F32),
                        pltpu.VMEM((n_i, N_KEYS, tt), BF16),
                        pltpu.VMEM((et, tt), BF16)],
        compiler_params=_params(("parallel", "arbitrary"), vmem=PEER_MIX_VMEM_LIMIT),
        name="peer_mix",
    )(hx, u_b, vt_b, r1, e1, c0, e0)


def _peer(hx, wq_t, keys, layer, u_b, vt_b):
    t = hx.shape[0]
    tt = min(t, 512)
    r1, e1, c0, e0 = _peer_select(hx, wq_t, keys, layer, tt)
    return _peer_mix(hx, u_b, vt_b, r1, e1, c0, e0, tt, 1024)


def _rope_tables(pos, half, reps):
    inv = ROPE_THETA ** (-jnp.arange(half, dtype=F32) / half)
    ang = pos.astype(F32)[:, None] * inv[None, :]
    return jnp.tile(jnp.cos(ang), (1, reps)), jnp.tile(jnp.sin(ang), (1, reps))


def _rotate_half_cols(w):
    half = w.shape[-1] // 2
    return jnp.concatenate([-w[..., half:], w[..., :half]], axis=-1)


def _pad_last(w, width):
    return jnp.pad(w, [(0, 0)] * (w.ndim - 1) + [(0, width - w.shape[-1])])


def _shared_weights(p):
    out = {}
    w_rope = p["mla_w_dkv"][:, KV_LORA:]
    out["w_dkv_ext"] = jnp.concatenate(
        [p["mla_w_dkv"][:, :KV_LORA], _pad_last(w_rope, LANES), _pad_last(_rotate_half_cols(w_rope), LANES)],
        axis=1)
    w_uq = p["mla_w_uq"][0].reshape(-1, MLA_HEADS, MLA_NOPE + MLA_ROPE)
    q_lora = w_uq.shape[0]
    out["w_uq_nope"] = w_uq[:, :, :MLA_NOPE].reshape(q_lora, MLA_HEADS * MLA_NOPE)
    w_qr = w_uq[:, :, MLA_NOPE:]
    out["w_uq_rope"] = _pad_last(w_qr, LANES).reshape(q_lora, MLA_HEADS * LANES)
    out["w_uq_rot"] = _pad_last(_rotate_half_cols(w_qr), LANES).reshape(q_lora, MLA_HEADS * LANES)
    out["w_uk_flat"] = p["mla_w_uk"].reshape(KV_LORA, MLA_HEADS * MLA_NOPE)
    out["w_uv_flat"] = p["mla_w_uv"].reshape(KV_LORA, MLA_HEADS * MLA_VDIM)
    out["w_uk_h"] = jnp.transpose(p["mla_w_uk"], (1, 0, 2))
    out["w_uv_h"] = jnp.transpose(p["mla_w_uv"], (1, 0, 2))
    out["peer_wq_t"] = [p["peer_w_q"][l].T for l in range(2)]
    out["peer_u_b"] = [p["peer_u"][l].astype(BF16) for l in range(2)]
    out["peer_vt_b"] = [p["peer_v"][l].astype(BF16).T for l in range(2)]
    return out


def _adaln(c_all, w, b, lead, n_out):
    m, d = c_all.shape
    tn = 512
    if w.ndim == 2:
        w = w[None]
        b = b[None]
    b3 = b.reshape(b.shape[0], 1, n_out)
    return _linear(
        c_all, [w], n_cols=n_out, tm=m, tn=tn, epilogue=_ep_bias, prologue=_silu, w_lead=lead,
        extras=[b3], extra_specs=[pl.BlockSpec((None, 1, tn), lambda i, j: (lead, 0, j))],
        out_shapes=[jax.ShapeDtypeStruct((m, n_out), F32)],
        out_specs=[pl.BlockSpec((m, tn), lambda i, j: (0, j))])[0]


def _trunk(x, pos, mods, kv_mod, p, sw, ret_state, cache_lat, cache_kr):
    nb, l, d = x.shape
    t = nb * l
    pos_rows = jnp.tile(pos, nb) if l < 256 else pos
    table_rows = pos_rows.shape[0]

    def vec(a):
        return a.reshape(nb, 1, d)

    def flat(a):
        return a.reshape(t, a.shape[-1])

    sh1, sc1, gt1, sh2, sc2, gt2 = [vec(m) for m in mods[0]]
    (h,) = _resid_mod(x, mods=[(p["norm_g"][0, 0][None], sh1, sc1)])
    h = flat(h)
    tm = _row_tile(t, d)
    cos_r, sin_r = _rope_tables(pos_rows, RET_DK // 2, 1)
    tspec = _table_spec(tm, table_rows)
    w_in = p["ret_w_in"]
    qk_cols = RET_HEADS * RET_DK
    v_cols = RET_HEADS * RET_DV
    tn = _col_tile(d, qk_cols)

    def rope_proj(col0, scale):
        return _linear(
            h, [w_in], n_cols=qk_cols, tm=tm, tn=tn, w_lead=0, col_block0=col0 // tn,
            epilogue=functools.partial(_ep_rope_half128, scale=scale),
            extras=[cos_r, sin_r], extra_specs=[tspec, tspec],
            out_shapes=[jax.ShapeDtypeStruct((t, qk_cols), BF16)],
            out_specs=[pl.BlockSpec((tm, tn), lambda i, j: (i, j))])[0]

    def plain_proj(x_in, w, n_cols, col0=0, lead=None, dtype=BF16, tm_=None):
        tm_ = tm_ or _row_tile(x_in.shape[0], x_in.shape[1])
        tn_ = _col_tile(x_in.shape[1], n_cols)
        return _linear(
            x_in, [w], n_cols=n_cols, tm=tm_, tn=tn_, w_lead=lead, col_block0=col0 // tn_,
            epilogue=_ep_plain,
            out_shapes=[jax.ShapeDtypeStruct((x_in.shape[0], n_cols), dtype)],
            out_specs=[pl.BlockSpec((tm_, tn_), lambda i, j: (i, j))])[0]

    def heads_proj(x_in, ws, n_cols, epilogue=_ep_heads, extras=(), extra_specs=()):
        tm_ = _row_tile(x_in.shape[0], x_in.shape[1])
        tn_ = _col_tile(x_in.shape[1], n_cols, len(ws))
        return _linear(
            x_in, ws, n_cols=n_cols, tm=tm_, tn=tn_, epilogue=epilogue, extras=extras, extra_specs=extra_specs,
            out_shapes=[jax.ShapeDtypeStruct((n_cols // LANES, x_in.shape[0], LANES), BF16)],
            out_specs=[pl.BlockSpec((tn_ // LANES, tm_, LANES), lambda i, j: (j, i, 0))])[0]

    q_r = rope_proj(0, 1.0)
    k_r = rope_proj(qk_cols, RET_DK ** -0.5)
    v_r = plain_proj(h, w_in, v_cols, col0=2 * qk_cols, lead=0)
    g_r = plain_proj(h, w_in, v_cols, col0=2 * qk_cols + v_cols, lead=0)
    s0 = None if ret_state is None else ret_state[0]
    y_r, s_new = _retention(q_r, k_r, v_r, g_r, p["ret_gn_g"], s0, nb, l)
    mix = plain_proj(y_r, p["ret_w_out"], d, lead=0, dtype=F32)
    x, h = _resid_mod(x, mix.reshape(nb, l, d), gt1, mods=[(p["norm_g"][0, 1][None], sh2, sc2)])
    ff = _peer(flat(h), sw["peer_wq_t"][0], p["peer_keys"], 0, sw["peer_u_b"][0], sw["peer_vt_b"][0])

    sh1, sc1, gt1b, sh2, sc2, gt2b = [vec(m) for m in mods[1]]
    kv_sh, kv_sc = [vec(m) for m in kv_mod]
    x, h, h_kv = _resid_mod(x, ff.reshape(nb, l, d), gt2,
                            mods=[(p["norm_g"][1, 0][None], sh1, sc1), (p["kv_norm_g"][None], kv_sh, kv_sc)])
    h = flat(h)
    h_kv = flat(h_kv)
    cos_m, sin_m = _rope_tables(pos_rows, MLA_ROPE // 2, LANES // (MLA_ROPE // 2))
    n_ext = sw["w_dkv_ext"].shape[1]
    lat, kr, kr_pad = _linear(
        h_kv, [sw["w_dkv_ext"]], n_cols=n_ext, tm=tm, tn=n_ext, epilogue=_ep_kv,
        extras=[p["mla_kv_norm_g"][None], cos_m, sin_m],
        extra_specs=[pl.BlockSpec((1, KV_LORA), lambda i, j: (0, 0)), tspec, tspec],
        out_shapes=[jax.ShapeDtypeStruct((t, KV_LORA), F32), jax.ShapeDtypeStruct((t, MLA_ROPE), F32),
                    jax.ShapeDtypeStruct((t, LANES), BF16)],
        out_specs=[pl.BlockSpec((tm, KV_LORA), lambda i, j: (i, 0)),
                   pl.BlockSpec((tm, MLA_ROPE), lambda i, j: (i, 0)),
                   pl.BlockSpec((tm, LANES), lambda i, j: (i, 0))])
    q_lora = p["mla_w_dq"].shape[-1]
    cq = _linear(
        h, [p["mla_w_dq"]], n_cols=q_lora, tm=tm, tn=q_lora, w_lead=0, epilogue=_ep_rms,
        extras=[p["mla_q_norm_g"]], extra_specs=[pl.BlockSpec((1, q_lora), lambda i, j: (0, 0))],
        out_shapes=[jax.ShapeDtypeStruct((t, q_lora), BF16)],
        out_specs=[pl.BlockSpec((tm, q_lora), lambda i, j: (i, 0))])[0]
    hl = MLA_HEADS * LANES
    qn = heads_proj(cq, [sw["w_uq_nope"]], hl)
    qr = heads_proj(cq, [sw["w_uq_rope"], sw["w_uq_rot"]], hl, epilogue=_ep_rot_heads,
                    extras=[cos_m, sin_m], extra_specs=[tspec, tspec])
    if cache_lat is None:
        kn = heads_proj(lat, [sw["w_uk_flat"]], hl)
        vv = heads_proj(lat, [sw["w_uv_flat"]], hl)
        o = _attn_prompt(qn, qr, kn, vv, kr_pad, nb, l)
    else:
        ckr = _pad_last(cache_kr, LANES).astype(BF16)
        o = _attn_cached(qn, qr, cache_lat, ckr, lat, kr_pad, sw["w_uk_h"], sw["w_uv_h"], nb, l)
    mix = plain_proj(o, p["mla_w_o"], d, lead=0, dtype=F32)
    x, h = _resid_mod(x, mix.reshape(nb, l, d), gt1b, mods=[(p["norm_g"][1, 1][None], sh2, sc2)])
    ff = _peer(flat(h), sw["peer_wq_t"][1], p["peer_keys"], 1, sw["peer_u_b"][1], sw["peer_vt_b"][1])
    (y,) = _resid_mod(x, ff.reshape(nb, l, d), gt2b, final_g=p["final_g"][None])
    return y, s_new[None], lat.reshape(nb, l, KV_LORA), kr.reshape(nb, l, MLA_ROPE)


def kernel(x_prompt, x_sample, c_prompt, c_sample, state_retention, cache_mla_latent, cache_mla_krope,
           ada_w, ada_b, norm_g, ret_w_in, ret_gn_g, ret_w_out,
           kv_ada_w, kv_ada_b, kv_norm_g, mla_w_dkv, mla_kv_norm_g, mla_w_uk, mla_w_uv,
           mla_w_dq, mla_q_norm_g, mla_w_uq, mla_w_o,
           peer_w_q, peer_keys, peer_u, peer_v, final_g):
    p = dict(norm_g=norm_g, ret_w_in=ret_w_in, ret_gn_g=ret_gn_g[0][None], ret_w_out=ret_w_out,
             kv_norm_g=kv_norm_g, mla_w_dkv=mla_w_dkv, mla_kv_norm_g=mla_kv_norm_g,
             mla_w_uk=mla_w_uk, mla_w_uv=mla_w_uv, mla_w_dq=mla_w_dq, mla_q_norm_g=mla_q_norm_g,
             mla_w_uq=mla_w_uq, mla_w_o=mla_w_o, peer_w_q=peer_w_q, peer_keys=peer_keys,
             peer_u=peer_u, peer_v=peer_v, final_g=final_g)
    sw = _shared_weights(p)
    nbp, lp, d = x_prompt.shape
    nbs, ls, _ = x_sample.shape
    past = cache_mla_latent.shape[1]

    n_c = nbp + nbs
    c_all = jnp.pad(jnp.concatenate([c_prompt, c_sample], axis=0), ((0, (-n_c) % 8), (0, 0)))
    layer_mods = [_adaln(c_all, ada_w, ada_b, layer, 6 * d) for layer in range(2)]
    kv_mods = _adaln(c_all, kv_ada_w, kv_ada_b, 0, 2 * d)

    def stream_mods(lo, hi):
        return ([jnp.split(m[lo:hi], 6, axis=-1) for m in layer_mods], jnp.split(kv_mods[lo:hi], 2, axis=-1))

    mods_p, kv_p = stream_mods(0, nbp)
    mods_s, kv_s = stream_mods(nbp, n_c)
    pos_p = jnp.arange(lp, dtype=jnp.int32)
    pos_s = past + jnp.arange(ls, dtype=jnp.int32)
    y_p, ret_p, lat_p, kr_p = _trunk(x_prompt, pos_p, mods_p, kv_p, p, sw, None, None, None)
    y_s, ret_s, lat_s, kr_s = _trunk(x_sample, pos_s, mods_s, kv_s, p, sw, state_retention,
                                     cache_mla_latent, cache_mla_krope)
    return (y_p, y_s, ret_p, ret_s, lat_p, kr_p, lat_s, kr_s)
```

```python
import functools
import math

import jax
import jax.numpy as jnp
from jax import lax
from jax.experimental import pallas as pl
from jax.experimental.pallas import tpu as pltpu

F32 = jnp.float32
BF16 = jnp.bfloat16

EPS = 1e-6
ROPE_THETA = 10000.0
NEG_INF = -1e30
CHUNK = 64

RET_HEADS = 8
RET_DK = 256
RET_DV = 512

MLA_HEADS = 16
MLA_NOPE = 128
MLA_ROPE = 64
MLA_VDIM = 128
KV_LORA = 512
MLA_SCALE = (MLA_NOPE + MLA_ROPE) ** -0.5

PEER_HEADS = 8
N_KEYS = 128
PEER_TOPK = 16

LANES = 128
VMEM_LIMIT = 52 * 1024 * 1024
PEER_MIX_VMEM_LIMIT = 58 * 1024 * 1024
SELECT_HEADS_PER_STEP = 2


def _params(sem, vmem=VMEM_LIMIT, flags=None):
    return pltpu.CompilerParams(dimension_semantics=sem, vmem_limit_bytes=vmem, flags=flags)


def _bdot(a, b):
    return jnp.dot(a.astype(BF16), b.astype(BF16), preferred_element_type=F32)


def _bdot_nt(a, b):
    return lax.dot_general(a.astype(BF16), b.astype(BF16), (((1,), (1,)), ((), ())),
                           preferred_element_type=F32)


def _bdot_tn(a, b):
    return lax.dot_general(a.astype(BF16), b.astype(BF16), (((0,), (0,)), ((), ())),
                           preferred_element_type=F32)


def _silu(x):
    return x * (1.0 / (1.0 + jnp.exp(-x)))


def _gelu(x):
    return 0.5 * x * (1.0 + lax.erf(x * (0.5 ** 0.5)))


def _linear_kernel(*refs, n_w, n_extra, prologue, epilogue):
    x_ref = refs[0]
    w_refs = refs[1:1 + n_w]
    extra_refs = refs[1 + n_w:1 + n_w + n_extra]
    out_refs = refs[1 + n_w + n_extra:]
    xv = x_ref[...]
    if prologue is not None:
        xv = prologue(xv)
    xb = xv.astype(BF16)
    accs = [jnp.dot(xb, w[...].astype(BF16), preferred_element_type=F32) for w in w_refs]
    epilogue(accs, extra_refs, out_refs)


def _linear(x, ws, *, n_cols, tm, tn, epilogue, out_shapes, out_specs, w_lead=None, col_block0=0,
            extras=(), extra_specs=(), prologue=None, name=None):
    m, k = x.shape
    if name is None:
        name = "linear" + getattr(epilogue, "func", epilogue).__name__
    assert m % tm == 0 and n_cols % tn == 0
    if w_lead is None:
        w_spec = pl.BlockSpec((k, tn), lambda i, j: (0, j + col_block0))
    else:
        w_spec = pl.BlockSpec((None, k, tn), lambda i, j: (w_lead, 0, j + col_block0))
    kern = functools.partial(_linear_kernel, n_w=len(ws), n_extra=len(extras), prologue=prologue,
                             epilogue=epilogue)
    return pl.pallas_call(
        kern,
        grid=(m // tm, n_cols // tn),
        in_specs=[pl.BlockSpec((tm, k), lambda i, j: (i, 0))] + [w_spec] * len(ws) + list(extra_specs),
        out_specs=out_specs,
        out_shape=out_shapes,
        compiler_params=_params(("parallel", "arbitrary")),
        name=name,
    )(x, *ws, *extras)


def _ep_plain(accs, extras, outs):
    outs[0][...] = accs[0].astype(outs[0].dtype)


def _ep_heads(accs, extras, outs):
    acc = accs[0]
    for jj in range(acc.shape[1] // LANES):
        outs[0][jj] = acc[:, jj * LANES:(jj + 1) * LANES].astype(outs[0].dtype)


def _ep_bias(accs, extras, outs):
    outs[0][...] = accs[0] + extras[0][...]


def _ep_resid(accs, extras, outs):
    outs[0][...] = extras[0][...] + (1.0 + extras[1][...]) * accs[0]


def _gate_operand(gate, l, rows_per_tile, cols_per_tile):
    nb, _, d = gate.shape
    col = (lambda j: j) if cols_per_tile < d else (lambda j: 0)
    if l % rows_per_tile == 0:
        tiles_per_batch = l // rows_per_tile
        return gate, pl.BlockSpec((None, 1, cols_per_tile), lambda i, j: (i // tiles_per_batch, 0, col(j)))
    rows = jnp.broadcast_to(gate, (nb, l, d)).reshape(nb * l, d)
    return rows, pl.BlockSpec((rows_per_tile, cols_per_tile), lambda i, j: (i, col(j)))


def _ep_rope_half128(accs, extras, outs, *, scale):
    acc = accs[0]
    cos = extras[0][...]
    sin = extras[1][...]
    for g in range(acc.shape[1] // 256):
        x1 = acc[:, g * 256:g * 256 + 128]
        x2 = acc[:, g * 256 + 128:(g + 1) * 256]
        outs[0][:, g * 256:g * 256 + 128] = ((x1 * cos - x2 * sin) * scale).astype(outs[0].dtype)
        outs[0][:, g * 256 + 128:(g + 1) * 256] = ((x1 * sin + x2 * cos) * scale).astype(outs[0].dtype)


def _ep_rot_heads(accs, extras, outs):
    cos = extras[0][...]
    sin = extras[1][...]
    a, b = accs
    for jj in range(a.shape[1] // LANES):
        sl = slice(jj * LANES, (jj + 1) * LANES)
        outs[0][jj] = (a[:, sl] * cos + b[:, sl] * sin).astype(outs[0].dtype)


def _ep_rms(accs, extras, outs):
    acc = accs[0]
    g = extras[0][...]
    y = acc * lax.rsqrt(jnp.mean(acc * acc, axis=-1, keepdims=True) + EPS) * g
    outs[0][...] = y.astype(outs[0].dtype)


def _ep_kv(accs, extras, outs):
    acc = accs[0]
    g = extras[0][...]
    cos = extras[1][...]
    sin = extras[2][...]
    c = acc[:, :KV_LORA]
    outs[0][...] = c * lax.rsqrt(jnp.mean(c * c, axis=-1, keepdims=True) + EPS) * g
    kr = acc[:, KV_LORA:KV_LORA + LANES] * cos + acc[:, KV_LORA + LANES:KV_LORA + 2 * LANES] * sin
    outs[1][...] = kr[:, :MLA_ROPE]
    outs[2][...] = kr.astype(BF16)


def _row_tile(m, k):
    return min(m, 1024)


def _col_tile(k, n, n_weights=1):
    for tn in (1024, 512):
        if n % tn == 0 and 2 * n_weights * k * tn * 4 <= 16 * 1024 * 1024:
            return tn
    return min(n, 512)


def _table_spec(tm, table_rows):
    nblk = table_rows // tm
    return pl.BlockSpec((tm, LANES), lambda i, j: (i % nblk, 0))


def _norm_mod_kernel(*refs, n_mod):
    x = refs[0][...]
    xn = x * lax.rsqrt(jnp.mean(x * x, axis=-1, keepdims=True) + EPS)
    if n_mod == 0:
        refs[2][...] = xn * refs[1][...]
        return
    outs = refs[1 + 3 * n_mod:]
    for m in range(n_mod):
        g, sh, sc = (refs[1 + 3 * m + k][...] for k in range(3))
        outs[m][...] = ((xn * g) * (1.0 + sc) + sh).astype(outs[m].dtype)


def _norm_mod(x, mods=(), final_g=None):
    nb, l, d = x.shape
    tl = min(l, 256)
    tok = pl.BlockSpec((None, tl, d), lambda b, i: (b, i, 0))
    per_b = pl.BlockSpec((None, 1, d), lambda b, i: (b, 0, 0))
    gain = pl.BlockSpec((1, d), lambda b, i: (0, 0))
    args, specs = [x], [tok]
    if final_g is not None:
        assert not mods
        args.append(final_g)
        specs.append(gain)
        out_shapes = [jax.ShapeDtypeStruct((nb, l, d), F32)]
    else:
        for g, sh, sc in mods:
            args += [g, sh, sc]
            specs += [gain, per_b, per_b]
        out_shapes = [jax.ShapeDtypeStruct((nb, l, d), BF16) for _ in mods]
    return pl.pallas_call(
        functools.partial(_norm_mod_kernel, n_mod=len(mods)),
        grid=(nb, l // tl), in_specs=specs, out_specs=[tok] * len(out_shapes), out_shape=out_shapes,
        compiler_params=_params(("parallel", "parallel")),
        name="norm_mod",
    )(*args)


def _ret_log_decay():
    return jnp.log1p(-jnp.exp2(-5.0 - jnp.arange(RET_HEADS, dtype=F32)))


def _retention_tables(c):
    log_g = _ret_log_decay()[:, None, None]
    n = jnp.arange(c, dtype=F32)
    dist = n[:, None] - n[None, :]
    same = (jnp.arange(c)[:, None] // CHUNK) == (jnp.arange(c)[None, :] // CHUNK)
    earlier = (jnp.arange(c)[None, :] // CHUNK) < (jnp.arange(c)[:, None] // CHUNK)
    mask = jnp.where(same[None], jnp.exp(jnp.abs(dist)[None] * log_g),
                     jnp.where(earlier[None], jnp.exp(dist[None] * log_g), 0.0))
    q_decay = jnp.exp((n[None, :, None] + 1.0) * log_g)
    k_decay = jnp.exp((c - 1.0 - n)[None, :, None] * log_g)
    blk_decay = jnp.exp(c * log_g)
    return mask, q_decay, k_decay, blk_decay


def _retention_kernel(*refs, has_s0, hps):
    if has_s0:
        (q_ref, k_ref, v_ref, g_ref, mask_ref, qd_ref, kd_ref, bd_ref, gn_ref, s0_ref,
         y_ref, s_out_ref, s_ref) = refs
    else:
        (q_ref, k_ref, v_ref, g_ref, mask_ref, qd_ref, kd_ref, bd_ref, gn_ref,
         y_ref, s_out_ref, s_ref) = refs
    c = pl.program_id(2)

    @pl.when(c == 0)
    def _():
        if has_s0:
            s_ref[...] = s0_ref[...]
        else:
            s_ref[...] = jnp.zeros_like(s_ref)

    for hh in range(hps):
        qk_cols = slice(hh * RET_DK, (hh + 1) * RET_DK)
        v_cols = slice(hh * RET_DV, (hh + 1) * RET_DV)
        q = q_ref[:, qk_cols]
        k = k_ref[:, qk_cols]
        v = v_ref[:, v_cols]
        s_prev = s_ref[hh]
        scores = _bdot_nt(q, k) * mask_ref[hh]
        y = _bdot(scores, v) + _bdot(q, s_prev) * qd_ref[hh]
        k_scaled = k.astype(F32) * kd_ref[hh]
        s_ref[hh] = bd_ref[hh] * s_prev + _bdot_tn(k_scaled, v)

        mu = jnp.mean(y, axis=-1, keepdims=True)
        yc = y - mu
        var = jnp.mean(yc * yc, axis=-1, keepdims=True)
        yn = yc * lax.rsqrt(var + EPS) * gn_ref[:, v_cols]
        y_ref[:, v_cols] = (_silu(g_ref[:, v_cols].astype(F32)) * yn).astype(y_ref.dtype)

    @pl.when(c == pl.num_programs(2) - 1)
    def _():
        s_out_ref[...] = s_ref[...]


def _retention(q, k, v, g, gn_g, s0, nb, l):
    cb = min(l, 256)
    nc = l // cb
    hps = 2 if nc > 1 else RET_HEADS
    mask, qd, kd, bd = _retention_tables(cb)
    row = lambda b, hg, c: (b * nc + c, hg)
    per_h3 = lambda b, hg, c: (hg, 0, 0)
    in_specs = [
        pl.BlockSpec((cb, hps * RET_DK), row), pl.BlockSpec((cb, hps * RET_DK), row),
        pl.BlockSpec((cb, hps * RET_DV), row), pl.BlockSpec((cb, hps * RET_DV), row),
        pl.BlockSpec((hps, cb, cb), per_h3), pl.BlockSpec((hps, cb, 1), per_h3),
        pl.BlockSpec((hps, cb, 1), per_h3), pl.BlockSpec((hps, 1, 1), per_h3),
        pl.BlockSpec((1, hps * RET_DV), lambda b, hg, c: (0, hg)),
    ]
    args = [q, k, v, g, mask, qd, kd, bd, gn_g]
    state_spec = pl.BlockSpec((None, hps, RET_DK, RET_DV), lambda b, hg, c: (b, hg, 0, 0))
    if s0 is not None:
        in_specs.append(state_spec)
        args.append(s0)
    y, s_new = pl.pallas_call(
        functools.partial(_retention_kernel, has_s0=s0 is not None, hps=hps),
        grid=(nb, RET_HEADS // hps, nc),
        in_specs=in_specs,
        out_specs=[pl.BlockSpec((cb, hps * RET_DV), row), state_spec],
        out_shape=[jax.ShapeDtypeStruct((nb * l, RET_HEADS * RET_DV), BF16),
                   jax.ShapeDtypeStruct((nb, RET_HEADS, RET_DK, RET_DV), F32)],
        scratch_shapes=[pltpu.VMEM((hps, RET_DK, RET_DV), F32)],
        compiler_params=_params(("parallel", "parallel", "arbitrary")),
        name="retention",
    )(*args)
    return y, s_new


def _attn_prompt_kernel(qn_ref, qr_ref, kn_ref, v_ref, kr_ref, o_ref, m_ref, acc_ref, *, tq, tk):
    qi = pl.program_id(1)
    ki = pl.program_id(2)
    exp2_scale = MLA_SCALE * math.log2(math.e)

    @pl.when(ki == 0)
    def _():
        m_ref[...] = jnp.full_like(m_ref, NEG_INF)
        acc_ref[...] = jnp.zeros_like(acc_ref)

    def block(masked):
        if masked:
            q_chunk = (qi * tq + lax.broadcasted_iota(jnp.int32, (tq, tk), 0)) // CHUNK
            k_chunk = (ki * tk + lax.broadcasted_iota(jnp.int32, (tq, tk), 1)) // CHUNK
            visible = k_chunk <= q_chunk
        kr = kr_ref[...]
        ones = jnp.ones((tk, LANES), BF16)

        def head(h, carry):
            q = jnp.concatenate([qn_ref[h], qr_ref[h]], axis=1)
            k = jnp.concatenate([kn_ref[h], kr], axis=1)
            s = _bdot_nt(q, k)
            if masked:
                s = jnp.where(visible, s, NEG_INF)
            m_prev = m_ref[h]
            m_new = jnp.maximum(m_prev, jnp.max(s, axis=-1, keepdims=True))
            alpha = jnp.exp2((m_prev - m_new) * exp2_scale)
            p = jnp.exp2((s - jnp.tile(m_new, (1, tk // LANES))) * exp2_scale)
            v_ext = jnp.concatenate([v_ref[h], ones], axis=1)
            acc_ref[h] = jnp.tile(alpha, (1, 2)) * acc_ref[h] + _bdot(p, v_ext)
            m_ref[h] = m_new
            return carry

        lax.fori_loop(0, MLA_HEADS, head, 0, unroll=True)

    @pl.when(ki < qi)
    def _():
        block(False)

    @pl.when(ki == qi)
    def _():
        block(True)
        for h in range(MLA_HEADS):
            acc = acc_ref[h]
            o_ref[:, h * MLA_VDIM:(h + 1) * MLA_VDIM] = (
                acc[:, :MLA_VDIM] / acc[:, MLA_VDIM:]).astype(o_ref.dtype)


def _attn_prompt(qn, qr, kn, v, kr, nb, l):
    tq = tk = min(l, 512)
    nq = l // tq
    q_spec = pl.BlockSpec((MLA_HEADS, tq, LANES), lambda b, qi, ki: (0, b * nq + qi, 0))
    k_spec = pl.BlockSpec((MLA_HEADS, tk, LANES), lambda b, qi, ki: (0, b * nq + jnp.minimum(ki, qi), 0))
    return pl.pallas_call(
        functools.partial(_attn_prompt_kernel, tq=tq, tk=tk),
        grid=(nb, nq, nq),
        in_specs=[q_spec, q_spec, k_spec, k_spec,
                  pl.BlockSpec((tk, LANES), lambda b, qi, ki: (b * nq + jnp.minimum(ki, qi), 0))],
        out_specs=pl.BlockSpec((tq, MLA_HEADS * MLA_VDIM), lambda b, qi, ki: (b * nq + qi, 0)),
        out_shape=jax.ShapeDtypeStruct((nb * l, MLA_HEADS * MLA_VDIM), BF16),
        scratch_shapes=[pltpu.VMEM((MLA_HEADS, tq, LANES), F32),
                        pltpu.VMEM((MLA_HEADS, tq, MLA_VDIM + LANES), F32)],
        compiler_params=_params(("parallel", "parallel", "arbitrary")),
        name="attn_prompt",
    )(qn, qr, kn, v, kr)


def _attn_cached_kernel(qn_ref, qr_ref, clat_ref, ckr_ref, nlat_ref, nkr_ref, wuk_ref, wuv_ref,
                        o_ref, ql_ref, qrs_ref, *, lq):
    for h in range(MLA_HEADS):
        ql_ref[h * lq:(h + 1) * lq, :] = _bdot_nt(qn_ref[h], wuk_ref[h]).astype(BF16)
        qrs_ref[h * lq:(h + 1) * lq, :] = qr_ref[h]
    ql = ql_ref[...]
    qr = qrs_ref[...]
    clat = clat_ref[...].astype(BF16)
    nlat = nlat_ref[...].astype(BF16)
    s_c = (_bdot_nt(ql, clat) + _bdot_nt(qr, ckr_ref[...])) * MLA_SCALE
    s_n = (_bdot_nt(ql, nlat) + _bdot_nt(qr, nkr_ref[...])) * MLA_SCALE
    m = jnp.maximum(jnp.max(s_c, axis=-1, keepdims=True), jnp.max(s_n, axis=-1, keepdims=True))
    p_c = jnp.exp(s_c - m)
    p_n = jnp.exp(s_n - m)
    denom = jnp.sum(p_c, axis=-1, keepdims=True) + jnp.sum(p_n, axis=-1, keepdims=True)
    o_lat = (_bdot(p_c, clat) + _bdot(p_n, nlat)) / denom
    for h in range(MLA_HEADS):
        o_ref[:, h * MLA_VDIM:(h + 1) * MLA_VDIM] = _bdot(
            o_lat[h * lq:(h + 1) * lq, :], wuv_ref[h]).astype(o_ref.dtype)


def _attn_cached(qn, qr, cache_lat, cache_kr_pad, new_lat, new_kr_pad, wuk_h, wuv_h, nb, lq):
    past = cache_lat.shape[1]
    whole = lambda b: (0, 0, 0)
    return pl.pallas_call(
        functools.partial(_attn_cached_kernel, lq=lq),
        grid=(nb,),
        in_specs=[
            pl.BlockSpec((MLA_HEADS, lq, LANES), lambda b: (0, b, 0)),
            pl.BlockSpec((MLA_HEADS, lq, LANES), lambda b: (0, b, 0)),
            pl.BlockSpec((None, past, KV_LORA), lambda b: (b, 0, 0)),
            pl.BlockSpec((None, past, LANES), lambda b: (b, 0, 0)),
            pl.BlockSpec((lq, KV_LORA), lambda b: (b, 0)),
            pl.BlockSpec((lq, LANES), lambda b: (b, 0)),
            pl.BlockSpec((MLA_HEADS, KV_LORA, MLA_NOPE), whole),
            pl.BlockSpec((MLA_HEADS, KV_LORA, MLA_VDIM), whole),
        ],
        out_specs=pl.BlockSpec((lq, MLA_HEADS * MLA_VDIM), lambda b: (b, 0)),
        out_shape=jax.ShapeDtypeStruct((nb * lq, MLA_HEADS * MLA_VDIM), BF16),
        scratch_shapes=[pltpu.VMEM((MLA_HEADS * lq, KV_LORA), BF16),
                        pltpu.VMEM((MLA_HEADS * lq, LANES), BF16)],
        compiler_params=_params(("parallel",)),
        name="attn_cached",
    )(qn, qr, cache_lat, cache_kr_pad, new_lat, new_kr_pad, wuk_h, wuv_h)


def _top_values(s, top_ref, want_rank):
    rank = jnp.full(s.shape, float(PEER_TOPK), F32) if want_rank else None
    for r in range(PEER_TOPK):
        m = jnp.max(s, axis=0, keepdims=True)
        top_ref[r:r + 1, :] = m
        hit = s == m
        if want_rank:
            rank = jnp.where(hit, float(r), rank)
        s = jnp.where(hit, -jnp.inf, s)
    return rank


_CAND_LIMITS = tuple(PEER_TOPK // (b + 1) for b in range(1, 8))
_HALF_TOPK = PEER_TOPK // 2


def _peer_select_kernel(h_ref, wq_ref, keys_ref, r1_ref, e1_ref, c0_ref, e0_ref, s_ref, a0_ref, a1_ref,
                        *, heads_per_step):
    hx = h_ref[...]
    for hh in range(heads_per_step):
        q_t = _bdot_nt(wq_ref[hh * 2 * N_KEYS:(hh + 1) * 2 * N_KEYS, :], hx)
        s_ref[2 * hh] = _bdot(keys_ref[hh, 0], q_t[:N_KEYS])
        s_ref[2 * hh + 1] = _bdot(keys_ref[hh, 1], q_t[N_KEYS:])
    for hh in range(heads_per_step):
        _peer_select_head(s_ref[2 * hh], s_ref[2 * hh + 1], r1_ref.at[hh], e1_ref.at[hh], c0_ref.at[hh],
                          e0_ref.at[hh], a0_ref.at[hh], a1_ref.at[hh])


def _peer_select_head(s0, s1, r1_ref, e1_ref, c0_ref, e0_ref, a0_ref, a1_ref):
    _top_values(s0, a0_ref, False)
    rank1 = _top_values(s1, a1_ref, True)
    a0 = a0_ref[...]
    a1 = a1_ref[...]
    a0_lo = a0[:_HALF_TOPK, :]
    row = lax.broadcasted_iota(jnp.int32, a0_lo.shape, 0)
    pieces = [a1[0:1, :] + a0]
    for b, limit in enumerate(_CAND_LIMITS, start=1):
        pieces.append(jnp.where(row < limit, a1[b:b + 1, :] + a0_lo, -jnp.inf))
    pieces.append(a1[_HALF_TOPK:, :] + a0[0:1, :])
    cand = jnp.concatenate(pieces, axis=0)
    c = cand
    for r in range(PEER_TOPK):
        tau = jnp.max(c, axis=0, keepdims=True)
        if r + 1 < PEER_TOPK:
            c = jnp.where(c == tau, -jnp.inf, c)
    sel = cand >= tau
    cmax = a0[0:1, :] + a1[0:1, :]
    z = jnp.sum(jnp.where(sel, jnp.exp(cand - cmax), 0.0), axis=0, keepdims=True)
    self32 = sel.astype(F32)
    n_hi = self32[_HALF_TOPK:PEER_TOPK, :]
    n_lo = self32[:_HALF_TOPK, :]
    for b in range(1, 8):
        lo = PEER_TOPK + (b - 1) * _HALF_TOPK
        n_lo = n_lo + self32[lo:lo + _HALF_TOPK, :]
    tail = jnp.sum(self32[PEER_TOPK + 7 * _HALF_TOPK:, :], axis=0, keepdims=True)
    n_lo = n_lo + jnp.where(row == 0, tail, 0.0)
    count0 = jnp.zeros(s0.shape, F32)
    for a in range(PEER_TOPK):
        n_a = n_lo[a:a + 1, :] if a < _HALF_TOPK else n_hi[a - _HALF_TOPK:a - _HALF_TOPK + 1, :]
        count0 = jnp.where(s0 == a0[a:a + 1, :], n_a, count0)
    r1_ref[...] = rank1.astype(r1_ref.dtype)
    c0_ref[...] = count0
    e0_ref[...] = jnp.exp(s0 - a0[0:1, :]) / z
    e1_ref[...] = jnp.exp(s1 - a1[0:1, :]).astype(e1_ref.dtype)


def _peer_select(hx, wq_t, keys, layer, tt):
    t, d = hx.shape
    hps = SELECT_HEADS_PER_STEP
    o_spec = pl.BlockSpec((hps, N_KEYS, tt), lambda i, g: (g, 0, i))
    return pl.pallas_call(
        functools.partial(_peer_select_kernel, heads_per_step=hps),
        grid=(t // tt, PEER_HEADS // hps),
        in_specs=[pl.BlockSpec((tt, d), lambda i, g: (i, 0)),
                  pl.BlockSpec((hps * 2 * N_KEYS, d), lambda i, g: (g, 0)),
                  pl.BlockSpec((None, hps, 2, N_KEYS, N_KEYS), lambda i, g: (layer, g, 0, 0, 0))],
        out_specs=[o_spec] * 4,
        out_shape=[jax.ShapeDtypeStruct((PEER_HEADS, N_KEYS, t), dt) for dt in (BF16, BF16, F32, F32)],
        scratch_shapes=[pltpu.VMEM((2 * hps, N_KEYS, tt), F32),
                        pltpu.VMEM((hps, PEER_TOPK, tt), F32), pltpu.VMEM((hps, PEER_TOPK, tt), F32)],
        compiler_params=_params(("parallel", "arbitrary")),
        name="peer_select",
    )(hx, wq_t, keys)


def _sublane_bcast_bf16(row, rows):
    tile_rows = 16
    packed = jnp.broadcast_to(row, (tile_rows, row.shape[1])).astype(BF16)
    return jnp.tile(packed, (rows // tile_rows, 1))


def _peer_mix_kernel(x_ref, u_ref, vt_ref, r1_ref, e1_ref, c0_ref, e0_ref, res_ref, gate_ref, o_ref,
                     acc_ref, a_ref, g_ref, p_ref, *, n_i):
    j = pl.program_id(1)

    @pl.when(j == 0)
    def _():
        acc_ref[...] = jnp.zeros_like(acc_ref)

    x = x_ref[...]
    rows_per_chunk = 2 * N_KEYS
    for c in range(n_i // 2):
        lo = c * rows_per_chunk
        a_ref[c] = _bdot_nt(u_ref[lo:lo + rows_per_chunk, :], x)
    zero = jnp.zeros((), BF16)
    for ii in range(n_i):
        w = None
        for h in range(PEER_HEADS):
            count = _sublane_bcast_bf16(c0_ref[h, ii:ii + 1, :], N_KEYS)
            e0 = _sublane_bcast_bf16(e0_ref[h, ii:ii + 1, :], N_KEYS)
            term = jnp.where(r1_ref[h] < count, e1_ref[h] * e0, zero)
            w = term if w is None else w + term
        g_ref[ii] = w
    for c in range(n_i // 2):
        lo = c * rows_per_chunk
        gates = jnp.concatenate([g_ref[2 * c], g_ref[2 * c + 1]], axis=0)
        p_ref[lo:lo + rows_per_chunk, :] = gates * _gelu(a_ref[c]).astype(BF16)
    acc_ref[...] += jnp.dot(vt_ref[...], p_ref[...], preferred_element_type=F32)

    @pl.when(j == pl.num_programs(1) - 1)
    def _():
        o_ref[...] = res_ref[...] + (1.0 + gate_ref[...]) * acc_ref[...].T


def _peer_mix(hx, u_b, vt_b, r1, e1, c0, e0, x_res, gate, l, tt, et):
    t, d = hx.shape
    e = u_b.shape[0]
    n_i = et // N_KEYS
    tok_all = pl.BlockSpec((PEER_HEADS, N_KEYS, tt), lambda i, j: (0, 0, i))
    tok_i = pl.BlockSpec((PEER_HEADS, n_i, tt), lambda i, j: (0, j, i))
    gate_arr, gate_spec = _gate_operand(gate, l, tt, d)
    return pl.pallas_call(
        functools.partial(_peer_mix_kernel, n_i=n_i),
        grid=(t // tt, e // et),
        in_specs=[pl.BlockSpec((tt, d), lambda i, j: (i, 0)),
                  pl.BlockSpec((et, d), lambda i, j: (j, 0)),
                  pl.BlockSpec((d, et), lambda i, j: (0, j)),
                  tok_all, tok_all, tok_i, tok_i,
                  pl.BlockSpec((tt, d), lambda i, j: (i, 0)), gate_spec],
        out_specs=pl.BlockSpec((tt, d), lambda i, j: (i, 0)),
        out_shape=jax.ShapeDtypeStruct((t, d), F32),
        scratch_shapes=[pltpu.VMEM((d, tt), F32),
                        pltpu.VMEM((n_i // 2, 2 * N_KEYS, tt), F32),
                        pltpu.VMEM((n_i, N_KEYS, tt), BF16),
                        pltpu.VMEM((et, tt), BF16)],
        compiler_params=_params(("parallel", "arbitrary"), vmem=PEER_MIX_VMEM_LIMIT),
        name="peer_mix",
    )(hx, u_b, vt_b, r1, e1, c0, e0, x_res, gate_arr)


def _peer(hx, wq_t, keys, layer, u_b, vt_b, x_res, gate, l):
    t = hx.shape[0]
    tt = min(t, 512)
    r1, e1, c0, e0 = _peer_select(hx, wq_t, keys, layer, tt)
    return _peer_mix(hx, u_b, vt_b, r1, e1, c0, e0, x_res, gate, l, tt, 1024)


def _rope_tables(pos, half, reps):
    inv = ROPE_THETA ** (-jnp.arange(half, dtype=F32) / half)
    ang = pos.astype(F32)[:, None] * inv[None, :]
    return jnp.tile(jnp.cos(ang), (1, reps)), jnp.tile(jnp.sin(ang), (1, reps))


def _rotate_half_cols(w):
    half = w.shape[-1] // 2
    return jnp.concatenate([-w[..., half:], w[..., :half]], axis=-1)


def _pad_last(w, width):
    return jnp.pad(w, [(0, 0)] * (w.ndim - 1) + [(0, width - w.shape[-1])])


def _shared_weights(p):
    out = {}
    w_rope = p["mla_w_dkv"][:, KV_LORA:]
    out["w_dkv_ext"] = jnp.concatenate(
        [p["mla_w_dkv"][:, :KV_LORA], _pad_last(w_rope, LANES), _pad_last(_rotate_half_cols(w_rope), LANES)],
        axis=1)
    w_uq = p["mla_w_uq"][0].reshape(-1, MLA_HEADS, MLA_NOPE + MLA_ROPE)
    q_lora = w_uq.shape[0]
    out["w_uq_nope"] = w_uq[:, :, :MLA_NOPE].reshape(q_lora, MLA_HEADS * MLA_NOPE)
    w_qr = w_uq[:, :, MLA_NOPE:]
    out["w_uq_rope"] = _pad_last(w_qr, LANES).reshape(q_lora, MLA_HEADS * LANES)
    out["w_uq_rot"] = _pad_last(_rotate_half_cols(w_qr), LANES).reshape(q_lora, MLA_HEADS * LANES)
    out["w_uk_flat"] = p["mla_w_uk"].reshape(KV_LORA, MLA_HEADS * MLA_NOPE)
    out["w_uv_flat"] = p["mla_w_uv"].reshape(KV_LORA, MLA_HEADS * MLA_VDIM)
    out["w_uk_h"] = jnp.transpose(p["mla_w_uk"], (1, 0, 2))
    out["w_uv_h"] = jnp.transpose(p["mla_w_uv"], (1, 0, 2))
    out["peer_wq_t"] = [p["peer_w_q"][l].T for l in range(2)]
    out["peer_u_b"] = [p["peer_u"][l].astype(BF16) for l in range(2)]
    out["peer_vt_b"] = [p["peer_v"][l].astype(BF16).T for l in range(2)]
    return out


def _adaln(c_all, w, b, lead, n_out):
    m, d = c_all.shape
    tn = 512
    if w.ndim == 2:
        w = w[None]
        b = b[None]
    b3 = b.reshape(b.shape[0], 1, n_out)
    return _linear(
        c_all, [w], n_cols=n_out, tm=m, tn=tn, epilogue=_ep_bias, prologue=_silu, w_lead=lead,
        extras=[b3], extra_specs=[pl.BlockSpec((None, 1, tn), lambda i, j: (lead, 0, j))],
        out_shapes=[jax.ShapeDtypeStruct((m, n_out), F32)],
        out_specs=[pl.BlockSpec((m, tn), lambda i, j: (0, j))])[0]


def _trunk(x, pos, mods, kv_mod, p, sw, ret_state, cache_lat, cache_kr):
    nb, l, d = x.shape
    t = nb * l
    pos_rows = jnp.tile(pos, nb) if l < 256 else pos
    table_rows = pos_rows.shape[0]
    assert table_rows % _row_tile(t, d) == 0, "a row tile must cover whole sequences or divide one"

    def vec(a):
        return a.reshape(nb, 1, d)

    def flat(a):
        return a.reshape(t, a.shape[-1])

    sh1, sc1, gt1, sh2, sc2, gt2 = [vec(m) for m in mods[0]]
    (h,) = _norm_mod(x, mods=[(p["norm_g"][0, 0][None], sh1, sc1)])
    h = flat(h)
    tm = _row_tile(t, d)
    cos_r, sin_r = _rope_tables(pos_rows, RET_DK // 2, 1)
    tspec = _table_spec(tm, table_rows)
    w_in = p["ret_w_in"]
    qk_cols = RET_HEADS * RET_DK
    v_cols = RET_HEADS * RET_DV
    tn = _col_tile(d, qk_cols)

    def rope_proj(col0, scale):
        return _linear(
            h, [w_in], n_cols=qk_cols, tm=tm, tn=tn, w_lead=0, col_block0=col0 // tn,
            epilogue=functools.partial(_ep_rope_half128, scale=scale),
            extras=[cos_r, sin_r], extra_specs=[tspec, tspec],
            out_shapes=[jax.ShapeDtypeStruct((t, qk_cols), BF16)],
            out_specs=[pl.BlockSpec((tm, tn), lambda i, j: (i, j))])[0]

    def plain_proj(x_in, w, n_cols, col0=0, lead=None, dtype=BF16, tm_=None):
        tm_ = tm_ or _row_tile(x_in.shape[0], x_in.shape[1])
        tn_ = _col_tile(x_in.shape[1], n_cols)
        return _linear(
            x_in, [w], n_cols=n_cols, tm=tm_, tn=tn_, w_lead=lead, col_block0=col0 // tn_,
            epilogue=_ep_plain,
            out_shapes=[jax.ShapeDtypeStruct((x_in.shape[0], n_cols), dtype)],
            out_specs=[pl.BlockSpec((tm_, tn_), lambda i, j: (i, j))])[0]

    def resid_proj(x_in, w, x_res, gate):
        tm_ = _row_tile(x_in.shape[0], x_in.shape[1])
        tn_ = _col_tile(x_in.shape[1], d)
        gate_arr, gate_spec = _gate_operand(gate, l, tm_, tn_)
        tile = pl.BlockSpec((tm_, tn_), lambda i, j: (i, j))
        return _linear(
            x_in, [w], n_cols=d, tm=tm_, tn=tn_, w_lead=0, epilogue=_ep_resid,
            extras=[x_res, gate_arr], extra_specs=[tile, gate_spec],
            out_shapes=[jax.ShapeDtypeStruct((t, d), F32)], out_specs=[tile])[0]

    def heads_proj(x_in, ws, n_cols, epilogue=_ep_heads, extras=(), extra_specs=()):
        tm_ = _row_tile(x_in.shape[0], x_in.shape[1])
        tn_ = _col_tile(x_in.shape[1], n_cols, len(ws))
        return _linear(
            x_in, ws, n_cols=n_cols, tm=tm_, tn=tn_, epilogue=epilogue, extras=extras, extra_specs=extra_specs,
            out_shapes=[jax.ShapeDtypeStruct((n_cols // LANES, x_in.shape[0], LANES), BF16)],
            out_specs=[pl.BlockSpec((tn_ // LANES, tm_, LANES), lambda i, j: (j, i, 0))])[0]

    q_r = rope_proj(0, 1.0)
    k_r = rope_proj(qk_cols, RET_DK ** -0.5)
    v_r = plain_proj(h, w_in, v_cols, col0=2 * qk_cols, lead=0)
    g_r = plain_proj(h, w_in, v_cols, col0=2 * qk_cols + v_cols, lead=0)
    s0 = None if ret_state is None else ret_state[0]
    y_r, s_new = _retention(q_r, k_r, v_r, g_r, p["ret_gn_g"], s0, nb, l)
    xf = resid_proj(y_r, p["ret_w_out"], flat(x), gt1)
    (h,) = _norm_mod(xf.reshape(nb, l, d), mods=[(p["norm_g"][0, 1][None], sh2, sc2)])
    xf = _peer(flat(h), sw["peer_wq_t"][0], p["peer_keys"], 0, sw["peer_u_b"][0], sw["peer_vt_b"][0],
               xf, gt2, l)

    sh1, sc1, gt1b, sh2, sc2, gt2b = [vec(m) for m in mods[1]]
    kv_sh, kv_sc = [vec(m) for m in kv_mod]
    h, h_kv = _norm_mod(xf.reshape(nb, l, d),
                         mods=[(p["norm_g"][1, 0][None], sh1, sc1), (p["kv_norm_g"][None], kv_sh, kv_sc)])
    h = flat(h)
    h_kv = flat(h_kv)
    cos_m, sin_m = _rope_tables(pos_rows, MLA_ROPE // 2, LANES // (MLA_ROPE // 2))
    n_ext = sw["w_dkv_ext"].shape[1]
    lat, kr, kr_pad = _linear(
        h_kv, [sw["w_dkv_ext"]], n_cols=n_ext, tm=tm, tn=n_ext, epilogue=_ep_kv,
        extras=[p["mla_kv_norm_g"][None], cos_m, sin_m],
        extra_specs=[pl.BlockSpec((1, KV_LORA), lambda i, j: (0, 0)), tspec, tspec],
        out_shapes=[jax.ShapeDtypeStruct((t, KV_LORA), F32), jax.ShapeDtypeStruct((t, MLA_ROPE), F32),
                    jax.ShapeDtypeStruct((t, LANES), BF16)],
        out_specs=[pl.BlockSpec((tm, KV_LORA), lambda i, j: (i, 0)),
                   pl.BlockSpec((tm, MLA_ROPE), lambda i, j: (i, 0)),
                   pl.BlockSpec((tm, LANES), lambda i, j: (i, 0))])
    q_lora = p["mla_w_dq"].shape[-1]
    cq = _linear(
        h, [p["mla_w_dq"]], n_cols=q_lora, tm=tm, tn=q_lora, w_lead=0, epilogue=_ep_rms,
        extras=[p["mla_q_norm_g"]], extra_specs=[pl.BlockSpec((1, q_lora), lambda i, j: (0, 0))],
        out_shapes=[jax.ShapeDtypeStruct((t, q_lora), BF16)],
        out_specs=[pl.BlockSpec((tm, q_lora), lambda i, j: (i, 0))])[0]
    hl = MLA_HEADS * LANES
    qn = heads_proj(cq, [sw["w_uq_nope"]], hl)
    qr = heads_proj(cq, [sw["w_uq_rope"], sw["w_uq_rot"]], hl, epilogue=_ep_rot_heads,
                    extras=[cos_m, sin_m], extra_specs=[tspec, tspec])
    if cache_lat is None:
        kn = heads_proj(lat, [sw["w_uk_flat"]], hl)
        vv = heads_proj(lat, [sw["w_uv_flat"]], hl)
        o = _attn_prompt(qn, qr, kn, vv, kr_pad, nb, l)
    else:
        ckr = _pad_last(cache_kr, LANES).astype(BF16)
        o = _attn_cached(qn, qr, cache_lat, ckr, lat, kr_pad, sw["w_uk_h"], sw["w_uv_h"], nb, l)
    xf = resid_proj(o, p["mla_w_o"], xf, gt1b)
    (h,) = _norm_mod(xf.reshape(nb, l, d), mods=[(p["norm_g"][1, 1][None], sh2, sc2)])
    xf = _peer(flat(h), sw["peer_wq_t"][1], p["peer_keys"], 1, sw["peer_u_b"][1], sw["peer_vt_b"][1],
               xf, gt2b, l)
    (y,) = _norm_mod(xf.reshape(nb, l, d), final_g=p["final_g"][None])
    return y, s_new[None], lat.reshape(nb, l, KV_LORA), kr.reshape(nb, l, MLA_ROPE)


def kernel(x_prompt, x_sample, c_prompt, c_sample, state_retention, cache_mla_latent, cache_mla_krope,
           ada_w, ada_b, norm_g, ret_w_in, ret_gn_g, ret_w_out,
           kv_ada_w, kv_ada_b, kv_norm_g, mla_w_dkv, mla_kv_norm_g, mla_w_uk, mla_w_uv,
           mla_w_dq, mla_q_norm_g, mla_w_uq, mla_w_o,
           peer_w_q, peer_keys, peer_u, peer_v, final_g):
    p = dict(norm_g=norm_g, ret_w_in=ret_w_in, ret_gn_g=ret_gn_g[0][None], ret_w_out=ret_w_out,
             kv_norm_g=kv_norm_g, mla_w_dkv=mla_w_dkv, mla_kv_norm_g=mla_kv_norm_g,
             mla_w_uk=mla_w_uk, mla_w_uv=mla_w_uv, mla_w_dq=mla_w_dq, mla_q_norm_g=mla_q_norm_g,
             mla_w_uq=mla_w_uq, mla_w_o=mla_w_o, peer_w_q=peer_w_q, peer_keys=peer_keys,
             peer_u=peer_u, peer_v=peer_v, final_g=final_g)
    sw = _shared_weights(p)
    nbp, lp, d = x_prompt.shape
    nbs, ls, _ = x_sample.shape
    past = cache_mla_latent.shape[1]

    n_c = nbp + nbs
    c_all = jnp.pad(jnp.concatenate([c_prompt, c_sample], axis=0), ((0, (-n_c) % 8), (0, 0)))
    layer_mods = [_adaln(c_all, ada_w, ada_b, layer, 6 * d) for layer in range(2)]
    kv_mods = _adaln(c_all, kv_ada_w, kv_ada_b, 0, 2 * d)

    def stream_mods(lo, hi):
        return ([jnp.split(m[lo:hi], 6, axis=-1) for m in layer_mods], jnp.split(kv_mods[lo:hi], 2, axis=-1))

    mods_p, kv_p = stream_mods(0, nbp)
    mods_s, kv_s = stream_mods(nbp, n_c)
    pos_p = jnp.arange(lp, dtype=jnp.int32)
    pos_s = past + jnp.arange(ls, dtype=jnp.int32)
    y_p, ret_p, lat_p, kr_p = _trunk(x_prompt, pos_p, mods_p, kv_p, p, sw, None, None, None)
    y_s, ret_s, lat_s, kr_s = _trunk(x_sample, pos_s, mods_s, kv_s, p, sw, state_retention,
                                     cache_mla_latent, cache_mla_krope)
    return (y_p, y_s, ret_p, ret_s, lat_p, kr_p, lat_s, kr_s)
```

```python
import functools
import math

import jax
import jax.numpy as jnp
from jax import lax
from jax.experimental import pallas as pl
from jax.experimental.pallas import tpu as pltpu

F32 = jnp.float32
BF16 = jnp.bfloat16

EPS = 1e-6
ROPE_THETA = 10000.0
NEG_INF = -1e30
CHUNK = 64

RET_HEADS = 8
RET_DK = 256
RET_DV = 512

MLA_HEADS = 16
MLA_NOPE = 128
MLA_ROPE = 64
MLA_VDIM = 128
KV_LORA = 512
MLA_SCALE = (MLA_NOPE + MLA_ROPE) ** -0.5

PEER_HEADS = 8
N_KEYS = 128
PEER_TOPK = 16

LANES = 128
VMEM_LIMIT = 52 * 1024 * 1024
PEER_MIX_VMEM_LIMIT = 58 * 1024 * 1024
SELECT_HEADS_PER_STEP = 4


def _params(sem, vmem=VMEM_LIMIT, flags=None):
    return pltpu.CompilerParams(dimension_semantics=sem, vmem_limit_bytes=vmem, flags=flags)


def _bdot(a, b):
    return jnp.dot(a.astype(BF16), b.astype(BF16), preferred_element_type=F32)


def _bdot_nt(a, b):
    return lax.dot_general(a.astype(BF16), b.astype(BF16), (((1,), (1,)), ((), ())),
                           preferred_element_type=F32)


def _bdot_tn(a, b):
    return lax.dot_general(a.astype(BF16), b.astype(BF16), (((0,), (0,)), ((), ())),
                           preferred_element_type=F32)


def _silu(x):
    return x * (1.0 / (1.0 + jnp.exp(-x)))


def _gelu(x):
    return 0.5 * x * (1.0 + lax.erf(x * (0.5 ** 0.5)))


def _linear_kernel(*refs, n_w, n_extra, prologue, epilogue):
    x_ref = refs[0]
    w_refs = refs[1:1 + n_w]
    extra_refs = refs[1 + n_w:1 + n_w + n_extra]
    out_refs = refs[1 + n_w + n_extra:]
    xv = x_ref[...]
    if prologue is not None:
        xv = prologue(xv)
    xb = xv.astype(BF16)
    accs = [jnp.dot(xb, w[...].astype(BF16), preferred_element_type=F32) for w in w_refs]
    epilogue(accs, extra_refs, out_refs)


def _linear(x, ws, *, n_cols, tm, tn, epilogue, out_shapes, out_specs, w_lead=None, col_block0=0,
            extras=(), extra_specs=(), prologue=None, name=None):
    m, k = x.shape
    if name is None:
        name = "linear" + getattr(epilogue, "func", epilogue).__name__
    assert m % tm == 0 and n_cols % tn == 0
    if w_lead is None:
        w_spec = pl.BlockSpec((k, tn), lambda i, j: (0, j + col_block0))
    else:
        w_spec = pl.BlockSpec((None, k, tn), lambda i, j: (w_lead, 0, j + col_block0))
    kern = functools.partial(_linear_kernel, n_w=len(ws), n_extra=len(extras), prologue=prologue,
                             epilogue=epilogue)
    return pl.pallas_call(
        kern,
        grid=(m // tm, n_cols // tn),
        in_specs=[pl.BlockSpec((tm, k), lambda i, j: (i, 0))] + [w_spec] * len(ws) + list(extra_specs),
        out_specs=out_specs,
        out_shape=out_shapes,
        compiler_params=_params(("parallel", "arbitrary")),
        name=name,
    )(x, *ws, *extras)


def _ep_plain(accs, extras, outs):
    outs[0][...] = accs[0].astype(outs[0].dtype)


def _ep_heads(accs, extras, outs):
    acc = accs[0]
    for jj in range(acc.shape[1] // LANES):
        outs[0][jj] = acc[:, jj * LANES:(jj + 1) * LANES].astype(outs[0].dtype)


def _ep_bias(accs, extras, outs):
    outs[0][...] = accs[0] + extras[0][...]


def _ep_resid(accs, extras, outs):
    outs[0][...] = extras[0][...] + (1.0 + extras[1][...]) * accs[0]


def _gate_operand(gate, l, rows_per_tile, cols_per_tile):
    nb, _, d = gate.shape
    col = (lambda j: j) if cols_per_tile < d else (lambda j: 0)
    if l % rows_per_tile == 0:
        tiles_per_batch = l // rows_per_tile
        return gate, pl.BlockSpec((None, 1, cols_per_tile), lambda i, j: (i // tiles_per_batch, 0, col(j)))
    rows = jnp.broadcast_to(gate, (nb, l, d)).reshape(nb * l, d)
    return rows, pl.BlockSpec((rows_per_tile, cols_per_tile), lambda i, j: (i, col(j)))


def _ep_rope_half128(accs, extras, outs, *, scale):
    acc = accs[0]
    cos = extras[0][...]
    sin = extras[1][...]
    for g in range(acc.shape[1] // 256):
        x1 = acc[:, g * 256:g * 256 + 128]
        x2 = acc[:, g * 256 + 128:(g + 1) * 256]
        outs[0][:, g * 256:g * 256 + 128] = ((x1 * cos - x2 * sin) * scale).astype(outs[0].dtype)
        outs[0][:, g * 256 + 128:(g + 1) * 256] = ((x1 * sin + x2 * cos) * scale).astype(outs[0].dtype)


def _ep_rot_heads(accs, extras, outs):
    cos = extras[0][...]
    sin = extras[1][...]
    a, b = accs
    for jj in range(a.shape[1] // LANES):
        sl = slice(jj * LANES, (jj + 1) * LANES)
        outs[0][jj] = (a[:, sl] * cos + b[:, sl] * sin).astype(outs[0].dtype)


def _ep_rms(accs, extras, outs):
    acc = accs[0]
    g = extras[0][...]
    y = acc * lax.rsqrt(jnp.mean(acc * acc, axis=-1, keepdims=True) + EPS) * g
    outs[0][...] = y.astype(outs[0].dtype)


def _ep_kv(accs, extras, outs):
    acc = accs[0]
    g = extras[0][...]
    cos = extras[1][...]
    sin = extras[2][...]
    c = acc[:, :KV_LORA]
    outs[0][...] = c * lax.rsqrt(jnp.mean(c * c, axis=-1, keepdims=True) + EPS) * g
    kr = acc[:, KV_LORA:KV_LORA + LANES] * cos + acc[:, KV_LORA + LANES:KV_LORA + 2 * LANES] * sin
    outs[1][...] = kr[:, :MLA_ROPE]
    outs[2][...] = kr.astype(BF16)


def _row_tile(m, k):
    return min(m, 1024)


def _col_tile(k, n, n_weights=1):
    for tn in (1024, 512):
        if n % tn == 0 and 2 * n_weights * k * tn * 4 <= 16 * 1024 * 1024:
            return tn
    return min(n, 512)


def _table_spec(tm, table_rows):
    nblk = table_rows // tm
    return pl.BlockSpec((tm, LANES), lambda i, j: (i % nblk, 0))


def _norm_mod_kernel(*refs, n_mod):
    x = refs[0][...]
    xn = x * lax.rsqrt(jnp.mean(x * x, axis=-1, keepdims=True) + EPS)
    if n_mod == 0:
        refs[2][...] = xn * refs[1][...]
        return
    outs = refs[1 + 3 * n_mod:]
    for m in range(n_mod):
        g, sh, sc = (refs[1 + 3 * m + k][...] for k in range(3))
        outs[m][...] = ((xn * g) * (1.0 + sc) + sh).astype(outs[m].dtype)


def _norm_mod(x, mods=(), final_g=None):
    nb, l, d = x.shape
    tl = min(l, 512)
    tok = pl.BlockSpec((None, tl, d), lambda b, i: (b, i, 0))
    per_b = pl.BlockSpec((None, 1, d), lambda b, i: (b, 0, 0))
    gain = pl.BlockSpec((1, d), lambda b, i: (0, 0))
    args, specs = [x], [tok]
    if final_g is not None:
        assert not mods
        args.append(final_g)
        specs.append(gain)
        out_shapes = [jax.ShapeDtypeStruct((nb, l, d), F32)]
    else:
        for g, sh, sc in mods:
            args += [g, sh, sc]
            specs += [gain, per_b, per_b]
        out_shapes = [jax.ShapeDtypeStruct((nb, l, d), BF16) for _ in mods]
    return pl.pallas_call(
        functools.partial(_norm_mod_kernel, n_mod=len(mods)),
        grid=(nb, l // tl), in_specs=specs, out_specs=[tok] * len(out_shapes), out_shape=out_shapes,
        compiler_params=_params(("parallel", "parallel")),
        name="norm_mod",
    )(*args)


def _ret_log_decay():
    return jnp.log1p(-jnp.exp2(-5.0 - jnp.arange(RET_HEADS, dtype=F32)))


def _retention_tables(c):
    log_g = _ret_log_decay()[:, None, None]
    n = jnp.arange(c, dtype=F32)
    dist = n[:, None] - n[None, :]
    same = (jnp.arange(c)[:, None] // CHUNK) == (jnp.arange(c)[None, :] // CHUNK)
    earlier = (jnp.arange(c)[None, :] // CHUNK) < (jnp.arange(c)[:, None] // CHUNK)
    mask = jnp.where(same[None], jnp.exp(jnp.abs(dist)[None] * log_g),
                     jnp.where(earlier[None], jnp.exp(dist[None] * log_g), 0.0))
    q_decay = jnp.exp((n[None, :, None] + 1.0) * log_g)
    k_decay = jnp.exp((c - 1.0 - n)[None, :, None] * log_g)
    blk_decay = jnp.exp(c * log_g)
    return mask, q_decay, k_decay, blk_decay


def _retention_kernel(*refs, has_s0, hps):
    if has_s0:
        (q_ref, k_ref, v_ref, g_ref, mask_ref, qd_ref, kd_ref, bd_ref, gn_ref, s0_ref,
         y_ref, s_out_ref, s_ref) = refs
    else:
        (q_ref, k_ref, v_ref, g_ref, mask_ref, qd_ref, kd_ref, bd_ref, gn_ref,
         y_ref, s_out_ref, s_ref) = refs
    c = pl.program_id(2)

    @pl.when(c == 0)
    def _():
        if has_s0:
            s_ref[...] = s0_ref[...]
        else:
            s_ref[...] = jnp.zeros_like(s_ref)

    for hh in range(hps):
        qk_cols = slice(hh * RET_DK, (hh + 1) * RET_DK)
        v_cols = slice(hh * RET_DV, (hh + 1) * RET_DV)
        q = q_ref[:, qk_cols]
        k = k_ref[:, qk_cols]
        v = v_ref[:, v_cols]
        s_prev = s_ref[hh]
        scores = _bdot_nt(q, k) * mask_ref[hh]
        y = _bdot(scores, v) + _bdot(q, s_prev) * qd_ref[hh]
        k_scaled = k.astype(F32) * kd_ref[hh]
        s_ref[hh] = bd_ref[hh] * s_prev + _bdot_tn(k_scaled, v)

        mu = jnp.mean(y, axis=-1, keepdims=True)
        yc = y - mu
        var = jnp.mean(yc * yc, axis=-1, keepdims=True)
        yn = yc * lax.rsqrt(var + EPS) * gn_ref[:, v_cols]
        y_ref[:, v_cols] = (_silu(g_ref[:, v_cols].astype(F32)) * yn).astype(y_ref.dtype)

    @pl.when(c == pl.num_programs(2) - 1)
    def _():
        s_out_ref[...] = s_ref[...]


def _retention(q, k, v, g, gn_g, s0, nb, l):
    cb = min(l, 256)
    nc = l // cb
    hps = 2 if nc > 1 else RET_HEADS
    mask, qd, kd, bd = _retention_tables(cb)
    row = lambda b, hg, c: (b * nc + c, hg)
    per_h3 = lambda b, hg, c: (hg, 0, 0)
    in_specs = [
        pl.BlockSpec((cb, hps * RET_DK), row), pl.BlockSpec((cb, hps * RET_DK), row),
        pl.BlockSpec((cb, hps * RET_DV), row), pl.BlockSpec((cb, hps * RET_DV), row),
        pl.BlockSpec((hps, cb, cb), per_h3), pl.BlockSpec((hps, cb, 1), per_h3),
        pl.BlockSpec((hps, cb, 1), per_h3), pl.BlockSpec((hps, 1, 1), per_h3),
        pl.BlockSpec((1, hps * RET_DV), lambda b, hg, c: (0, hg)),
    ]
    args = [q, k, v, g, mask, qd, kd, bd, gn_g]
    state_spec = pl.BlockSpec((None, hps, RET_DK, RET_DV), lambda b, hg, c: (b, hg, 0, 0))
    if s0 is not None:
        in_specs.append(state_spec)
        args.append(s0)
    y, s_new = pl.pallas_call(
        functools.partial(_retention_kernel, has_s0=s0 is not None, hps=hps),
        grid=(nb, RET_HEADS // hps, nc),
        in_specs=in_specs,
        out_specs=[pl.BlockSpec((cb, hps * RET_DV), row), state_spec],
        out_shape=[jax.ShapeDtypeStruct((nb * l, RET_HEADS * RET_DV), BF16),
                   jax.ShapeDtypeStruct((nb, RET_HEADS, RET_DK, RET_DV), F32)],
        scratch_shapes=[pltpu.VMEM((hps, RET_DK, RET_DV), F32)],
        compiler_params=_params(("parallel", "parallel", "arbitrary")),
        name="retention",
    )(*args)
    return y, s_new


def _attn_prompt_kernel(qn_ref, qr_ref, kn_ref, v_ref, kr_ref, o_ref, m_ref, acc_ref, *, tq, tk):
    qi = pl.program_id(1)
    ki = pl.program_id(2)
    exp2_scale = MLA_SCALE * math.log2(math.e)

    @pl.when(ki == 0)
    def _():
        m_ref[...] = jnp.full_like(m_ref, NEG_INF)
        acc_ref[...] = jnp.zeros_like(acc_ref)

    def block(masked):
        if masked:
            q_chunk = (qi * tq + lax.broadcasted_iota(jnp.int32, (tq, tk), 0)) // CHUNK
            k_chunk = (ki * tk + lax.broadcasted_iota(jnp.int32, (tq, tk), 1)) // CHUNK
            visible = k_chunk <= q_chunk
        kr = kr_ref[...]
        ones = jnp.ones((tk, LANES), BF16)

        def head(h, carry):
            q = jnp.concatenate([qn_ref[h], qr_ref[h]], axis=1)
            k = jnp.concatenate([kn_ref[h], kr], axis=1)
            s = _bdot_nt(q, k)
            if masked:
                s = jnp.where(visible, s, NEG_INF)
            m_prev = m_ref[h]
            m_new = jnp.maximum(m_prev, jnp.max(s, axis=-1, keepdims=True))
            alpha = jnp.exp2((m_prev - m_new) * exp2_scale)
            p = jnp.exp2((s - jnp.tile(m_new, (1, tk // LANES))) * exp2_scale)
            v_ext = jnp.concatenate([v_ref[h], ones], axis=1)
            acc_ref[h] = jnp.tile(alpha, (1, 2)) * acc_ref[h] + _bdot(p, v_ext)
            m_ref[h] = m_new
            return carry

        lax.fori_loop(0, MLA_HEADS, head, 0, unroll=True)

    @pl.when(ki < qi)
    def _():
        block(False)

    @pl.when(ki == qi)
    def _():
        block(True)
        for h in range(MLA_HEADS):
            acc = acc_ref[h]
            o_ref[:, h * MLA_VDIM:(h + 1) * MLA_VDIM] = (
                acc[:, :MLA_VDIM] / acc[:, MLA_VDIM:]).astype(o_ref.dtype)


def _attn_prompt(qn, qr, kn, v, kr, nb, l):
    tq = tk = min(l, 512)
    nq = l // tq
    q_spec = pl.BlockSpec((MLA_HEADS, tq, LANES), lambda b, qi, ki: (0, b * nq + qi, 0))
    k_spec = pl.BlockSpec((MLA_HEADS, tk, LANES), lambda b, qi, ki: (0, b * nq + jnp.minimum(ki, qi), 0))
    return pl.pallas_call(
        functools.partial(_attn_prompt_kernel, tq=tq, tk=tk),
        grid=(nb, nq, nq),
        in_specs=[q_spec, q_spec, k_spec, k_spec,
                  pl.BlockSpec((tk, LANES), lambda b, qi, ki: (b * nq + jnp.minimum(ki, qi), 0))],
        out_specs=pl.BlockSpec((tq, MLA_HEADS * MLA_VDIM), lambda b, qi, ki: (b * nq + qi, 0)),
        out_shape=jax.ShapeDtypeStruct((nb * l, MLA_HEADS * MLA_VDIM), BF16),
        scratch_shapes=[pltpu.VMEM((MLA_HEADS, tq, LANES), F32),
                        pltpu.VMEM((MLA_HEADS, tq, MLA_VDIM + LANES), F32)],
        compiler_params=_params(("parallel", "parallel", "arbitrary")),
        name="attn_prompt",
    )(qn, qr, kn, v, kr)


def _attn_cached_kernel(qn_ref, qr_ref, clat_ref, ckr_ref, nlat_ref, nkr_ref, wuk_ref, wuv_ref,
                        o_ref, ql_ref, qrs_ref, *, lq):
    for h in range(MLA_HEADS):
        ql_ref[h * lq:(h + 1) * lq, :] = _bdot_nt(qn_ref[h], wuk_ref[h]).astype(BF16)
        qrs_ref[h * lq:(h + 1) * lq, :] = qr_ref[h]
    ql = ql_ref[...]
    qr = qrs_ref[...]
    clat = clat_ref[...].astype(BF16)
    nlat = nlat_ref[...].astype(BF16)
    s_c = (_bdot_nt(ql, clat) + _bdot_nt(qr, ckr_ref[...])) * MLA_SCALE
    s_n = (_bdot_nt(ql, nlat) + _bdot_nt(qr, nkr_ref[...])) * MLA_SCALE
    m = jnp.maximum(jnp.max(s_c, axis=-1, keepdims=True), jnp.max(s_n, axis=-1, keepdims=True))
    p_c = jnp.exp(s_c - m)
    p_n = jnp.exp(s_n - m)
    denom = jnp.sum(p_c, axis=-1, keepdims=True) + jnp.sum(p_n, axis=-1, keepdims=True)
    o_lat = (_bdot(p_c, clat) + _bdot(p_n, nlat)) / denom
    for h in range(MLA_HEADS):
        o_ref[:, h * MLA_VDIM:(h + 1) * MLA_VDIM] = _bdot(
            o_lat[h * lq:(h + 1) * lq, :], wuv_ref[h]).astype(o_ref.dtype)


def _attn_cached(qn, qr, cache_lat, cache_kr_pad, new_lat, new_kr_pad, wuk_h, wuv_h, nb, lq):
    past = cache_lat.shape[1]
    whole = lambda b: (0, 0, 0)
    return pl.pallas_call(
        functools.partial(_attn_cached_kernel, lq=lq),
        grid=(nb,),
        in_specs=[
            pl.BlockSpec((MLA_HEADS, lq, LANES), lambda b: (0, b, 0)),
            pl.BlockSpec((MLA_HEADS, lq, LANES), lambda b: (0, b, 0)),
            pl.BlockSpec((None, past, KV_LORA), lambda b: (b, 0, 0)),
            pl.BlockSpec((None, past, LANES), lambda b: (b, 0, 0)),
            pl.BlockSpec((lq, KV_LORA), lambda b: (b, 0)),
            pl.BlockSpec((lq, LANES), lambda b: (b, 0)),
            pl.BlockSpec((MLA_HEADS, KV_LORA, MLA_NOPE), whole),
            pl.BlockSpec((MLA_HEADS, KV_LORA, MLA_VDIM), whole),
        ],
        out_specs=pl.BlockSpec((lq, MLA_HEADS * MLA_VDIM), lambda b: (b, 0)),
        out_shape=jax.ShapeDtypeStruct((nb * lq, MLA_HEADS * MLA_VDIM), BF16),
        scratch_shapes=[pltpu.VMEM((MLA_HEADS * lq, KV_LORA), BF16),
                        pltpu.VMEM((MLA_HEADS * lq, LANES), BF16)],
        compiler_params=_params(("parallel",)),
        name="attn_cached",
    )(qn, qr, cache_lat, cache_kr_pad, new_lat, new_kr_pad, wuk_h, wuv_h)


def _top_values(s, top_ref, want_rank):
    rank = jnp.full(s.shape, float(PEER_TOPK), F32) if want_rank else None
    for r in range(PEER_TOPK):
        m = jnp.max(s, axis=0, keepdims=True)
        top_ref[r:r + 1, :] = m
        hit = s == m
        if want_rank:
            rank = jnp.where(hit, float(r), rank)
        s = jnp.where(hit, -jnp.inf, s)
    return rank


_CAND_LIMITS = tuple(PEER_TOPK // (b + 1) for b in range(1, 8))
_HALF_TOPK = PEER_TOPK // 2


def _peer_select_kernel(h_ref, wq_ref, keys_ref, r1_ref, e1_ref, c0_ref, e0_ref, s_ref, a0_ref, a1_ref,
                        *, heads_per_step):
    hx = h_ref[...]
    for hh in range(heads_per_step):
        q_t = _bdot_nt(wq_ref[hh * 2 * N_KEYS:(hh + 1) * 2 * N_KEYS, :], hx)
        s_ref[2 * hh] = _bdot(keys_ref[hh, 0], q_t[:N_KEYS])
        s_ref[2 * hh + 1] = _bdot(keys_ref[hh, 1], q_t[N_KEYS:])
    for hh in range(heads_per_step):
        _peer_select_head(s_ref[2 * hh], s_ref[2 * hh + 1], r1_ref.at[hh], e1_ref.at[hh], c0_ref.at[hh],
                          e0_ref.at[hh], a0_ref.at[hh], a1_ref.at[hh])


def _peer_select_head(s0, s1, r1_ref, e1_ref, c0_ref, e0_ref, a0_ref, a1_ref):
    _top_values(s0, a0_ref, False)
    rank1 = _top_values(s1, a1_ref, True)
    a0 = a0_ref[...]
    a1 = a1_ref[...]
    a0_lo = a0[:_HALF_TOPK, :]
    row = lax.broadcasted_iota(jnp.int32, a0_lo.shape, 0)
    pieces = [a1[0:1, :] + a0]
    for b, limit in enumerate(_CAND_LIMITS, start=1):
        pieces.append(jnp.where(row < limit, a1[b:b + 1, :] + a0_lo, -jnp.inf))
    pieces.append(a1[_HALF_TOPK:, :] + a0[0:1, :])
    cand = jnp.concatenate(pieces, axis=0)
    c = cand
    for r in range(PEER_TOPK):
        tau = jnp.max(c, axis=0, keepdims=True)
        if r + 1 < PEER_TOPK:
            c = jnp.where(c == tau, -jnp.inf, c)
    sel = cand >= tau
    cmax = a0[0:1, :] + a1[0:1, :]
    z = jnp.sum(jnp.where(sel, jnp.exp(cand - cmax), 0.0), axis=0, keepdims=True)
    self32 = sel.astype(F32)
    n_hi = self32[_HALF_TOPK:PEER_TOPK, :]
    n_lo = self32[:_HALF_TOPK, :]
    for b in range(1, 8):
        lo = PEER_TOPK + (b - 1) * _HALF_TOPK
        n_lo = n_lo + self32[lo:lo + _HALF_TOPK, :]
    tail = jnp.sum(self32[PEER_TOPK + 7 * _HALF_TOPK:, :], axis=0, keepdims=True)
    n_lo = n_lo + jnp.where(row == 0, tail, 0.0)
    count0 = jnp.zeros(s0.shape, F32)
    for a in range(PEER_TOPK):
        n_a = n_lo[a:a + 1, :] if a < _HALF_TOPK else n_hi[a - _HALF_TOPK:a - _HALF_TOPK + 1, :]
        count0 = jnp.where(s0 == a0[a:a + 1, :], n_a, count0)
    r1_ref[...] = rank1.astype(r1_ref.dtype)
    c0_ref[...] = count0
    e0_ref[...] = jnp.exp(s0 - a0[0:1, :]) / z
    e1_ref[...] = jnp.exp(s1 - a1[0:1, :]).astype(e1_ref.dtype)


def _peer_select(hx, wq_t, keys, layer, tt):
    t, d = hx.shape
    hps = SELECT_HEADS_PER_STEP
    o_spec = pl.BlockSpec((hps, N_KEYS, tt), lambda i, g: (g, 0, i))
    return pl.pallas_call(
        functools.partial(_peer_select_kernel, heads_per_step=hps),
        grid=(t // tt, PEER_HEADS // hps),
        in_specs=[pl.BlockSpec((tt, d), lambda i, g: (i, 0)),
                  pl.BlockSpec((hps * 2 * N_KEYS, d), lambda i, g: (g, 0)),
                  pl.BlockSpec((None, hps, 2, N_KEYS, N_KEYS), lambda i, g: (layer, g, 0, 0, 0))],
        out_specs=[o_spec] * 4,
        out_shape=[jax.ShapeDtypeStruct((PEER_HEADS, N_KEYS, t), dt) for dt in (BF16, BF16, F32, F32)],
        scratch_shapes=[pltpu.VMEM((2 * hps, N_KEYS, tt), F32),
                        pltpu.VMEM((hps, PEER_TOPK, tt), F32), pltpu.VMEM((hps, PEER_TOPK, tt), F32)],
        compiler_params=_params(("parallel", "arbitrary")),
        name="peer_select",
    )(hx, wq_t, keys)


def _sublane_bcast_bf16(row, rows):
    tile_rows = 16
    packed = jnp.broadcast_to(row, (tile_rows, row.shape[1])).astype(BF16)
    return jnp.tile(packed, (rows // tile_rows, 1))


def _peer_mix_kernel(x_ref, u_ref, vt_ref, r1_ref, e1_ref, c0_ref, e0_ref, res_ref, gate_ref, o_ref,
                     acc_ref, a_ref, g_ref, p_ref, *, n_i):
    j = pl.program_id(1)

    @pl.when(j == 0)
    def _():
        acc_ref[...] = jnp.zeros_like(acc_ref)

    x = x_ref[...]
    rows_per_chunk = 2 * N_KEYS
    for c in range(n_i // 2):
        lo = c * rows_per_chunk
        a_ref[c] = _bdot_nt(u_ref[lo:lo + rows_per_chunk, :], x)
    zero = jnp.zeros((), BF16)
    for ii in range(n_i):
        w = None
        for h in range(PEER_HEADS):
            count = _sublane_bcast_bf16(c0_ref[h, ii:ii + 1, :], N_KEYS)
            e0 = _sublane_bcast_bf16(e0_ref[h, ii:ii + 1, :], N_KEYS)
            term = jnp.where(r1_ref[h] < count, e1_ref[h] * e0, zero)
            w = term if w is None else w + term
        g_ref[ii] = w
    for c in range(n_i // 2):
        lo = c * rows_per_chunk
        gates = jnp.concatenate([g_ref[2 * c], g_ref[2 * c + 1]], axis=0)
        p_ref[lo:lo + rows_per_chunk, :] = gates * _gelu(a_ref[c]).astype(BF16)
    acc_ref[...] += jnp.dot(vt_ref[...], p_ref[...], preferred_element_type=F32)

    @pl.when(j == pl.num_programs(1) - 1)
    def _():
        o_ref[...] = res_ref[...] + (1.0 + gate_ref[...]) * acc_ref[...].T


def _peer_mix(hx, u_b, vt_b, layer, r1, e1, c0, e0, x_res, gate, l, tt, et):
    t, d = hx.shape
    e = u_b.shape[1]
    n_i = et // N_KEYS
    tok_all = pl.BlockSpec((PEER_HEADS, N_KEYS, tt), lambda i, j: (0, 0, i))
    tok_i = pl.BlockSpec((PEER_HEADS, n_i, tt), lambda i, j: (0, j, i))
    gate_arr, gate_spec = _gate_operand(gate, l, tt, d)
    return pl.pallas_call(
        functools.partial(_peer_mix_kernel, n_i=n_i),
        grid=(t // tt, e // et),
        in_specs=[pl.BlockSpec((tt, d), lambda i, j: (i, 0)),
                  pl.BlockSpec((None, et, d), lambda i, j: (layer, j, 0)),
                  pl.BlockSpec((None, d, et), lambda i, j: (layer, 0, j)),
                  tok_all, tok_all, tok_i, tok_i,
                  pl.BlockSpec((tt, d), lambda i, j: (i, 0)), gate_spec],
        out_specs=pl.BlockSpec((tt, d), lambda i, j: (i, 0)),
        out_shape=jax.ShapeDtypeStruct((t, d), F32),
        scratch_shapes=[pltpu.VMEM((d, tt), F32),
                        pltpu.VMEM((n_i // 2, 2 * N_KEYS, tt), F32),
                        pltpu.VMEM((n_i, N_KEYS, tt), BF16),
                        pltpu.VMEM((et, tt), BF16)],
        compiler_params=_params(("parallel", "arbitrary"), vmem=PEER_MIX_VMEM_LIMIT),
        name="peer_mix",
    )(hx, u_b, vt_b, r1, e1, c0, e0, x_res, gate_arr)


def _peer(hx, wq_t, keys, layer, u_b, vt_b, x_res, gate, l):
    t = hx.shape[0]
    tt = min(t, 512)
    r1, e1, c0, e0 = _peer_select(hx, wq_t, keys, layer, tt)
    return _peer_mix(hx, u_b, vt_b, layer, r1, e1, c0, e0, x_res, gate, l, tt, 1024)


def _rope_tables(pos, half, reps):
    inv = ROPE_THETA ** (-jnp.arange(half, dtype=F32) / half)
    ang = pos.astype(F32)[:, None] * inv[None, :]
    return jnp.tile(jnp.cos(ang), (1, reps)), jnp.tile(jnp.sin(ang), (1, reps))


def _rotate_half_cols(w):
    half = w.shape[-1] // 2
    return jnp.concatenate([-w[..., half:], w[..., :half]], axis=-1)


def _pad_last(w, width):
    return jnp.pad(w, [(0, 0)] * (w.ndim - 1) + [(0, width - w.shape[-1])])


def _shared_weights(p):
    out = {}
    w_rope = p["mla_w_dkv"][:, KV_LORA:]
    out["w_dkv_ext"] = jnp.concatenate(
        [p["mla_w_dkv"][:, :KV_LORA], _pad_last(w_rope, LANES), _pad_last(_rotate_half_cols(w_rope), LANES)],
        axis=1)
    w_uq = p["mla_w_uq"][0].reshape(-1, MLA_HEADS, MLA_NOPE + MLA_ROPE)
    q_lora = w_uq.shape[0]
    out["w_uq_nope"] = w_uq[:, :, :MLA_NOPE].reshape(q_lora, MLA_HEADS * MLA_NOPE)
    w_qr = w_uq[:, :, MLA_NOPE:]
    out["w_uq_rope"] = _pad_last(w_qr, LANES).reshape(q_lora, MLA_HEADS * LANES)
    out["w_uq_rot"] = _pad_last(_rotate_half_cols(w_qr), LANES).reshape(q_lora, MLA_HEADS * LANES)
    out["w_uk_flat"] = p["mla_w_uk"].reshape(KV_LORA, MLA_HEADS * MLA_NOPE)
    out["w_uv_flat"] = p["mla_w_uv"].reshape(KV_LORA, MLA_HEADS * MLA_VDIM)
    out["w_uk_h"] = jnp.transpose(p["mla_w_uk"], (1, 0, 2))
    out["w_uv_h"] = jnp.transpose(p["mla_w_uv"], (1, 0, 2))
    out["peer_wq_t"] = [p["peer_w_q"][l].T for l in range(2)]
    out["peer_u_b"] = p["peer_u"].astype(BF16)
    out["peer_vt_b"] = jnp.swapaxes(p["peer_v"].astype(BF16), 1, 2)
    return out


def _adaln(c_all, w, b, lead, n_out):
    m, d = c_all.shape
    tn = 512
    if w.ndim == 2:
        w = w[None]
        b = b[None]
    b3 = b.reshape(b.shape[0], 1, n_out)
    return _linear(
        c_all, [w], n_cols=n_out, tm=m, tn=tn, epilogue=_ep_bias, prologue=_silu, w_lead=lead,
        extras=[b3], extra_specs=[pl.BlockSpec((None, 1, tn), lambda i, j: (lead, 0, j))],
        out_shapes=[jax.ShapeDtypeStruct((m, n_out), F32)],
        out_specs=[pl.BlockSpec((m, tn), lambda i, j: (0, j))])[0]


def _trunk(x, pos, mods, kv_mod, p, sw, ret_state, cache_lat, cache_kr):
    nb, l, d = x.shape
    t = nb * l
    pos_rows = jnp.tile(pos, nb) if l < 256 else pos
    table_rows = pos_rows.shape[0]
    assert table_rows % _row_tile(t, d) == 0, "a row tile must cover whole sequences or divide one"

    def vec(a):
        return a.reshape(nb, 1, d)

    def flat(a):
        return a.reshape(t, a.shape[-1])

    sh1, sc1, gt1, sh2, sc2, gt2 = [vec(m) for m in mods[0]]
    (h,) = _norm_mod(x, mods=[(p["norm_g"][0, 0][None], sh1, sc1)])
    h = flat(h)
    tm = _row_tile(t, d)
    cos_r, sin_r = _rope_tables(pos_rows, RET_DK // 2, 1)
    tspec = _table_spec(tm, table_rows)
    w_in = p["ret_w_in"]
    qk_cols = RET_HEADS * RET_DK
    v_cols = RET_HEADS * RET_DV
    tn = _col_tile(d, qk_cols)

    def rope_proj(col0, scale):
        return _linear(
            h, [w_in], n_cols=qk_cols, tm=tm, tn=tn, w_lead=0, col_block0=col0 // tn,
            epilogue=functools.partial(_ep_rope_half128, scale=scale),
            extras=[cos_r, sin_r], extra_specs=[tspec, tspec],
            out_shapes=[jax.ShapeDtypeStruct((t, qk_cols), BF16)],
            out_specs=[pl.BlockSpec((tm, tn), lambda i, j: (i, j))])[0]

    def plain_proj(x_in, w, n_cols, col0=0, lead=None, dtype=BF16, tm_=None):
        tm_ = tm_ or _row_tile(x_in.shape[0], x_in.shape[1])
        tn_ = _col_tile(x_in.shape[1], n_cols)
        return _linear(
            x_in, [w], n_cols=n_cols, tm=tm_, tn=tn_, w_lead=lead, col_block0=col0 // tn_,
            epilogue=_ep_plain,
            out_shapes=[jax.ShapeDtypeStruct((x_in.shape[0], n_cols), dtype)],
            out_specs=[pl.BlockSpec((tm_, tn_), lambda i, j: (i, j))])[0]

    def resid_proj(x_in, w, x_res, gate):
        tm_ = _row_tile(x_in.shape[0], x_in.shape[1])
        tn_ = _col_tile(x_in.shape[1], d)
        gate_arr, gate_spec = _gate_operand(gate, l, tm_, tn_)
        tile = pl.BlockSpec((tm_, tn_), lambda i, j: (i, j))
        return _linear(
            x_in, [w], n_cols=d, tm=tm_, tn=tn_, w_lead=0, epilogue=_ep_resid,
            extras=[x_res, gate_arr], extra_specs=[tile, gate_spec],
            out_shapes=[jax.ShapeDtypeStruct((t, d), F32)], out_specs=[tile])[0]

    def heads_proj(x_in, ws, n_cols, epilogue=_ep_heads, extras=(), extra_specs=()):
        tm_ = _row_tile(x_in.shape[0], x_in.shape[1])
        tn_ = _col_tile(x_in.shape[1], n_cols, len(ws))
        return _linear(
            x_in, ws, n_cols=n_cols, tm=tm_, tn=tn_, epilogue=epilogue, extras=extras, extra_specs=extra_specs,
            out_shapes=[jax.ShapeDtypeStruct((n_cols // LANES, x_in.shape[0], LANES), BF16)],
            out_specs=[pl.BlockSpec((tn_ // LANES, tm_, LANES), lambda i, j: (j, i, 0))])[0]

    q_r = rope_proj(0, 1.0)
    k_r = rope_proj(qk_cols, RET_DK ** -0.5)
    v_r = plain_proj(h, w_in, v_cols, col0=2 * qk_cols, lead=0)
    g_r = plain_proj(h, w_in, v_cols, col0=2 * qk_cols + v_cols, lead=0)
    s0 = None if ret_state is None else ret_state[0]
    y_r, s_new = _retention(q_r, k_r, v_r, g_r, p["ret_gn_g"], s0, nb, l)
    xf = resid_proj(y_r, p["ret_w_out"], flat(x), gt1)
    (h,) = _norm_mod(xf.reshape(nb, l, d), mods=[(p["norm_g"][0, 1][None], sh2, sc2)])
    xf = _peer(flat(h), sw["peer_wq_t"][0], p["peer_keys"], 0, sw["peer_u_b"], sw["peer_vt_b"],
               xf, gt2, l)

    sh1, sc1, gt1b, sh2, sc2, gt2b = [vec(m) for m in mods[1]]
    kv_sh, kv_sc = [vec(m) for m in kv_mod]
    h, h_kv = _norm_mod(xf.reshape(nb, l, d),
                         mods=[(p["norm_g"][1, 0][None], sh1, sc1), (p["kv_norm_g"][None], kv_sh, kv_sc)])
    h = flat(h)
    h_kv = flat(h_kv)
    cos_m, sin_m = _rope_tables(pos_rows, MLA_ROPE // 2, LANES // (MLA_ROPE // 2))
    n_ext = sw["w_dkv_ext"].shape[1]
    lat, kr, kr_pad = _linear(
        h_kv, [sw["w_dkv_ext"]], n_cols=n_ext, tm=tm, tn=n_ext, epilogue=_ep_kv,
        extras=[p["mla_kv_norm_g"][None], cos_m, sin_m],
        extra_specs=[pl.BlockSpec((1, KV_LORA), lambda i, j: (0, 0)), tspec, tspec],
        out_shapes=[jax.ShapeDtypeStruct((t, KV_LORA), F32), jax.ShapeDtypeStruct((t, MLA_ROPE), F32),
                    jax.ShapeDtypeStruct((t, LANES), BF16)],
        out_specs=[pl.BlockSpec((tm, KV_LORA), lambda i, j: (i, 0)),
                   pl.BlockSpec((tm, MLA_ROPE), lambda i, j: (i, 0)),
                   pl.BlockSpec((tm, LANES), lambda i, j: (i, 0))])
    q_lora = p["mla_w_dq"].shape[-1]
    cq = _linear(
        h, [p["mla_w_dq"]], n_cols=q_lora, tm=tm, tn=q_lora, w_lead=0, epilogue=_ep_rms,
        extras=[p["mla_q_norm_g"]], extra_specs=[pl.BlockSpec((1, q_lora), lambda i, j: (0, 0))],
        out_shapes=[jax.ShapeDtypeStruct((t, q_lora), BF16)],
        out_specs=[pl.BlockSpec((tm, q_lora), lambda i, j: (i, 0))])[0]
    hl = MLA_HEADS * LANES
    qn = heads_proj(cq, [sw["w_uq_nope"]], hl)
    qr = heads_proj(cq, [sw["w_uq_rope"], sw["w_uq_rot"]], hl, epilogue=_ep_rot_heads,
                    extras=[cos_m, sin_m], extra_specs=[tspec, tspec])
    if cache_lat is None:
        kn = heads_proj(lat, [sw["w_uk_flat"]], hl)
        vv = heads_proj(lat, [sw["w_uv_flat"]], hl)
        o = _attn_prompt(qn, qr, kn, vv, kr_pad, nb, l)
    else:
        ckr = _pad_last(cache_kr, LANES).astype(BF16)
        o = _attn_cached(qn, qr, cache_lat, ckr, lat, kr_pad, sw["w_uk_h"], sw["w_uv_h"], nb, l)
    xf = resid_proj(o, p["mla_w_o"], xf, gt1b)
    (h,) = _norm_mod(xf.reshape(nb, l, d), mods=[(p["norm_g"][1, 1][None], sh2, sc2)])
    xf = _peer(flat(h), sw["peer_wq_t"][1], p["peer_keys"], 1, sw["peer_u_b"], sw["peer_vt_b"],
               xf, gt2b, l)
    (y,) = _norm_mod(xf.reshape(nb, l, d), final_g=p["final_g"][None])
    return y, s_new[None], lat.reshape(nb, l, KV_LORA), kr.reshape(nb, l, MLA_ROPE)


def kernel(x_prompt, x_sample, c_prompt, c_sample, state_retention, cache_mla_latent, cache_mla_krope,
           ada_w, ada_b, norm_g, ret_w_in, ret_gn_g, ret_w_out,
           kv_ada_w, kv_ada_b, kv_norm_g, mla_w_dkv, mla_kv_norm_g, mla_w_uk, mla_w_uv,
           mla_w_dq, mla_q_norm_g, mla_w_uq, mla_w_o,
           peer_w_q, peer_keys, peer_u, peer_v, final_g):
    p = dict(norm_g=norm_g, ret_w_in=ret_w_in, ret_gn_g=ret_gn_g[0][None], ret_w_out=ret_w_out,
             kv_norm_g=kv_norm_g, mla_w_dkv=mla_w_dkv, mla_kv_norm_g=mla_kv_norm_g,
             mla_w_uk=mla_w_uk, mla_w_uv=mla_w_uv, mla_w_dq=mla_w_dq, mla_q_norm_g=mla_q_norm_g,
             mla_w_uq=mla_w_uq, mla_w_o=mla_w_o, peer_w_q=peer_w_q, peer_keys=peer_keys,
             peer_u=peer_u, peer_v=peer_v, final_g=final_g)
    sw = _shared_weights(p)
    nbp, lp, d = x_prompt.shape
    nbs, ls, _ = x_sample.shape
    past = cache_mla_latent.shape[1]

    n_c = nbp + nbs
    c_all = jnp.pad(jnp.concatenate([c_prompt, c_sample], axis=0), ((0, (-n_c) % 8), (0, 0)))
    layer_mods = [_adaln(c_all, ada_w, ada_b, layer, 6 * d) for layer in range(2)]
    kv_mods = _adaln(c_all, kv_ada_w, kv_ada_b, 0, 2 * d)

    def stream_mods(lo, hi):
        return ([jnp.split(m[lo:hi], 6, axis=-1) for m in layer_mods], jnp.split(kv_mods[lo:hi], 2, axis=-1))

    mods_p, kv_p = stream_mods(0, nbp)
    mods_s, kv_s = stream_mods(nbp, n_c)
    pos_p = jnp.arange(lp, dtype=jnp.int32)
    pos_s = past + jnp.arange(ls, dtype=jnp.int32)
    y_p, ret_p, lat_p, kr_p = _trunk(x_prompt, pos_p, mods_p, kv_p, p, sw, None, None, None)
    y_s, ret_s, lat_s, kr_s = _trunk(x_sample, pos_s, mods_s, kv_s, p, sw, state_retention,
                                     cache_mla_latent, cache_mla_krope)
    return (y_p, y_s, ret_p, ret_s, lat_p, kr_p, lat_s, kr_s)
```

```python
import functools
import math

import jax
import jax.numpy as jnp
from jax import lax
from jax.experimental import pallas as pl
from jax.experimental.pallas import tpu as pltpu

F32 = jnp.float32
BF16 = jnp.bfloat16

EPS = 1e-6
ROPE_THETA = 10000.0
NEG_INF = -1e30
CHUNK = 64

RET_HEADS = 8
RET_DK = 256
RET_DV = 512

MLA_HEADS = 16
MLA_NOPE = 128
MLA_ROPE = 64
MLA_VDIM = 128
KV_LORA = 512
MLA_SCALE = (MLA_NOPE + MLA_ROPE) ** -0.5

PEER_HEADS = 8
N_KEYS = 128
PEER_TOPK = 16

LANES = 128
VMEM_LIMIT = 52 * 1024 * 1024
PEER_MIX_VMEM_LIMIT = 58 * 1024 * 1024
SELECT_HEADS_PER_STEP = 4


def _params(sem, vmem=VMEM_LIMIT, flags=None):
    return pltpu.CompilerParams(dimension_semantics=sem, vmem_limit_bytes=vmem, flags=flags)


def _bdot(a, b):
    return jnp.dot(a.astype(BF16), b.astype(BF16), preferred_element_type=F32)


def _bdot_nt(a, b):
    return lax.dot_general(a.astype(BF16), b.astype(BF16), (((1,), (1,)), ((), ())),
                           preferred_element_type=F32)


def _bdot_tn(a, b):
    return lax.dot_general(a.astype(BF16), b.astype(BF16), (((0,), (0,)), ((), ())),
                           preferred_element_type=F32)


def _silu(x):
    return x * (1.0 / (1.0 + jnp.exp(-x)))


def _gelu(x):
    return 0.5 * x * (1.0 + lax.erf(x * (0.5 ** 0.5)))


def _linear_kernel(*refs, n_w, n_extra, prologue, epilogue):
    x_ref = refs[0]
    w_refs = refs[1:1 + n_w]
    extra_refs = refs[1 + n_w:1 + n_w + n_extra]
    out_refs = refs[1 + n_w + n_extra:]
    xv = x_ref[...]
    if prologue is not None:
        xv = prologue(xv)
    xb = xv.astype(BF16)
    accs = [jnp.dot(xb, w[...].astype(BF16), preferred_element_type=F32) for w in w_refs]
    epilogue(accs, extra_refs, out_refs)


def _linear(x, ws, *, n_cols, tm, tn, epilogue, out_shapes, out_specs, w_lead=None, col_block0=0,
            extras=(), extra_specs=(), prologue=None, name=None):
    m, k = x.shape
    if name is None:
        name = "linear" + getattr(epilogue, "func", epilogue).__name__
    assert m % tm == 0 and n_cols % tn == 0
    if w_lead is None:
        w_spec = pl.BlockSpec((k, tn), lambda i, j: (0, j + col_block0))
    else:
        w_spec = pl.BlockSpec((None, k, tn), lambda i, j: (w_lead, 0, j + col_block0))
    kern = functools.partial(_linear_kernel, n_w=len(ws), n_extra=len(extras), prologue=prologue,
                             epilogue=epilogue)
    return pl.pallas_call(
        kern,
        grid=(m // tm, n_cols // tn),
        in_specs=[pl.BlockSpec((tm, k), lambda i, j: (i, 0))] + [w_spec] * len(ws) + list(extra_specs),
        out_specs=out_specs,
        out_shape=out_shapes,
        compiler_params=_params(("parallel", "arbitrary")),
        name=name,
    )(x, *ws, *extras)


def _ep_plain(accs, extras, outs):
    outs[0][...] = accs[0].astype(outs[0].dtype)


def _ep_heads(accs, extras, outs):
    acc = accs[0]
    for jj in range(acc.shape[1] // LANES):
        outs[0][jj] = acc[:, jj * LANES:(jj + 1) * LANES].astype(outs[0].dtype)


def _ep_bias(accs, extras, outs):
    outs[0][...] = accs[0] + extras[0][...]


def _ep_resid(accs, extras, outs):
    outs[0][...] = extras[0][...] + (1.0 + extras[1][...]) * accs[0]


def _gate_operand(gate, l, rows_per_tile, cols_per_tile):
    nb, _, d = gate.shape
    col = (lambda j: j) if cols_per_tile < d else (lambda j: 0)
    if l % rows_per_tile == 0:
        tiles_per_batch = l // rows_per_tile
        return gate, pl.BlockSpec((None, 1, cols_per_tile), lambda i, j: (i // tiles_per_batch, 0, col(j)))
    rows = jnp.broadcast_to(gate, (nb, l, d)).reshape(nb * l, d)
    return rows, pl.BlockSpec((rows_per_tile, cols_per_tile), lambda i, j: (i, col(j)))


def _ep_rope_half128(accs, extras, outs, *, scale):
    acc = accs[0]
    cos = extras[0][...]
    sin = extras[1][...]
    for g in range(acc.shape[1] // 256):
        x1 = acc[:, g * 256:g * 256 + 128]
        x2 = acc[:, g * 256 + 128:(g + 1) * 256]
        outs[0][:, g * 256:g * 256 + 128] = ((x1 * cos - x2 * sin) * scale).astype(outs[0].dtype)
        outs[0][:, g * 256 + 128:(g + 1) * 256] = ((x1 * sin + x2 * cos) * scale).astype(outs[0].dtype)


def _ep_rot_heads(accs, extras, outs):
    cos = extras[0][...]
    sin = extras[1][...]
    a, b = accs
    for jj in range(a.shape[1] // LANES):
        sl = slice(jj * LANES, (jj + 1) * LANES)
        outs[0][jj] = (a[:, sl] * cos + b[:, sl] * sin).astype(outs[0].dtype)


def _ep_rms(accs, extras, outs):
    acc = accs[0]
    g = extras[0][...]
    y = acc * lax.rsqrt(jnp.mean(acc * acc, axis=-1, keepdims=True) + EPS) * g
    outs[0][...] = y.astype(outs[0].dtype)


def _ep_kv(accs, extras, outs):
    acc = accs[0]
    g = extras[0][...]
    cos = extras[1][...]
    sin = extras[2][...]
    c = acc[:, :KV_LORA]
    outs[0][...] = c * lax.rsqrt(jnp.mean(c * c, axis=-1, keepdims=True) + EPS) * g
    kr = acc[:, KV_LORA:KV_LORA + LANES] * cos + acc[:, KV_LORA + LANES:KV_LORA + 2 * LANES] * sin
    outs[1][...] = kr[:, :MLA_ROPE]
    outs[2][...] = kr.astype(BF16)


def _row_tile(m, k):
    return min(m, 1024)


def _col_tile(k, n, n_weights=1):
    for tn in (1024, 512):
        if n % tn == 0 and 2 * n_weights * k * tn * 4 <= 16 * 1024 * 1024:
            return tn
    return min(n, 512)


def _table_spec(tm, table_rows):
    nblk = table_rows // tm
    return pl.BlockSpec((tm, LANES), lambda i, j: (i % nblk, 0))


def _norm_mod_kernel(*refs, n_mod):
    x = refs[0][...]
    xn = x * lax.rsqrt(jnp.mean(x * x, axis=-1, keepdims=True) + EPS)
    if n_mod == 0:
        refs[2][...] = xn * refs[1][...]
        return
    outs = refs[1 + 3 * n_mod:]
    for m in range(n_mod):
        g, sh, sc = (refs[1 + 3 * m + k][...] for k in range(3))
        outs[m][...] = ((xn * g) * (1.0 + sc) + sh).astype(outs[m].dtype)


def _norm_mod(x, mods=(), final_g=None):
    nb, l, d = x.shape
    tl = min(l, 512)
    tok = pl.BlockSpec((None, tl, d), lambda b, i: (b, i, 0))
    per_b = pl.BlockSpec((None, 1, d), lambda b, i: (b, 0, 0))
    gain = pl.BlockSpec((1, d), lambda b, i: (0, 0))
    args, specs = [x], [tok]
    if final_g is not None:
        assert not mods
        args.append(final_g)
        specs.append(gain)
        out_shapes = [jax.ShapeDtypeStruct((nb, l, d), F32)]
    else:
        for g, sh, sc in mods:
            args += [g, sh, sc]
            specs += [gain, per_b, per_b]
        out_shapes = [jax.ShapeDtypeStruct((nb, l, d), BF16) for _ in mods]
    return pl.pallas_call(
        functools.partial(_norm_mod_kernel, n_mod=len(mods)),
        grid=(nb, l // tl), in_specs=specs, out_specs=[tok] * len(out_shapes), out_shape=out_shapes,
        compiler_params=_params(("parallel", "parallel")),
        name="norm_mod",
    )(*args)


def _ret_log_decay():
    return jnp.log1p(-jnp.exp2(-5.0 - jnp.arange(RET_HEADS, dtype=F32)))


def _retention_tables(c):
    log_g = _ret_log_decay()[:, None, None]
    n = jnp.arange(c, dtype=F32)
    dist = n[:, None] - n[None, :]
    same = (jnp.arange(c)[:, None] // CHUNK) == (jnp.arange(c)[None, :] // CHUNK)
    earlier = (jnp.arange(c)[None, :] // CHUNK) < (jnp.arange(c)[:, None] // CHUNK)
    mask = jnp.where(same[None], jnp.exp(jnp.abs(dist)[None] * log_g),
                     jnp.where(earlier[None], jnp.exp(dist[None] * log_g), 0.0))
    q_decay = jnp.exp((n[None, :, None] + 1.0) * log_g)
    k_decay = jnp.exp((c - 1.0 - n)[None, :, None] * log_g)
    blk_decay = jnp.exp(c * log_g)
    return mask, q_decay, k_decay, blk_decay


def _retention_kernel(*refs, has_s0, hps):
    if has_s0:
        (q_ref, k_ref, v_ref, g_ref, mask_ref, qd_ref, kd_ref, bd_ref, gn_ref, s0_ref,
         y_ref, s_out_ref, s_ref) = refs
    else:
        (q_ref, k_ref, v_ref, g_ref, mask_ref, qd_ref, kd_ref, bd_ref, gn_ref,
         y_ref, s_out_ref, s_ref) = refs
    c = pl.program_id(2)

    @pl.when(c == 0)
    def _():
        if has_s0:
            s_ref[...] = s0_ref[...]
        else:
            s_ref[...] = jnp.zeros_like(s_ref)

    for hh in range(hps):
        qk_cols = slice(hh * RET_DK, (hh + 1) * RET_DK)
        v_cols = slice(hh * RET_DV, (hh + 1) * RET_DV)
        q = q_ref[:, qk_cols]
        k = k_ref[:, qk_cols]
        v = v_ref[:, v_cols]
        s_prev = s_ref[hh]
        scores = _bdot_nt(q, k) * mask_ref[hh]
        y = _bdot(scores, v) + _bdot(q, s_prev) * qd_ref[hh]
        k_scaled = k.astype(F32) * kd_ref[hh]
        s_ref[hh] = bd_ref[hh] * s_prev + _bdot_tn(k_scaled, v)

        mu = jnp.mean(y, axis=-1, keepdims=True)
        yc = y - mu
        var = jnp.mean(yc * yc, axis=-1, keepdims=True)
        yn = yc * lax.rsqrt(var + EPS) * gn_ref[:, v_cols]
        y_ref[:, v_cols] = (_silu(g_ref[:, v_cols].astype(F32)) * yn).astype(y_ref.dtype)

    @pl.when(c == pl.num_programs(2) - 1)
    def _():
        s_out_ref[...] = s_ref[...]


def _retention(q, k, v, g, gn_g, s0, nb, l):
    cb = min(l, 256)
    nc = l // cb
    hps = 4 if nc > 1 else RET_HEADS
    mask, qd, kd, bd = _retention_tables(cb)
    row = lambda b, hg, c: (b * nc + c, hg)
    per_h3 = lambda b, hg, c: (hg, 0, 0)
    in_specs = [
        pl.BlockSpec((cb, hps * RET_DK), row), pl.BlockSpec((cb, hps * RET_DK), row),
        pl.BlockSpec((cb, hps * RET_DV), row), pl.BlockSpec((cb, hps * RET_DV), row),
        pl.BlockSpec((hps, cb, cb), per_h3), pl.BlockSpec((hps, cb, 1), per_h3),
        pl.BlockSpec((hps, cb, 1), per_h3), pl.BlockSpec((hps, 1, 1), per_h3),
        pl.BlockSpec((1, hps * RET_DV), lambda b, hg, c: (0, hg)),
    ]
    args = [q, k, v, g, mask, qd, kd, bd, gn_g]
    state_spec = pl.BlockSpec((None, hps, RET_DK, RET_DV), lambda b, hg, c: (b, hg, 0, 0))
    if s0 is not None:
        in_specs.append(state_spec)
        args.append(s0)
    y, s_new = pl.pallas_call(
        functools.partial(_retention_kernel, has_s0=s0 is not None, hps=hps),
        grid=(nb, RET_HEADS // hps, nc),
        in_specs=in_specs,
        out_specs=[pl.BlockSpec((cb, hps * RET_DV), row), state_spec],
        out_shape=[jax.ShapeDtypeStruct((nb * l, RET_HEADS * RET_DV), BF16),
                   jax.ShapeDtypeStruct((nb, RET_HEADS, RET_DK, RET_DV), F32)],
        scratch_shapes=[pltpu.VMEM((hps, RET_DK, RET_DV), F32)],
        compiler_params=_params(("parallel", "parallel", "arbitrary")),
        name="retention",
    )(*args)
    return y, s_new


def _attn_prompt_kernel(qi_ref, ki_ref, qn_ref, qr_ref, kn_ref, v_ref, kr_ref, o_ref, m_ref, acc_ref, *, tq, tk):
    qi = qi_ref[pl.program_id(1)]
    ki = ki_ref[pl.program_id(1)]
    exp2_scale = MLA_SCALE * math.log2(math.e)

    @pl.when(ki == 0)
    def _():
        m_ref[...] = jnp.full_like(m_ref, NEG_INF)
        acc_ref[...] = jnp.zeros_like(acc_ref)

    def block(masked):
        if masked:
            q_chunk = (qi * tq + lax.broadcasted_iota(jnp.int32, (tq, tk), 0)) // CHUNK
            k_chunk = (ki * tk + lax.broadcasted_iota(jnp.int32, (tq, tk), 1)) // CHUNK
            visible = k_chunk <= q_chunk
        kr = kr_ref[...]
        ones = jnp.ones((tk, LANES), BF16)

        def head(h, carry):
            q = jnp.concatenate([qn_ref[h], qr_ref[h]], axis=1)
            k = jnp.concatenate([kn_ref[h], kr], axis=1)
            s = _bdot_nt(q, k)
            if masked:
                s = jnp.where(visible, s, NEG_INF)
            m_prev = m_ref[h]
            m_new = jnp.maximum(m_prev, jnp.max(s, axis=-1, keepdims=True))
            alpha = jnp.exp2((m_prev - m_new) * exp2_scale)
            p = jnp.exp2((s - jnp.tile(m_new, (1, tk // LANES))) * exp2_scale)
            v_ext = jnp.concatenate([v_ref[h], ones], axis=1)
            acc_ref[h] = jnp.tile(alpha, (1, 2)) * acc_ref[h] + _bdot(p, v_ext)
            m_ref[h] = m_new
            return carry

        lax.fori_loop(0, MLA_HEADS, head, 0, unroll=True)

    @pl.when(ki < qi)
    def _():
        block(False)

    @pl.when(ki == qi)
    def _():
        block(True)
        for h in range(MLA_HEADS):
            acc = acc_ref[h]
            o_ref[:, h * MLA_VDIM:(h + 1) * MLA_VDIM] = (
                acc[:, :MLA_VDIM] / acc[:, MLA_VDIM:]).astype(o_ref.dtype)


def _attn_prompt(qn, qr, kv, kr, nb, l):
    tq = tk = min(l, 512)
    nq = l // tq
    pairs = [(qi, ki) for qi in range(nq) for ki in range(qi + 1)]
    qi_of = jnp.asarray([qk[0] for qk in pairs], jnp.int32)
    ki_of = jnp.asarray([qk[1] for qk in pairs], jnp.int32)
    q_spec = pl.BlockSpec((MLA_HEADS, tq, LANES), lambda b, s, qi, ki: (0, b * nq + qi[s], 0))
    k_spec = pl.BlockSpec((MLA_HEADS, tk, LANES), lambda b, s, qi, ki: (0, b * nq + ki[s], 0))
    v_spec = pl.BlockSpec((MLA_HEADS, tk, LANES), lambda b, s, qi, ki: (1, b * nq + ki[s], 0))
    return pl.pallas_call(
        functools.partial(_attn_prompt_kernel, tq=tq, tk=tk),
        grid_spec=pltpu.PrefetchScalarGridSpec(
            num_scalar_prefetch=2,
            grid=(nb, len(pairs)),
            in_specs=[q_spec, q_spec, k_spec, v_spec,
                      pl.BlockSpec((tk, LANES), lambda b, s, qi, ki: (b * nq + ki[s], 0))],
            out_specs=pl.BlockSpec((tq, MLA_HEADS * MLA_VDIM), lambda b, s, qi, ki: (b * nq + qi[s], 0)),
            scratch_shapes=[pltpu.VMEM((MLA_HEADS, tq, LANES), F32),
                            pltpu.VMEM((MLA_HEADS, tq, MLA_VDIM + LANES), F32)]),
        out_shape=jax.ShapeDtypeStruct((nb * l, MLA_HEADS * MLA_VDIM), BF16),
        compiler_params=_params(("parallel", "arbitrary")),
        name="attn_prompt",
    )(qi_of, ki_of, qn, qr, kv, kv, kr)


def _attn_cached_kernel(qn_ref, qr_ref, clat_ref, ckr_ref, nlat_ref, nkr_ref, wuk_ref, wuv_ref,
                        o_ref, ql_ref, qrs_ref, *, lq):
    for h in range(MLA_HEADS):
        ql_ref[h * lq:(h + 1) * lq, :] = _bdot_nt(qn_ref[h], wuk_ref[h]).astype(BF16)
        qrs_ref[h * lq:(h + 1) * lq, :] = qr_ref[h]
    ql = ql_ref[...]
    qr = qrs_ref[...]
    clat = clat_ref[...].astype(BF16)
    nlat = nlat_ref[...].astype(BF16)
    s_c = (_bdot_nt(ql, clat) + _bdot_nt(qr, ckr_ref[...])) * MLA_SCALE
    s_n = (_bdot_nt(ql, nlat) + _bdot_nt(qr, nkr_ref[...])) * MLA_SCALE
    m = jnp.maximum(jnp.max(s_c, axis=-1, keepdims=True), jnp.max(s_n, axis=-1, keepdims=True))
    p_c = jnp.exp(s_c - m)
    p_n = jnp.exp(s_n - m)
    denom = jnp.sum(p_c, axis=-1, keepdims=True) + jnp.sum(p_n, axis=-1, keepdims=True)
    o_lat = (_bdot(p_c, clat) + _bdot(p_n, nlat)) / denom
    for h in range(MLA_HEADS):
        o_ref[:, h * MLA_VDIM:(h + 1) * MLA_VDIM] = _bdot(
            o_lat[h * lq:(h + 1) * lq, :], wuv_ref[h]).astype(o_ref.dtype)


def _attn_cached(qn, qr, cache_lat, cache_kr_pad, new_lat, new_kr_pad, wuk_h, wuv_h, nb, lq):
    past = cache_lat.shape[1]
    whole = lambda b: (0, 0, 0)
    return pl.pallas_call(
        functools.partial(_attn_cached_kernel, lq=lq),
        grid=(nb,),
        in_specs=[
            pl.BlockSpec((MLA_HEADS, lq, LANES), lambda b: (0, b, 0)),
            pl.BlockSpec((MLA_HEADS, lq, LANES), lambda b: (0, b, 0)),
            pl.BlockSpec((None, past, KV_LORA), lambda b: (b, 0, 0)),
            pl.BlockSpec((None, past, LANES), lambda b: (b, 0, 0)),
            pl.BlockSpec((lq, KV_LORA), lambda b: (b, 0)),
            pl.BlockSpec((lq, LANES), lambda b: (b, 0)),
            pl.BlockSpec((MLA_HEADS, KV_LORA, MLA_NOPE), whole),
            pl.BlockSpec((MLA_HEADS, KV_LORA, MLA_VDIM), whole),
        ],
        out_specs=pl.BlockSpec((lq, MLA_HEADS * MLA_VDIM), lambda b: (b, 0)),
        out_shape=jax.ShapeDtypeStruct((nb * lq, MLA_HEADS * MLA_VDIM), BF16),
        scratch_shapes=[pltpu.VMEM((MLA_HEADS * lq, KV_LORA), BF16),
                        pltpu.VMEM((MLA_HEADS * lq, LANES), BF16)],
        compiler_params=_params(("parallel",)),
        name="attn_cached",
    )(qn, qr, cache_lat, cache_kr_pad, new_lat, new_kr_pad, wuk_h, wuv_h)


def _top_values(s, top_ref, want_rank):
    rank = jnp.full(s.shape, float(PEER_TOPK), F32) if want_rank else None
    for r in range(PEER_TOPK):
        m = jnp.max(s, axis=0, keepdims=True)
        top_ref[r:r + 1, :] = m
        hit = s == m
        if want_rank:
            rank = jnp.where(hit, float(r), rank)
        s = jnp.where(hit, -jnp.inf, s)
    return rank


_CAND_LIMITS = tuple(PEER_TOPK // (b + 1) for b in range(1, 8))
_HALF_TOPK = PEER_TOPK // 2


def _peer_select_kernel(h_ref, wq_ref, keys_ref, r1_ref, e1_ref, c0_ref, e0_ref, s_ref, a0_ref, a1_ref,
                        *, heads_per_step):
    hx = h_ref[...]
    for hh in range(heads_per_step):
        q_t = _bdot_nt(wq_ref[hh * 2 * N_KEYS:(hh + 1) * 2 * N_KEYS, :], hx)
        s_ref[2 * hh] = _bdot(keys_ref[hh, 0], q_t[:N_KEYS])
        s_ref[2 * hh + 1] = _bdot(keys_ref[hh, 1], q_t[N_KEYS:])
    for hh in range(heads_per_step):
        _peer_select_head(s_ref[2 * hh], s_ref[2 * hh + 1], r1_ref.at[hh], e1_ref.at[hh], c0_ref.at[hh],
                          e0_ref.at[hh], a0_ref.at[hh], a1_ref.at[hh])


def _peer_select_head(s0, s1, r1_ref, e1_ref, c0_ref, e0_ref, a0_ref, a1_ref):
    _top_values(s0, a0_ref, False)
    rank1 = _top_values(s1, a1_ref, True)
    a0 = a0_ref[...]
    a1 = a1_ref[...]
    a0_lo = a0[:_HALF_TOPK, :]
    row = lax.broadcasted_iota(jnp.int32, a0_lo.shape, 0)
    pieces = [a1[0:1, :] + a0]
    for b, limit in enumerate(_CAND_LIMITS, start=1):
        pieces.append(jnp.where(row < limit, a1[b:b + 1, :] + a0_lo, -jnp.inf))
    pieces.append(a1[_HALF_TOPK:, :] + a0[0:1, :])
    cand = jnp.concatenate(pieces, axis=0)
    c = cand
    for r in range(PEER_TOPK):
        tau = jnp.max(c, axis=0, keepdims=True)
        if r + 1 < PEER_TOPK:
            c = jnp.where(c == tau, -jnp.inf, c)
    sel = cand >= tau
    cmax = a0[0:1, :] + a1[0:1, :]
    z = jnp.sum(jnp.where(sel, jnp.exp(cand - cmax), 0.0), axis=0, keepdims=True)
    self32 = sel.astype(F32)
    n_hi = self32[_HALF_TOPK:PEER_TOPK, :]
    n_lo = self32[:_HALF_TOPK, :]
    for b in range(1, 8):
        lo = PEER_TOPK + (b - 1) * _HALF_TOPK
        n_lo = n_lo + self32[lo:lo + _HALF_TOPK, :]
    tail = jnp.sum(self32[PEER_TOPK + 7 * _HALF_TOPK:, :], axis=0, keepdims=True)
    n_lo = n_lo + jnp.where(row == 0, tail, 0.0)
    count0 = jnp.zeros(s0.shape, F32)
    for a in range(PEER_TOPK):
        n_a = n_lo[a:a + 1, :] if a < _HALF_TOPK else n_hi[a - _HALF_TOPK:a - _HALF_TOPK + 1, :]
        count0 = jnp.where(s0 == a0[a:a + 1, :], n_a, count0)
    r1_ref[...] = rank1.astype(r1_ref.dtype)
    c0_ref[...] = count0
    e0_ref[...] = jnp.exp(s0 - a0[0:1, :]) / z
    e1_ref[...] = jnp.exp(s1 - a1[0:1, :]).astype(e1_ref.dtype)


def _peer_select(hx, wq_t, keys, layer, tt):
    t, d = hx.shape
    hps = SELECT_HEADS_PER_STEP
    o_spec = pl.BlockSpec((hps, N_KEYS, tt), lambda i, g: (g, 0, i))
    return pl.pallas_call(
        functools.partial(_peer_select_kernel, heads_per_step=hps),
        grid=(t // tt, PEER_HEADS // hps),
        in_specs=[pl.BlockSpec((tt, d), lambda i, g: (i, 0)),
                  pl.BlockSpec((hps * 2 * N_KEYS, d), lambda i, g: (g, 0)),
                  pl.BlockSpec((None, hps, 2, N_KEYS, N_KEYS), lambda i, g: (layer, g, 0, 0, 0))],
        out_specs=[o_spec] * 4,
        out_shape=[jax.ShapeDtypeStruct((PEER_HEADS, N_KEYS, t), dt) for dt in (BF16, BF16, F32, F32)],
        scratch_shapes=[pltpu.VMEM((2 * hps, N_KEYS, tt), F32),
                        pltpu.VMEM((hps, PEER_TOPK, tt), F32), pltpu.VMEM((hps, PEER_TOPK, tt), F32)],
        compiler_params=_params(("parallel", "arbitrary")),
        name="peer_select",
    )(hx, wq_t, keys)


def _sublane_bcast_bf16(row, rows):
    tile_rows = 16
    packed = jnp.broadcast_to(row, (tile_rows, row.shape[1])).astype(BF16)
    return jnp.tile(packed, (rows // tile_rows, 1))


def _peer_mix_kernel(x_ref, u_ref, vt_ref, r1_ref, e1_ref, c0_ref, e0_ref, res_ref, gate_ref, o_ref,
                     acc_ref, a_ref, g_ref, p_ref, *, n_i):
    j = pl.program_id(1)

    @pl.when(j == 0)
    def _():
        acc_ref[...] = jnp.zeros_like(acc_ref)

    x = x_ref[...]
    rows_per_chunk = 2 * N_KEYS
    for c in range(n_i // 2):
        lo = c * rows_per_chunk
        a_ref[c] = _bdot_nt(u_ref[lo:lo + rows_per_chunk, :], x)
    zero = jnp.zeros((), BF16)
    for ii in range(n_i):
        w = None
        for h in range(PEER_HEADS):
            count = _sublane_bcast_bf16(c0_ref[h, ii:ii + 1, :], N_KEYS)
            e0 = _sublane_bcast_bf16(e0_ref[h, ii:ii + 1, :], N_KEYS)
            term = jnp.where(r1_ref[h] < count, e1_ref[h] * e0, zero)
            w = term if w is None else w + term
        g_ref[ii] = w
    for c in range(n_i // 2):
        lo = c * rows_per_chunk
        gates = jnp.concatenate([g_ref[2 * c], g_ref[2 * c + 1]], axis=0)
        p_ref[lo:lo + rows_per_chunk, :] = gates * _gelu(a_ref[c]).astype(BF16)
    acc_ref[...] += jnp.dot(vt_ref[...], p_ref[...], preferred_element_type=F32)

    @pl.when(j == pl.num_programs(1) - 1)
    def _():
        o_ref[...] = res_ref[...] + (1.0 + gate_ref[...]) * acc_ref[...].T


def _peer_mix(hx, u_b, vt_b, layer, r1, e1, c0, e0, x_res, gate, l, tt, et):
    t, d = hx.shape
    e = u_b.shape[1]
    n_i = et // N_KEYS
    tok_all = pl.BlockSpec((PEER_HEADS, N_KEYS, tt), lambda i, j: (0, 0, i))
    tok_i = pl.BlockSpec((PEER_HEADS, n_i, tt), lambda i, j: (0, j, i))
    gate_arr, gate_spec = _gate_operand(gate, l, tt, d)
    return pl.pallas_call(
        functools.partial(_peer_mix_kernel, n_i=n_i),
        grid=(t // tt, e // et),
        in_specs=[pl.BlockSpec((tt, d), lambda i, j: (i, 0)),
                  pl.BlockSpec((None, et, d), lambda i, j: (layer, j, 0)),
                  pl.BlockSpec((None, d, et), lambda i, j: (layer, 0, j)),
                  tok_all, tok_all, tok_i, tok_i,
                  pl.BlockSpec((tt, d), lambda i, j: (i, 0)), gate_spec],
        out_specs=pl.BlockSpec((tt, d), lambda i, j: (i, 0)),
        out_shape=jax.ShapeDtypeStruct((t, d), F32),
        scratch_shapes=[pltpu.VMEM((d, tt), F32),
                        pltpu.VMEM((n_i // 2, 2 * N_KEYS, tt), F32),
                        pltpu.VMEM((n_i, N_KEYS, tt), BF16),
                        pltpu.VMEM((et, tt), BF16)],
        compiler_params=_params(("parallel", "arbitrary"), vmem=PEER_MIX_VMEM_LIMIT),
        name="peer_mix",
    )(hx, u_b, vt_b, r1, e1, c0, e0, x_res, gate_arr)


def _peer(hx, wq_t, keys, layer, u_b, vt_b, x_res, gate, l):
    t = hx.shape[0]
    tt = min(t, 512)
    r1, e1, c0, e0 = _peer_select(hx, wq_t, keys, layer, tt)
    return _peer_mix(hx, u_b, vt_b, layer, r1, e1, c0, e0, x_res, gate, l, tt, 1024)


def _rope_tables(pos, half, reps):
    inv = ROPE_THETA ** (-jnp.arange(half, dtype=F32) / half)
    ang = pos.astype(F32)[:, None] * inv[None, :]
    return jnp.tile(jnp.cos(ang), (1, reps)), jnp.tile(jnp.sin(ang), (1, reps))


def _rotate_half_cols(w):
    half = w.shape[-1] // 2
    return jnp.concatenate([-w[..., half:], w[..., :half]], axis=-1)


def _pad_last(w, width):
    return jnp.pad(w, [(0, 0)] * (w.ndim - 1) + [(0, width - w.shape[-1])])


def _shared_weights(p):
    out = {}
    w_rope = p["mla_w_dkv"][:, KV_LORA:]
    out["w_dkv_ext"] = jnp.concatenate(
        [p["mla_w_dkv"][:, :KV_LORA], _pad_last(w_rope, LANES), _pad_last(_rotate_half_cols(w_rope), LANES)],
        axis=1)
    w_uq = p["mla_w_uq"][0].reshape(-1, MLA_HEADS, MLA_NOPE + MLA_ROPE)
    q_lora = w_uq.shape[0]
    out["w_uq_nope"] = w_uq[:, :, :MLA_NOPE].reshape(q_lora, MLA_HEADS * MLA_NOPE)
    w_qr = w_uq[:, :, MLA_NOPE:]
    out["w_uq_rope"] = _pad_last(w_qr, LANES).reshape(q_lora, MLA_HEADS * LANES)
    out["w_uq_rot"] = _pad_last(_rotate_half_cols(w_qr), LANES).reshape(q_lora, MLA_HEADS * LANES)
    out["w_ukv_flat"] = jnp.concatenate([p["mla_w_uk"].reshape(KV_LORA, MLA_HEADS * MLA_NOPE),
                                         p["mla_w_uv"].reshape(KV_LORA, MLA_HEADS * MLA_VDIM)], axis=1)
    out["w_uk_h"] = jnp.transpose(p["mla_w_uk"], (1, 0, 2))
    out["w_uv_h"] = jnp.transpose(p["mla_w_uv"], (1, 0, 2))
    out["peer_wq_t"] = [p["peer_w_q"][l].T for l in range(2)]
    out["peer_u_b"] = p["peer_u"].astype(BF16)
    out["peer_vt_b"] = jnp.swapaxes(p["peer_v"].astype(BF16), 1, 2)
    return out


def _adaln(c_all, w, b, lead, n_out):
    m, d = c_all.shape
    tn = 2048
    if w.ndim == 2:
        w = w[None]
        b = b[None]
    b3 = b.reshape(b.shape[0], 1, n_out)
    return _linear(
        c_all, [w], n_cols=n_out, tm=m, tn=tn, epilogue=_ep_bias, prologue=_silu, w_lead=lead,
        extras=[b3], extra_specs=[pl.BlockSpec((None, 1, tn), lambda i, j: (lead, 0, j))],
        out_shapes=[jax.ShapeDtypeStruct((m, n_out), F32)],
        out_specs=[pl.BlockSpec((m, tn), lambda i, j: (0, j))])[0]


def _trunk(x, pos, mods, kv_mod, p, sw, ret_state, cache_lat, cache_kr):
    nb, l, d = x.shape
    t = nb * l
    pos_rows = jnp.tile(pos, nb) if l < 256 else pos
    table_rows = pos_rows.shape[0]
    assert table_rows % _row_tile(t, d) == 0, "a row tile must cover whole sequences or divide one"

    def vec(a):
        return a.reshape(nb, 1, d)

    def flat(a):
        return a.reshape(t, a.shape[-1])

    sh1, sc1, gt1, sh2, sc2, gt2 = [vec(m) for m in mods[0]]
    (h,) = _norm_mod(x, mods=[(p["norm_g"][0, 0][None], sh1, sc1)])
    h = flat(h)
    tm = _row_tile(t, d)
    cos_r, sin_r = _rope_tables(pos_rows, RET_DK // 2, 1)
    tspec = _table_spec(tm, table_rows)
    w_in = p["ret_w_in"]
    qk_cols = RET_HEADS * RET_DK
    v_cols = RET_HEADS * RET_DV
    tn = _col_tile(d, qk_cols)

    def rope_proj(col0, scale):
        return _linear(
            h, [w_in], n_cols=qk_cols, tm=tm, tn=tn, w_lead=0, col_block0=col0 // tn,
            epilogue=functools.partial(_ep_rope_half128, scale=scale),
            extras=[cos_r, sin_r], extra_specs=[tspec, tspec],
            out_shapes=[jax.ShapeDtypeStruct((t, qk_cols), BF16)],
            out_specs=[pl.BlockSpec((tm, tn), lambda i, j: (i, j))])[0]

    def plain_proj(x_in, w, n_cols, col0=0, lead=None, dtype=BF16, tm_=None):
        tm_ = tm_ or _row_tile(x_in.shape[0], x_in.shape[1])
        tn_ = _col_tile(x_in.shape[1], n_cols)
        return _linear(
            x_in, [w], n_cols=n_cols, tm=tm_, tn=tn_, w_lead=lead, col_block0=col0 // tn_,
            epilogue=_ep_plain,
            out_shapes=[jax.ShapeDtypeStruct((x_in.shape[0], n_cols), dtype)],
            out_specs=[pl.BlockSpec((tm_, tn_), lambda i, j: (i, j))])[0]

    def resid_proj(x_in, w, x_res, gate):
        tm_ = _row_tile(x_in.shape[0], x_in.shape[1])
        tn_ = _col_tile(x_in.shape[1], d)
        gate_arr, gate_spec = _gate_operand(gate, l, tm_, tn_)
        tile = pl.BlockSpec((tm_, tn_), lambda i, j: (i, j))
        return _linear(
            x_in, [w], n_cols=d, tm=tm_, tn=tn_, w_lead=0, epilogue=_ep_resid,
            extras=[x_res, gate_arr], extra_specs=[tile, gate_spec],
            out_shapes=[jax.ShapeDtypeStruct((t, d), F32)], out_specs=[tile])[0]

    def heads_proj(x_in, ws, n_cols, epilogue=_ep_heads, extras=(), extra_specs=()):
        tm_ = _row_tile(x_in.shape[0], x_in.shape[1])
        tn_ = _col_tile(x_in.shape[1], n_cols, len(ws))
        return _linear(
            x_in, ws, n_cols=n_cols, tm=tm_, tn=tn_, epilogue=epilogue, extras=extras, extra_specs=extra_specs,
            out_shapes=[jax.ShapeDtypeStruct((n_cols // LANES, x_in.shape[0], LANES), BF16)],
            out_specs=[pl.BlockSpec((tn_ // LANES, tm_, LANES), lambda i, j: (j, i, 0))])[0]

    q_r = rope_proj(0, 1.0)
    k_r = rope_proj(qk_cols, RET_DK ** -0.5)
    v_r = plain_proj(h, w_in, v_cols, col0=2 * qk_cols, lead=0)
    g_r = plain_proj(h, w_in, v_cols, col0=2 * qk_cols + v_cols, lead=0)
    s0 = None if ret_state is None else ret_state[0]
    y_r, s_new = _retention(q_r, k_r, v_r, g_r, p["ret_gn_g"], s0, nb, l)
    xf = resid_proj(y_r, p["ret_w_out"], flat(x), gt1)
    (h,) = _norm_mod(xf.reshape(nb, l, d), mods=[(p["norm_g"][0, 1][None], sh2, sc2)])
    xf = _peer(flat(h), sw["peer_wq_t"][0], p["peer_keys"], 0, sw["peer_u_b"], sw["peer_vt_b"],
               xf, gt2, l)

    sh1, sc1, gt1b, sh2, sc2, gt2b = [vec(m) for m in mods[1]]
    kv_sh, kv_sc = [vec(m) for m in kv_mod]
    h, h_kv = _norm_mod(xf.reshape(nb, l, d),
                         mods=[(p["norm_g"][1, 0][None], sh1, sc1), (p["kv_norm_g"][None], kv_sh, kv_sc)])
    h = flat(h)
    h_kv = flat(h_kv)
    cos_m, sin_m = _rope_tables(pos_rows, MLA_ROPE // 2, LANES // (MLA_ROPE // 2))
    n_ext = sw["w_dkv_ext"].shape[1]
    lat, kr, kr_pad = _linear(
        h_kv, [sw["w_dkv_ext"]], n_cols=n_ext, tm=tm, tn=n_ext, epilogue=_ep_kv,
        extras=[p["mla_kv_norm_g"][None], cos_m, sin_m],
        extra_specs=[pl.BlockSpec((1, KV_LORA), lambda i, j: (0, 0)), tspec, tspec],
        out_shapes=[jax.ShapeDtypeStruct((t, KV_LORA), F32), jax.ShapeDtypeStruct((t, MLA_ROPE), F32),
                    jax.ShapeDtypeStruct((t, LANES), BF16)],
        out_specs=[pl.BlockSpec((tm, KV_LORA), lambda i, j: (i, 0)),
                   pl.BlockSpec((tm, MLA_ROPE), lambda i, j: (i, 0)),
                   pl.BlockSpec((tm, LANES), lambda i, j: (i, 0))])
    q_lora = p["mla_w_dq"].shape[-1]
    cq = _linear(
        h, [p["mla_w_dq"]], n_cols=q_lora, tm=tm, tn=q_lora, w_lead=0, epilogue=_ep_rms,
        extras=[p["mla_q_norm_g"]], extra_specs=[pl.BlockSpec((1, q_lora), lambda i, j: (0, 0))],
        out_shapes=[jax.ShapeDtypeStruct((t, q_lora), BF16)],
        out_specs=[pl.BlockSpec((tm, q_lora), lambda i, j: (i, 0))])[0]
    hl = MLA_HEADS * LANES
    qn = heads_proj(cq, [sw["w_uq_nope"]], hl)
    qr = heads_proj(cq, [sw["w_uq_rope"], sw["w_uq_rot"]], hl, epilogue=_ep_rot_heads,
                    extras=[cos_m, sin_m], extra_specs=[tspec, tspec])
    if cache_lat is None:
        kv_h = heads_proj(lat, [sw["w_ukv_flat"]], 2 * hl)
        o = _attn_prompt(qn, qr, kv_h, kr_pad, nb, l)
    else:
        ckr = _pad_last(cache_kr, LANES).astype(BF16)
        o = _attn_cached(qn, qr, cache_lat, ckr, lat, kr_pad, sw["w_uk_h"], sw["w_uv_h"], nb, l)
    xf = resid_proj(o, p["mla_w_o"], xf, gt1b)
    (h,) = _norm_mod(xf.reshape(nb, l, d), mods=[(p["norm_g"][1, 1][None], sh2, sc2)])
    xf = _peer(flat(h), sw["peer_wq_t"][1], p["peer_keys"], 1, sw["peer_u_b"], sw["peer_vt_b"],
               xf, gt2b, l)
    (y,) = _norm_mod(xf.reshape(nb, l, d), final_g=p["final_g"][None])
    return y, s_new[None], lat.reshape(nb, l, KV_LORA), kr.reshape(nb, l, MLA_ROPE)


def kernel(x_prompt, x_sample, c_prompt, c_sample, state_retention, cache_mla_latent, cache_mla_krope,
           ada_w, ada_b, norm_g, ret_w_in, ret_gn_g, ret_w_out,
           kv_ada_w, kv_ada_b, kv_norm_g, mla_w_dkv, mla_kv_norm_g, mla_w_uk, mla_w_uv,
           mla_w_dq, mla_q_norm_g, mla_w_uq, mla_w_o,
           peer_w_q, peer_keys, peer_u, peer_v, final_g):
    p = dict(norm_g=norm_g, ret_w_in=ret_w_in, ret_gn_g=ret_gn_g[0][None], ret_w_out=ret_w_out,
             kv_norm_g=kv_norm_g, mla_w_dkv=mla_w_dkv, mla_kv_norm_g=mla_kv_norm_g,
             mla_w_uk=mla_w_uk, mla_w_uv=mla_w_uv, mla_w_dq=mla_w_dq, mla_q_norm_g=mla_q_norm_g,
             mla_w_uq=mla_w_uq, mla_w_o=mla_w_o, peer_w_q=peer_w_q, peer_keys=peer_keys,
             peer_u=peer_u, peer_v=peer_v, final_g=final_g)
    sw = _shared_weights(p)
    nbp, lp, d = x_prompt.shape
    nbs, ls, _ = x_sample.shape
    past = cache_mla_latent.shape[1]

    n_c = nbp + nbs
    c_all = jnp.pad(jnp.concatenate([c_prompt, c_sample], axis=0), ((0, (-n_c) % 8), (0, 0)))
    layer_mods = [_adaln(c_all, ada_w, ada_b, layer, 6 * d) for layer in range(2)]
    kv_mods = _adaln(c_all, kv_ada_w, kv_ada_b, 0, 2 * d)

    def stream_mods(lo, hi):
        return ([jnp.split(m[lo:hi], 6, axis=-1) for m in layer_mods], jnp.split(kv_mods[lo:hi], 2, axis=-1))

    mods_p, kv_p = stream_mods(0, nbp)
    mods_s, kv_s = stream_mods(nbp, n_c)
    pos_p = jnp.arange(lp, dtype=jnp.int32)
    pos_s = past + jnp.arange(ls, dtype=jnp.int32)
    y_p, ret_p, lat_p, kr_p = _trunk(x_prompt, pos_p, mods_p, kv_p, p, sw, None, None, None)
    y_s, ret_s, lat_s, kr_s = _trunk(x_sample, pos_s, mods_s, kv_s, p, sw, state_retention,
                                     cache_mla_latent, cache_mla_krope)
    return (y_p, y_s, ret_p, ret_s, lat_p, kr_p, lat_s, kr_s)
```

```python
import functools
import math

import jax
import jax.numpy as jnp
from jax import lax
from jax.experimental import pallas as pl
from jax.experimental.pallas import tpu as pltpu

F32 = jnp.float32
BF16 = jnp.bfloat16

EPS = 1e-6
ROPE_THETA = 10000.0
NEG_INF = -1e30
CHUNK = 64

RET_HEADS = 8
RET_DK = 256
RET_DV = 512

MLA_HEADS = 16
MLA_NOPE = 128
MLA_ROPE = 64
MLA_VDIM = 128
KV_LORA = 512
MLA_SCALE = (MLA_NOPE + MLA_ROPE) ** -0.5

PEER_HEADS = 8
N_KEYS = 128
PEER_TOPK = 16

LANES = 128
VMEM_LIMIT = 52 * 1024 * 1024
PEER_MIX_VMEM_LIMIT = 58 * 1024 * 1024
SELECT_HEADS_PER_STEP = 4


def _params(sem, vmem=VMEM_LIMIT, flags=None):
    return pltpu.CompilerParams(dimension_semantics=sem, vmem_limit_bytes=vmem, flags=flags)


def _bdot(a, b):
    return jnp.dot(a.astype(BF16), b.astype(BF16), preferred_element_type=F32)


def _bdot_nt(a, b):
    return lax.dot_general(a.astype(BF16), b.astype(BF16), (((1,), (1,)), ((), ())),
                           preferred_element_type=F32)


def _bdot_tn(a, b):
    return lax.dot_general(a.astype(BF16), b.astype(BF16), (((0,), (0,)), ((), ())),
                           preferred_element_type=F32)


def _silu(x):
    return x * (1.0 / (1.0 + jnp.exp(-x)))


def _gelu(x):
    return 0.5 * x * (1.0 + lax.erf(x * (0.5 ** 0.5)))


def _linear_kernel(*refs, n_w, n_extra, prologue, epilogue):
    x_ref = refs[0]
    w_refs = refs[1:1 + n_w]
    extra_refs = refs[1 + n_w:1 + n_w + n_extra]
    out_refs = refs[1 + n_w + n_extra:]
    xv = x_ref[...]
    if prologue is not None:
        xv = prologue(xv)
    xb = xv.astype(BF16)
    accs = [jnp.dot(xb, w[...].astype(BF16), preferred_element_type=F32) for w in w_refs]
    epilogue(accs, extra_refs, out_refs)


def _linear(x, ws, *, n_cols, tm, tn, epilogue, out_shapes, out_specs, w_lead=None, col_block0=0,
            extras=(), extra_specs=(), prologue=None, name=None):
    m, k = x.shape
    if name is None:
        name = "linear" + getattr(epilogue, "func", epilogue).__name__
    assert m % tm == 0 and n_cols % tn == 0
    if w_lead is None:
        w_spec = pl.BlockSpec((k, tn), lambda i, j: (0, j + col_block0))
    else:
        w_spec = pl.BlockSpec((None, k, tn), lambda i, j: (w_lead, 0, j + col_block0))
    kern = functools.partial(_linear_kernel, n_w=len(ws), n_extra=len(extras), prologue=prologue,
                             epilogue=epilogue)
    return pl.pallas_call(
        kern,
        grid=(m // tm, n_cols // tn),
        in_specs=[pl.BlockSpec((tm, k), lambda i, j: (i, 0))] + [w_spec] * len(ws) + list(extra_specs),
        out_specs=out_specs,
        out_shape=out_shapes,
        compiler_params=_params(("parallel", "arbitrary")),
        name=name,
    )(x, *ws, *extras)


def _ep_plain(accs, extras, outs):
    outs[0][...] = accs[0].astype(outs[0].dtype)


def _ep_heads(accs, extras, outs):
    acc = accs[0]
    for jj in range(acc.shape[1] // LANES):
        outs[0][jj] = acc[:, jj * LANES:(jj + 1) * LANES].astype(outs[0].dtype)


def _ep_bias(accs, extras, outs):
    outs[0][...] = accs[0] + extras[0][...]


def _ep_resid(accs, extras, outs):
    outs[0][...] = extras[0][...] + (1.0 + extras[1][...]) * accs[0]


def _gate_operand(gate, l, rows_per_tile, cols_per_tile):
    nb, _, d = gate.shape
    col = (lambda j: j) if cols_per_tile < d else (lambda j: 0)
    if l % rows_per_tile == 0:
        tiles_per_batch = l // rows_per_tile
        return gate, pl.BlockSpec((None, 1, cols_per_tile), lambda i, j: (i // tiles_per_batch, 0, col(j)))
    rows = jnp.broadcast_to(gate, (nb, l, d)).reshape(nb * l, d)
    return rows, pl.BlockSpec((rows_per_tile, cols_per_tile), lambda i, j: (i, col(j)))


def _ep_rope_half128(accs, extras, outs, *, scale):
    acc = accs[0]
    cos = extras[0][...]
    sin = extras[1][...]
    for g in range(acc.shape[1] // 256):
        x1 = acc[:, g * 256:g * 256 + 128]
        x2 = acc[:, g * 256 + 128:(g + 1) * 256]
        outs[0][:, g * 256:g * 256 + 128] = ((x1 * cos - x2 * sin) * scale).astype(outs[0].dtype)
        outs[0][:, g * 256 + 128:(g + 1) * 256] = ((x1 * sin + x2 * cos) * scale).astype(outs[0].dtype)


def _ep_rot_heads(accs, extras, outs):
    cos = extras[0][...]
    sin = extras[1][...]
    a, b = accs
    for jj in range(a.shape[1] // LANES):
        sl = slice(jj * LANES, (jj + 1) * LANES)
        outs[0][jj] = (a[:, sl] * cos + b[:, sl] * sin).astype(outs[0].dtype)


def _ep_rms(accs, extras, outs):
    acc = accs[0]
    g = extras[0][...]
    y = acc * lax.rsqrt(jnp.mean(acc * acc, axis=-1, keepdims=True) + EPS) * g
    outs[0][...] = y.astype(outs[0].dtype)


def _ep_kv(accs, extras, outs):
    acc = accs[0]
    g = extras[0][...]
    cos = extras[1][...]
    sin = extras[2][...]
    c = acc[:, :KV_LORA]
    outs[0][...] = c * lax.rsqrt(jnp.mean(c * c, axis=-1, keepdims=True) + EPS) * g
    kr = acc[:, KV_LORA:KV_LORA + LANES] * cos + acc[:, KV_LORA + LANES:KV_LORA + 2 * LANES] * sin
    outs[1][...] = kr[:, :MLA_ROPE]
    outs[2][...] = kr.astype(BF16)


def _row_tile(m, k):
    return min(m, 1024)


def _col_tile(k, n, n_weights=1):
    for tn in (1024, 512):
        if n % tn == 0 and 2 * n_weights * k * tn * 4 <= 16 * 1024 * 1024:
            return tn
    return min(n, 512)


def _table_spec(tm, table_rows):
    nblk = table_rows // tm
    return pl.BlockSpec((tm, LANES), lambda i, j: (i % nblk, 0))


def _norm_mod_kernel(*refs, n_mod):
    x = refs[0][...]
    xn = x * lax.rsqrt(jnp.mean(x * x, axis=-1, keepdims=True) + EPS)
    if n_mod == 0:
        refs[2][...] = xn * refs[1][...]
        return
    outs = refs[1 + 3 * n_mod:]
    for m in range(n_mod):
        g, sh, sc = (refs[1 + 3 * m + k][...] for k in range(3))
        outs[m][...] = ((xn * g) * (1.0 + sc) + sh).astype(outs[m].dtype)


def _norm_mod(x, mods=(), final_g=None):
    nb, l, d = x.shape
    tl = min(l, 512)
    tok = pl.BlockSpec((None, tl, d), lambda b, i: (b, i, 0))
    per_b = pl.BlockSpec((None, 1, d), lambda b, i: (b, 0, 0))
    gain = pl.BlockSpec((1, d), lambda b, i: (0, 0))
    args, specs = [x], [tok]
    if final_g is not None:
        assert not mods
        args.append(final_g)
        specs.append(gain)
        out_shapes = [jax.ShapeDtypeStruct((nb, l, d), F32)]
    else:
        for g, sh, sc in mods:
            args += [g, sh, sc]
            specs += [gain, per_b, per_b]
        out_shapes = [jax.ShapeDtypeStruct((nb, l, d), BF16) for _ in mods]
    return pl.pallas_call(
        functools.partial(_norm_mod_kernel, n_mod=len(mods)),
        grid=(nb, l // tl), in_specs=specs, out_specs=[tok] * len(out_shapes), out_shape=out_shapes,
        compiler_params=_params(("parallel", "parallel")),
        name="norm_mod",
    )(*args)


def _ret_log_decay():
    return jnp.log1p(-jnp.exp2(-5.0 - jnp.arange(RET_HEADS, dtype=F32)))


def _retention_tables(c):
    log_g = _ret_log_decay()[:, None, None]
    n = jnp.arange(c, dtype=F32)
    dist = n[:, None] - n[None, :]
    same = (jnp.arange(c)[:, None] // CHUNK) == (jnp.arange(c)[None, :] // CHUNK)
    earlier = (jnp.arange(c)[None, :] // CHUNK) < (jnp.arange(c)[:, None] // CHUNK)
    mask = jnp.where(same[None], jnp.exp(jnp.abs(dist)[None] * log_g),
                     jnp.where(earlier[None], jnp.exp(dist[None] * log_g), 0.0))
    q_decay = jnp.exp((n[None, :, None] + 1.0) * log_g)
    k_decay = jnp.exp((c - 1.0 - n)[None, :, None] * log_g)
    blk_decay = jnp.exp(c * log_g)
    return mask, q_decay, k_decay, blk_decay


def _retention_kernel(*refs, has_s0, hps):
    if has_s0:
        (q_ref, k_ref, v_ref, g_ref, mask_ref, qd_ref, kd_ref, bd_ref, gn_ref, s0_ref,
         y_ref, s_out_ref, s_ref) = refs
    else:
        (q_ref, k_ref, v_ref, g_ref, mask_ref, qd_ref, kd_ref, bd_ref, gn_ref,
         y_ref, s_out_ref, s_ref) = refs
    c = pl.program_id(2)

    @pl.when(c == 0)
    def _():
        if has_s0:
            s_ref[...] = s0_ref[...]
        else:
            s_ref[...] = jnp.zeros_like(s_ref)

    for hh in range(hps):
        qk_cols = slice(hh * RET_DK, (hh + 1) * RET_DK)
        v_cols = slice(hh * RET_DV, (hh + 1) * RET_DV)
        q = q_ref[:, qk_cols]
        k = k_ref[:, qk_cols]
        v = v_ref[:, v_cols]
        s_prev = s_ref[hh]
        scores = _bdot_nt(q, k) * mask_ref[hh]
        y = _bdot(scores, v) + _bdot(q, s_prev) * qd_ref[hh]
        k_scaled = k.astype(F32) * kd_ref[hh]
        s_ref[hh] = bd_ref[hh] * s_prev + _bdot_tn(k_scaled, v)

        mu = jnp.mean(y, axis=-1, keepdims=True)
        yc = y - mu
        var = jnp.mean(yc * yc, axis=-1, keepdims=True)
        yn = yc * lax.rsqrt(var + EPS) * gn_ref[:, v_cols]
        y_ref[:, v_cols] = (_silu(g_ref[:, v_cols].astype(F32)) * yn).astype(y_ref.dtype)

    @pl.when(c == pl.num_programs(2) - 1)
    def _():
        s_out_ref[...] = s_ref[...]


def _retention(q, k, v, g, gn_g, s0, nb, l):
    cb = min(l, 256)
    nc = l // cb
    hps = 4 if nc > 1 else RET_HEADS
    mask, qd, kd, bd = _retention_tables(cb)
    row = lambda b, hg, c: (b * nc + c, hg)
    per_h3 = lambda b, hg, c: (hg, 0, 0)
    in_specs = [
        pl.BlockSpec((cb, hps * RET_DK), row), pl.BlockSpec((cb, hps * RET_DK), row),
        pl.BlockSpec((cb, hps * RET_DV), row), pl.BlockSpec((cb, hps * RET_DV), row),
        pl.BlockSpec((hps, cb, cb), per_h3), pl.BlockSpec((hps, cb, 1), per_h3),
        pl.BlockSpec((hps, cb, 1), per_h3), pl.BlockSpec((hps, 1, 1), per_h3),
        pl.BlockSpec((1, hps * RET_DV), lambda b, hg, c: (0, hg)),
    ]
    args = [q, k, v, g, mask, qd, kd, bd, gn_g]
    state_spec = pl.BlockSpec((None, hps, RET_DK, RET_DV), lambda b, hg, c: (b, hg, 0, 0))
    if s0 is not None:
        in_specs.append(state_spec)
        args.append(s0)
    y, s_new = pl.pallas_call(
        functools.partial(_retention_kernel, has_s0=s0 is not None, hps=hps),
        grid=(nb, RET_HEADS // hps, nc),
        in_specs=in_specs,
        out_specs=[pl.BlockSpec((cb, hps * RET_DV), row), state_spec],
        out_shape=[jax.ShapeDtypeStruct((nb * l, RET_HEADS * RET_DV), BF16),
                   jax.ShapeDtypeStruct((nb, RET_HEADS, RET_DK, RET_DV), F32)],
        scratch_shapes=[pltpu.VMEM((hps, RET_DK, RET_DV), F32)],
        compiler_params=_params(("parallel", "parallel", "arbitrary")),
        name="retention",
    )(*args)
    return y, s_new


def _attn_prompt_kernel(qi_ref, ki_ref, qn_ref, qr_ref, kn_ref, v_ref, kr_ref, o_ref, m_ref, acc_ref, *, tq, tk):
    qi = qi_ref[pl.program_id(1)]
    ki = ki_ref[pl.program_id(1)]
    exp2_scale = MLA_SCALE * math.log2(math.e)

    @pl.when(ki == 0)
    def _():
        m_ref[...] = jnp.full_like(m_ref, NEG_INF)
        acc_ref[...] = jnp.zeros_like(acc_ref)

    def block(masked):
        if masked:
            q_chunk = (qi * tq + lax.broadcasted_iota(jnp.int32, (tq, tk), 0)) // CHUNK
            k_chunk = (ki * tk + lax.broadcasted_iota(jnp.int32, (tq, tk), 1)) // CHUNK
            visible = k_chunk <= q_chunk
        kr = kr_ref[...]
        ones = jnp.ones((tk, LANES), BF16)

        def head(h, carry):
            q = jnp.concatenate([qn_ref[h], qr_ref[h]], axis=1)
            k = jnp.concatenate([kn_ref[h], kr], axis=1)
            s = _bdot_nt(q, k)
            if masked:
                s = jnp.where(visible, s, NEG_INF)
            m_prev = m_ref[h]
            m_new = jnp.maximum(m_prev, jnp.max(s, axis=-1, keepdims=True))
            alpha = jnp.exp2((m_prev - m_new) * exp2_scale)
            p = jnp.exp2((s - jnp.tile(m_new, (1, tk // LANES))) * exp2_scale)
            v_ext = jnp.concatenate([v_ref[h], ones], axis=1)
            acc_ref[h] = jnp.tile(alpha, (1, 2)) * acc_ref[h] + _bdot(p, v_ext)
            m_ref[h] = m_new
            return carry

        lax.fori_loop(0, MLA_HEADS, head, 0, unroll=True)

    @pl.when(ki < qi)
    def _():
        block(False)

    @pl.when(ki == qi)
    def _():
        block(True)
        for h in range(MLA_HEADS):
            acc = acc_ref[h]
            o_ref[:, h * MLA_VDIM:(h + 1) * MLA_VDIM] = (
                acc[:, :MLA_VDIM] / acc[:, MLA_VDIM:]).astype(o_ref.dtype)


def _attn_prompt(qn, qr, kv, kr, nb, l):
    tq = tk = min(l, 512)
    nq = l // tq
    pairs = [(qi, ki) for qi in range(nq) for ki in range(qi + 1)]
    qi_of = jnp.asarray([qk[0] for qk in pairs], jnp.int32)
    ki_of = jnp.asarray([qk[1] for qk in pairs], jnp.int32)
    q_spec = pl.BlockSpec((MLA_HEADS, tq, LANES), lambda b, s, qi, ki: (0, b * nq + qi[s], 0))
    k_spec = pl.BlockSpec((MLA_HEADS, tk, LANES), lambda b, s, qi, ki: (0, b * nq + ki[s], 0))
    v_spec = pl.BlockSpec((MLA_HEADS, tk, LANES), lambda b, s, qi, ki: (1, b * nq + ki[s], 0))
    return pl.pallas_call(
        functools.partial(_attn_prompt_kernel, tq=tq, tk=tk),
        grid_spec=pltpu.PrefetchScalarGridSpec(
            num_scalar_prefetch=2,
            grid=(nb, len(pairs)),
            in_specs=[q_spec, q_spec, k_spec, v_spec,
                      pl.BlockSpec((tk, LANES), lambda b, s, qi, ki: (b * nq + ki[s], 0))],
            out_specs=pl.BlockSpec((tq, MLA_HEADS * MLA_VDIM), lambda b, s, qi, ki: (b * nq + qi[s], 0)),
            scratch_shapes=[pltpu.VMEM((MLA_HEADS, tq, LANES), F32),
                            pltpu.VMEM((MLA_HEADS, tq, MLA_VDIM + LANES), F32)]),
        out_shape=jax.ShapeDtypeStruct((nb * l, MLA_HEADS * MLA_VDIM), BF16),
        compiler_params=_params(("parallel", "arbitrary")),
        name="attn_prompt",
    )(qi_of, ki_of, qn, qr, kv, kv, kr)


def _attn_cached_kernel(qn_ref, qr_ref, clat_ref, ckr_ref, nlat_ref, nkr_ref, wuk_ref, wuv_ref,
                        o_ref, ql_ref, qrs_ref, *, lq):
    for h in range(MLA_HEADS):
        ql_ref[h * lq:(h + 1) * lq, :] = _bdot_nt(qn_ref[h], wuk_ref[h]).astype(BF16)
        qrs_ref[h * lq:(h + 1) * lq, :] = qr_ref[h]
    ql = ql_ref[...]
    qr = qrs_ref[...]
    clat = clat_ref[...].astype(BF16)
    nlat = nlat_ref[...].astype(BF16)
    s_c = (_bdot_nt(ql, clat) + _bdot_nt(qr, ckr_ref[...])) * MLA_SCALE
    s_n = (_bdot_nt(ql, nlat) + _bdot_nt(qr, nkr_ref[...])) * MLA_SCALE
    m = jnp.maximum(jnp.max(s_c, axis=-1, keepdims=True), jnp.max(s_n, axis=-1, keepdims=True))
    p_c = jnp.exp(s_c - m)
    p_n = jnp.exp(s_n - m)
    denom = jnp.sum(p_c, axis=-1, keepdims=True) + jnp.sum(p_n, axis=-1, keepdims=True)
    o_lat = (_bdot(p_c, clat) + _bdot(p_n, nlat)) / denom
    for h in range(MLA_HEADS):
        o_ref[:, h * MLA_VDIM:(h + 1) * MLA_VDIM] = _bdot(
            o_lat[h * lq:(h + 1) * lq, :], wuv_ref[h]).astype(o_ref.dtype)


def _attn_cached(qn, qr, cache_lat, cache_kr_pad, new_lat, new_kr_pad, wuk_h, wuv_h, nb, lq):
    past = cache_lat.shape[1]
    whole = lambda b: (0, 0, 0)
    return pl.pallas_call(
        functools.partial(_attn_cached_kernel, lq=lq),
        grid=(nb,),
        in_specs=[
            pl.BlockSpec((MLA_HEADS, lq, LANES), lambda b: (0, b, 0)),
            pl.BlockSpec((MLA_HEADS, lq, LANES), lambda b: (0, b, 0)),
            pl.BlockSpec((None, past, KV_LORA), lambda b: (b, 0, 0)),
            pl.BlockSpec((None, past, LANES), lambda b: (b, 0, 0)),
            pl.BlockSpec((lq, KV_LORA), lambda b: (b, 0)),
            pl.BlockSpec((lq, LANES), lambda b: (b, 0)),
            pl.BlockSpec((MLA_HEADS, KV_LORA, MLA_NOPE), whole),
            pl.BlockSpec((MLA_HEADS, KV_LORA, MLA_VDIM), whole),
        ],
        out_specs=pl.BlockSpec((lq, MLA_HEADS * MLA_VDIM), lambda b: (b, 0)),
        out_shape=jax.ShapeDtypeStruct((nb * lq, MLA_HEADS * MLA_VDIM), BF16),
        scratch_shapes=[pltpu.VMEM((MLA_HEADS * lq, KV_LORA), BF16),
                        pltpu.VMEM((MLA_HEADS * lq, LANES), BF16)],
        compiler_params=_params(("parallel",)),
        name="attn_cached",
    )(qn, qr, cache_lat, cache_kr_pad, new_lat, new_kr_pad, wuk_h, wuv_h)


def _top_values(s, top_ref, want_rank, one_per_round=False):
    rank = jnp.full(s.shape, float(PEER_TOPK), F32) if want_rank else None
    rows = lax.broadcasted_iota(jnp.int32, s.shape, 0).astype(F32) if one_per_round else None
    for r in range(PEER_TOPK):
        m = jnp.max(s, axis=0, keepdims=True)
        top_ref[r:r + 1, :] = m
        hit = s == m
        if one_per_round:
            first = jnp.min(jnp.where(hit, rows, float(s.shape[0])), axis=0, keepdims=True)
            hit = rows == first
        if want_rank:
            rank = jnp.where(hit, float(r), rank)
        s = jnp.where(hit, -jnp.inf, s)
    removed = jnp.sum((s == -jnp.inf).astype(F32), axis=0, keepdims=True)
    return rank, removed


_CAND_LIMITS = tuple(PEER_TOPK // (b + 1) for b in range(1, 8))
_HALF_TOPK = PEER_TOPK // 2


def _peer_select_kernel(h_ref, wq_ref, keys_ref, r1_ref, e1_ref, c0_ref, e0_ref, s_ref, a0_ref, a1_ref,
                        n_ref, *, heads_per_step):
    hx = h_ref[...]
    for hh in range(heads_per_step):
        q_t = _bdot_nt(wq_ref[hh * 2 * N_KEYS:(hh + 1) * 2 * N_KEYS, :], hx)
        s_ref[2 * hh] = _bdot(keys_ref[hh, 0], q_t[:N_KEYS])
        s_ref[2 * hh + 1] = _bdot(keys_ref[hh, 1], q_t[N_KEYS:])
    tied = None
    for hh in range(heads_per_step):
        t_h = _peer_select_head(s_ref[2 * hh], s_ref[2 * hh + 1], r1_ref.at[hh], e1_ref.at[hh],
                                c0_ref.at[hh], e0_ref.at[hh], a0_ref.at[hh], a1_ref.at[hh])
        tied = t_h if tied is None else tied | t_h

    @pl.when(jnp.max(tied.astype(F32)) > 0.0)
    def _():
        for hh in range(heads_per_step):
            _peer_select_head_ties(s_ref[2 * hh], s_ref[2 * hh + 1], r1_ref.at[hh], e1_ref.at[hh],
                                   c0_ref.at[hh], e0_ref.at[hh], a0_ref.at[hh], a1_ref.at[hh], n_ref)


def _peer_select_head(s0, s1, r1_ref, e1_ref, c0_ref, e0_ref, a0_ref, a1_ref):
    _, removed0 = _top_values(s0, a0_ref, False)
    rank1, removed1 = _top_values(s1, a1_ref, True)
    a0 = a0_ref[...]
    a1 = a1_ref[...]
    a0_lo = a0[:_HALF_TOPK, :]
    row = lax.broadcasted_iota(jnp.int32, a0_lo.shape, 0)
    pieces = [a1[0:1, :] + a0]
    for b, limit in enumerate(_CAND_LIMITS, start=1):
        pieces.append(jnp.where(row < limit, a1[b:b + 1, :] + a0_lo, -jnp.inf))
    pieces.append(a1[_HALF_TOPK:, :] + a0[0:1, :])
    cand = jnp.concatenate(pieces, axis=0)
    c = cand
    for r in range(PEER_TOPK):
        tau = jnp.max(c, axis=0, keepdims=True)
        if r + 1 < PEER_TOPK:
            c = jnp.where(c == tau, -jnp.inf, c)
    sel = cand >= tau
    cmax = a0[0:1, :] + a1[0:1, :]
    z = jnp.sum(jnp.where(sel, jnp.exp(cand - cmax), 0.0), axis=0, keepdims=True)
    self32 = sel.astype(F32)
    n_hi = self32[_HALF_TOPK:PEER_TOPK, :]
    n_lo = self32[:_HALF_TOPK, :]
    for b in range(1, 8):
        lo = PEER_TOPK + (b - 1) * _HALF_TOPK
        n_lo = n_lo + self32[lo:lo + _HALF_TOPK, :]
    tail = jnp.sum(self32[PEER_TOPK + 7 * _HALF_TOPK:, :], axis=0, keepdims=True)
    n_lo = n_lo + jnp.where(row == 0, tail, 0.0)
    count0 = jnp.zeros(s0.shape, F32)
    for a in range(PEER_TOPK):
        n_a = n_lo[a:a + 1, :] if a < _HALF_TOPK else n_hi[a - _HALF_TOPK:a - _HALF_TOPK + 1, :]
        count0 = jnp.where(s0 == a0[a:a + 1, :], n_a, count0)
    r1_ref[...] = rank1.astype(r1_ref.dtype)
    c0_ref[...] = count0
    e0_ref[...] = jnp.exp(s0 - a0[0:1, :]) / z
    e1_ref[...] = jnp.exp(s1 - a1[0:1, :]).astype(e1_ref.dtype)

    n_sel = jnp.sum(self32, axis=0, keepdims=True)
    k = float(PEER_TOPK)
    return (removed0 != k) | (removed1 != k) | (n_sel != k)


def _peer_select_head_ties(s0, s1, r1_ref, e1_ref, c0_ref, e0_ref, a0_ref, a1_ref, n_ref):
    rank0, _ = _top_values(s0, a0_ref, True, one_per_round=True)
    rank1, _ = _top_values(s1, a1_ref, True, one_per_round=True)
    a0 = a0_ref[...]
    a1 = a1_ref[...]
    cand = jnp.concatenate([a0[a:a + 1, :] + a1 for a in range(PEER_TOPK)], axis=0)
    rows = lax.broadcasted_iota(jnp.int32, cand.shape, 0).astype(F32)
    c = cand
    sel = jnp.zeros(cand.shape, F32)
    for _ in range(PEER_TOPK):
        m = jnp.max(c, axis=0, keepdims=True)
        first = jnp.min(jnp.where(c == m, rows, float(cand.shape[0])), axis=0, keepdims=True)
        hit = rows == first
        sel = jnp.where(hit, 1.0, sel)
        c = jnp.where(hit, -jnp.inf, c)
    cmax = a0[0:1, :] + a1[0:1, :]
    z = jnp.sum(sel * jnp.exp(cand - cmax), axis=0, keepdims=True)
    for a in range(PEER_TOPK):
        n_ref[a:a + 1, :] = jnp.sum(sel[a * PEER_TOPK:(a + 1) * PEER_TOPK, :], axis=0, keepdims=True)
    n = n_ref[...]
    count0 = jnp.zeros(s0.shape, F32)
    for a in range(PEER_TOPK):
        count0 = jnp.where(rank0 == float(a), n[a:a + 1, :], count0)
    r1_ref[...] = rank1.astype(r1_ref.dtype)
    c0_ref[...] = count0
    e0_ref[...] = jnp.exp(s0 - a0[0:1, :]) / z
    e1_ref[...] = jnp.exp(s1 - a1[0:1, :]).astype(e1_ref.dtype)


def _peer_select(hx, wq_t, keys, layer, tt):
    t, d = hx.shape
    hps = SELECT_HEADS_PER_STEP
    o_spec = pl.BlockSpec((hps, N_KEYS, tt), lambda i, g: (g, 0, i))
    return pl.pallas_call(
        functools.partial(_peer_select_kernel, heads_per_step=hps),
        grid=(t // tt, PEER_HEADS // hps),
        in_specs=[pl.BlockSpec((tt, d), lambda i, g: (i, 0)),
                  pl.BlockSpec((hps * 2 * N_KEYS, d), lambda i, g: (g, 0)),
                  pl.BlockSpec((None, hps, 2, N_KEYS, N_KEYS), lambda i, g: (layer, g, 0, 0, 0))],
        out_specs=[o_spec] * 4,
        out_shape=[jax.ShapeDtypeStruct((PEER_HEADS, N_KEYS, t), dt) for dt in (BF16, BF16, F32, F32)],
        scratch_shapes=[pltpu.VMEM((2 * hps, N_KEYS, tt), F32),
                        pltpu.VMEM((hps, PEER_TOPK, tt), F32), pltpu.VMEM((hps, PEER_TOPK, tt), F32),
                        pltpu.VMEM((PEER_TOPK, tt), F32)],
        compiler_params=_params(("parallel", "arbitrary")),
        name="peer_select",
    )(hx, wq_t, keys)


def _sublane_bcast_bf16(row, rows):
    tile_rows = 16
    packed = jnp.broadcast_to(row, (tile_rows, row.shape[1])).astype(BF16)
    return jnp.tile(packed, (rows // tile_rows, 1))


def _peer_mix_kernel(x_ref, u_ref, vt_ref, r1_ref, e1_ref, c0_ref, e0_ref, res_ref, gate_ref, o_ref,
                     acc_ref, a_ref, g_ref, p_ref, *, n_i):
    j = pl.program_id(1)

    @pl.when(j == 0)
    def _():
        acc_ref[...] = jnp.zeros_like(acc_ref)

    x = x_ref[...]
    rows_per_chunk = 2 * N_KEYS
    for c in range(n_i // 2):
        lo = c * rows_per_chunk
        a_ref[c] = _bdot_nt(u_ref[lo:lo + rows_per_chunk, :], x)
    zero = jnp.zeros((), BF16)
    for ii in range(n_i):
        w = None
        for h in range(PEER_HEADS):
            count = _sublane_bcast_bf16(c0_ref[h, ii:ii + 1, :], N_KEYS)
            e0 = _sublane_bcast_bf16(e0_ref[h, ii:ii + 1, :], N_KEYS)
            term = jnp.where(r1_ref[h] < count, e1_ref[h] * e0, zero)
            w = term if w is None else w + term
        g_ref[ii] = w
    for c in range(n_i // 2):
        lo = c * rows_per_chunk
        gates = jnp.concatenate([g_ref[2 * c], g_ref[2 * c + 1]], axis=0)
        p_ref[lo:lo + rows_per_chunk, :] = gates * _gelu(a_ref[c]).astype(BF16)
    acc_ref[...] += jnp.dot(vt_ref[...], p_ref[...], preferred_element_type=F32)

    @pl.when(j == pl.num_programs(1) - 1)
    def _():
        o_ref[...] = res_ref[...] + (1.0 + gate_ref[...]) * acc_ref[...].T


def _peer_mix(hx, u_b, vt_b, layer, r1, e1, c0, e0, x_res, gate, l, tt, et):
    t, d = hx.shape
    e = u_b.shape[1]
    n_i = et // N_KEYS
    tok_all = pl.BlockSpec((PEER_HEADS, N_KEYS, tt), lambda i, j: (0, 0, i))
    tok_i = pl.BlockSpec((PEER_HEADS, n_i, tt), lambda i, j: (0, j, i))
    gate_arr, gate_spec = _gate_operand(gate, l, tt, d)
    return pl.pallas_call(
        functools.partial(_peer_mix_kernel, n_i=n_i),
        grid=(t // tt, e // et),
        in_specs=[pl.BlockSpec((tt, d), lambda i, j: (i, 0)),
                  pl.BlockSpec((None, et, d), lambda i, j: (layer, j, 0)),
                  pl.BlockSpec((None, d, et), lambda i, j: (layer, 0, j)),
                  tok_all, tok_all, tok_i, tok_i,
                  pl.BlockSpec((tt, d), lambda i, j: (i, 0)), gate_spec],
        out_specs=pl.BlockSpec((tt, d), lambda i, j: (i, 0)),
        out_shape=jax.ShapeDtypeStruct((t, d), F32),
        scratch_shapes=[pltpu.VMEM((d, tt), F32),
                        pltpu.VMEM((n_i // 2, 2 * N_KEYS, tt), F32),
                        pltpu.VMEM((n_i, N_KEYS, tt), BF16),
                        pltpu.VMEM((et, tt), BF16)],
        compiler_params=_params(("parallel", "arbitrary"), vmem=PEER_MIX_VMEM_LIMIT),
        name="peer_mix",
    )(hx, u_b, vt_b, r1, e1, c0, e0, x_res, gate_arr)


def _peer(hx, wq_t, keys, layer, u_b, vt_b, x_res, gate, l):
    t = hx.shape[0]
    tt = min(t, 512)
    r1, e1, c0, e0 = _peer_select(hx, wq_t, keys, layer, tt)
    return _peer_mix(hx, u_b, vt_b, layer, r1, e1, c0, e0, x_res, gate, l, tt, 1024)


def _rope_tables(pos, half, reps):
    inv = ROPE_THETA ** (-jnp.arange(half, dtype=F32) / half)
    ang = pos.astype(F32)[:, None] * inv[None, :]
    return jnp.tile(jnp.cos(ang), (1, reps)), jnp.tile(jnp.sin(ang), (1, reps))


def _rotate_half_cols(w):
    half = w.shape[-1] // 2
    return jnp.concatenate([-w[..., half:], w[..., :half]], axis=-1)


def _pad_last(w, width):
    return jnp.pad(w, [(0, 0)] * (w.ndim - 1) + [(0, width - w.shape[-1])])


def _shared_weights(p):
    out = {}
    w_rope = p["mla_w_dkv"][:, KV_LORA:]
    out["w_dkv_ext"] = jnp.concatenate(
        [p["mla_w_dkv"][:, :KV_LORA], _pad_last(w_rope, LANES), _pad_last(_rotate_half_cols(w_rope), LANES)],
        axis=1)
    w_uq = p["mla_w_uq"][0].reshape(-1, MLA_HEADS, MLA_NOPE + MLA_ROPE)
    q_lora = w_uq.shape[0]
    out["w_uq_nope"] = w_uq[:, :, :MLA_NOPE].reshape(q_lora, MLA_HEADS * MLA_NOPE)
    w_qr = w_uq[:, :, MLA_NOPE:]
    out["w_uq_rope"] = _pad_last(w_qr, LANES).reshape(q_lora, MLA_HEADS * LANES)
    out["w_uq_rot"] = _pad_last(_rotate_half_cols(w_qr), LANES).reshape(q_lora, MLA_HEADS * LANES)
    out["w_ukv_flat"] = jnp.concatenate([p["mla_w_uk"].reshape(KV_LORA, MLA_HEADS * MLA_NOPE),
                                         p["mla_w_uv"].reshape(KV_LORA, MLA_HEADS * MLA_VDIM)], axis=1)
    out["w_uk_h"] = jnp.transpose(p["mla_w_uk"], (1, 0, 2))
    out["w_uv_h"] = jnp.transpose(p["mla_w_uv"], (1, 0, 2))
    out["peer_wq_t"] = [p["peer_w_q"][l].T for l in range(2)]
    out["peer_u_b"] = p["peer_u"].astype(BF16)
    out["peer_vt_b"] = jnp.swapaxes(p["peer_v"].astype(BF16), 1, 2)
    return out


def _adaln(c_all, w, b, lead, n_out):
    m, d = c_all.shape
    tn = 2048
    if w.ndim == 2:
        w = w[None]
        b = b[None]
    b3 = b.reshape(b.shape[0], 1, n_out)
    return _linear(
        c_all, [w], n_cols=n_out, tm=m, tn=tn, epilogue=_ep_bias, prologue=_silu, w_lead=lead,
        extras=[b3], extra_specs=[pl.BlockSpec((None, 1, tn), lambda i, j: (lead, 0, j))],
        out_shapes=[jax.ShapeDtypeStruct((m, n_out), F32)],
        out_specs=[pl.BlockSpec((m, tn), lambda i, j: (0, j))])[0]


def _trunk(x, pos, mods, kv_mod, p, sw, ret_state, cache_lat, cache_kr):
    nb, l, d = x.shape
    t = nb * l
    pos_rows = jnp.tile(pos, nb) if l < 256 else pos
    table_rows = pos_rows.shape[0]
    assert table_rows % _row_tile(t, d) == 0, "a row tile must cover whole sequences or divide one"

    def vec(a):
        return a.reshape(nb, 1, d)

    def flat(a):
        return a.reshape(t, a.shape[-1])

    sh1, sc1, gt1, sh2, sc2, gt2 = [vec(m) for m in mods[0]]
    (h,) = _norm_mod(x, mods=[(p["norm_g"][0, 0][None], sh1, sc1)])
    h = flat(h)
    tm = _row_tile(t, d)
    cos_r, sin_r = _rope_tables(pos_rows, RET_DK // 2, 1)
    tspec = _table_spec(tm, table_rows)
    w_in = p["ret_w_in"]
    qk_cols = RET_HEADS * RET_DK
    v_cols = RET_HEADS * RET_DV
    tn = _col_tile(d, qk_cols)

    def rope_proj(col0, scale):
        return _linear(
            h, [w_in], n_cols=qk_cols, tm=tm, tn=tn, w_lead=0, col_block0=col0 // tn,
            epilogue=functools.partial(_ep_rope_half128, scale=scale),
            extras=[cos_r, sin_r], extra_specs=[tspec, tspec],
            out_shapes=[jax.ShapeDtypeStruct((t, qk_cols), BF16)],
            out_specs=[pl.BlockSpec((tm, tn), lambda i, j: (i, j))])[0]

    def plain_proj(x_in, w, n_cols, col0=0, lead=None, dtype=BF16, tm_=None):
        tm_ = tm_ or _row_tile(x_in.shape[0], x_in.shape[1])
        tn_ = _col_tile(x_in.shape[1], n_cols)
        return _linear(
            x_in, [w], n_cols=n_cols, tm=tm_, tn=tn_, w_lead=lead, col_block0=col0 // tn_,
            epilogue=_ep_plain,
            out_shapes=[jax.ShapeDtypeStruct((x_in.shape[0], n_cols), dtype)],
            out_specs=[pl.BlockSpec((tm_, tn_), lambda i, j: (i, j))])[0]

    def resid_proj(x_in, w, x_res, gate):
        tm_ = _row_tile(x_in.shape[0], x_in.shape[1])
        tn_ = _col_tile(x_in.shape[1], d)
        gate_arr, gate_spec = _gate_operand(gate, l, tm_, tn_)
        tile = pl.BlockSpec((tm_, tn_), lambda i, j: (i, j))
        return _linear(
            x_in, [w], n_cols=d, tm=tm_, tn=tn_, w_lead=0, epilogue=_ep_resid,
            extras=[x_res, gate_arr], extra_specs=[tile, gate_spec],
            out_shapes=[jax.ShapeDtypeStruct((t, d), F32)], out_specs=[tile])[0]

    def heads_proj(x_in, ws, n_cols, epilogue=_ep_heads, extras=(), extra_specs=()):
        tm_ = _row_tile(x_in.shape[0], x_in.shape[1])
        tn_ = _col_tile(x_in.shape[1], n_cols, len(ws))
        return _linear(
            x_in, ws, n_cols=n_cols, tm=tm_, tn=tn_, epilogue=epilogue, extras=extras, extra_specs=extra_specs,
            out_shapes=[jax.ShapeDtypeStruct((n_cols // LANES, x_in.shape[0], LANES), BF16)],
            out_specs=[pl.BlockSpec((tn_ // LANES, tm_, LANES), lambda i, j: (j, i, 0))])[0]

    q_r = rope_proj(0, 1.0)
    k_r = rope_proj(qk_cols, RET_DK ** -0.5)
    v_r = plain_proj(h, w_in, v_cols, col0=2 * qk_cols, lead=0)
    g_r = plain_proj(h, w_in, v_cols, col0=2 * qk_cols + v_cols, lead=0)
    s0 = None if ret_state is None else ret_state[0]
    y_r, s_new = _retention(q_r, k_r, v_r, g_r, p["ret_gn_g"], s0, nb, l)
    xf = resid_proj(y_r, p["ret_w_out"], flat(x), gt1)
    (h,) = _norm_mod(xf.reshape(nb, l, d), mods=[(p["norm_g"][0, 1][None], sh2, sc2)])
    xf = _peer(flat(h), sw["peer_wq_t"][0], p["peer_keys"], 0, sw["peer_u_b"], sw["peer_vt_b"],
               xf, gt2, l)

    sh1, sc1, gt1b, sh2, sc2, gt2b = [vec(m) for m in mods[1]]
    kv_sh, kv_sc = [vec(m) for m in kv_mod]
    h, h_kv = _norm_mod(xf.reshape(nb, l, d),
                         mods=[(p["norm_g"][1, 0][None], sh1, sc1), (p["kv_norm_g"][None], kv_sh, kv_sc)])
    h = flat(h)
    h_kv = flat(h_kv)
    cos_m, sin_m = _rope_tables(pos_rows, MLA_ROPE // 2, LANES // (MLA_ROPE // 2))
    n_ext = sw["w_dkv_ext"].shape[1]
    lat, kr, kr_pad = _linear(
        h_kv, [sw["w_dkv_ext"]], n_cols=n_ext, tm=tm, tn=n_ext, epilogue=_ep_kv,
        extras=[p["mla_kv_norm_g"][None], cos_m, sin_m],
        extra_specs=[pl.BlockSpec((1, KV_LORA), lambda i, j: (0, 0)), tspec, tspec],
        out_shapes=[jax.ShapeDtypeStruct((t, KV_LORA), F32), jax.ShapeDtypeStruct((t, MLA_ROPE), F32),
                    jax.ShapeDtypeStruct((t, LANES), BF16)],
        out_specs=[pl.BlockSpec((tm, KV_LORA), lambda i, j: (i, 0)),
                   pl.BlockSpec((tm, MLA_ROPE), lambda i, j: (i, 0)),
                   pl.BlockSpec((tm, LANES), lambda i, j: (i, 0))])
    q_lora = p["mla_w_dq"].shape[-1]
    cq = _linear(
        h, [p["mla_w_dq"]], n_cols=q_lora, tm=tm, tn=q_lora, w_lead=0, epilogue=_ep_rms,
        extras=[p["mla_q_norm_g"]], extra_specs=[pl.BlockSpec((1, q_lora), lambda i, j: (0, 0))],
        out_shapes=[jax.ShapeDtypeStruct((t, q_lora), BF16)],
        out_specs=[pl.BlockSpec((tm, q_lora), lambda i, j: (i, 0))])[0]
    hl = MLA_HEADS * LANES
    qn = heads_proj(cq, [sw["w_uq_nope"]], hl)
    qr = heads_proj(cq, [sw["w_uq_rope"], sw["w_uq_rot"]], hl, epilogue=_ep_rot_heads,
                    extras=[cos_m, sin_m], extra_specs=[tspec, tspec])
    if cache_lat is None:
        kv_h = heads_proj(lat, [sw["w_ukv_flat"]], 2 * hl)
        o = _attn_prompt(qn, qr, kv_h, kr_pad, nb, l)
    else:
        ckr = _pad_last(cache_kr, LANES).astype(BF16)
        o = _attn_cached(qn, qr, cache_lat, ckr, lat, kr_pad, sw["w_uk_h"], sw["w_uv_h"], nb, l)
    xf = resid_proj(o, p["mla_w_o"], xf, gt1b)
    (h,) = _norm_mod(xf.reshape(nb, l, d), mods=[(p["norm_g"][1, 1][None], sh2, sc2)])
    xf = _peer(flat(h), sw["peer_wq_t"][1], p["peer_keys"], 1, sw["peer_u_b"], sw["peer_vt_b"],
               xf, gt2b, l)
    (y,) = _norm_mod(xf.reshape(nb, l, d), final_g=p["final_g"][None])
    return y, s_new[None], lat.reshape(nb, l, KV_LORA), kr.reshape(nb, l, MLA_ROPE)


def kernel(x_prompt, x_sample, c_prompt, c_sample, state_retention, cache_mla_latent, cache_mla_krope,
           ada_w, ada_b, norm_g, ret_w_in, ret_gn_g, ret_w_out,
           kv_ada_w, kv_ada_b, kv_norm_g, mla_w_dkv, mla_kv_norm_g, mla_w_uk, mla_w_uv,
           mla_w_dq, mla_q_norm_g, mla_w_uq, mla_w_o,
           peer_w_q, peer_keys, peer_u, peer_v, final_g):
    p = dict(norm_g=norm_g, ret_w_in=ret_w_in, ret_gn_g=ret_gn_g[0][None], ret_w_out=ret_w_out,
             kv_norm_g=kv_norm_g, mla_w_dkv=mla_w_dkv, mla_kv_norm_g=mla_kv_norm_g,
             mla_w_uk=mla_w_uk, mla_w_uv=mla_w_uv, mla_w_dq=mla_w_dq, mla_q_norm_g=mla_q_norm_g,
             mla_w_uq=mla_w_uq, mla_w_o=mla_w_o, peer_w_q=peer_w_q, peer_keys=peer_keys,
             peer_u=peer_u, peer_v=peer_v, final_g=final_g)
    sw = _shared_weights(p)
    nbp, lp, d = x_prompt.shape
    nbs, ls, _ = x_sample.shape
    past = cache_mla_latent.shape[1]

    n_c = nbp + nbs
    c_all = jnp.pad(jnp.concatenate([c_prompt, c_sample], axis=0), ((0, (-n_c) % 8), (0, 0)))
    layer_mods = [_adaln(c_all, ada_w, ada_b, layer, 6 * d) for layer in range(2)]
    kv_mods = _adaln(c_all, kv_ada_w, kv_ada_b, 0, 2 * d)

    def stream_mods(lo, hi):
        return ([jnp.split(m[lo:hi], 6, axis=-1) for m in layer_mods], jnp.split(kv_mods[lo:hi], 2, axis=-1))

    mods_p, kv_p = stream_mods(0, nbp)
    mods_s, kv_s = stream_mods(nbp, n_c)
    pos_p = jnp.arange(lp, dtype=jnp.int32)
    pos_s = past + jnp.arange(ls, dtype=jnp.int32)
    y_p, ret_p, lat_p, kr_p = _trunk(x_prompt, pos_p, mods_p, kv_p, p, sw, None, None, None)
    y_s, ret_s, lat_s, kr_s = _trunk(x_sample, pos_s, mods_s, kv_s, p, sw, state_retention,
                                     cache_mla_latent, cache_mla_krope)
    return (y_p, y_s, ret_p, ret_s, lat_p, kr_p, lat_s, kr_s)
```

```python
import functools
import math

import jax
import jax.numpy as jnp
from jax import lax
from jax.experimental import pallas as pl
from jax.experimental.pallas import tpu as pltpu

F32 = jnp.float32
BF16 = jnp.bfloat16

EPS = 1e-6
ROPE_THETA = 10000.0
NEG_INF = -1e30
CHUNK = 64

RET_HEADS = 8
RET_DK = 256
RET_DV = 512

MLA_HEADS = 16
MLA_NOPE = 128
MLA_ROPE = 64
MLA_VDIM = 128
KV_LORA = 512
MLA_SCALE = (MLA_NOPE + MLA_ROPE) ** -0.5

PEER_HEADS = 8
N_KEYS = 128
PEER_TOPK = 16

LANES = 128
VMEM_LIMIT = 52 * 1024 * 1024
PEER_MIX_VMEM_LIMIT = 58 * 1024 * 1024
SELECT_HEADS_PER_STEP = 4


def _params(sem, vmem=VMEM_LIMIT, flags=None):
    return pltpu.CompilerParams(dimension_semantics=sem, vmem_limit_bytes=vmem, flags=flags)


def _bdot(a, b):
    return jnp.dot(a.astype(BF16), b.astype(BF16), preferred_element_type=F32)


def _bdot_nt(a, b):
    return lax.dot_general(a.astype(BF16), b.astype(BF16), (((1,), (1,)), ((), ())),
                           preferred_element_type=F32)


def _bdot_tn(a, b):
    return lax.dot_general(a.astype(BF16), b.astype(BF16), (((0,), (0,)), ((), ())),
                           preferred_element_type=F32)


def _silu(x):
    return x * (1.0 / (1.0 + jnp.exp(-x)))


def _gelu(x):
    return 0.5 * x * (1.0 + lax.erf(x * (0.5 ** 0.5)))


def _linear_kernel(*refs, n_w, n_extra, prologue, epilogue):
    x_ref = refs[0]
    w_refs = refs[1:1 + n_w]
    extra_refs = refs[1 + n_w:1 + n_w + n_extra]
    out_refs = refs[1 + n_w + n_extra:]
    xv = x_ref[...]
    if prologue is not None:
        xv = prologue(xv)
    xb = xv.astype(BF16)
    accs = [jnp.dot(xb, w[...].astype(BF16), preferred_element_type=F32) for w in w_refs]
    epilogue(accs, extra_refs, out_refs)


def _linear(x, ws, *, n_cols, tm, tn, epilogue, out_shapes, out_specs, w_lead=None, col_block0=0,
            extras=(), extra_specs=(), prologue=None, name=None):
    m, k = x.shape
    if name is None:
        name = "linear" + getattr(epilogue, "func", epilogue).__name__
    assert m % tm == 0 and n_cols % tn == 0
    if w_lead is None:
        w_spec = pl.BlockSpec((k, tn), lambda i, j: (0, j + col_block0))
    else:
        w_spec = pl.BlockSpec((None, k, tn), lambda i, j: (w_lead, 0, j + col_block0))
    kern = functools.partial(_linear_kernel, n_w=len(ws), n_extra=len(extras), prologue=prologue,
                             epilogue=epilogue)
    return pl.pallas_call(
        kern,
        grid=(m // tm, n_cols // tn),
        in_specs=[pl.BlockSpec((tm, k), lambda i, j: (i, 0))] + [w_spec] * len(ws) + list(extra_specs),
        out_specs=out_specs,
        out_shape=out_shapes,
        compiler_params=_params(("parallel", "arbitrary")),
        name=name,
    )(x, *ws, *extras)


def _ep_plain(accs, extras, outs):
    outs[0][...] = accs[0].astype(outs[0].dtype)


def _ep_heads(accs, extras, outs):
    acc = accs[0]
    for jj in range(acc.shape[1] // LANES):
        outs[0][jj] = acc[:, jj * LANES:(jj + 1) * LANES].astype(outs[0].dtype)


def _ep_bias(accs, extras, outs):
    outs[0][...] = accs[0] + extras[0][...]


def _ep_resid(accs, extras, outs):
    outs[0][...] = extras[0][...] + (1.0 + extras[1][...]) * accs[0]


def _gate_operand(gate, l, rows_per_tile, cols_per_tile):
    nb, _, d = gate.shape
    col = (lambda j: j) if cols_per_tile < d else (lambda j: 0)
    if l % rows_per_tile == 0:
        tiles_per_batch = l // rows_per_tile
        return gate, pl.BlockSpec((None, 1, cols_per_tile), lambda i, j: (i // tiles_per_batch, 0, col(j)))
    rows = jnp.broadcast_to(gate, (nb, l, d)).reshape(nb * l, d)
    return rows, pl.BlockSpec((rows_per_tile, cols_per_tile), lambda i, j: (i, col(j)))


def _ep_rope_half128(accs, extras, outs, *, scale):
    acc = accs[0]
    cos = extras[0][...]
    sin = extras[1][...]
    for g in range(acc.shape[1] // 256):
        x1 = acc[:, g * 256:g * 256 + 128]
        x2 = acc[:, g * 256 + 128:(g + 1) * 256]
        outs[0][:, g * 256:g * 256 + 128] = ((x1 * cos - x2 * sin) * scale).astype(outs[0].dtype)
        outs[0][:, g * 256 + 128:(g + 1) * 256] = ((x1 * sin + x2 * cos) * scale).astype(outs[0].dtype)


def _ep_rot_heads(accs, extras, outs):
    cos = extras[0][...]
    sin = extras[1][...]
    a, b = accs
    for jj in range(a.shape[1] // LANES):
        sl = slice(jj * LANES, (jj + 1) * LANES)
        outs[0][jj] = (a[:, sl] * cos + b[:, sl] * sin).astype(outs[0].dtype)


def _ep_rms(accs, extras, outs):
    acc = accs[0]
    g = extras[0][...]
    y = acc * lax.rsqrt(jnp.mean(acc * acc, axis=-1, keepdims=True) + EPS) * g
    outs[0][...] = y.astype(outs[0].dtype)


def _ep_kv(accs, extras, outs):
    acc = accs[0]
    g = extras[0][...]
    cos = extras[1][...]
    sin = extras[2][...]
    c = acc[:, :KV_LORA]
    outs[0][...] = c * lax.rsqrt(jnp.mean(c * c, axis=-1, keepdims=True) + EPS) * g
    kr = acc[:, KV_LORA:KV_LORA + LANES] * cos + acc[:, KV_LORA + LANES:KV_LORA + 2 * LANES] * sin
    outs[1][...] = kr[:, :MLA_ROPE]
    outs[2][...] = kr.astype(BF16)


def _row_tile(m, k):
    return min(m, 1024)


def _col_tile(k, n, n_weights=1):
    for tn in (1024, 512):
        if n % tn == 0 and 2 * n_weights * k * tn * 4 <= 16 * 1024 * 1024:
            return tn
    return min(n, 512)


def _table_spec(tm, table_rows):
    nblk = table_rows // tm
    return pl.BlockSpec((tm, LANES), lambda i, j: (i % nblk, 0))


def _norm_mod_kernel(*refs, n_mod):
    x = refs[0][...]
    xn = x * lax.rsqrt(jnp.mean(x * x, axis=-1, keepdims=True) + EPS)
    if n_mod == 0:
        refs[2][...] = xn * refs[1][...]
        return
    outs = refs[1 + 3 * n_mod:]
    for m in range(n_mod):
        g, sh, sc = (refs[1 + 3 * m + k][...] for k in range(3))
        outs[m][...] = ((xn * g) * (1.0 + sc) + sh).astype(outs[m].dtype)


def _norm_mod(x, mods=(), final_g=None):
    nb, l, d = x.shape
    tl = min(l, 512)
    tok = pl.BlockSpec((None, tl, d), lambda b, i: (b, i, 0))
    per_b = pl.BlockSpec((None, 1, d), lambda b, i: (b, 0, 0))
    gain = pl.BlockSpec((1, d), lambda b, i: (0, 0))
    args, specs = [x], [tok]
    if final_g is not None:
        assert not mods
        args.append(final_g)
        specs.append(gain)
        out_shapes = [jax.ShapeDtypeStruct((nb, l, d), F32)]
    else:
        for g, sh, sc in mods:
            args += [g, sh, sc]
            specs += [gain, per_b, per_b]
        out_shapes = [jax.ShapeDtypeStruct((nb, l, d), BF16) for _ in mods]
    return pl.pallas_call(
        functools.partial(_norm_mod_kernel, n_mod=len(mods)),
        grid=(nb, l // tl), in_specs=specs, out_specs=[tok] * len(out_shapes), out_shape=out_shapes,
        compiler_params=_params(("parallel", "parallel")),
        name="norm_mod",
    )(*args)


def _ret_log_decay():
    return jnp.log1p(-jnp.exp2(-5.0 - jnp.arange(RET_HEADS, dtype=F32)))


def _retention_tables(c):
    log_g = _ret_log_decay()[:, None, None]
    n = jnp.arange(c, dtype=F32)
    dist = n[:, None] - n[None, :]
    same = (jnp.arange(c)[:, None] // CHUNK) == (jnp.arange(c)[None, :] // CHUNK)
    earlier = (jnp.arange(c)[None, :] // CHUNK) < (jnp.arange(c)[:, None] // CHUNK)
    mask = jnp.where(same[None], jnp.exp(jnp.abs(dist)[None] * log_g),
                     jnp.where(earlier[None], jnp.exp(dist[None] * log_g), 0.0))
    q_decay = jnp.exp((n[None, :, None] + 1.0) * log_g)
    k_decay = jnp.exp((c - 1.0 - n)[None, :, None] * log_g)
    blk_decay = jnp.exp(c * log_g)
    return mask, q_decay, k_decay, blk_decay


def _retention_kernel(*refs, has_s0, hps):
    if has_s0:
        (q_ref, k_ref, v_ref, g_ref, mask_ref, qd_ref, kd_ref, bd_ref, gn_ref, s0_ref,
         y_ref, s_out_ref, s_ref) = refs
    else:
        (q_ref, k_ref, v_ref, g_ref, mask_ref, qd_ref, kd_ref, bd_ref, gn_ref,
         y_ref, s_out_ref, s_ref) = refs
    c = pl.program_id(2)

    @pl.when(c == 0)
    def _():
        if has_s0:
            s_ref[...] = s0_ref[...]
        else:
            s_ref[...] = jnp.zeros_like(s_ref)

    for hh in range(hps):
        qk_cols = slice(hh * RET_DK, (hh + 1) * RET_DK)
        v_cols = slice(hh * RET_DV, (hh + 1) * RET_DV)
        q = q_ref[:, qk_cols]
        k = k_ref[:, qk_cols]
        v = v_ref[:, v_cols]
        s_prev = s_ref[hh]
        scores = _bdot_nt(q, k) * mask_ref[hh]
        y = _bdot(scores, v) + _bdot(q, s_prev) * qd_ref[hh]
        k_scaled = k.astype(F32) * kd_ref[hh]
        s_ref[hh] = bd_ref[hh] * s_prev + _bdot_tn(k_scaled, v)

        mu = jnp.mean(y, axis=-1, keepdims=True)
        yc = y - mu
        var = jnp.mean(yc * yc, axis=-1, keepdims=True)
        yn = yc * lax.rsqrt(var + EPS) * gn_ref[:, v_cols]
        y_ref[:, v_cols] = (_silu(g_ref[:, v_cols].astype(F32)) * yn).astype(y_ref.dtype)

    @pl.when(c == pl.num_programs(2) - 1)
    def _():
        s_out_ref[...] = s_ref[...]


def _retention(q, k, v, g, gn_g, s0, nb, l):
    cb = min(l, 256)
    nc = l // cb
    hps = 4 if nc > 1 else RET_HEADS
    mask, qd, kd, bd = _retention_tables(cb)
    row = lambda b, hg, c: (b * nc + c, hg)
    per_h3 = lambda b, hg, c: (hg, 0, 0)
    in_specs = [
        pl.BlockSpec((cb, hps * RET_DK), row), pl.BlockSpec((cb, hps * RET_DK), row),
        pl.BlockSpec((cb, hps * RET_DV), row), pl.BlockSpec((cb, hps * RET_DV), row),
        pl.BlockSpec((hps, cb, cb), per_h3), pl.BlockSpec((hps, cb, 1), per_h3),
        pl.BlockSpec((hps, cb, 1), per_h3), pl.BlockSpec((hps, 1, 1), per_h3),
        pl.BlockSpec((1, hps * RET_DV), lambda b, hg, c: (0, hg)),
    ]
    args = [q, k, v, g, mask, qd, kd, bd, gn_g]
    state_spec = pl.BlockSpec((None, hps, RET_DK, RET_DV), lambda b, hg, c: (b, hg, 0, 0))
    if s0 is not None:
        in_specs.append(state_spec)
        args.append(s0)
    y, s_new = pl.pallas_call(
        functools.partial(_retention_kernel, has_s0=s0 is not None, hps=hps),
        grid=(nb, RET_HEADS // hps, nc),
        in_specs=in_specs,
        out_specs=[pl.BlockSpec((cb, hps * RET_DV), row), state_spec],
        out_shape=[jax.ShapeDtypeStruct((nb * l, RET_HEADS * RET_DV), BF16),
                   jax.ShapeDtypeStruct((nb, RET_HEADS, RET_DK, RET_DV), F32)],
        scratch_shapes=[pltpu.VMEM((hps, RET_DK, RET_DV), F32)],
        compiler_params=_params(("parallel", "parallel", "arbitrary")),
        name="retention",
    )(*args)
    return y, s_new


def _attn_prompt_kernel(qi_ref, ki_ref, qn_ref, qr_ref, kn_ref, v_ref, kr_ref, o_ref, m_ref, acc_ref, *, tq, tk):
    qi = qi_ref[pl.program_id(1)]
    ki = ki_ref[pl.program_id(1)]
    exp2_scale = MLA_SCALE * math.log2(math.e)

    @pl.when(ki == 0)
    def _():
        m_ref[...] = jnp.full_like(m_ref, NEG_INF)
        acc_ref[...] = jnp.zeros_like(acc_ref)

    def block(masked):
        if masked:
            q_chunk = (qi * tq + lax.broadcasted_iota(jnp.int32, (tq, tk), 0)) // CHUNK
            k_chunk = (ki * tk + lax.broadcasted_iota(jnp.int32, (tq, tk), 1)) // CHUNK
            visible = k_chunk <= q_chunk
        kr = kr_ref[...]
        ones = jnp.ones((tk, LANES), BF16)

        def head(h, carry):
            q = jnp.concatenate([qn_ref[h], qr_ref[h]], axis=1)
            k = jnp.concatenate([kn_ref[h], kr], axis=1)
            s = _bdot_nt(q, k)
            if masked:
                s = jnp.where(visible, s, NEG_INF)
            m_prev = m_ref[h]
            m_new = jnp.maximum(m_prev, jnp.max(s, axis=-1, keepdims=True))
            alpha = jnp.exp2((m_prev - m_new) * exp2_scale)
            p = jnp.exp2((s - jnp.tile(m_new, (1, tk // LANES))) * exp2_scale)
            v_ext = jnp.concatenate([v_ref[h], ones], axis=1)
            acc_ref[h] = jnp.tile(alpha, (1, 2)) * acc_ref[h] + _bdot(p, v_ext)
            m_ref[h] = m_new
            return carry

        lax.fori_loop(0, MLA_HEADS, head, 0, unroll=True)

    @pl.when(ki < qi)
    def _():
        block(False)

    @pl.when(ki == qi)
    def _():
        block(True)
        for h in range(MLA_HEADS):
            acc = acc_ref[h]
            o_ref[:, h * MLA_VDIM:(h + 1) * MLA_VDIM] = (
                acc[:, :MLA_VDIM] / acc[:, MLA_VDIM:]).astype(o_ref.dtype)


def _attn_prompt(qn, qr, kv, kr, nb, l):
    tq = tk = min(l, 512)
    nq = l // tq
    pairs = [(qi, ki) for qi in range(nq) for ki in range(qi + 1)]
    qi_of = jnp.asarray([qk[0] for qk in pairs], jnp.int32)
    ki_of = jnp.asarray([qk[1] for qk in pairs], jnp.int32)
    q_spec = pl.BlockSpec((MLA_HEADS, tq, LANES), lambda b, s, qi, ki: (0, b * nq + qi[s], 0))
    k_spec = pl.BlockSpec((MLA_HEADS, tk, LANES), lambda b, s, qi, ki: (0, b * nq + ki[s], 0))
    v_spec = pl.BlockSpec((MLA_HEADS, tk, LANES), lambda b, s, qi, ki: (1, b * nq + ki[s], 0))
    return pl.pallas_call(
        functools.partial(_attn_prompt_kernel, tq=tq, tk=tk),
        grid_spec=pltpu.PrefetchScalarGridSpec(
            num_scalar_prefetch=2,
            grid=(nb, len(pairs)),
            in_specs=[q_spec, q_spec, k_spec, v_spec,
                      pl.BlockSpec((tk, LANES), lambda b, s, qi, ki: (b * nq + ki[s], 0))],
            out_specs=pl.BlockSpec((tq, MLA_HEADS * MLA_VDIM), lambda b, s, qi, ki: (b * nq + qi[s], 0)),
            scratch_shapes=[pltpu.VMEM((MLA_HEADS, tq, LANES), F32),
                            pltpu.VMEM((MLA_HEADS, tq, MLA_VDIM + LANES), F32)]),
        out_shape=jax.ShapeDtypeStruct((nb * l, MLA_HEADS * MLA_VDIM), BF16),
        compiler_params=_params(("parallel", "arbitrary")),
        name="attn_prompt",
    )(qi_of, ki_of, qn, qr, kv, kv, kr)


def _attn_cached_kernel(qn_ref, qr_ref, clat_ref, ckr_ref, nlat_ref, nkr_ref, wuk_ref, wuv_ref,
                        o_ref, ql_ref, qrs_ref, *, lq):
    for h in range(MLA_HEADS):
        ql_ref[h * lq:(h + 1) * lq, :] = _bdot_nt(qn_ref[h], wuk_ref[h]).astype(BF16)
        qrs_ref[h * lq:(h + 1) * lq, :] = qr_ref[h]
    ql = ql_ref[...]
    qr = qrs_ref[...]
    clat = clat_ref[...].astype(BF16)
    nlat = nlat_ref[...].astype(BF16)
    s_c = (_bdot_nt(ql, clat) + _bdot_nt(qr, ckr_ref[...])) * MLA_SCALE
    s_n = (_bdot_nt(ql, nlat) + _bdot_nt(qr, nkr_ref[...])) * MLA_SCALE
    m = jnp.maximum(jnp.max(s_c, axis=-1, keepdims=True), jnp.max(s_n, axis=-1, keepdims=True))
    p_c = jnp.exp(s_c - m)
    p_n = jnp.exp(s_n - m)
    denom = jnp.sum(p_c, axis=-1, keepdims=True) + jnp.sum(p_n, axis=-1, keepdims=True)
    o_lat = (_bdot(p_c, clat) + _bdot(p_n, nlat)) / denom
    for h in range(MLA_HEADS):
        o_ref[:, h * MLA_VDIM:(h + 1) * MLA_VDIM] = _bdot(
            o_lat[h * lq:(h + 1) * lq, :], wuv_ref[h]).astype(o_ref.dtype)


def _attn_cached(qn, qr, cache_lat, cache_kr_pad, new_lat, new_kr_pad, wuk_h, wuv_h, nb, lq):
    past = cache_lat.shape[1]
    whole = lambda b: (0, 0, 0)
    return pl.pallas_call(
        functools.partial(_attn_cached_kernel, lq=lq),
        grid=(nb,),
        in_specs=[
            pl.BlockSpec((MLA_HEADS, lq, LANES), lambda b: (0, b, 0)),
            pl.BlockSpec((MLA_HEADS, lq, LANES), lambda b: (0, b, 0)),
            pl.BlockSpec((None, past, KV_LORA), lambda b: (b, 0, 0)),
            pl.BlockSpec((None, past, LANES), lambda b: (b, 0, 0)),
            pl.BlockSpec((lq, KV_LORA), lambda b: (b, 0)),
            pl.BlockSpec((lq, LANES), lambda b: (b, 0)),
            pl.BlockSpec((MLA_HEADS, KV_LORA, MLA_NOPE), whole),
            pl.BlockSpec((MLA_HEADS, KV_LORA, MLA_VDIM), whole),
        ],
        out_specs=pl.BlockSpec((lq, MLA_HEADS * MLA_VDIM), lambda b: (b, 0)),
        out_shape=jax.ShapeDtypeStruct((nb * lq, MLA_HEADS * MLA_VDIM), BF16),
        scratch_shapes=[pltpu.VMEM((MLA_HEADS * lq, KV_LORA), BF16),
                        pltpu.VMEM((MLA_HEADS * lq, LANES), BF16)],
        compiler_params=_params(("parallel",)),
        name="attn_cached",
    )(qn, qr, cache_lat, cache_kr_pad, new_lat, new_kr_pad, wuk_h, wuv_h)


def _top_values(s, top_ref, want_rank, one_per_round=False):
    rank = jnp.full(s.shape, float(PEER_TOPK), F32) if want_rank else None
    rows = lax.broadcasted_iota(jnp.int32, s.shape, 0).astype(F32) if one_per_round else None
    for r in range(PEER_TOPK):
        m = jnp.max(s, axis=0, keepdims=True)
        top_ref[r:r + 1, :] = m
        hit = s == m
        if one_per_round:
            first = jnp.min(jnp.where(hit, rows, float(s.shape[0])), axis=0, keepdims=True)
            hit = rows == first
        if want_rank:
            rank = jnp.where(hit, float(r), rank)
        s = jnp.where(hit, -jnp.inf, s)
    removed = jnp.sum((s == -jnp.inf).astype(F32), axis=0, keepdims=True)
    return rank, removed


_CAND_LIMITS = tuple(PEER_TOPK // (b + 1) for b in range(1, 8))
_HALF_TOPK = PEER_TOPK // 2


def _peer_select_kernel(h_ref, wq_ref, keys_ref, r1_ref, e1_ref, c0_ref, e0_ref, s_ref, a0_ref, a1_ref,
                        n_ref, *, heads_per_step):
    hx = h_ref[...]
    for hh in range(heads_per_step):
        q_t = _bdot_nt(wq_ref[hh * 2 * N_KEYS:(hh + 1) * 2 * N_KEYS, :], hx)
        s_ref[2 * hh] = _bdot(keys_ref[hh, 0], q_t[:N_KEYS])
        s_ref[2 * hh + 1] = _bdot(keys_ref[hh, 1], q_t[N_KEYS:])
    tied = [_peer_select_head(s_ref[2 * hh], s_ref[2 * hh + 1], r1_ref.at[hh], e1_ref.at[hh],
                              c0_ref.at[hh], e0_ref.at[hh], a0_ref.at[hh], a1_ref.at[hh])
            for hh in range(heads_per_step)]

    def redo(hh, cols):
        _peer_select_head_ties(s_ref[2 * hh, :, cols], s_ref[2 * hh + 1, :, cols],
                               r1_ref.at[hh, :, cols], e1_ref.at[hh, :, cols], c0_ref.at[hh, :, cols],
                               e0_ref.at[hh, :, cols], a0_ref.at[hh, :, cols], a1_ref.at[hh, :, cols],
                               n_ref.at[:, cols])

    for hh in range(heads_per_step):
        for lo in range(0, hx.shape[0], LANES):
            cols = slice(lo, lo + LANES)
            pl.when(jnp.max(tied[hh][:, cols].astype(F32)) > 0.0)(functools.partial(redo, hh, cols))


def _peer_select_head(s0, s1, r1_ref, e1_ref, c0_ref, e0_ref, a0_ref, a1_ref):
    _, removed0 = _top_values(s0, a0_ref, False)
    rank1, removed1 = _top_values(s1, a1_ref, True)
    a0 = a0_ref[...]
    a1 = a1_ref[...]
    a0_lo = a0[:_HALF_TOPK, :]
    row = lax.broadcasted_iota(jnp.int32, a0_lo.shape, 0)
    pieces = [a1[0:1, :] + a0]
    for b, limit in enumerate(_CAND_LIMITS, start=1):
        pieces.append(jnp.where(row < limit, a1[b:b + 1, :] + a0_lo, -jnp.inf))
    pieces.append(a1[_HALF_TOPK:, :] + a0[0:1, :])
    cand = jnp.concatenate(pieces, axis=0)
    c = cand
    for r in range(PEER_TOPK):
        tau = jnp.max(c, axis=0, keepdims=True)
        if r + 1 < PEER_TOPK:
            c = jnp.where(c == tau, -jnp.inf, c)
    sel = cand >= tau
    cmax = a0[0:1, :] + a1[0:1, :]
    z = jnp.sum(jnp.where(sel, jnp.exp(cand - cmax), 0.0), axis=0, keepdims=True)
    self32 = sel.astype(F32)
    n_hi = self32[_HALF_TOPK:PEER_TOPK, :]
    n_lo = self32[:_HALF_TOPK, :]
    for b in range(1, 8):
        lo = PEER_TOPK + (b - 1) * _HALF_TOPK
        n_lo = n_lo + self32[lo:lo + _HALF_TOPK, :]
    tail = jnp.sum(self32[PEER_TOPK + 7 * _HALF_TOPK:, :], axis=0, keepdims=True)
    n_lo = n_lo + jnp.where(row == 0, tail, 0.0)
    count0 = jnp.zeros(s0.shape, F32)
    for a in range(PEER_TOPK):
        n_a = n_lo[a:a + 1, :] if a < _HALF_TOPK else n_hi[a - _HALF_TOPK:a - _HALF_TOPK + 1, :]
        count0 = jnp.where(s0 == a0[a:a + 1, :], n_a, count0)
    r1_ref[...] = rank1.astype(r1_ref.dtype)
    c0_ref[...] = count0
    e0_ref[...] = jnp.exp(s0 - a0[0:1, :]) / z
    e1_ref[...] = jnp.exp(s1 - a1[0:1, :]).astype(e1_ref.dtype)

    n_sel = jnp.sum(self32, axis=0, keepdims=True)
    k = float(PEER_TOPK)
    return (removed0 != k) | (removed1 != k) | (n_sel != k)


def _peer_select_head_ties(s0, s1, r1_ref, e1_ref, c0_ref, e0_ref, a0_ref, a1_ref, n_ref):
    rank0, _ = _top_values(s0, a0_ref, True, one_per_round=True)
    rank1, _ = _top_values(s1, a1_ref, True, one_per_round=True)
    a0 = a0_ref[...]
    a1 = a1_ref[...]
    cand = jnp.concatenate([a0[a:a + 1, :] + a1 for a in range(PEER_TOPK)], axis=0)
    rows = lax.broadcasted_iota(jnp.int32, cand.shape, 0).astype(F32)
    c = cand
    sel = jnp.zeros(cand.shape, F32)
    for _ in range(PEER_TOPK):
        m = jnp.max(c, axis=0, keepdims=True)
        first = jnp.min(jnp.where(c == m, rows, float(cand.shape[0])), axis=0, keepdims=True)
        hit = rows == first
        sel = jnp.where(hit, 1.0, sel)
        c = jnp.where(hit, -jnp.inf, c)
    cmax = a0[0:1, :] + a1[0:1, :]
    z = jnp.sum(sel * jnp.exp(cand - cmax), axis=0, keepdims=True)
    for a in range(PEER_TOPK):
        n_ref[a:a + 1, :] = jnp.sum(sel[a * PEER_TOPK:(a + 1) * PEER_TOPK, :], axis=0, keepdims=True)
    n = n_ref[...]
    count0 = jnp.zeros(s0.shape, F32)
    for a in range(PEER_TOPK):
        count0 = jnp.where(rank0 == float(a), n[a:a + 1, :], count0)
    r1_ref[...] = rank1.astype(r1_ref.dtype)
    c0_ref[...] = count0
    e0_ref[...] = jnp.exp(s0 - a0[0:1, :]) / z
    e1_ref[...] = jnp.exp(s1 - a1[0:1, :]).astype(e1_ref.dtype)


def _peer_select(hx, wq_t, keys, layer, tt):
    t, d = hx.shape
    hps = SELECT_HEADS_PER_STEP
    o_spec = pl.BlockSpec((hps, N_KEYS, tt), lambda i, g: (g, 0, i))
    return pl.pallas_call(
        functools.partial(_peer_select_kernel, heads_per_step=hps),
        grid=(t // tt, PEER_HEADS // hps),
        in_specs=[pl.BlockSpec((tt, d), lambda i, g: (i, 0)),
                  pl.BlockSpec((hps * 2 * N_KEYS, d), lambda i, g: (g, 0)),
                  pl.BlockSpec((None, hps, 2, N_KEYS, N_KEYS), lambda i, g: (layer, g, 0, 0, 0))],
        out_specs=[o_spec] * 4,
        out_shape=[jax.ShapeDtypeStruct((PEER_HEADS, N_KEYS, t), dt) for dt in (BF16, BF16, F32, F32)],
        scratch_shapes=[pltpu.VMEM((2 * hps, N_KEYS, tt), F32),
                        pltpu.VMEM((hps, PEER_TOPK, tt), F32), pltpu.VMEM((hps, PEER_TOPK, tt), F32),
                        pltpu.VMEM((PEER_TOPK, tt), F32)],
        compiler_params=_params(("parallel", "arbitrary")),
        name="peer_select",
    )(hx, wq_t, keys)


def _sublane_bcast_bf16(row, rows):
    tile_rows = 16
    packed = jnp.broadcast_to(row, (tile_rows, row.shape[1])).astype(BF16)
    return jnp.tile(packed, (rows // tile_rows, 1))


def _peer_mix_kernel(x_ref, u_ref, vt_ref, r1_ref, e1_ref, c0_ref, e0_ref, res_ref, gate_ref, o_ref,
                     acc_ref, a_ref, g_ref, p_ref, *, n_i):
    j = pl.program_id(1)

    @pl.when(j == 0)
    def _():
        acc_ref[...] = jnp.zeros_like(acc_ref)

    x = x_ref[...]
    rows_per_chunk = 2 * N_KEYS
    for c in range(n_i // 2):
        lo = c * rows_per_chunk
        a_ref[c] = _bdot_nt(u_ref[lo:lo + rows_per_chunk, :], x)
    zero = jnp.zeros((), BF16)
    for ii in range(n_i):
        w = None
        for h in range(PEER_HEADS):
            count = _sublane_bcast_bf16(c0_ref[h, ii:ii + 1, :], N_KEYS)
            e0 = _sublane_bcast_bf16(e0_ref[h, ii:ii + 1, :], N_KEYS)
            term = jnp.where(r1_ref[h] < count, e1_ref[h] * e0, zero)
            w = term if w is None else w + term
        g_ref[ii] = w
    for c in range(n_i // 2):
        lo = c * rows_per_chunk
        gates = jnp.concatenate([g_ref[2 * c], g_ref[2 * c + 1]], axis=0)
        p_ref[lo:lo + rows_per_chunk, :] = gates * _gelu(a_ref[c]).astype(BF16)
    acc_ref[...] += jnp.dot(vt_ref[...], p_ref[...], preferred_element_type=F32)

    @pl.when(j == pl.num_programs(1) - 1)
    def _():
        o_ref[...] = res_ref[...] + (1.0 + gate_ref[...]) * acc_ref[...].T


def _peer_mix(hx, u_b, vt_b, layer, r1, e1, c0, e0, x_res, gate, l, tt, et):
    t, d = hx.shape
    e = u_b.shape[1]
    n_i = et // N_KEYS
    tok_all = pl.BlockSpec((PEER_HEADS, N_KEYS, tt), lambda i, j: (0, 0, i))
    tok_i = pl.BlockSpec((PEER_HEADS, n_i, tt), lambda i, j: (0, j, i))
    gate_arr, gate_spec = _gate_operand(gate, l, tt, d)
    return pl.pallas_call(
        functools.partial(_peer_mix_kernel, n_i=n_i),
        grid=(t // tt, e // et),
        in_specs=[pl.BlockSpec((tt, d), lambda i, j: (i, 0)),
                  pl.BlockSpec((None, et, d), lambda i, j: (layer, j, 0)),
                  pl.BlockSpec((None, d, et), lambda i, j: (layer, 0, j)),
                  tok_all, tok_all, tok_i, tok_i,
                  pl.BlockSpec((tt, d), lambda i, j: (i, 0)), gate_spec],
        out_specs=pl.BlockSpec((tt, d), lambda i, j: (i, 0)),
        out_shape=jax.ShapeDtypeStruct((t, d), F32),
        scratch_shapes=[pltpu.VMEM((d, tt), F32),
                        pltpu.VMEM((n_i // 2, 2 * N_KEYS, tt), F32),
                        pltpu.VMEM((n_i, N_KEYS, tt), BF16),
                        pltpu.VMEM((et, tt), BF16)],
        compiler_params=_params(("parallel", "arbitrary"), vmem=PEER_MIX_VMEM_LIMIT),
        name="peer_mix",
    )(hx, u_b, vt_b, r1, e1, c0, e0, x_res, gate_arr)


def _peer(hx, wq_t, keys, layer, u_b, vt_b, x_res, gate, l):
    t = hx.shape[0]
    tt = min(t, 512)
    r1, e1, c0, e0 = _peer_select(hx, wq_t, keys, layer, tt)
    return _peer_mix(hx, u_b, vt_b, layer, r1, e1, c0, e0, x_res, gate, l, tt, 1024)


def _rope_tables(pos, half, reps):
    inv = ROPE_THETA ** (-jnp.arange(half, dtype=F32) / half)
    ang = pos.astype(F32)[:, None] * inv[None, :]
    return jnp.tile(jnp.cos(ang), (1, reps)), jnp.tile(jnp.sin(ang), (1, reps))


def _rotate_half_cols(w):
    half = w.shape[-1] // 2
    return jnp.concatenate([-w[..., half:], w[..., :half]], axis=-1)


def _pad_last(w, width):
    return jnp.pad(w, [(0, 0)] * (w.ndim - 1) + [(0, width - w.shape[-1])])


def _shared_weights(p):
    out = {}
    w_rope = p["mla_w_dkv"][:, KV_LORA:]
    out["w_dkv_ext"] = jnp.concatenate(
        [p["mla_w_dkv"][:, :KV_LORA], _pad_last(w_rope, LANES), _pad_last(_rotate_half_cols(w_rope), LANES)],
        axis=1)
    w_uq = p["mla_w_uq"][0].reshape(-1, MLA_HEADS, MLA_NOPE + MLA_ROPE)
    q_lora = w_uq.shape[0]
    out["w_uq_nope"] = w_uq[:, :, :MLA_NOPE].reshape(q_lora, MLA_HEADS * MLA_NOPE)
    w_qr = w_uq[:, :, MLA_NOPE:]
    out["w_uq_rope"] = _pad_last(w_qr, LANES).reshape(q_lora, MLA_HEADS * LANES)
    out["w_uq_rot"] = _pad_last(_rotate_half_cols(w_qr), LANES).reshape(q_lora, MLA_HEADS * LANES)
    out["w_ukv_flat"] = jnp.concatenate([p["mla_w_uk"].reshape(KV_LORA, MLA_HEADS * MLA_NOPE),
                                         p["mla_w_uv"].reshape(KV_LORA, MLA_HEADS * MLA_VDIM)], axis=1)
    out["w_uk_h"] = jnp.transpose(p["mla_w_uk"], (1, 0, 2))
    out["w_uv_h"] = jnp.transpose(p["mla_w_uv"], (1, 0, 2))
    out["peer_wq_t"] = [p["peer_w_q"][l].T for l in range(2)]
    out["peer_u_b"] = p["peer_u"].astype(BF16)
    out["peer_vt_b"] = jnp.swapaxes(p["peer_v"].astype(BF16), 1, 2)
    return out


def _adaln(c_all, w, b, lead, n_out):
    m, d = c_all.shape
    tn = 2048
    if w.ndim == 2:
        w = w[None]
        b = b[None]
    b3 = b.reshape(b.shape[0], 1, n_out)
    return _linear(
        c_all, [w], n_cols=n_out, tm=m, tn=tn, epilogue=_ep_bias, prologue=_silu, w_lead=lead,
        extras=[b3], extra_specs=[pl.BlockSpec((None, 1, tn), lambda i, j: (lead, 0, j))],
        out_shapes=[jax.ShapeDtypeStruct((m, n_out), F32)],
        out_specs=[pl.BlockSpec((m, tn), lambda i, j: (0, j))])[0]


def _trunk(x, pos, mods, kv_mod, p, sw, ret_state, cache_lat, cache_kr):
    nb, l, d = x.shape
    t = nb * l
    pos_rows = jnp.tile(pos, nb) if l < 256 else pos
    table_rows = pos_rows.shape[0]
    assert table_rows % _row_tile(t, d) == 0, "a row tile must cover whole sequences or divide one"

    def vec(a):
        return a.reshape(nb, 1, d)

    def flat(a):
        return a.reshape(t, a.shape[-1])

    sh1, sc1, gt1, sh2, sc2, gt2 = [vec(m) for m in mods[0]]
    (h,) = _norm_mod(x, mods=[(p["norm_g"][0, 0][None], sh1, sc1)])
    h = flat(h)
    tm = _row_tile(t, d)
    cos_r, sin_r = _rope_tables(pos_rows, RET_DK // 2, 1)
    tspec = _table_spec(tm, table_rows)
    w_in = p["ret_w_in"]
    qk_cols = RET_HEADS * RET_DK
    v_cols = RET_HEADS * RET_DV
    tn = _col_tile(d, qk_cols)

    def rope_proj(col0, scale):
        return _linear(
            h, [w_in], n_cols=qk_cols, tm=tm, tn=tn, w_lead=0, col_block0=col0 // tn,
            epilogue=functools.partial(_ep_rope_half128, scale=scale),
            extras=[cos_r, sin_r], extra_specs=[tspec, tspec],
            out_shapes=[jax.ShapeDtypeStruct((t, qk_cols), BF16)],
            out_specs=[pl.BlockSpec((tm, tn), lambda i, j: (i, j))])[0]

    def plain_proj(x_in, w, n_cols, col0=0, lead=None, dtype=BF16, tm_=None):
        tm_ = tm_ or _row_tile(x_in.shape[0], x_in.shape[1])
        tn_ = _col_tile(x_in.shape[1], n_cols)
        return _linear(
            x_in, [w], n_cols=n_cols, tm=tm_, tn=tn_, w_lead=lead, col_block0=col0 // tn_,
            epilogue=_ep_plain,
            out_shapes=[jax.ShapeDtypeStruct((x_in.shape[0], n_cols), dtype)],
            out_specs=[pl.BlockSpec((tm_, tn_), lambda i, j: (i, j))])[0]

    def resid_proj(x_in, w, x_res, gate):
        tm_ = _row_tile(x_in.shape[0], x_in.shape[1])
        tn_ = _col_tile(x_in.shape[1], d)
        gate_arr, gate_spec = _gate_operand(gate, l, tm_, tn_)
        tile = pl.BlockSpec((tm_, tn_), lambda i, j: (i, j))
        return _linear(
            x_in, [w], n_cols=d, tm=tm_, tn=tn_, w_lead=0, epilogue=_ep_resid,
            extras=[x_res, gate_arr], extra_specs=[tile, gate_spec],
            out_shapes=[jax.ShapeDtypeStruct((t, d), F32)], out_specs=[tile])[0]

    def heads_proj(x_in, ws, n_cols, epilogue=_ep_heads, extras=(), extra_specs=()):
        tm_ = _row_tile(x_in.shape[0], x_in.shape[1])
        tn_ = _col_tile(x_in.shape[1], n_cols, len(ws))
        return _linear(
            x_in, ws, n_cols=n_cols, tm=tm_, tn=tn_, epilogue=epilogue, extras=extras, extra_specs=extra_specs,
            out_shapes=[jax.ShapeDtypeStruct((n_cols // LANES, x_in.shape[0], LANES), BF16)],
            out_specs=[pl.BlockSpec((tn_ // LANES, tm_, LANES), lambda i, j: (j, i, 0))])[0]

    q_r = rope_proj(0, 1.0)
    k_r = rope_proj(qk_cols, RET_DK ** -0.5)
    v_r = plain_proj(h, w_in, v_cols, col0=2 * qk_cols, lead=0)
    g_r = plain_proj(h, w_in, v_cols, col0=2 * qk_cols + v_cols, lead=0)
    s0 = None if ret_state is None else ret_state[0]
    y_r, s_new = _retention(q_r, k_r, v_r, g_r, p["ret_gn_g"], s0, nb, l)
    xf = resid_proj(y_r, p["ret_w_out"], flat(x), gt1)
    (h,) = _norm_mod(xf.reshape(nb, l, d), mods=[(p["norm_g"][0, 1][None], sh2, sc2)])
    xf = _peer(flat(h), sw["peer_wq_t"][0], p["peer_keys"], 0, sw["peer_u_b"], sw["peer_vt_b"],
               xf, gt2, l)

    sh1, sc1, gt1b, sh2, sc2, gt2b = [vec(m) for m in mods[1]]
    kv_sh, kv_sc = [vec(m) for m in kv_mod]
    h, h_kv = _norm_mod(xf.reshape(nb, l, d),
                         mods=[(p["norm_g"][1, 0][None], sh1, sc1), (p["kv_norm_g"][None], kv_sh, kv_sc)])
    h = flat(h)
    h_kv = flat(h_kv)
    cos_m, sin_m = _rope_tables(pos_rows, MLA_ROPE // 2, LANES // (MLA_ROPE // 2))
    n_ext = sw["w_dkv_ext"].shape[1]
    lat, kr, kr_pad = _linear(
        h_kv, [sw["w_dkv_ext"]], n_cols=n_ext, tm=tm, tn=n_ext, epilogue=_ep_kv,
        extras=[p["mla_kv_norm_g"][None], cos_m, sin_m],
        extra_specs=[pl.BlockSpec((1, KV_LORA), lambda i, j: (0, 0)), tspec, tspec],
        out_shapes=[jax.ShapeDtypeStruct((t, KV_LORA), F32), jax.ShapeDtypeStruct((t, MLA_ROPE), F32),
                    jax.ShapeDtypeStruct((t, LANES), BF16)],
        out_specs=[pl.BlockSpec((tm, KV_LORA), lambda i, j: (i, 0)),
                   pl.BlockSpec((tm, MLA_ROPE), lambda i, j: (i, 0)),
                   pl.BlockSpec((tm, LANES), lambda i, j: (i, 0))])
    q_lora = p["mla_w_dq"].shape[-1]
    cq = _linear(
        h, [p["mla_w_dq"]], n_cols=q_lora, tm=tm, tn=q_lora, w_lead=0, epilogue=_ep_rms,
        extras=[p["mla_q_norm_g"]], extra_specs=[pl.BlockSpec((1, q_lora), lambda i, j: (0, 0))],
        out_shapes=[jax.ShapeDtypeStruct((t, q_lora), BF16)],
        out_specs=[pl.BlockSpec((tm, q_lora), lambda i, j: (i, 0))])[0]
    hl = MLA_HEADS * LANES
    qn = heads_proj(cq, [sw["w_uq_nope"]], hl)
    qr = heads_proj(cq, [sw["w_uq_rope"], sw["w_uq_rot"]], hl, epilogue=_ep_rot_heads,
                    extras=[cos_m, sin_m], extra_specs=[tspec, tspec])
    if cache_lat is None:
        kv_h = heads_proj(lat, [sw["w_ukv_flat"]], 2 * hl)
        o = _attn_prompt(qn, qr, kv_h, kr_pad, nb, l)
    else:
        ckr = _pad_last(cache_kr, LANES).astype(BF16)
        o = _attn_cached(qn, qr, cache_lat, ckr, lat, kr_pad, sw["w_uk_h"], sw["w_uv_h"], nb, l)
    xf = resid_proj(o, p["mla_w_o"], xf, gt1b)
    (h,) = _norm_mod(xf.reshape(nb, l, d), mods=[(p["norm_g"][1, 1][None], sh2, sc2)])
    xf = _peer(flat(h), sw["peer_wq_t"][1], p["peer_keys"], 1, sw["peer_u_b"], sw["peer_vt_b"],
               xf, gt2b, l)
    (y,) = _norm_mod(xf.reshape(nb, l, d), final_g=p["final_g"][None])
    return y, s_new[None], lat.reshape(nb, l, KV_LORA), kr.reshape(nb, l, MLA_ROPE)


def kernel(x_prompt, x_sample, c_prompt, c_sample, state_retention, cache_mla_latent, cache_mla_krope,
           ada_w, ada_b, norm_g, ret_w_in, ret_gn_g, ret_w_out,
           kv_ada_w, kv_ada_b, kv_norm_g, mla_w_dkv, mla_kv_norm_g, mla_w_uk, mla_w_uv,
           mla_w_dq, mla_q_norm_g, mla_w_uq, mla_w_o,
           peer_w_q, peer_keys, peer_u, peer_v, final_g):
    p = dict(norm_g=norm_g, ret_w_in=ret_w_in, ret_gn_g=ret_gn_g[0][None], ret_w_out=ret_w_out,
             kv_norm_g=kv_norm_g, mla_w_dkv=mla_w_dkv, mla_kv_norm_g=mla_kv_norm_g,
             mla_w_uk=mla_w_uk, mla_w_uv=mla_w_uv, mla_w_dq=mla_w_dq, mla_q_norm_g=mla_q_norm_g,
             mla_w_uq=mla_w_uq, mla_w_o=mla_w_o, peer_w_q=peer_w_q, peer_keys=peer_keys,
             peer_u=peer_u, peer_v=peer_v, final_g=final_g)
    sw = _shared_weights(p)
    nbp, lp, d = x_prompt.shape
    nbs, ls, _ = x_sample.shape
    past = cache_mla_latent.shape[1]

    n_c = nbp + nbs
    c_all = jnp.pad(jnp.concatenate([c_prompt, c_sample], axis=0), ((0, (-n_c) % 8), (0, 0)))
    layer_mods = [_adaln(c_all, ada_w, ada_b, layer, 6 * d) for layer in range(2)]
    kv_mods = _adaln(c_all, kv_ada_w, kv_ada_b, 0, 2 * d)

    def stream_mods(lo, hi):
        return ([jnp.split(m[lo:hi], 6, axis=-1) for m in layer_mods], jnp.split(kv_mods[lo:hi], 2, axis=-1))

    mods_p, kv_p = stream_mods(0, nbp)
    mods_s, kv_s = stream_mods(nbp, n_c)
    pos_p = jnp.arange(lp, dtype=jnp.int32)
    pos_s = past + jnp.arange(ls, dtype=jnp.int32)
    y_p, ret_p, lat_p, kr_p = _trunk(x_prompt, pos_p, mods_p, kv_p, p, sw, None, None, None)
    y_s, ret_s, lat_s, kr_s = _trunk(x_sample, pos_s, mods_s, kv_s, p, sw, state_retention,
                                     cache_mla_latent, cache_mla_krope)
    return (y_p, y_s, ret_p, ret_s, lat_p, kr_p, lat_s, kr_s)
```

```python
import functools
import math

import jax
import jax.numpy as jnp
from jax import lax
from jax.experimental import pallas as pl
from jax.experimental.pallas import tpu as pltpu

F32 = jnp.float32
BF16 = jnp.bfloat16

EPS = 1e-6
ROPE_THETA = 10000.0
NEG_INF = -1e30
CHUNK = 64

RET_HEADS = 8
RET_DK = 256
RET_DV = 512

MLA_HEADS = 16
MLA_NOPE = 128
MLA_ROPE = 64
MLA_VDIM = 128
KV_LORA = 512
MLA_SCALE = (MLA_NOPE + MLA_ROPE) ** -0.5

PEER_HEADS = 8
N_KEYS = 128
PEER_TOPK = 16

LANES = 128
VMEM_LIMIT = 52 * 1024 * 1024
PEER_MIX_VMEM_LIMIT = 58 * 1024 * 1024
SELECT_HEADS_PER_STEP = 4


def _params(sem, vmem=VMEM_LIMIT, flags=None):
    return pltpu.CompilerParams(dimension_semantics=sem, vmem_limit_bytes=vmem, flags=flags)


def _bdot(a, b):
    return jnp.dot(a.astype(BF16), b.astype(BF16), preferred_element_type=F32)


def _bdot_nt(a, b):
    return lax.dot_general(a.astype(BF16), b.astype(BF16), (((1,), (1,)), ((), ())),
                           preferred_element_type=F32)


def _bdot_tn(a, b):
    return lax.dot_general(a.astype(BF16), b.astype(BF16), (((0,), (0,)), ((), ())),
                           preferred_element_type=F32)


def _silu(x):
    return x * (1.0 / (1.0 + jnp.exp(-x)))


def _gelu(x):
    return 0.5 * x * (1.0 + lax.erf(x * (0.5 ** 0.5)))


def _linear_kernel(*refs, n_w, n_extra, prologue, epilogue):
    x_ref = refs[0]
    w_refs = refs[1:1 + n_w]
    extra_refs = refs[1 + n_w:1 + n_w + n_extra]
    out_refs = refs[1 + n_w + n_extra:]
    xv = x_ref[...]
    if prologue is not None:
        xv = prologue(xv)
    xb = xv.astype(BF16)
    accs = [jnp.dot(xb, w[...].astype(BF16), preferred_element_type=F32) for w in w_refs]
    epilogue(accs, extra_refs, out_refs)


def _linear(x, ws, *, n_cols, tm, tn, epilogue, out_shapes, out_specs, w_lead=None, col_block0=0,
            extras=(), extra_specs=(), prologue=None, name=None):
    m, k = x.shape
    if name is None:
        name = "linear" + getattr(epilogue, "func", epilogue).__name__
    assert m % tm == 0 and n_cols % tn == 0
    if w_lead is None:
        w_spec = pl.BlockSpec((k, tn), lambda i, j: (0, j + col_block0))
    else:
        w_spec = pl.BlockSpec((None, k, tn), lambda i, j: (w_lead, 0, j + col_block0))
    kern = functools.partial(_linear_kernel, n_w=len(ws), n_extra=len(extras), prologue=prologue,
                             epilogue=epilogue)
    return pl.pallas_call(
        kern,
        grid=(m // tm, n_cols // tn),
        in_specs=[pl.BlockSpec((tm, k), lambda i, j: (i, 0))] + [w_spec] * len(ws) + list(extra_specs),
        out_specs=out_specs,
        out_shape=out_shapes,
        compiler_params=_params(("parallel", "arbitrary")),
        name=name,
    )(x, *ws, *extras)


def _ep_plain(accs, extras, outs):
    outs[0][...] = accs[0].astype(outs[0].dtype)


def _ep_heads(accs, extras, outs):
    acc = accs[0]
    for jj in range(acc.shape[1] // LANES):
        outs[0][jj] = acc[:, jj * LANES:(jj + 1) * LANES].astype(outs[0].dtype)


def _ep_bias(accs, extras, outs):
    outs[0][...] = accs[0] + extras[0][...]


def _ep_resid(accs, extras, outs):
    outs[0][...] = extras[0][...] + (1.0 + extras[1][...]) * accs[0]


def _gate_operand(gate, l, rows_per_tile, cols_per_tile):
    nb, _, d = gate.shape
    col = (lambda j: j) if cols_per_tile < d else (lambda j: 0)
    if l % rows_per_tile == 0:
        tiles_per_batch = l // rows_per_tile
        return gate, pl.BlockSpec((None, 1, cols_per_tile), lambda i, j: (i // tiles_per_batch, 0, col(j)))
    rows = jnp.broadcast_to(gate, (nb, l, d)).reshape(nb * l, d)
    return rows, pl.BlockSpec((rows_per_tile, cols_per_tile), lambda i, j: (i, col(j)))


def _ep_rope_half128(accs, extras, outs, *, scale):
    acc = accs[0]
    cos = extras[0][...]
    sin = extras[1][...]
    for g in range(acc.shape[1] // 256):
        x1 = acc[:, g * 256:g * 256 + 128]
        x2 = acc[:, g * 256 + 128:(g + 1) * 256]
        outs[0][:, g * 256:g * 256 + 128] = ((x1 * cos - x2 * sin) * scale).astype(outs[0].dtype)
        outs[0][:, g * 256 + 128:(g + 1) * 256] = ((x1 * sin + x2 * cos) * scale).astype(outs[0].dtype)


def _ep_rot_heads(accs, extras, outs):
    cos = extras[0][...]
    sin = extras[1][...]
    a, b = accs
    for jj in range(a.shape[1] // LANES):
        sl = slice(jj * LANES, (jj + 1) * LANES)
        outs[0][jj] = (a[:, sl] * cos + b[:, sl] * sin).astype(outs[0].dtype)


def _ep_rms(accs, extras, outs):
    acc = accs[0]
    g = extras[0][...]
    y = acc * lax.rsqrt(jnp.mean(acc * acc, axis=-1, keepdims=True) + EPS) * g
    outs[0][...] = y.astype(outs[0].dtype)


def _ep_kv(accs, extras, outs):
    acc = accs[0]
    g = extras[0][...]
    cos = extras[1][...]
    sin = extras[2][...]
    c = acc[:, :KV_LORA]
    outs[0][...] = c * lax.rsqrt(jnp.mean(c * c, axis=-1, keepdims=True) + EPS) * g
    kr = acc[:, KV_LORA:KV_LORA + LANES] * cos + acc[:, KV_LORA + LANES:KV_LORA + 2 * LANES] * sin
    outs[1][...] = kr[:, :MLA_ROPE]
    outs[2][...] = kr.astype(BF16)


def _row_tile(m, k):
    return min(m, 1024)


def _col_tile(k, n, n_weights=1):
    for tn in (1024, 512):
        if n % tn == 0 and 2 * n_weights * k * tn * 4 <= 16 * 1024 * 1024:
            return tn
    return min(n, 512)


def _table_spec(tm, table_rows):
    nblk = table_rows // tm
    return pl.BlockSpec((tm, LANES), lambda i, j: (i % nblk, 0))


def _norm_mod_kernel(*refs, n_mod):
    x = refs[0][...]
    xn = x * lax.rsqrt(jnp.mean(x * x, axis=-1, keepdims=True) + EPS)
    if n_mod == 0:
        refs[2][...] = xn * refs[1][...]
        return
    outs = refs[1 + 3 * n_mod:]
    for m in range(n_mod):
        g, sh, sc = (refs[1 + 3 * m + k][...] for k in range(3))
        outs[m][...] = ((xn * g) * (1.0 + sc) + sh).astype(outs[m].dtype)


def _norm_mod(x, mods=(), final_g=None):
    nb, l, d = x.shape
    tl = min(l, 512)
    tok = pl.BlockSpec((None, tl, d), lambda b, i: (b, i, 0))
    per_b = pl.BlockSpec((None, 1, d), lambda b, i: (b, 0, 0))
    gain = pl.BlockSpec((1, d), lambda b, i: (0, 0))
    args, specs = [x], [tok]
    if final_g is not None:
        assert not mods
        args.append(final_g)
        specs.append(gain)
        out_shapes = [jax.ShapeDtypeStruct((nb, l, d), F32)]
    else:
        for g, sh, sc in mods:
            args += [g, sh, sc]
            specs += [gain, per_b, per_b]
        out_shapes = [jax.ShapeDtypeStruct((nb, l, d), BF16) for _ in mods]
    return pl.pallas_call(
        functools.partial(_norm_mod_kernel, n_mod=len(mods)),
        grid=(nb, l // tl), in_specs=specs, out_specs=[tok] * len(out_shapes), out_shape=out_shapes,
        compiler_params=_params(("parallel", "parallel")),
        name="norm_mod",
    )(*args)


def _ret_log_decay():
    return jnp.log1p(-jnp.exp2(-5.0 - jnp.arange(RET_HEADS, dtype=F32)))


def _retention_tables(c):
    log_g = _ret_log_decay()[:, None, None]
    n = jnp.arange(c, dtype=F32)
    dist = n[:, None] - n[None, :]
    same = (jnp.arange(c)[:, None] // CHUNK) == (jnp.arange(c)[None, :] // CHUNK)
    earlier = (jnp.arange(c)[None, :] // CHUNK) < (jnp.arange(c)[:, None] // CHUNK)
    mask = jnp.where(same[None], jnp.exp(jnp.abs(dist)[None] * log_g),
                     jnp.where(earlier[None], jnp.exp(dist[None] * log_g), 0.0))
    q_decay = jnp.exp((n[None, :, None] + 1.0) * log_g)
    k_decay = jnp.exp((c - 1.0 - n)[None, :, None] * log_g)
    blk_decay = jnp.exp(c * log_g)
    return mask, q_decay, k_decay, blk_decay


def _retention_kernel(*refs, has_s0, hps):
    if has_s0:
        (q_ref, k_ref, v_ref, g_ref, mask_ref, qd_ref, kd_ref, bd_ref, gn_ref, s0_ref,
         y_ref, s_out_ref, s_ref) = refs
    else:
        (q_ref, k_ref, v_ref, g_ref, mask_ref, qd_ref, kd_ref, bd_ref, gn_ref,
         y_ref, s_out_ref, s_ref) = refs
    c = pl.program_id(2)

    @pl.when(c == 0)
    def _():
        if has_s0:
            s_ref[...] = s0_ref[...]
        else:
            s_ref[...] = jnp.zeros_like(s_ref)

    for hh in range(hps):
        qk_cols = slice(hh * RET_DK, (hh + 1) * RET_DK)
        v_cols = slice(hh * RET_DV, (hh + 1) * RET_DV)
        q = q_ref[:, qk_cols]
        k = k_ref[:, qk_cols]
        v = v_ref[:, v_cols]
        s_prev = s_ref[hh]
        scores = _bdot_nt(q, k) * mask_ref[hh]
        y = _bdot(scores, v) + _bdot(q, s_prev) * qd_ref[hh]
        k_scaled = k.astype(F32) * kd_ref[hh]
        s_ref[hh] = bd_ref[hh] * s_prev + _bdot_tn(k_scaled, v)

        mu = jnp.mean(y, axis=-1, keepdims=True)
        yc = y - mu
        var = jnp.mean(yc * yc, axis=-1, keepdims=True)
        yn = yc * lax.rsqrt(var + EPS) * gn_ref[:, v_cols]
        y_ref[:, v_cols] = (_silu(g_ref[:, v_cols].astype(F32)) * yn).astype(y_ref.dtype)

    @pl.when(c == pl.num_programs(2) - 1)
    def _():
        s_out_ref[...] = s_ref[...]


def _retention(q, k, v, g, gn_g, s0, nb, l):
    cb = min(l, 256)
    nc = l // cb
    hps = 4 if nc > 1 else RET_HEADS
    mask, qd, kd, bd = _retention_tables(cb)
    row = lambda b, hg, c: (b * nc + c, hg)
    per_h3 = lambda b, hg, c: (hg, 0, 0)
    in_specs = [
        pl.BlockSpec((cb, hps * RET_DK), row), pl.BlockSpec((cb, hps * RET_DK), row),
        pl.BlockSpec((cb, hps * RET_DV), row), pl.BlockSpec((cb, hps * RET_DV), row),
        pl.BlockSpec((hps, cb, cb), per_h3), pl.BlockSpec((hps, cb, 1), per_h3),
        pl.BlockSpec((hps, cb, 1), per_h3), pl.BlockSpec((hps, 1, 1), per_h3),
        pl.BlockSpec((1, hps * RET_DV), lambda b, hg, c: (0, hg)),
    ]
    args = [q, k, v, g, mask, qd, kd, bd, gn_g]
    state_spec = pl.BlockSpec((None, hps, RET_DK, RET_DV), lambda b, hg, c: (b, hg, 0, 0))
    if s0 is not None:
        in_specs.append(state_spec)
        args.append(s0)
    y, s_new = pl.pallas_call(
        functools.partial(_retention_kernel, has_s0=s0 is not None, hps=hps),
        grid=(nb, RET_HEADS // hps, nc),
        in_specs=in_specs,
        out_specs=[pl.BlockSpec((cb, hps * RET_DV), row), state_spec],
        out_shape=[jax.ShapeDtypeStruct((nb * l, RET_HEADS * RET_DV), BF16),
                   jax.ShapeDtypeStruct((nb, RET_HEADS, RET_DK, RET_DV), F32)],
        scratch_shapes=[pltpu.VMEM((hps, RET_DK, RET_DV), F32)],
        compiler_params=_params(("parallel", "parallel", "arbitrary")),
        name="retention",
    )(*args)
    return y, s_new


def _attn_prompt_kernel(qi_ref, ki_ref, qn_ref, qr_ref, kn_ref, v_ref, kr_ref, o_ref, m_ref, acc_ref, *, tq, tk):
    qi = qi_ref[pl.program_id(1)]
    ki = ki_ref[pl.program_id(1)]
    exp2_scale = MLA_SCALE * math.log2(math.e)

    @pl.when(ki == 0)
    def _():
        m_ref[...] = jnp.full_like(m_ref, NEG_INF)
        acc_ref[...] = jnp.zeros_like(acc_ref)

    def block(masked):
        if masked:
            q_chunk = (qi * tq + lax.broadcasted_iota(jnp.int32, (tq, tk), 0)) // CHUNK
            k_chunk = (ki * tk + lax.broadcasted_iota(jnp.int32, (tq, tk), 1)) // CHUNK
            visible = k_chunk <= q_chunk
        kr = kr_ref[...]
        ones = jnp.ones((tk, LANES), BF16)

        def head(h, carry):
            q = jnp.concatenate([qn_ref[h], qr_ref[h]], axis=1)
            k = jnp.concatenate([kn_ref[h], kr], axis=1)
            s = _bdot_nt(q, k)
            if masked:
                s = jnp.where(visible, s, NEG_INF)
            m_prev = m_ref[h]
            m_new = jnp.maximum(m_prev, jnp.max(s, axis=-1, keepdims=True))
            alpha = jnp.exp2((m_prev - m_new) * exp2_scale)
            p = jnp.exp2((s - jnp.tile(m_new, (1, tk // LANES))) * exp2_scale)
            v_ext = jnp.concatenate([v_ref[h], ones], axis=1)
            acc_ref[h] = jnp.tile(alpha, (1, 2)) * acc_ref[h] + _bdot(p, v_ext)
            m_ref[h] = m_new
            return carry

        lax.fori_loop(0, MLA_HEADS, head, 0, unroll=True)

    @pl.when(ki < qi)
    def _():
        block(False)

    @pl.when(ki == qi)
    def _():
        block(True)
        for h in range(MLA_HEADS):
            acc = acc_ref[h]
            o_ref[:, h * MLA_VDIM:(h + 1) * MLA_VDIM] = (
                acc[:, :MLA_VDIM] / acc[:, MLA_VDIM:]).astype(o_ref.dtype)


def _attn_prompt(qn, qr, kv, kr, nb, l):
    tq = tk = min(l, 512)
    nq = l // tq
    pairs = [(qi, ki) for qi in range(nq) for ki in range(qi + 1)]
    qi_of = jnp.asarray([qk[0] for qk in pairs], jnp.int32)
    ki_of = jnp.asarray([qk[1] for qk in pairs], jnp.int32)
    q_spec = pl.BlockSpec((MLA_HEADS, tq, LANES), lambda b, s, qi, ki: (0, b * nq + qi[s], 0))
    k_spec = pl.BlockSpec((MLA_HEADS, tk, LANES), lambda b, s, qi, ki: (0, b * nq + ki[s], 0))
    v_spec = pl.BlockSpec((MLA_HEADS, tk, LANES), lambda b, s, qi, ki: (1, b * nq + ki[s], 0))
    return pl.pallas_call(
        functools.partial(_attn_prompt_kernel, tq=tq, tk=tk),
        grid_spec=pltpu.PrefetchScalarGridSpec(
            num_scalar_prefetch=2,
            grid=(nb, len(pairs)),
            in_specs=[q_spec, q_spec, k_spec, v_spec,
                      pl.BlockSpec((tk, LANES), lambda b, s, qi, ki: (b * nq + ki[s], 0))],
            out_specs=pl.BlockSpec((tq, MLA_HEADS * MLA_VDIM), lambda b, s, qi, ki: (b * nq + qi[s], 0)),
            scratch_shapes=[pltpu.VMEM((MLA_HEADS, tq, LANES), F32),
                            pltpu.VMEM((MLA_HEADS, tq, MLA_VDIM + LANES), F32)]),
        out_shape=jax.ShapeDtypeStruct((nb * l, MLA_HEADS * MLA_VDIM), BF16),
        compiler_params=_params(("parallel", "arbitrary")),
        name="attn_prompt",
    )(qi_of, ki_of, qn, qr, kv, kv, kr)


def _attn_cached_kernel(qn_ref, qr_ref, clat_ref, ckr_ref, nlat_ref, nkr_ref, wuk_ref, wuv_ref,
                        o_ref, ql_ref, qrs_ref, *, lq):
    for h in range(MLA_HEADS):
        ql_ref[h * lq:(h + 1) * lq, :] = _bdot_nt(qn_ref[h], wuk_ref[h]).astype(BF16)
        qrs_ref[h * lq:(h + 1) * lq, :] = qr_ref[h]
    ql = ql_ref[...]
    qr = qrs_ref[...]
    clat = clat_ref[...].astype(BF16)
    nlat = nlat_ref[...].astype(BF16)
    s_c = (_bdot_nt(ql, clat) + _bdot_nt(qr, ckr_ref[...])) * MLA_SCALE
    s_n = (_bdot_nt(ql, nlat) + _bdot_nt(qr, nkr_ref[...])) * MLA_SCALE
    m = jnp.maximum(jnp.max(s_c, axis=-1, keepdims=True), jnp.max(s_n, axis=-1, keepdims=True))
    p_c = jnp.exp(s_c - m)
    p_n = jnp.exp(s_n - m)
    denom = jnp.sum(p_c, axis=-1, keepdims=True) + jnp.sum(p_n, axis=-1, keepdims=True)
    o_lat = (_bdot(p_c, clat) + _bdot(p_n, nlat)) / denom
    for h in range(MLA_HEADS):
        o_ref[:, h * MLA_VDIM:(h + 1) * MLA_VDIM] = _bdot(
            o_lat[h * lq:(h + 1) * lq, :], wuv_ref[h]).astype(o_ref.dtype)


def _attn_cached(qn, qr, cache_lat, cache_kr_pad, new_lat, new_kr_pad, wuk_h, wuv_h, nb, lq):
    past = cache_lat.shape[1]
    whole = lambda b: (0, 0, 0)
    return pl.pallas_call(
        functools.partial(_attn_cached_kernel, lq=lq),
        grid=(nb,),
        in_specs=[
            pl.BlockSpec((MLA_HEADS, lq, LANES), lambda b: (0, b, 0)),
            pl.BlockSpec((MLA_HEADS, lq, LANES), lambda b: (0, b, 0)),
            pl.BlockSpec((None, past, KV_LORA), lambda b: (b, 0, 0)),
            pl.BlockSpec((None, past, LANES), lambda b: (b, 0, 0)),
            pl.BlockSpec((lq, KV_LORA), lambda b: (b, 0)),
            pl.BlockSpec((lq, LANES), lambda b: (b, 0)),
            pl.BlockSpec((MLA_HEADS, KV_LORA, MLA_NOPE), whole),
            pl.BlockSpec((MLA_HEADS, KV_LORA, MLA_VDIM), whole),
        ],
        out_specs=pl.BlockSpec((lq, MLA_HEADS * MLA_VDIM), lambda b: (b, 0)),
        out_shape=jax.ShapeDtypeStruct((nb * lq, MLA_HEADS * MLA_VDIM), BF16),
        scratch_shapes=[pltpu.VMEM((MLA_HEADS * lq, KV_LORA), BF16),
                        pltpu.VMEM((MLA_HEADS * lq, LANES), BF16)],
        compiler_params=_params(("parallel",)),
        name="attn_cached",
    )(qn, qr, cache_lat, cache_kr_pad, new_lat, new_kr_pad, wuk_h, wuv_h)


def _top_values(s, top_ref, want_rank, one_per_round=False):
    rank = jnp.full(s.shape, float(PEER_TOPK), F32) if want_rank else None
    rows = lax.broadcasted_iota(jnp.int32, s.shape, 0).astype(F32) if one_per_round else None
    for r in range(PEER_TOPK):
        m = jnp.max(s, axis=0, keepdims=True)
        top_ref[r:r + 1, :] = m
        hit = s == m
        if one_per_round:
            first = jnp.min(jnp.where(hit, rows, float(s.shape[0])), axis=0, keepdims=True)
            hit = rows == first
        if want_rank:
            rank = jnp.where(hit, float(r), rank)
        s = jnp.where(hit, -jnp.inf, s)
    removed = jnp.sum((s == -jnp.inf).astype(F32), axis=0, keepdims=True)
    return rank, removed


_CAND_LIMITS = tuple(PEER_TOPK // (b + 1) for b in range(1, 8))
_HALF_TOPK = PEER_TOPK // 2


def _peer_select_kernel(h_ref, wq_ref, keys_ref, r1_ref, e1_ref, c0_ref, e0_ref, s_ref, a0_ref, a1_ref,
                        n_ref, *, heads_per_step):
    hx = h_ref[...]
    for hh in range(heads_per_step):
        q_t = _bdot_nt(wq_ref[hh * 2 * N_KEYS:(hh + 1) * 2 * N_KEYS, :], hx)
        s_ref[2 * hh] = _bdot(keys_ref[hh, 0], q_t[:N_KEYS])
        s_ref[2 * hh + 1] = _bdot(keys_ref[hh, 1], q_t[N_KEYS:])
    tied = [_peer_select_head(s_ref[2 * hh], s_ref[2 * hh + 1], r1_ref.at[hh], e1_ref.at[hh],
                              c0_ref.at[hh], e0_ref.at[hh], a0_ref.at[hh], a1_ref.at[hh])
            for hh in range(heads_per_step)]

    def redo(hh):
        _peer_select_head_ties(s_ref[2 * hh], s_ref[2 * hh + 1], r1_ref.at[hh], e1_ref.at[hh],
                               c0_ref.at[hh], e0_ref.at[hh], a0_ref.at[hh], a1_ref.at[hh], n_ref)

    for hh in range(heads_per_step):
        pl.when(jnp.max(tied[hh].astype(F32)) > 0.0)(functools.partial(redo, hh))


def _peer_select_head(s0, s1, r1_ref, e1_ref, c0_ref, e0_ref, a0_ref, a1_ref):
    _, removed0 = _top_values(s0, a0_ref, False)
    rank1, removed1 = _top_values(s1, a1_ref, True)
    a0 = a0_ref[...]
    a1 = a1_ref[...]
    a0_lo = a0[:_HALF_TOPK, :]
    row = lax.broadcasted_iota(jnp.int32, a0_lo.shape, 0)
    pieces = [a1[0:1, :] + a0]
    for b, limit in enumerate(_CAND_LIMITS, start=1):
        pieces.append(jnp.where(row < limit, a1[b:b + 1, :] + a0_lo, -jnp.inf))
    pieces.append(a1[_HALF_TOPK:, :] + a0[0:1, :])
    cand = jnp.concatenate(pieces, axis=0)
    c = cand
    for r in range(PEER_TOPK):
        tau = jnp.max(c, axis=0, keepdims=True)
        if r + 1 < PEER_TOPK:
            c = jnp.where(c == tau, -jnp.inf, c)
    sel = cand >= tau
    cmax = a0[0:1, :] + a1[0:1, :]
    z = jnp.sum(jnp.where(sel, jnp.exp(cand - cmax), 0.0), axis=0, keepdims=True)
    self32 = sel.astype(F32)
    n_hi = self32[_HALF_TOPK:PEER_TOPK, :]
    n_lo = self32[:_HALF_TOPK, :]
    for b in range(1, 8):
        lo = PEER_TOPK + (b - 1) * _HALF_TOPK
        n_lo = n_lo + self32[lo:lo + _HALF_TOPK, :]
    tail = jnp.sum(self32[PEER_TOPK + 7 * _HALF_TOPK:, :], axis=0, keepdims=True)
    n_lo = n_lo + jnp.where(row == 0, tail, 0.0)
    count0 = jnp.zeros(s0.shape, F32)
    for a in range(PEER_TOPK):
        n_a = n_lo[a:a + 1, :] if a < _HALF_TOPK else n_hi[a - _HALF_TOPK:a - _HALF_TOPK + 1, :]
        count0 = jnp.where(s0 == a0[a:a + 1, :], n_a, count0)
    r1_ref[...] = rank1.astype(r1_ref.dtype)
    c0_ref[...] = count0
    e0_ref[...] = jnp.exp(s0 - a0[0:1, :]) / z
    e1_ref[...] = jnp.exp(s1 - a1[0:1, :]).astype(e1_ref.dtype)

    n_sel = jnp.sum(self32, axis=0, keepdims=True)
    k = float(PEER_TOPK)
    return (removed0 != k) | (removed1 != k) | (n_sel != k)


def _peer_select_head_ties(s0, s1, r1_ref, e1_ref, c0_ref, e0_ref, a0_ref, a1_ref, n_ref):
    rank0, _ = _top_values(s0, a0_ref, True, one_per_round=True)
    rank1, _ = _top_values(s1, a1_ref, True, one_per_round=True)
    a0 = a0_ref[...]
    a1 = a1_ref[...]
    cand = jnp.concatenate([a0[a:a + 1, :] + a1 for a in range(PEER_TOPK)], axis=0)
    rows = lax.broadcasted_iota(jnp.int32, cand.shape, 0).astype(F32)
    c = cand
    sel = jnp.zeros(cand.shape, F32)
    for _ in range(PEER_TOPK):
        m = jnp.max(c, axis=0, keepdims=True)
        first = jnp.min(jnp.where(c == m, rows, float(cand.shape[0])), axis=0, keepdims=True)
        hit = rows == first
        sel = jnp.where(hit, 1.0, sel)
        c = jnp.where(hit, -jnp.inf, c)
    cmax = a0[0:1, :] + a1[0:1, :]
    z = jnp.sum(sel * jnp.exp(cand - cmax), axis=0, keepdims=True)
    for a in range(PEER_TOPK):
        n_ref[a:a + 1, :] = jnp.sum(sel[a * PEER_TOPK:(a + 1) * PEER_TOPK, :], axis=0, keepdims=True)
    n = n_ref[...]
    count0 = jnp.zeros(s0.shape, F32)
    for a in range(PEER_TOPK):
        count0 = jnp.where(rank0 == float(a), n[a:a + 1, :], count0)
    r1_ref[...] = rank1.astype(r1_ref.dtype)
    c0_ref[...] = count0
    e0_ref[...] = jnp.exp(s0 - a0[0:1, :]) / z
    e1_ref[...] = jnp.exp(s1 - a1[0:1, :]).astype(e1_ref.dtype)


def _peer_select(hx, wq_t, keys, layer, tt):
    t, d = hx.shape
    hps = SELECT_HEADS_PER_STEP
    o_spec = pl.BlockSpec((hps, N_KEYS, tt), lambda i, g: (g, 0, i))
    return pl.pallas_call(
        functools.partial(_peer_select_kernel, heads_per_step=hps),
        grid=(t // tt, PEER_HEADS // hps),
        in_specs=[pl.BlockSpec((tt, d), lambda i, g: (i, 0)),
                  pl.BlockSpec((hps * 2 * N_KEYS, d), lambda i, g: (g, 0)),
                  pl.BlockSpec((None, hps, 2, N_KEYS, N_KEYS), lambda i, g: (layer, g, 0, 0, 0))],
        out_specs=[o_spec] * 4,
        out_shape=[jax.ShapeDtypeStruct((PEER_HEADS, N_KEYS, t), dt) for dt in (BF16, BF16, F32, F32)],
        scratch_shapes=[pltpu.VMEM((2 * hps, N_KEYS, tt), F32),
                        pltpu.VMEM((hps, PEER_TOPK, tt), F32), pltpu.VMEM((hps, PEER_TOPK, tt), F32),
                        pltpu.VMEM((PEER_TOPK, tt), F32)],
        compiler_params=_params(("parallel", "arbitrary")),
        name="peer_select",
    )(hx, wq_t, keys)


def _sublane_bcast_bf16(row, rows):
    tile_rows = 16
    packed = jnp.broadcast_to(row, (tile_rows, row.shape[1])).astype(BF16)
    return jnp.tile(packed, (rows // tile_rows, 1))


def _peer_mix_kernel(x_ref, u_ref, vt_ref, r1_ref, e1_ref, c0_ref, e0_ref, res_ref, gate_ref, o_ref,
                     acc_ref, a_ref, g_ref, p_ref, *, n_i):
    j = pl.program_id(1)

    @pl.when(j == 0)
    def _():
        acc_ref[...] = jnp.zeros_like(acc_ref)

    x = x_ref[...]
    rows_per_chunk = 2 * N_KEYS
    for c in range(n_i // 2):
        lo = c * rows_per_chunk
        a_ref[c] = _bdot_nt(u_ref[lo:lo + rows_per_chunk, :], x)
    zero = jnp.zeros((), BF16)
    for ii in range(n_i):
        w = None
        for h in range(PEER_HEADS):
            count = _sublane_bcast_bf16(c0_ref[h, ii:ii + 1, :], N_KEYS)
            e0 = _sublane_bcast_bf16(e0_ref[h, ii:ii + 1, :], N_KEYS)
            term = jnp.where(r1_ref[h] < count, e1_ref[h] * e0, zero)
            w = term if w is None else w + term
        g_ref[ii] = w
    for c in range(n_i // 2):
        lo = c * rows_per_chunk
        gates = jnp.concatenate([g_ref[2 * c], g_ref[2 * c + 1]], axis=0)
        p_ref[lo:lo + rows_per_chunk, :] = gates * _gelu(a_ref[c]).astype(BF16)
    acc_ref[...] += jnp.dot(vt_ref[...], p_ref[...], preferred_element_type=F32)

    @pl.when(j == pl.num_programs(1) - 1)
    def _():
        o_ref[...] = res_ref[...] + (1.0 + gate_ref[...]) * acc_ref[...].T


def _peer_mix(hx, u_b, vt_b, layer, r1, e1, c0, e0, x_res, gate, l, tt, et):
    t, d = hx.shape
    e = u_b.shape[1]
    n_i = et // N_KEYS
    tok_all = pl.BlockSpec((PEER_HEADS, N_KEYS, tt), lambda i, j: (0, 0, i))
    tok_i = pl.BlockSpec((PEER_HEADS, n_i, tt), lambda i, j: (0, j, i))
    gate_arr, gate_spec = _gate_operand(gate, l, tt, d)
    return pl.pallas_call(
        functools.partial(_peer_mix_kernel, n_i=n_i),
        grid=(t // tt, e // et),
        in_specs=[pl.BlockSpec((tt, d), lambda i, j: (i, 0)),
                  pl.BlockSpec((None, et, d), lambda i, j: (layer, j, 0)),
                  pl.BlockSpec((None, d, et), lambda i, j: (layer, 0, j)),
                  tok_all, tok_all, tok_i, tok_i,
                  pl.BlockSpec((tt, d), lambda i, j: (i, 0)), gate_spec],
        out_specs=pl.BlockSpec((tt, d), lambda i, j: (i, 0)),
        out_shape=jax.ShapeDtypeStruct((t, d), F32),
        scratch_shapes=[pltpu.VMEM((d, tt), F32),
                        pltpu.VMEM((n_i // 2, 2 * N_KEYS, tt), F32),
                        pltpu.VMEM((n_i, N_KEYS, tt), BF16),
                        pltpu.VMEM((et, tt), BF16)],
        compiler_params=_params(("parallel", "arbitrary"), vmem=PEER_MIX_VMEM_LIMIT),
        name="peer_mix",
    )(hx, u_b, vt_b, r1, e1, c0, e0, x_res, gate_arr)


def _peer(hx, wq_t, keys, layer, u_b, vt_b, x_res, gate, l):
    t = hx.shape[0]
    tt = min(t, 512)
    r1, e1, c0, e0 = _peer_select(hx, wq_t, keys, layer, tt)
    return _peer_mix(hx, u_b, vt_b, layer, r1, e1, c0, e0, x_res, gate, l, tt, 1024)


def _rope_tables(pos, half, reps):
    inv = ROPE_THETA ** (-jnp.arange(half, dtype=F32) / half)
    ang = pos.astype(F32)[:, None] * inv[None, :]
    return jnp.tile(jnp.cos(ang), (1, reps)), jnp.tile(jnp.sin(ang), (1, reps))


def _rotate_half_cols(w):
    half = w.shape[-1] // 2
    return jnp.concatenate([-w[..., half:], w[..., :half]], axis=-1)


def _pad_last(w, width):
    return jnp.pad(w, [(0, 0)] * (w.ndim - 1) + [(0, width - w.shape[-1])])


def _shared_weights(p):
    out = {}
    w_rope = p["mla_w_dkv"][:, KV_LORA:]
    out["w_dkv_ext"] = jnp.concatenate(
        [p["mla_w_dkv"][:, :KV_LORA], _pad_last(w_rope, LANES), _pad_last(_rotate_half_cols(w_rope), LANES)],
        axis=1)
    w_uq = p["mla_w_uq"][0].reshape(-1, MLA_HEADS, MLA_NOPE + MLA_ROPE)
    q_lora = w_uq.shape[0]
    out["w_uq_nope"] = w_uq[:, :, :MLA_NOPE].reshape(q_lora, MLA_HEADS * MLA_NOPE)
    w_qr = w_uq[:, :, MLA_NOPE:]
    out["w_uq_rope"] = _pad_last(w_qr, LANES).reshape(q_lora, MLA_HEADS * LANES)
    out["w_uq_rot"] = _pad_last(_rotate_half_cols(w_qr), LANES).reshape(q_lora, MLA_HEADS * LANES)
    out["w_ukv_flat"] = jnp.concatenate([p["mla_w_uk"].reshape(KV_LORA, MLA_HEADS * MLA_NOPE),
                                         p["mla_w_uv"].reshape(KV_LORA, MLA_HEADS * MLA_VDIM)], axis=1)
    out["w_uk_h"] = jnp.transpose(p["mla_w_uk"], (1, 0, 2))
    out["w_uv_h"] = jnp.transpose(p["mla_w_uv"], (1, 0, 2))
    out["peer_wq_t"] = [p["peer_w_q"][l].T for l in range(2)]
    out["peer_u_b"] = p["peer_u"].astype(BF16)
    out["peer_vt_b"] = jnp.swapaxes(p["peer_v"].astype(BF16), 1, 2)
    return out


def _adaln(c_all, w, b, lead, n_out):
    m, d = c_all.shape
    tn = 2048
    if w.ndim == 2:
        w = w[None]
        b = b[None]
    b3 = b.reshape(b.shape[0], 1, n_out)
    return _linear(
        c_all, [w], n_cols=n_out, tm=m, tn=tn, epilogue=_ep_bias, prologue=_silu, w_lead=lead,
        extras=[b3], extra_specs=[pl.BlockSpec((None, 1, tn), lambda i, j: (lead, 0, j))],
        out_shapes=[jax.ShapeDtypeStruct((m, n_out), F32)],
        out_specs=[pl.BlockSpec((m, tn), lambda i, j: (0, j))])[0]


def _trunk(x, pos, mods, kv_mod, p, sw, ret_state, cache_lat, cache_kr):
    nb, l, d = x.shape
    t = nb * l
    pos_rows = jnp.tile(pos, nb) if l < 256 else pos
    table_rows = pos_rows.shape[0]
    assert table_rows % _row_tile(t, d) == 0, "a row tile must cover whole sequences or divide one"

    def vec(a):
        return a.reshape(nb, 1, d)

    def flat(a):
        return a.reshape(t, a.shape[-1])

    sh1, sc1, gt1, sh2, sc2, gt2 = [vec(m) for m in mods[0]]
    (h,) = _norm_mod(x, mods=[(p["norm_g"][0, 0][None], sh1, sc1)])
    h = flat(h)
    tm = _row_tile(t, d)
    cos_r, sin_r = _rope_tables(pos_rows, RET_DK // 2, 1)
    tspec = _table_spec(tm, table_rows)
    w_in = p["ret_w_in"]
    qk_cols = RET_HEADS * RET_DK
    v_cols = RET_HEADS * RET_DV
    tn = _col_tile(d, qk_cols)

    def rope_proj(col0, scale):
        return _linear(
            h, [w_in], n_cols=qk_cols, tm=tm, tn=tn, w_lead=0, col_block0=col0 // tn,
            epilogue=functools.partial(_ep_rope_half128, scale=scale),
            extras=[cos_r, sin_r], extra_specs=[tspec, tspec],
            out_shapes=[jax.ShapeDtypeStruct((t, qk_cols), BF16)],
            out_specs=[pl.BlockSpec((tm, tn), lambda i, j: (i, j))])[0]

    def plain_proj(x_in, w, n_cols, col0=0, lead=None, dtype=BF16, tm_=None):
        tm_ = tm_ or _row_tile(x_in.shape[0], x_in.shape[1])
        tn_ = _col_tile(x_in.shape[1], n_cols)
        return _linear(
            x_in, [w], n_cols=n_cols, tm=tm_, tn=tn_, w_lead=lead, col_block0=col0 // tn_,
            epilogue=_ep_plain,
            out_shapes=[jax.ShapeDtypeStruct((x_in.shape[0], n_cols), dtype)],
            out_specs=[pl.BlockSpec((tm_, tn_), lambda i, j: (i, j))])[0]

    def resid_proj(x_in, w, x_res, gate):
        tm_ = _row_tile(x_in.shape[0], x_in.shape[1])
        tn_ = _col_tile(x_in.shape[1], d)
        gate_arr, gate_spec = _gate_operand(gate, l, tm_, tn_)
        tile = pl.BlockSpec((tm_, tn_), lambda i, j: (i, j))
        return _linear(
            x_in, [w], n_cols=d, tm=tm_, tn=tn_, w_lead=0, epilogue=_ep_resid,
            extras=[x_res, gate_arr], extra_specs=[tile, gate_spec],
            out_shapes=[jax.ShapeDtypeStruct((t, d), F32)], out_specs=[tile])[0]

    def heads_proj(x_in, ws, n_cols, epilogue=_ep_heads, extras=(), extra_specs=()):
        tm_ = _row_tile(x_in.shape[0], x_in.shape[1])
        tn_ = _col_tile(x_in.shape[1], n_cols, len(ws))
        return _linear(
            x_in, ws, n_cols=n_cols, tm=tm_, tn=tn_, epilogue=epilogue, extras=extras, extra_specs=extra_specs,
            out_shapes=[jax.ShapeDtypeStruct((n_cols // LANES, x_in.shape[0], LANES), BF16)],
            out_specs=[pl.BlockSpec((tn_ // LANES, tm_, LANES), lambda i, j: (j, i, 0))])[0]

    q_r = rope_proj(0, 1.0)
    k_r = rope_proj(qk_cols, RET_DK ** -0.5)
    v_r = plain_proj(h, w_in, v_cols, col0=2 * qk_cols, lead=0)
    g_r = plain_proj(h, w_in, v_cols, col0=2 * qk_cols + v_cols, lead=0)
    s0 = None if ret_state is None else ret_state[0]
    y_r, s_new = _retention(q_r, k_r, v_r, g_r, p["ret_gn_g"], s0, nb, l)
    xf = resid_proj(y_r, p["ret_w_out"], flat(x), gt1)
    (h,) = _norm_mod(xf.reshape(nb, l, d), mods=[(p["norm_g"][0, 1][None], sh2, sc2)])
    xf = _peer(flat(h), sw["peer_wq_t"][0], p["peer_keys"], 0, sw["peer_u_b"], sw["peer_vt_b"],
               xf, gt2, l)

    sh1, sc1, gt1b, sh2, sc2, gt2b = [vec(m) for m in mods[1]]
    kv_sh, kv_sc = [vec(m) for m in kv_mod]
    h, h_kv = _norm_mod(xf.reshape(nb, l, d),
                         mods=[(p["norm_g"][1, 0][None], sh1, sc1), (p["kv_norm_g"][None], kv_sh, kv_sc)])
    h = flat(h)
    h_kv = flat(h_kv)
    cos_m, sin_m = _rope_tables(pos_rows, MLA_ROPE // 2, LANES // (MLA_ROPE // 2))
    n_ext = sw["w_dkv_ext"].shape[1]
    lat, kr, kr_pad = _linear(
        h_kv, [sw["w_dkv_ext"]], n_cols=n_ext, tm=tm, tn=n_ext, epilogue=_ep_kv,
        extras=[p["mla_kv_norm_g"][None], cos_m, sin_m],
        extra_specs=[pl.BlockSpec((1, KV_LORA), lambda i, j: (0, 0)), tspec, tspec],
        out_shapes=[jax.ShapeDtypeStruct((t, KV_LORA), F32), jax.ShapeDtypeStruct((t, MLA_ROPE), F32),
                    jax.ShapeDtypeStruct((t, LANES), BF16)],
        out_specs=[pl.BlockSpec((tm, KV_LORA), lambda i, j: (i, 0)),
                   pl.BlockSpec((tm, MLA_ROPE), lambda i, j: (i, 0)),
                   pl.BlockSpec((tm, LANES), lambda i, j: (i, 0))])
    q_lora = p["mla_w_dq"].shape[-1]
    cq = _linear(
        h, [p["mla_w_dq"]], n_cols=q_lora, tm=tm, tn=q_lora, w_lead=0, epilogue=_ep_rms,
        extras=[p["mla_q_norm_g"]], extra_specs=[pl.BlockSpec((1, q_lora), lambda i, j: (0, 0))],
        out_shapes=[jax.ShapeDtypeStruct((t, q_lora), BF16)],
        out_specs=[pl.BlockSpec((tm, q_lora), lambda i, j: (i, 0))])[0]
    hl = MLA_HEADS * LANES
    qn = heads_proj(cq, [sw["w_uq_nope"]], hl)
    qr = heads_proj(cq, [sw["w_uq_rope"], sw["w_uq_rot"]], hl, epilogue=_ep_rot_heads,
                    extras=[cos_m, sin_m], extra_specs=[tspec, tspec])
    if cache_lat is None:
        kv_h = heads_proj(lat, [sw["w_ukv_flat"]], 2 * hl)
        o = _attn_prompt(qn, qr, kv_h, kr_pad, nb, l)
    else:
        ckr = _pad_last(cache_kr, LANES).astype(BF16)
        o = _attn_cached(qn, qr, cache_lat, ckr, lat, kr_pad, sw["w_uk_h"], sw["w_uv_h"], nb, l)
    xf = resid_proj(o, p["mla_w_o"], xf, gt1b)
    (h,) = _norm_mod(xf.reshape(nb, l, d), mods=[(p["norm_g"][1, 1][None], sh2, sc2)])
    xf = _peer(flat(h), sw["peer_wq_t"][1], p["peer_keys"], 1, sw["peer_u_b"], sw["peer_vt_b"],
               xf, gt2b, l)
    (y,) = _norm_mod(xf.reshape(nb, l, d), final_g=p["final_g"][None])
    return y, s_new[None], lat.reshape(nb, l, KV_LORA), kr.reshape(nb, l, MLA_ROPE)


def kernel(x_prompt, x_sample, c_prompt, c_sample, state_retention, cache_mla_latent, cache_mla_krope,
           ada_w, ada_b, norm_g, ret_w_in, ret_gn_g, ret_w_out,
           kv_ada_w, kv_ada_b, kv_norm_g, mla_w_dkv, mla_kv_norm_g, mla_w_uk, mla_w_uv,
           mla_w_dq, mla_q_norm_g, mla_w_uq, mla_w_o,
           peer_w_q, peer_keys, peer_u, peer_v, final_g):
    p = dict(norm_g=norm_g, ret_w_in=ret_w_in, ret_gn_g=ret_gn_g[0][None], ret_w_out=ret_w_out,
             kv_norm_g=kv_norm_g, mla_w_dkv=mla_w_dkv, mla_kv_norm_g=mla_kv_norm_g,
             mla_w_uk=mla_w_uk, mla_w_uv=mla_w_uv, mla_w_dq=mla_w_dq, mla_q_norm_g=mla_q_norm_g,
             mla_w_uq=mla_w_uq, mla_w_o=mla_w_o, peer_w_q=peer_w_q, peer_keys=peer_keys,
             peer_u=peer_u, peer_v=peer_v, final_g=final_g)
    sw = _shared_weights(p)
    nbp, lp, d = x_prompt.shape
    nbs, ls, _ = x_sample.shape
    past = cache_mla_latent.shape[1]

    n_c = nbp + nbs
    c_all = jnp.pad(jnp.concatenate([c_prompt, c_sample], axis=0), ((0, (-n_c) % 8), (0, 0)))
    layer_mods = [_adaln(c_all, ada_w, ada_b, layer, 6 * d) for layer in range(2)]
    kv_mods = _adaln(c_all, kv_ada_w, kv_ada_b, 0, 2 * d)

    def stream_mods(lo, hi):
        return ([jnp.split(m[lo:hi], 6, axis=-1) for m in layer_mods], jnp.split(kv_mods[lo:hi], 2, axis=-1))

    mods_p, kv_p = stream_mods(0, nbp)
    mods_s, kv_s = stream_mods(nbp, n_c)
    pos_p = jnp.arange(lp, dtype=jnp.int32)
    pos_s = past + jnp.arange(ls, dtype=jnp.int32)
    y_p, ret_p, lat_p, kr_p = _trunk(x_prompt, pos_p, mods_p, kv_p, p, sw, None, None, None)
    y_s, ret_s, lat_s, kr_s = _trunk(x_sample, pos_s, mods_s, kv_s, p, sw, state_retention,
                                     cache_mla_latent, cache_mla_krope)
    return (y_p, y_s, ret_p, ret_s, lat_p, kr_p, lat_s, kr_s)
```

```python
import functools
import math

import jax
import jax.numpy as jnp
from jax import lax
from jax.experimental import pallas as pl
from jax.experimental.pallas import tpu as pltpu

F32 = jnp.float32
BF16 = jnp.bfloat16

EPS = 1e-6
ROPE_THETA = 10000.0
NEG_INF = -1e30
CHUNK = 64

RET_HEADS = 8
RET_DK = 256
RET_DV = 512

MLA_HEADS = 16
MLA_NOPE = 128
MLA_ROPE = 64
MLA_VDIM = 128
KV_LORA = 512
MLA_SCALE = (MLA_NOPE + MLA_ROPE) ** -0.5

PEER_HEADS = 8
N_KEYS = 128
PEER_TOPK = 16

LANES = 128
VMEM_LIMIT = 52 * 1024 * 1024
PEER_MIX_VMEM_LIMIT = 58 * 1024 * 1024
SELECT_HEADS_PER_STEP = 4


def _params(sem, vmem=VMEM_LIMIT):
    return pltpu.CompilerParams(dimension_semantics=sem, vmem_limit_bytes=vmem)


def _bdot(a, b):
    return jnp.dot(a.astype(BF16), b.astype(BF16), preferred_element_type=F32)


def _bdot_nt(a, b):
    return lax.dot_general(a.astype(BF16), b.astype(BF16), (((1,), (1,)), ((), ())),
                           preferred_element_type=F32)


def _bdot_tn(a, b):
    return lax.dot_general(a.astype(BF16), b.astype(BF16), (((0,), (0,)), ((), ())),
                           preferred_element_type=F32)


def _silu(x):
    return x * (1.0 / (1.0 + jnp.exp(-x)))


def _gelu(x):
    return 0.5 * x * (1.0 + lax.erf(x * (0.5 ** 0.5)))


def _linear_kernel(*refs, n_w, n_extra, prologue, epilogue):
    x_ref = refs[0]
    w_refs = refs[1:1 + n_w]
    extra_refs = refs[1 + n_w:1 + n_w + n_extra]
    out_refs = refs[1 + n_w + n_extra:]
    xv = x_ref[...]
    if prologue is not None:
        xv = prologue(xv)
    xb = xv.astype(BF16)
    accs = [jnp.dot(xb, w[...].astype(BF16), preferred_element_type=F32) for w in w_refs]
    epilogue(accs, extra_refs, out_refs)


def _linear(x, ws, *, n_cols, tm, tn, epilogue, out_shapes, out_specs, w_lead=None, col_block0=0,
            extras=(), extra_specs=(), prologue=None, name=None):
    m, k = x.shape
    if name is None:
        name = "linear" + getattr(epilogue, "func", epilogue).__name__
    assert m % tm == 0 and n_cols % tn == 0
    if w_lead is None:
        w_spec = pl.BlockSpec((k, tn), lambda i, j: (0, j + col_block0))
    else:
        w_spec = pl.BlockSpec((None, k, tn), lambda i, j: (w_lead, 0, j + col_block0))
    kern = functools.partial(_linear_kernel, n_w=len(ws), n_extra=len(extras), prologue=prologue,
                             epilogue=epilogue)
    return pl.pallas_call(
        kern,
        grid=(m // tm, n_cols // tn),
        in_specs=[pl.BlockSpec((tm, k), lambda i, j: (i, 0))] + [w_spec] * len(ws) + list(extra_specs),
        out_specs=out_specs,
        out_shape=out_shapes,
        compiler_params=_params(("parallel", "arbitrary")),
        name=name,
    )(x, *ws, *extras)


def _ep_plain(accs, extras, outs):
    outs[0][...] = accs[0].astype(outs[0].dtype)


def _ep_heads(accs, extras, outs):
    acc = accs[0]
    for jj in range(acc.shape[1] // LANES):
        outs[0][jj] = acc[:, jj * LANES:(jj + 1) * LANES].astype(outs[0].dtype)


def _ep_bias(accs, extras, outs):
    outs[0][...] = accs[0] + extras[0][...]


def _ep_resid(accs, extras, outs):
    outs[0][...] = extras[0][...] + (1.0 + extras[1][...]) * accs[0]


def _gate_operand(gate, l, rows_per_tile, cols_per_tile):
    nb, _, d = gate.shape
    col = (lambda j: j) if cols_per_tile < d else (lambda j: 0)
    if l % rows_per_tile == 0:
        tiles_per_batch = l // rows_per_tile
        return gate, pl.BlockSpec((None, 1, cols_per_tile), lambda i, j: (i // tiles_per_batch, 0, col(j)))
    rows = jnp.broadcast_to(gate, (nb, l, d)).reshape(nb * l, d)
    return rows, pl.BlockSpec((rows_per_tile, cols_per_tile), lambda i, j: (i, col(j)))


def _ep_rope_half128(accs, extras, outs, *, scale):
    acc = accs[0]
    cos = extras[0][...]
    sin = extras[1][...]
    for g in range(acc.shape[1] // 256):
        x1 = acc[:, g * 256:g * 256 + 128]
        x2 = acc[:, g * 256 + 128:(g + 1) * 256]
        outs[0][:, g * 256:g * 256 + 128] = ((x1 * cos - x2 * sin) * scale).astype(outs[0].dtype)
        outs[0][:, g * 256 + 128:(g + 1) * 256] = ((x1 * sin + x2 * cos) * scale).astype(outs[0].dtype)


def _ep_rot_heads(accs, extras, outs):
    cos = extras[0][...]
    sin = extras[1][...]
    a, b = accs
    for jj in range(a.shape[1] // LANES):
        sl = slice(jj * LANES, (jj + 1) * LANES)
        outs[0][jj] = (a[:, sl] * cos + b[:, sl] * sin).astype(outs[0].dtype)


def _ep_rms(accs, extras, outs):
    acc = accs[0]
    g = extras[0][...]
    y = acc * lax.rsqrt(jnp.mean(acc * acc, axis=-1, keepdims=True) + EPS) * g
    outs[0][...] = y.astype(outs[0].dtype)


def _ep_kv(accs, extras, outs):
    acc = accs[0]
    g = extras[0][...]
    cos = extras[1][...]
    sin = extras[2][...]
    c = acc[:, :KV_LORA]
    outs[0][...] = c * lax.rsqrt(jnp.mean(c * c, axis=-1, keepdims=True) + EPS) * g
    kr = acc[:, KV_LORA:KV_LORA + LANES] * cos + acc[:, KV_LORA + LANES:KV_LORA + 2 * LANES] * sin
    outs[1][...] = kr[:, :MLA_ROPE]
    outs[2][...] = kr.astype(BF16)


def _row_tile(m):
    return min(m, 1024)


def _col_tile(k, n, n_weights=1):
    for tn in (1024, 512):
        if n % tn == 0 and 2 * n_weights * k * tn * 4 <= 16 * 1024 * 1024:
            return tn
    return min(n, 512)


def _table_spec(tm, table_rows):
    nblk = table_rows // tm
    return pl.BlockSpec((tm, LANES), lambda i, j: (i % nblk, 0))


def _norm_mod_kernel(*refs, n_mod):
    x = refs[0][...]
    xn = x * lax.rsqrt(jnp.mean(x * x, axis=-1, keepdims=True) + EPS)
    if n_mod == 0:
        refs[2][...] = xn * refs[1][...]
        return
    outs = refs[1 + 3 * n_mod:]
    for m in range(n_mod):
        g, sh, sc = (refs[1 + 3 * m + k][...] for k in range(3))
        outs[m][...] = ((xn * g) * (1.0 + sc) + sh).astype(outs[m].dtype)


def _norm_mod(x, mods=(), final_g=None):
    nb, l, d = x.shape
    tl = min(l, 512)
    tok = pl.BlockSpec((None, tl, d), lambda b, i: (b, i, 0))
    per_b = pl.BlockSpec((None, 1, d), lambda b, i: (b, 0, 0))
    gain = pl.BlockSpec((1, d), lambda b, i: (0, 0))
    args, specs = [x], [tok]
    if final_g is not None:
        assert not mods
        args.append(final_g)
        specs.append(gain)
        out_shapes = [jax.ShapeDtypeStruct((nb, l, d), F32)]
    else:
        for g, sh, sc in mods:
            args += [g, sh, sc]
            specs += [gain, per_b, per_b]
        out_shapes = [jax.ShapeDtypeStruct((nb, l, d), BF16) for _ in mods]
    return pl.pallas_call(
        functools.partial(_norm_mod_kernel, n_mod=len(mods)),
        grid=(nb, l // tl), in_specs=specs, out_specs=[tok] * len(out_shapes), out_shape=out_shapes,
        compiler_params=_params(("parallel", "parallel")),
        name="norm_mod",
    )(*args)


def _ret_log_decay():
    return jnp.log1p(-jnp.exp2(-5.0 - jnp.arange(RET_HEADS, dtype=F32)))


def _retention_tables(c):
    log_g = _ret_log_decay()[:, None, None]
    n = jnp.arange(c, dtype=F32)
    dist = n[:, None] - n[None, :]
    same = (jnp.arange(c)[:, None] // CHUNK) == (jnp.arange(c)[None, :] // CHUNK)
    earlier = (jnp.arange(c)[None, :] // CHUNK) < (jnp.arange(c)[:, None] // CHUNK)
    mask = jnp.where(same[None], jnp.exp(jnp.abs(dist)[None] * log_g),
                     jnp.where(earlier[None], jnp.exp(dist[None] * log_g), 0.0))
    q_decay = jnp.exp((n[None, :, None] + 1.0) * log_g)
    k_decay = jnp.exp((c - 1.0 - n)[None, :, None] * log_g)
    blk_decay = jnp.exp(c * log_g)
    return mask, q_decay, k_decay, blk_decay


def _retention_kernel(*refs, has_s0, hps):
    if has_s0:
        (q_ref, k_ref, v_ref, g_ref, mask_ref, qd_ref, kd_ref, bd_ref, gn_ref, s0_ref,
         y_ref, s_out_ref, s_ref) = refs
    else:
        (q_ref, k_ref, v_ref, g_ref, mask_ref, qd_ref, kd_ref, bd_ref, gn_ref,
         y_ref, s_out_ref, s_ref) = refs
    c = pl.program_id(2)

    @pl.when(c == 0)
    def _():
        if has_s0:
            s_ref[...] = s0_ref[...]
        else:
            s_ref[...] = jnp.zeros_like(s_ref)

    for hh in range(hps):
        qk_cols = slice(hh * RET_DK, (hh + 1) * RET_DK)
        v_cols = slice(hh * RET_DV, (hh + 1) * RET_DV)
        q = q_ref[:, qk_cols]
        k = k_ref[:, qk_cols]
        v = v_ref[:, v_cols]
        s_prev = s_ref[hh]
        scores = _bdot_nt(q, k) * mask_ref[hh]
        y = _bdot(scores, v) + _bdot(q, s_prev) * qd_ref[hh]
        k_scaled = k.astype(F32) * kd_ref[hh]
        s_ref[hh] = bd_ref[hh] * s_prev + _bdot_tn(k_scaled, v)

        mu = jnp.mean(y, axis=-1, keepdims=True)
        yc = y - mu
        var = jnp.mean(yc * yc, axis=-1, keepdims=True)
        yn = yc * lax.rsqrt(var + EPS) * gn_ref[:, v_cols]
        y_ref[:, v_cols] = (_silu(g_ref[:, v_cols].astype(F32)) * yn).astype(y_ref.dtype)

    @pl.when(c == pl.num_programs(2) - 1)
    def _():
        s_out_ref[...] = s_ref[...]


def _retention(q, k, v, g, gn_g, s0, nb, l):
    cb = min(l, 256)
    nc = l // cb
    hps = 4 if nc > 1 else RET_HEADS
    mask, qd, kd, bd = _retention_tables(cb)
    row = lambda b, hg, c: (b * nc + c, hg)
    per_h3 = lambda b, hg, c: (hg, 0, 0)
    in_specs = [
        pl.BlockSpec((cb, hps * RET_DK), row), pl.BlockSpec((cb, hps * RET_DK), row),
        pl.BlockSpec((cb, hps * RET_DV), row), pl.BlockSpec((cb, hps * RET_DV), row),
        pl.BlockSpec((hps, cb, cb), per_h3), pl.BlockSpec((hps, cb, 1), per_h3),
        pl.BlockSpec((hps, cb, 1), per_h3), pl.BlockSpec((hps, 1, 1), per_h3),
        pl.BlockSpec((1, hps * RET_DV), lambda b, hg, c: (0, hg)),
    ]
    args = [q, k, v, g, mask, qd, kd, bd, gn_g]
    state_spec = pl.BlockSpec((None, hps, RET_DK, RET_DV), lambda b, hg, c: (b, hg, 0, 0))
    if s0 is not None:
        in_specs.append(state_spec)
        args.append(s0)
    y, s_new = pl.pallas_call(
        functools.partial(_retention_kernel, has_s0=s0 is not None, hps=hps),
        grid=(nb, RET_HEADS // hps, nc),
        in_specs=in_specs,
        out_specs=[pl.BlockSpec((cb, hps * RET_DV), row), state_spec],
        out_shape=[jax.ShapeDtypeStruct((nb * l, RET_HEADS * RET_DV), BF16),
                   jax.ShapeDtypeStruct((nb, RET_HEADS, RET_DK, RET_DV), F32)],
        scratch_shapes=[pltpu.VMEM((hps, RET_DK, RET_DV), F32)],
        compiler_params=_params(("parallel", "parallel", "arbitrary")),
        name="retention",
    )(*args)
    return y, s_new


def _attn_prompt_kernel(qi_ref, ki_ref, qn_ref, qr_ref, kn_ref, v_ref, kr_ref, o_ref, m_ref, acc_ref, *, tq, tk):
    qi = qi_ref[pl.program_id(1)]
    ki = ki_ref[pl.program_id(1)]
    exp2_scale = MLA_SCALE * math.log2(math.e)

    @pl.when(ki == 0)
    def _():
        m_ref[...] = jnp.full_like(m_ref, NEG_INF)
        acc_ref[...] = jnp.zeros_like(acc_ref)

    def block(masked):
        if masked:
            q_chunk = (qi * tq + lax.broadcasted_iota(jnp.int32, (tq, tk), 0)) // CHUNK
            k_chunk = (ki * tk + lax.broadcasted_iota(jnp.int32, (tq, tk), 1)) // CHUNK
            visible = k_chunk <= q_chunk
        kr = kr_ref[...]
        ones = jnp.ones((tk, LANES), BF16)

        def head(h, carry):
            q = jnp.concatenate([qn_ref[h], qr_ref[h]], axis=1)
            k = jnp.concatenate([kn_ref[h], kr], axis=1)
            s = _bdot_nt(q, k)
            if masked:
                s = jnp.where(visible, s, NEG_INF)
            m_prev = m_ref[h]
            m_new = jnp.maximum(m_prev, jnp.max(s, axis=-1, keepdims=True))
            alpha = jnp.exp2((m_prev - m_new) * exp2_scale)
            p = jnp.exp2((s - jnp.tile(m_new, (1, tk // LANES))) * exp2_scale)
            v_ext = jnp.concatenate([v_ref[h], ones], axis=1)
            acc_ref[h] = jnp.tile(alpha, (1, 2)) * acc_ref[h] + _bdot(p, v_ext)
            m_ref[h] = m_new
            return carry

        lax.fori_loop(0, MLA_HEADS, head, 0, unroll=True)

    @pl.when(ki < qi)
    def _():
        block(False)

    @pl.when(ki == qi)
    def _():
        block(True)
        for h in range(MLA_HEADS):
            acc = acc_ref[h]
            o_ref[:, h * MLA_VDIM:(h + 1) * MLA_VDIM] = (
                acc[:, :MLA_VDIM] / acc[:, MLA_VDIM:]).astype(o_ref.dtype)


def _attn_prompt(qn, qr, kv, kr, nb, l):
    tq = tk = min(l, 512)
    nq = l // tq
    pairs = [(qi, ki) for qi in range(nq) for ki in range(qi + 1)]
    qi_of = jnp.asarray([qk[0] for qk in pairs], jnp.int32)
    ki_of = jnp.asarray([qk[1] for qk in pairs], jnp.int32)
    q_spec = pl.BlockSpec((MLA_HEADS, tq, LANES), lambda b, s, qi, ki: (0, b * nq + qi[s], 0))
    k_spec = pl.BlockSpec((MLA_HEADS, tk, LANES), lambda b, s, qi, ki: (0, b * nq + ki[s], 0))
    v_spec = pl.BlockSpec((MLA_HEADS, tk, LANES), lambda b, s, qi, ki: (1, b * nq + ki[s], 0))
    return pl.pallas_call(
        functools.partial(_attn_prompt_kernel, tq=tq, tk=tk),
        grid_spec=pltpu.PrefetchScalarGridSpec(
            num_scalar_prefetch=2,
            grid=(nb, len(pairs)),
            in_specs=[q_spec, q_spec, k_spec, v_spec,
                      pl.BlockSpec((tk, LANES), lambda b, s, qi, ki: (b * nq + ki[s], 0))],
            out_specs=pl.BlockSpec((tq, MLA_HEADS * MLA_VDIM), lambda b, s, qi, ki: (b * nq + qi[s], 0)),
            scratch_shapes=[pltpu.VMEM((MLA_HEADS, tq, LANES), F32),
                            pltpu.VMEM((MLA_HEADS, tq, MLA_VDIM + LANES), F32)]),
        out_shape=jax.ShapeDtypeStruct((nb * l, MLA_HEADS * MLA_VDIM), BF16),
        compiler_params=_params(("parallel", "arbitrary")),
        name="attn_prompt",
    )(qi_of, ki_of, qn, qr, kv, kv, kr)


def _attn_cached_kernel(qn_ref, qr_ref, clat_ref, ckr_ref, nlat_ref, nkr_ref, wuk_ref, wuv_ref,
                        o_ref, ql_ref, qrs_ref, *, lq):
    for h in range(MLA_HEADS):
        ql_ref[h * lq:(h + 1) * lq, :] = _bdot_nt(qn_ref[h], wuk_ref[h]).astype(BF16)
        qrs_ref[h * lq:(h + 1) * lq, :] = qr_ref[h]
    ql = ql_ref[...]
    qr = qrs_ref[...]
    clat = clat_ref[...].astype(BF16)
    nlat = nlat_ref[...].astype(BF16)
    s_c = (_bdot_nt(ql, clat) + _bdot_nt(qr, ckr_ref[...])) * MLA_SCALE
    s_n = (_bdot_nt(ql, nlat) + _bdot_nt(qr, nkr_ref[...])) * MLA_SCALE
    m = jnp.maximum(jnp.max(s_c, axis=-1, keepdims=True), jnp.max(s_n, axis=-1, keepdims=True))
    p_c = jnp.exp(s_c - m)
    p_n = jnp.exp(s_n - m)
    denom = jnp.sum(p_c, axis=-1, keepdims=True) + jnp.sum(p_n, axis=-1, keepdims=True)
    o_lat = (_bdot(p_c, clat) + _bdot(p_n, nlat)) / denom
    for h in range(MLA_HEADS):
        o_ref[:, h * MLA_VDIM:(h + 1) * MLA_VDIM] = _bdot(
            o_lat[h * lq:(h + 1) * lq, :], wuv_ref[h]).astype(o_ref.dtype)


def _attn_cached(qn, qr, cache_lat, cache_kr_pad, new_lat, new_kr_pad, wuk_h, wuv_h, nb, lq):
    past = cache_lat.shape[1]
    whole = lambda b: (0, 0, 0)
    return pl.pallas_call(
        functools.partial(_attn_cached_kernel, lq=lq),
        grid=(nb,),
        in_specs=[
            pl.BlockSpec((MLA_HEADS, lq, LANES), lambda b: (0, b, 0)),
            pl.BlockSpec((MLA_HEADS, lq, LANES), lambda b: (0, b, 0)),
            pl.BlockSpec((None, past, KV_LORA), lambda b: (b, 0, 0)),
            pl.BlockSpec((None, past, LANES), lambda b: (b, 0, 0)),
            pl.BlockSpec((lq, KV_LORA), lambda b: (b, 0)),
            pl.BlockSpec((lq, LANES), lambda b: (b, 0)),
            pl.BlockSpec((MLA_HEADS, KV_LORA, MLA_NOPE), whole),
            pl.BlockSpec((MLA_HEADS, KV_LORA, MLA_VDIM), whole),
        ],
        out_specs=pl.BlockSpec((lq, MLA_HEADS * MLA_VDIM), lambda b: (b, 0)),
        out_shape=jax.ShapeDtypeStruct((nb * lq, MLA_HEADS * MLA_VDIM), BF16),
        scratch_shapes=[pltpu.VMEM((MLA_HEADS * lq, KV_LORA), BF16),
                        pltpu.VMEM((MLA_HEADS * lq, LANES), BF16)],
        compiler_params=_params(("parallel",)),
        name="attn_cached",
    )(qn, qr, cache_lat, cache_kr_pad, new_lat, new_kr_pad, wuk_h, wuv_h)


def _top_values(s, top_ref, want_rank, one_per_round=False):
    rank = jnp.full(s.shape, float(PEER_TOPK), F32) if want_rank else None
    rows = lax.broadcasted_iota(jnp.int32, s.shape, 0).astype(F32) if one_per_round else None
    for r in range(PEER_TOPK):
        m = jnp.max(s, axis=0, keepdims=True)
        top_ref[r:r + 1, :] = m
        hit = s == m
        if one_per_round:
            first = jnp.min(jnp.where(hit, rows, float(s.shape[0])), axis=0, keepdims=True)
            hit = rows == first
        if want_rank:
            rank = jnp.where(hit, float(r), rank)
        s = jnp.where(hit, -jnp.inf, s)
    return rank, m


_CAND_LIMITS = tuple(PEER_TOPK // (b + 1) for b in range(1, 8))
_HALF_TOPK = PEER_TOPK // 2


def _peer_select_kernel(h_ref, wq_ref, keys_ref, r1_ref, e1_ref, c0_ref, e0_ref, s_ref, a0_ref, a1_ref,
                        n_ref, *, heads_per_step):
    hx = h_ref[...]
    for hh in range(heads_per_step):
        q_t = _bdot_nt(wq_ref[hh * 2 * N_KEYS:(hh + 1) * 2 * N_KEYS, :], hx)
        s_ref[2 * hh] = _bdot(keys_ref[hh, 0], q_t[:N_KEYS])
        s_ref[2 * hh + 1] = _bdot(keys_ref[hh, 1], q_t[N_KEYS:])
    tied = [_peer_select_head(s_ref[2 * hh], s_ref[2 * hh + 1], r1_ref.at[hh], e1_ref.at[hh],
                              c0_ref.at[hh], e0_ref.at[hh], a0_ref.at[hh], a1_ref.at[hh])
            for hh in range(heads_per_step)]

    def redo(hh):
        _peer_select_head_ties(s_ref[2 * hh], s_ref[2 * hh + 1], r1_ref.at[hh], e1_ref.at[hh],
                               c0_ref.at[hh], e0_ref.at[hh], a0_ref.at[hh], a1_ref.at[hh], n_ref)

    for hh in range(heads_per_step):
        pl.when(jnp.max(tied[hh].astype(F32)) > 0.0)(functools.partial(redo, hh))


def _peer_select_head(s0, s1, r1_ref, e1_ref, c0_ref, e0_ref, a0_ref, a1_ref):
    _, last0 = _top_values(s0, a0_ref, False)
    rank1, _ = _top_values(s1, a1_ref, True)
    removed0 = jnp.sum((s0 >= last0).astype(F32), axis=0, keepdims=True)
    removed1 = jnp.sum((rank1 < float(PEER_TOPK)).astype(F32), axis=0, keepdims=True)
    a0 = a0_ref[...]
    a1 = a1_ref[...]
    a0_lo = a0[:_HALF_TOPK, :]
    row = lax.broadcasted_iota(jnp.int32, a0_lo.shape, 0)
    pieces = [a1[0:1, :] + a0]
    for b, limit in enumerate(_CAND_LIMITS, start=1):
        pieces.append(jnp.where(row < limit, a1[b:b + 1, :] + a0_lo, -jnp.inf))
    pieces.append(a1[_HALF_TOPK:, :] + a0[0:1, :])
    cand = jnp.concatenate(pieces, axis=0)
    c = cand
    for r in range(PEER_TOPK):
        tau = jnp.max(c, axis=0, keepdims=True)
        if r + 1 < PEER_TOPK:
            c = jnp.where(c == tau, -jnp.inf, c)
    sel = cand >= tau
    cmax = a0[0:1, :] + a1[0:1, :]
    z = jnp.sum(jnp.where(sel, jnp.exp(cand - cmax), 0.0), axis=0, keepdims=True)
    self32 = sel.astype(F32)
    n_hi = self32[_HALF_TOPK:PEER_TOPK, :]
    n_lo = self32[:_HALF_TOPK, :]
    for b in range(1, 8):
        lo = PEER_TOPK + (b - 1) * _HALF_TOPK
        n_lo = n_lo + self32[lo:lo + _HALF_TOPK, :]
    tail = jnp.sum(self32[PEER_TOPK + 7 * _HALF_TOPK:, :], axis=0, keepdims=True)
    n_lo = n_lo + jnp.where(row == 0, tail, 0.0)
    count0 = jnp.zeros(s0.shape, F32)
    for a in range(PEER_TOPK):
        n_a = n_lo[a:a + 1, :] if a < _HALF_TOPK else n_hi[a - _HALF_TOPK:a - _HALF_TOPK + 1, :]
        count0 = jnp.where(s0 == a0[a:a + 1, :], n_a, count0)
    r1_ref[...] = rank1.astype(r1_ref.dtype)
    c0_ref[...] = count0
    e0_ref[...] = jnp.exp(s0 - a0[0:1, :]) / z
    e1_ref[...] = jnp.exp(s1 - a1[0:1, :]).astype(e1_ref.dtype)

    n_sel = jnp.sum(self32, axis=0, keepdims=True)
    k = float(PEER_TOPK)
    return (removed0 != k) | (removed1 != k) | (n_sel != k)


def _peer_select_head_ties(s0, s1, r1_ref, e1_ref, c0_ref, e0_ref, a0_ref, a1_ref, n_ref):
    rank0, _ = _top_values(s0, a0_ref, True, one_per_round=True)
    rank1, _ = _top_values(s1, a1_ref, True, one_per_round=True)
    a0 = a0_ref[...]
    a1 = a1_ref[...]
    cand = jnp.concatenate([a0[a:a + 1, :] + a1 for a in range(PEER_TOPK)], axis=0)
    rows = lax.broadcasted_iota(jnp.int32, cand.shape, 0).astype(F32)
    c = cand
    sel = jnp.zeros(cand.shape, F32)
    for _ in range(PEER_TOPK):
        m = jnp.max(c, axis=0, keepdims=True)
        first = jnp.min(jnp.where(c == m, rows, float(cand.shape[0])), axis=0, keepdims=True)
        hit = rows == first
        sel = jnp.where(hit, 1.0, sel)
        c = jnp.where(hit, -jnp.inf, c)
    cmax = a0[0:1, :] + a1[0:1, :]
    z = jnp.sum(sel * jnp.exp(cand - cmax), axis=0, keepdims=True)
    for a in range(PEER_TOPK):
        n_ref[a:a + 1, :] = jnp.sum(sel[a * PEER_TOPK:(a + 1) * PEER_TOPK, :], axis=0, keepdims=True)
    n = n_ref[...]
    count0 = jnp.zeros(s0.shape, F32)
    for a in range(PEER_TOPK):
        count0 = jnp.where(rank0 == float(a), n[a:a + 1, :], count0)
    r1_ref[...] = rank1.astype(r1_ref.dtype)
    c0_ref[...] = count0
    e0_ref[...] = jnp.exp(s0 - a0[0:1, :]) / z
    e1_ref[...] = jnp.exp(s1 - a1[0:1, :]).astype(e1_ref.dtype)


def _peer_select(hx, wq_t, keys, layer, tt):
    t, d = hx.shape
    hps = SELECT_HEADS_PER_STEP
    o_spec = pl.BlockSpec((hps, N_KEYS, tt), lambda i, g: (g, 0, i))
    return pl.pallas_call(
        functools.partial(_peer_select_kernel, heads_per_step=hps),
        grid=(t // tt, PEER_HEADS // hps),
        in_specs=[pl.BlockSpec((tt, d), lambda i, g: (i, 0)),
                  pl.BlockSpec((hps * 2 * N_KEYS, d), lambda i, g: (g, 0)),
                  pl.BlockSpec((None, hps, 2, N_KEYS, N_KEYS), lambda i, g: (layer, g, 0, 0, 0))],
        out_specs=[o_spec] * 4,
        out_shape=[jax.ShapeDtypeStruct((PEER_HEADS, N_KEYS, t), dt) for dt in (BF16, BF16, F32, F32)],
        scratch_shapes=[pltpu.VMEM((2 * hps, N_KEYS, tt), F32),
                        pltpu.VMEM((hps, PEER_TOPK, tt), F32), pltpu.VMEM((hps, PEER_TOPK, tt), F32),
                        pltpu.VMEM((PEER_TOPK, tt), F32)],
        compiler_params=_params(("parallel", "arbitrary")),
        name="peer_select",
    )(hx, wq_t, keys)


def _sublane_bcast_bf16(row, rows):
    tile_rows = 16
    packed = jnp.broadcast_to(row, (tile_rows, row.shape[1])).astype(BF16)
    return jnp.tile(packed, (rows // tile_rows, 1))


def _peer_mix_kernel(x_ref, u_ref, vt_ref, r1_ref, e1_ref, c0_ref, e0_ref, res_ref, gate_ref, o_ref,
                     acc_ref, a_ref, g_ref, p_ref, *, n_i):
    j = pl.program_id(1)

    @pl.when(j == 0)
    def _():
        acc_ref[...] = jnp.zeros_like(acc_ref)

    x = x_ref[...]
    rows_per_chunk = 2 * N_KEYS
    for c in range(n_i // 2):
        lo = c * rows_per_chunk
        a_ref[c] = _bdot_nt(u_ref[lo:lo + rows_per_chunk, :], x)
    zero = jnp.zeros((), BF16)
    for ii in range(n_i):
        w = None
        for h in range(PEER_HEADS):
            count = _sublane_bcast_bf16(c0_ref[h, ii:ii + 1, :], N_KEYS)
            e0 = _sublane_bcast_bf16(e0_ref[h, ii:ii + 1, :], N_KEYS)
            term = jnp.where(r1_ref[h] < count, e1_ref[h] * e0, zero)
            w = term if w is None else w + term
        g_ref[ii] = w
    for c in range(n_i // 2):
        lo = c * rows_per_chunk
        gates = jnp.concatenate([g_ref[2 * c], g_ref[2 * c + 1]], axis=0)
        p_ref[lo:lo + rows_per_chunk, :] = gates * _gelu(a_ref[c]).astype(BF16)
    acc_ref[...] += jnp.dot(vt_ref[...], p_ref[...], preferred_element_type=F32)

    @pl.when(j == pl.num_programs(1) - 1)
    def _():
        o_ref[...] = res_ref[...] + (1.0 + gate_ref[...]) * acc_ref[...].T


def _peer_mix(hx, u_b, vt_b, layer, r1, e1, c0, e0, x_res, gate, l, tt, et):
    t, d = hx.shape
    e = u_b.shape[1]
    n_i = et // N_KEYS
    tok_all = pl.BlockSpec((PEER_HEADS, N_KEYS, tt), lambda i, j: (0, 0, i))
    tok_i = pl.BlockSpec((PEER_HEADS, n_i, tt), lambda i, j: (0, j, i))
    gate_arr, gate_spec = _gate_operand(gate, l, tt, d)
    return pl.pallas_call(
        functools.partial(_peer_mix_kernel, n_i=n_i),
        grid=(t // tt, e // et),
        in_specs=[pl.BlockSpec((tt, d), lambda i, j: (i, 0)),
                  pl.BlockSpec((None, et, d), lambda i, j: (layer, j, 0)),
                  pl.BlockSpec((None, d, et), lambda i, j: (layer, 0, j)),
                  tok_all, tok_all, tok_i, tok_i,
                  pl.BlockSpec((tt, d), lambda i, j: (i, 0)), gate_spec],
        out_specs=pl.BlockSpec((tt, d), lambda i, j: (i, 0)),
        out_shape=jax.ShapeDtypeStruct((t, d), F32),
        scratch_shapes=[pltpu.VMEM((d, tt), F32),
                        pltpu.VMEM((n_i // 2, 2 * N_KEYS, tt), F32),
                        pltpu.VMEM((n_i, N_KEYS, tt), BF16),
                        pltpu.VMEM((et, tt), BF16)],
        compiler_params=_params(("parallel", "arbitrary"), vmem=PEER_MIX_VMEM_LIMIT),
        name="peer_mix",
    )(hx, u_b, vt_b, r1, e1, c0, e0, x_res, gate_arr)


def _peer(hx, wq_t, keys, layer, u_b, vt_b, x_res, gate, l):
    t = hx.shape[0]
    tt = min(t, 512)
    r1, e1, c0, e0 = _peer_select(hx, wq_t, keys, layer, tt)
    return _peer_mix(hx, u_b, vt_b, layer, r1, e1, c0, e0, x_res, gate, l, tt, 1024)


def _rope_tables(pos, half, reps):
    inv = ROPE_THETA ** (-jnp.arange(half, dtype=F32) / half)
    ang = pos.astype(F32)[:, None] * inv[None, :]
    return jnp.tile(jnp.cos(ang), (1, reps)), jnp.tile(jnp.sin(ang), (1, reps))


def _rotate_half_cols(w):
    half = w.shape[-1] // 2
    return jnp.concatenate([-w[..., half:], w[..., :half]], axis=-1)


def _pad_last(w, width):
    return jnp.pad(w, [(0, 0)] * (w.ndim - 1) + [(0, width - w.shape[-1])])


def _shared_weights(p):
    out = {}
    w_rope = p["mla_w_dkv"][:, KV_LORA:]
    out["w_dkv_ext"] = jnp.concatenate(
        [p["mla_w_dkv"][:, :KV_LORA], _pad_last(w_rope, LANES), _pad_last(_rotate_half_cols(w_rope), LANES)],
        axis=1)
    w_uq = p["mla_w_uq"][0].reshape(-1, MLA_HEADS, MLA_NOPE + MLA_ROPE)
    q_lora = w_uq.shape[0]
    out["w_uq_nope"] = w_uq[:, :, :MLA_NOPE].reshape(q_lora, MLA_HEADS * MLA_NOPE)
    w_qr = w_uq[:, :, MLA_NOPE:]
    out["w_uq_rope"] = _pad_last(w_qr, LANES).reshape(q_lora, MLA_HEADS * LANES)
    out["w_uq_rot"] = _pad_last(_rotate_half_cols(w_qr), LANES).reshape(q_lora, MLA_HEADS * LANES)
    out["w_ukv_flat"] = jnp.concatenate([p["mla_w_uk"].reshape(KV_LORA, MLA_HEADS * MLA_NOPE),
                                         p["mla_w_uv"].reshape(KV_LORA, MLA_HEADS * MLA_VDIM)], axis=1)
    out["w_uk_h"] = jnp.transpose(p["mla_w_uk"], (1, 0, 2))
    out["w_uv_h"] = jnp.transpose(p["mla_w_uv"], (1, 0, 2))
    out["peer_wq_t"] = [p["peer_w_q"][l].T for l in range(2)]
    out["peer_u_b"] = p["peer_u"].astype(BF16)
    out["peer_vt_b"] = jnp.swapaxes(p["peer_v"].astype(BF16), 1, 2)
    return out


def _adaln(c_all, w, b, lead, n_out):
    m, d = c_all.shape
    tn = 2048
    if w.ndim == 2:
        w = w[None]
        b = b[None]
    b3 = b.reshape(b.shape[0], 1, n_out)
    return _linear(
        c_all, [w], n_cols=n_out, tm=m, tn=tn, epilogue=_ep_bias, prologue=_silu, w_lead=lead,
        extras=[b3], extra_specs=[pl.BlockSpec((None, 1, tn), lambda i, j: (lead, 0, j))],
        out_shapes=[jax.ShapeDtypeStruct((m, n_out), F32)],
        out_specs=[pl.BlockSpec((m, tn), lambda i, j: (0, j))])[0]


def _trunk(x, pos, mods, kv_mod, p, sw, ret_state, cache_lat, cache_kr):
    nb, l, d = x.shape
    t = nb * l
    pos_rows = jnp.tile(pos, nb) if l < 256 else pos
    table_rows = pos_rows.shape[0]
    assert table_rows % _row_tile(t) == 0, "a row tile must cover whole sequences or divide one"

    def vec(a):
        return a.reshape(nb, 1, d)

    def flat(a):
        return a.reshape(t, a.shape[-1])

    sh1, sc1, gt1, sh2, sc2, gt2 = [vec(m) for m in mods[0]]
    (h,) = _norm_mod(x, mods=[(p["norm_g"][0, 0][None], sh1, sc1)])
    h = flat(h)
    tm = _row_tile(t)
    cos_r, sin_r = _rope_tables(pos_rows, RET_DK // 2, 1)
    tspec = _table_spec(tm, table_rows)
    w_in = p["ret_w_in"]
    qk_cols = RET_HEADS * RET_DK
    v_cols = RET_HEADS * RET_DV
    tn = _col_tile(d, qk_cols)

    def rope_proj(col0, scale):
        return _linear(
            h, [w_in], n_cols=qk_cols, tm=tm, tn=tn, w_lead=0, col_block0=col0 // tn,
            epilogue=functools.partial(_ep_rope_half128, scale=scale),
            extras=[cos_r, sin_r], extra_specs=[tspec, tspec],
            out_shapes=[jax.ShapeDtypeStruct((t, qk_cols), BF16)],
            out_specs=[pl.BlockSpec((tm, tn), lambda i, j: (i, j))])[0]

    def plain_proj(x_in, w, n_cols, col0=0, lead=None, dtype=BF16):
        tm_ = _row_tile(x_in.shape[0])
        tn_ = _col_tile(x_in.shape[1], n_cols)
        return _linear(
            x_in, [w], n_cols=n_cols, tm=tm_, tn=tn_, w_lead=lead, col_block0=col0 // tn_,
            epilogue=_ep_plain,
            out_shapes=[jax.ShapeDtypeStruct((x_in.shape[0], n_cols), dtype)],
            out_specs=[pl.BlockSpec((tm_, tn_), lambda i, j: (i, j))])[0]

    def resid_proj(x_in, w, x_res, gate):
        tm_ = _row_tile(x_in.shape[0])
        tn_ = _col_tile(x_in.shape[1], d)
        gate_arr, gate_spec = _gate_operand(gate, l, tm_, tn_)
        tile = pl.BlockSpec((tm_, tn_), lambda i, j: (i, j))
        return _linear(
            x_in, [w], n_cols=d, tm=tm_, tn=tn_, w_lead=0, epilogue=_ep_resid,
            extras=[x_res, gate_arr], extra_specs=[tile, gate_spec],
            out_shapes=[jax.ShapeDtypeStruct((t, d), F32)], out_specs=[tile])[0]

    def heads_proj(x_in, ws, n_cols, epilogue=_ep_heads, extras=(), extra_specs=()):
        tm_ = _row_tile(x_in.shape[0])
        tn_ = _col_tile(x_in.shape[1], n_cols, len(ws))
        return _linear(
            x_in, ws, n_cols=n_cols, tm=tm_, tn=tn_, epilogue=epilogue, extras=extras, extra_specs=extra_specs,
            out_shapes=[jax.ShapeDtypeStruct((n_cols // LANES, x_in.shape[0], LANES), BF16)],
            out_specs=[pl.BlockSpec((tn_ // LANES, tm_, LANES), lambda i, j: (j, i, 0))])[0]

    q_r = rope_proj(0, 1.0)
    k_r = rope_proj(qk_cols, RET_DK ** -0.5)
    v_r = plain_proj(h, w_in, v_cols, col0=2 * qk_cols, lead=0)
    g_r = plain_proj(h, w_in, v_cols, col0=2 * qk_cols + v_cols, lead=0)
    s0 = None if ret_state is None else ret_state[0]
    y_r, s_new = _retention(q_r, k_r, v_r, g_r, p["ret_gn_g"], s0, nb, l)
    xf = resid_proj(y_r, p["ret_w_out"], flat(x), gt1)
    (h,) = _norm_mod(xf.reshape(nb, l, d), mods=[(p["norm_g"][0, 1][None], sh2, sc2)])
    xf = _peer(flat(h), sw["peer_wq_t"][0], p["peer_keys"], 0, sw["peer_u_b"], sw["peer_vt_b"],
               xf, gt2, l)

    sh1, sc1, gt1b, sh2, sc2, gt2b = [vec(m) for m in mods[1]]
    kv_sh, kv_sc = [vec(m) for m in kv_mod]
    h, h_kv = _norm_mod(xf.reshape(nb, l, d),
                         mods=[(p["norm_g"][1, 0][None], sh1, sc1), (p["kv_norm_g"][None], kv_sh, kv_sc)])
    h = flat(h)
    h_kv = flat(h_kv)
    cos_m, sin_m = _rope_tables(pos_rows, MLA_ROPE // 2, LANES // (MLA_ROPE // 2))
    n_ext = sw["w_dkv_ext"].shape[1]
    lat, kr, kr_pad = _linear(
        h_kv, [sw["w_dkv_ext"]], n_cols=n_ext, tm=tm, tn=n_ext, epilogue=_ep_kv,
        extras=[p["mla_kv_norm_g"][None], cos_m, sin_m],
        extra_specs=[pl.BlockSpec((1, KV_LORA), lambda i, j: (0, 0)), tspec, tspec],
        out_shapes=[jax.ShapeDtypeStruct((t, KV_LORA), F32), jax.ShapeDtypeStruct((t, MLA_ROPE), F32),
                    jax.ShapeDtypeStruct((t, LANES), BF16)],
        out_specs=[pl.BlockSpec((tm, KV_LORA), lambda i, j: (i, 0)),
                   pl.BlockSpec((tm, MLA_ROPE), lambda i, j: (i, 0)),
                   pl.BlockSpec((tm, LANES), lambda i, j: (i, 0))])
    q_lora = p["mla_w_dq"].shape[-1]
    cq = _linear(
        h, [p["mla_w_dq"]], n_cols=q_lora, tm=tm, tn=q_lora, w_lead=0, epilogue=_ep_rms,
        extras=[p["mla_q_norm_g"]], extra_specs=[pl.BlockSpec((1, q_lora), lambda i, j: (0, 0))],
        out_shapes=[jax.ShapeDtypeStruct((t, q_lora), BF16)],
        out_specs=[pl.BlockSpec((tm, q_lora), lambda i, j: (i, 0))])[0]
    hl = MLA_HEADS * LANES
    qn = heads_proj(cq, [sw["w_uq_nope"]], hl)
    qr = heads_proj(cq, [sw["w_uq_rope"], sw["w_uq_rot"]], hl, epilogue=_ep_rot_heads,
                    extras=[cos_m, sin_m], extra_specs=[tspec, tspec])
    if cache_lat is None:
        kv_h = heads_proj(lat, [sw["w_ukv_flat"]], 2 * hl)
        o = _attn_prompt(qn, qr, kv_h, kr_pad, nb, l)
    else:
        ckr = _pad_last(cache_kr, LANES).astype(BF16)
        o = _attn_cached(qn, qr, cache_lat, ckr, lat, kr_pad, sw["w_uk_h"], sw["w_uv_h"], nb, l)
    xf = resid_proj(o, p["mla_w_o"], xf, gt1b)
    (h,) = _norm_mod(xf.reshape(nb, l, d), mods=[(p["norm_g"][1, 1][None], sh2, sc2)])
    xf = _peer(flat(h), sw["peer_wq_t"][1], p["peer_keys"], 1, sw["peer_u_b"], sw["peer_vt_b"],
               xf, gt2b, l)
    (y,) = _norm_mod(xf.reshape(nb, l, d), final_g=p["final_g"][None])
    return y, s_new[None], lat.reshape(nb, l, KV_LORA), kr.reshape(nb, l, MLA_ROPE)


def kernel(x_prompt, x_sample, c_prompt, c_sample, state_retention, cache_mla_latent, cache_mla_krope,
           ada_w, ada_b, norm_g, ret_w_in, ret_gn_g, ret_w_out,
           kv_ada_w, kv_ada_b, kv_norm_g, mla_w_dkv, mla_kv_norm_g, mla_w_uk, mla_w_uv,
           mla_w_dq, mla_q_norm_g, mla_w_uq, mla_w_o,
           peer_w_q, peer_keys, peer_u, peer_v, final_g):
    p = dict(norm_g=norm_g, ret_w_in=ret_w_in, ret_gn_g=ret_gn_g[0][None], ret_w_out=ret_w_out,
             kv_norm_g=kv_norm_g, mla_w_dkv=mla_w_dkv, mla_kv_norm_g=mla_kv_norm_g,
             mla_w_uk=mla_w_uk, mla_w_uv=mla_w_uv, mla_w_dq=mla_w_dq, mla_q_norm_g=mla_q_norm_g,
             mla_w_uq=mla_w_uq, mla_w_o=mla_w_o, peer_w_q=peer_w_q, peer_keys=peer_keys,
             peer_u=peer_u, peer_v=peer_v, final_g=final_g)
    sw = _shared_weights(p)
    nbp, lp, d = x_prompt.shape
    nbs, ls, _ = x_sample.shape
    past = cache_mla_latent.shape[1]

    n_c = nbp + nbs
    c_all = jnp.pad(jnp.concatenate([c_prompt, c_sample], axis=0), ((0, (-n_c) % 8), (0, 0)))
    layer_mods = [_adaln(c_all, ada_w, ada_b, layer, 6 * d) for layer in range(2)]
    kv_mods = _adaln(c_all, kv_ada_w, kv_ada_b, 0, 2 * d)

    def stream_mods(lo, hi):
        return ([jnp.split(m[lo:hi], 6, axis=-1) for m in layer_mods], jnp.split(kv_mods[lo:hi], 2, axis=-1))

    mods_p, kv_p = stream_mods(0, nbp)
    mods_s, kv_s = stream_mods(nbp, n_c)
    pos_p = jnp.arange(lp, dtype=jnp.int32)
    pos_s = past + jnp.arange(ls, dtype=jnp.int32)
    y_p, ret_p, lat_p, kr_p = _trunk(x_prompt, pos_p, mods_p, kv_p, p, sw, None, None, None)
    y_s, ret_s, lat_s, kr_s = _trunk(x_sample, pos_s, mods_s, kv_s, p, sw, state_retention,
                                     cache_mla_latent, cache_mla_krope)
    return (y_p, y_s, ret_p, ret_s, lat_p, kr_p, lat_s, kr_s)
```

```python
import functools
import math

import jax
import jax.numpy as jnp
from jax import lax
from jax.experimental import pallas as pl
from jax.experimental.pallas import tpu as pltpu

F32 = jnp.float32
BF16 = jnp.bfloat16

EPS = 1e-6
ROPE_THETA = 10000.0
NEG_INF = -1e30
CHUNK = 64

RET_HEADS = 8
RET_DK = 256
RET_DV = 512

MLA_HEADS = 16
MLA_NOPE = 128
MLA_ROPE = 64
MLA_VDIM = 128
KV_LORA = 512
MLA_SCALE = (MLA_NOPE + MLA_ROPE) ** -0.5

PEER_HEADS = 8
N_KEYS = 128
PEER_TOPK = 16

LANES = 128
VMEM_LIMIT = 52 * 1024 * 1024
PEER_MIX_VMEM_LIMIT = 58 * 1024 * 1024
SELECT_HEADS_PER_STEP = 4


def _params(sem, vmem=VMEM_LIMIT):
    return pltpu.CompilerParams(dimension_semantics=sem, vmem_limit_bytes=vmem)


def _bdot(a, b):
    return jnp.dot(a.astype(BF16), b.astype(BF16), preferred_element_type=F32)


def _bdot_nt(a, b):
    return lax.dot_general(a.astype(BF16), b.astype(BF16), (((1,), (1,)), ((), ())),
                           preferred_element_type=F32)


def _bdot_tn(a, b):
    return lax.dot_general(a.astype(BF16), b.astype(BF16), (((0,), (0,)), ((), ())),
                           preferred_element_type=F32)


def _silu(x):
    return x * (1.0 / (1.0 + jnp.exp(-x)))


def _gelu(x):
    return 0.5 * x * (1.0 + lax.erf(x * (0.5 ** 0.5)))


def _linear_kernel(*refs, n_w, n_extra, prologue, epilogue):
    x_ref = refs[0]
    w_refs = refs[1:1 + n_w]
    extra_refs = refs[1 + n_w:1 + n_w + n_extra]
    out_refs = refs[1 + n_w + n_extra:]
    xv = x_ref[...]
    if prologue is not None:
        xv = prologue(xv)
    xb = xv.astype(BF16)
    accs = [jnp.dot(xb, w[...].astype(BF16), preferred_element_type=F32) for w in w_refs]
    epilogue(accs, extra_refs, out_refs)


def _linear(x, ws, *, n_cols, tm, tn, epilogue, out_shapes, out_specs, w_lead=None, col_block0=0,
            extras=(), extra_specs=(), prologue=None, name=None):
    m, k = x.shape
    if name is None:
        name = "linear" + getattr(epilogue, "func", epilogue).__name__
    assert m % tm == 0 and n_cols % tn == 0
    if w_lead is None:
        w_spec = pl.BlockSpec((k, tn), lambda i, j: (0, j + col_block0))
    else:
        w_spec = pl.BlockSpec((None, k, tn), lambda i, j: (w_lead, 0, j + col_block0))
    kern = functools.partial(_linear_kernel, n_w=len(ws), n_extra=len(extras), prologue=prologue,
                             epilogue=epilogue)
    return pl.pallas_call(
        kern,
        grid=(m // tm, n_cols // tn),
        in_specs=[pl.BlockSpec((tm, k), lambda i, j: (i, 0))] + [w_spec] * len(ws) + list(extra_specs),
        out_specs=out_specs,
        out_shape=out_shapes,
        compiler_params=_params(("parallel", "arbitrary")),
        name=name,
    )(x, *ws, *extras)


def _ep_plain(accs, extras, outs):
    outs[0][...] = accs[0].astype(outs[0].dtype)


def _ep_heads(accs, extras, outs):
    acc = accs[0]
    for jj in range(acc.shape[1] // LANES):
        outs[0][jj] = acc[:, jj * LANES:(jj + 1) * LANES].astype(outs[0].dtype)


def _ep_bias(accs, extras, outs):
    outs[0][...] = accs[0] + extras[0][...]


def _ep_resid(accs, extras, outs):
    outs[0][...] = extras[0][...] + (1.0 + extras[1][...]) * accs[0]


def _gate_operand(gate, l, rows_per_tile, cols_per_tile):
    nb, _, d = gate.shape
    col = (lambda j: j) if cols_per_tile < d else (lambda j: 0)
    if l % rows_per_tile == 0:
        tiles_per_batch = l // rows_per_tile
        return gate, pl.BlockSpec((None, 1, cols_per_tile), lambda i, j: (i // tiles_per_batch, 0, col(j)))
    rows = jnp.broadcast_to(gate, (nb, l, d)).reshape(nb * l, d)
    return rows, pl.BlockSpec((rows_per_tile, cols_per_tile), lambda i, j: (i, col(j)))


def _ep_rope_half128(accs, extras, outs, *, scale):
    acc = accs[0]
    cos = extras[0][...]
    sin = extras[1][...]
    for g in range(acc.shape[1] // 256):
        x1 = acc[:, g * 256:g * 256 + 128]
        x2 = acc[:, g * 256 + 128:(g + 1) * 256]
        outs[0][:, g * 256:g * 256 + 128] = ((x1 * cos - x2 * sin) * scale).astype(outs[0].dtype)
        outs[0][:, g * 256 + 128:(g + 1) * 256] = ((x1 * sin + x2 * cos) * scale).astype(outs[0].dtype)


def _ep_query_heads(accs, extras, outs):
    cos = extras[0][...]
    sin = extras[1][...]
    plain, a, b = accs
    for jj in range(a.shape[1] // LANES):
        sl = slice(jj * LANES, (jj + 1) * LANES)
        outs[0][jj] = plain[:, sl].astype(outs[0].dtype)
        outs[1][jj] = (a[:, sl] * cos + b[:, sl] * sin).astype(outs[1].dtype)


def _ep_rms(accs, extras, outs):
    acc = accs[0]
    g = extras[0][...]
    y = acc * lax.rsqrt(jnp.mean(acc * acc, axis=-1, keepdims=True) + EPS) * g
    outs[0][...] = y.astype(outs[0].dtype)


def _ep_kv(accs, extras, outs):
    acc = accs[0]
    g = extras[0][...]
    cos = extras[1][...]
    sin = extras[2][...]
    c = acc[:, :KV_LORA]
    outs[0][...] = c * lax.rsqrt(jnp.mean(c * c, axis=-1, keepdims=True) + EPS) * g
    kr = acc[:, KV_LORA:KV_LORA + LANES] * cos + acc[:, KV_LORA + LANES:KV_LORA + 2 * LANES] * sin
    outs[1][...] = kr[:, :MLA_ROPE]
    outs[2][...] = kr.astype(BF16)


def _row_tile(m):
    return min(m, 1024)


def _col_tile(k, n, n_weights=1):
    for tn in (1024, 512):
        if n % tn == 0 and 2 * n_weights * k * tn * 4 <= 16 * 1024 * 1024:
            return tn
    return min(n, 512)


def _table_spec(tm, table_rows):
    nblk = table_rows // tm
    return pl.BlockSpec((tm, LANES), lambda i, j: (i % nblk, 0))


def _norm_mod_kernel(*refs, n_mod):
    x = refs[0][...]
    xn = x * lax.rsqrt(jnp.mean(x * x, axis=-1, keepdims=True) + EPS)
    if n_mod == 0:
        refs[2][...] = xn * refs[1][...]
        return
    outs = refs[1 + 3 * n_mod:]
    for m in range(n_mod):
        g, sh, sc = (refs[1 + 3 * m + k][...] for k in range(3))
        outs[m][...] = ((xn * g) * (1.0 + sc) + sh).astype(outs[m].dtype)


def _norm_mod(x, mods=(), final_g=None):
    nb, l, d = x.shape
    tl = min(l, 512)
    tok = pl.BlockSpec((None, tl, d), lambda b, i: (b, i, 0))
    per_b = pl.BlockSpec((None, 1, d), lambda b, i: (b, 0, 0))
    gain = pl.BlockSpec((1, d), lambda b, i: (0, 0))
    args, specs = [x], [tok]
    if final_g is not None:
        assert not mods
        args.append(final_g)
        specs.append(gain)
        out_shapes = [jax.ShapeDtypeStruct((nb, l, d), F32)]
    else:
        for g, sh, sc in mods:
            args += [g, sh, sc]
            specs += [gain, per_b, per_b]
        out_shapes = [jax.ShapeDtypeStruct((nb, l, d), BF16) for _ in mods]
    return pl.pallas_call(
        functools.partial(_norm_mod_kernel, n_mod=len(mods)),
        grid=(nb, l // tl), in_specs=specs, out_specs=[tok] * len(out_shapes), out_shape=out_shapes,
        compiler_params=_params(("parallel", "parallel")),
        name="norm_mod",
    )(*args)


def _ret_log_decay():
    return jnp.log1p(-jnp.exp2(-5.0 - jnp.arange(RET_HEADS, dtype=F32)))


def _retention_tables(c):
    log_g = _ret_log_decay()[:, None, None]
    n = jnp.arange(c, dtype=F32)
    dist = n[:, None] - n[None, :]
    same = (jnp.arange(c)[:, None] // CHUNK) == (jnp.arange(c)[None, :] // CHUNK)
    earlier = (jnp.arange(c)[None, :] // CHUNK) < (jnp.arange(c)[:, None] // CHUNK)
    mask = jnp.where(same[None], jnp.exp(jnp.abs(dist)[None] * log_g),
                     jnp.where(earlier[None], jnp.exp(dist[None] * log_g), 0.0))
    q_decay = jnp.exp((n[None, :, None] + 1.0) * log_g)
    k_decay = jnp.exp((c - 1.0 - n)[None, :, None] * log_g)
    blk_decay = jnp.exp(c * log_g)
    return mask, q_decay, k_decay, blk_decay


def _retention_kernel(*refs, has_s0, hps):
    if has_s0:
        (q_ref, k_ref, v_ref, g_ref, mask_ref, qd_ref, kd_ref, bd_ref, gn_ref, s0_ref,
         y_ref, s_out_ref, s_ref) = refs
    else:
        (q_ref, k_ref, v_ref, g_ref, mask_ref, qd_ref, kd_ref, bd_ref, gn_ref,
         y_ref, s_out_ref, s_ref) = refs
    c = pl.program_id(2)

    @pl.when(c == 0)
    def _():
        if has_s0:
            s_ref[...] = s0_ref[...]
        else:
            s_ref[...] = jnp.zeros_like(s_ref)

    for hh in range(hps):
        qk_cols = slice(hh * RET_DK, (hh + 1) * RET_DK)
        v_cols = slice(hh * RET_DV, (hh + 1) * RET_DV)
        q = q_ref[:, qk_cols]
        k = k_ref[:, qk_cols]
        v = v_ref[:, v_cols]
        s_prev = s_ref[hh]
        scores = _bdot_nt(q, k) * mask_ref[hh]
        y = _bdot(scores, v) + _bdot(q, s_prev) * qd_ref[hh]
        k_scaled = k.astype(F32) * kd_ref[hh]
        s_ref[hh] = bd_ref[hh] * s_prev + _bdot_tn(k_scaled, v)

        mu = jnp.mean(y, axis=-1, keepdims=True)
        yc = y - mu
        var = jnp.mean(yc * yc, axis=-1, keepdims=True)
        yn = yc * lax.rsqrt(var + EPS) * gn_ref[:, v_cols]
        y_ref[:, v_cols] = (_silu(g_ref[:, v_cols].astype(F32)) * yn).astype(y_ref.dtype)

    @pl.when(c == pl.num_programs(2) - 1)
    def _():
        s_out_ref[...] = s_ref[...]


def _retention(q, k, v, g, gn_g, s0, nb, l):
    cb = min(l, 256)
    nc = l // cb
    hps = RET_HEADS
    mask, qd, kd, bd = _retention_tables(cb)
    row = lambda b, hg, c: (b * nc + c, hg)
    per_h3 = lambda b, hg, c: (hg, 0, 0)
    in_specs = [
        pl.BlockSpec((cb, hps * RET_DK), row), pl.BlockSpec((cb, hps * RET_DK), row),
        pl.BlockSpec((cb, hps * RET_DV), row), pl.BlockSpec((cb, hps * RET_DV), row),
        pl.BlockSpec((hps, cb, cb), per_h3), pl.BlockSpec((hps, cb, 1), per_h3),
        pl.BlockSpec((hps, cb, 1), per_h3), pl.BlockSpec((hps, 1, 1), per_h3),
        pl.BlockSpec((1, hps * RET_DV), lambda b, hg, c: (0, hg)),
    ]
    args = [q, k, v, g, mask, qd, kd, bd, gn_g]
    state_spec = pl.BlockSpec((None, hps, RET_DK, RET_DV), lambda b, hg, c: (b, hg, 0, 0))
    if s0 is not None:
        in_specs.append(state_spec)
        args.append(s0)
    y, s_new = pl.pallas_call(
        functools.partial(_retention_kernel, has_s0=s0 is not None, hps=hps),
        grid=(nb, RET_HEADS // hps, nc),
        in_specs=in_specs,
        out_specs=[pl.BlockSpec((cb, hps * RET_DV), row), state_spec],
        out_shape=[jax.ShapeDtypeStruct((nb * l, RET_HEADS * RET_DV), BF16),
                   jax.ShapeDtypeStruct((nb, RET_HEADS, RET_DK, RET_DV), F32)],
        scratch_shapes=[pltpu.VMEM((hps, RET_DK, RET_DV), F32)],
        compiler_params=_params(("parallel", "parallel", "arbitrary")),
        name="retention",
    )(*args)
    return y, s_new


def _attn_prompt_kernel(qi_ref, ki_ref, qn_ref, qr_ref, kn_ref, v_ref, kr_ref, o_ref, m_ref, acc_ref, *, tq, tk):
    qi = qi_ref[pl.program_id(1)]
    ki = ki_ref[pl.program_id(1)]
    exp2_scale = MLA_SCALE * math.log2(math.e)

    @pl.when(ki == 0)
    def _():
        m_ref[...] = jnp.full_like(m_ref, NEG_INF)
        acc_ref[...] = jnp.zeros_like(acc_ref)

    def block(masked):
        if masked:
            q_chunk = (qi * tq + lax.broadcasted_iota(jnp.int32, (tq, tk), 0)) // CHUNK
            k_chunk = (ki * tk + lax.broadcasted_iota(jnp.int32, (tq, tk), 1)) // CHUNK
            visible = k_chunk <= q_chunk
        kr = kr_ref[...]
        ones = jnp.ones((tk, LANES), BF16)

        def head(h, carry):
            q = jnp.concatenate([qn_ref[h], qr_ref[h]], axis=1)
            k = jnp.concatenate([kn_ref[h], kr], axis=1)
            s = _bdot_nt(q, k)
            if masked:
                s = jnp.where(visible, s, NEG_INF)
            m_prev = m_ref[h]
            m_new = jnp.maximum(m_prev, jnp.max(s, axis=-1, keepdims=True))
            alpha = jnp.exp2((m_prev - m_new) * exp2_scale)
            p = jnp.exp2((s - jnp.tile(m_new, (1, tk // LANES))) * exp2_scale)
            v_ext = jnp.concatenate([v_ref[h], ones], axis=1)
            acc_ref[h] = jnp.tile(alpha, (1, 2)) * acc_ref[h] + _bdot(p, v_ext)
            m_ref[h] = m_new
            return carry

        lax.fori_loop(0, MLA_HEADS, head, 0, unroll=True)

    @pl.when(ki < qi)
    def _():
        block(False)

    @pl.when(ki == qi)
    def _():
        block(True)
        for h in range(MLA_HEADS):
            acc = acc_ref[h]
            o_ref[:, h * MLA_VDIM:(h + 1) * MLA_VDIM] = (
                acc[:, :MLA_VDIM] / acc[:, MLA_VDIM:]).astype(o_ref.dtype)


def _attn_prompt(qn, qr, kv, kr, nb, l):
    tq = tk = min(l, 512)
    nq = l // tq
    pairs = [(qi, ki) for qi in range(nq) for ki in range(qi + 1)]
    qi_of = jnp.asarray([qk[0] for qk in pairs], jnp.int32)
    ki_of = jnp.asarray([qk[1] for qk in pairs], jnp.int32)
    q_spec = pl.BlockSpec((MLA_HEADS, tq, LANES), lambda b, s, qi, ki: (0, b * nq + qi[s], 0))
    k_spec = pl.BlockSpec((MLA_HEADS, tk, LANES), lambda b, s, qi, ki: (0, b * nq + ki[s], 0))
    v_spec = pl.BlockSpec((MLA_HEADS, tk, LANES), lambda b, s, qi, ki: (1, b * nq + ki[s], 0))
    return pl.pallas_call(
        functools.partial(_attn_prompt_kernel, tq=tq, tk=tk),
        grid_spec=pltpu.PrefetchScalarGridSpec(
            num_scalar_prefetch=2,
            grid=(nb, len(pairs)),
            in_specs=[q_spec, q_spec, k_spec, v_spec,
                      pl.BlockSpec((tk, LANES), lambda b, s, qi, ki: (b * nq + ki[s], 0))],
            out_specs=pl.BlockSpec((tq, MLA_HEADS * MLA_VDIM), lambda b, s, qi, ki: (b * nq + qi[s], 0)),
            scratch_shapes=[pltpu.VMEM((MLA_HEADS, tq, LANES), F32),
                            pltpu.VMEM((MLA_HEADS, tq, MLA_VDIM + LANES), F32)]),
        out_shape=jax.ShapeDtypeStruct((nb * l, MLA_HEADS * MLA_VDIM), BF16),
        compiler_params=_params(("parallel", "arbitrary")),
        name="attn_prompt",
    )(qi_of, ki_of, qn, qr, kv, kv, kr)


def _attn_cached_kernel(qn_ref, qr_ref, clat_ref, ckr_ref, nlat_ref, nkr_ref, wuk_ref, wuv_ref,
                        o_ref, ql_ref, qrs_ref, *, lq):
    for h in range(MLA_HEADS):
        ql_ref[h * lq:(h + 1) * lq, :] = _bdot_nt(qn_ref[h], wuk_ref[h]).astype(BF16)
        qrs_ref[h * lq:(h + 1) * lq, :] = qr_ref[h]
    ql = ql_ref[...]
    qr = qrs_ref[...]
    clat = clat_ref[...].astype(BF16)
    nlat = nlat_ref[...].astype(BF16)
    s_c = (_bdot_nt(ql, clat) + _bdot_nt(qr, ckr_ref[...])) * MLA_SCALE
    s_n = (_bdot_nt(ql, nlat) + _bdot_nt(qr, nkr_ref[...])) * MLA_SCALE
    m = jnp.maximum(jnp.max(s_c, axis=-1, keepdims=True), jnp.max(s_n, axis=-1, keepdims=True))
    p_c = jnp.exp(s_c - m)
    p_n = jnp.exp(s_n - m)
    denom = jnp.sum(p_c, axis=-1, keepdims=True) + jnp.sum(p_n, axis=-1, keepdims=True)
    o_lat = (_bdot(p_c, clat) + _bdot(p_n, nlat)) / denom
    for h in range(MLA_HEADS):
        o_ref[:, h * MLA_VDIM:(h + 1) * MLA_VDIM] = _bdot(
            o_lat[h * lq:(h + 1) * lq, :], wuv_ref[h]).astype(o_ref.dtype)


def _attn_cached(qn, qr, cache_lat, cache_kr_pad, new_lat, new_kr_pad, wuk_h, wuv_h, nb, lq):
    past = cache_lat.shape[1]
    whole = lambda b: (0, 0, 0)
    return pl.pallas_call(
        functools.partial(_attn_cached_kernel, lq=lq),
        grid=(nb,),
        in_specs=[
            pl.BlockSpec((MLA_HEADS, lq, LANES), lambda b: (0, b, 0)),
            pl.BlockSpec((MLA_HEADS, lq, LANES), lambda b: (0, b, 0)),
            pl.BlockSpec((None, past, KV_LORA), lambda b: (b, 0, 0)),
            pl.BlockSpec((None, past, LANES), lambda b: (b, 0, 0)),
            pl.BlockSpec((lq, KV_LORA), lambda b: (b, 0)),
            pl.BlockSpec((lq, LANES), lambda b: (b, 0)),
            pl.BlockSpec((MLA_HEADS, KV_LORA, MLA_NOPE), whole),
            pl.BlockSpec((MLA_HEADS, KV_LORA, MLA_VDIM), whole),
        ],
        out_specs=pl.BlockSpec((lq, MLA_HEADS * MLA_VDIM), lambda b: (b, 0)),
        out_shape=jax.ShapeDtypeStruct((nb * lq, MLA_HEADS * MLA_VDIM), BF16),
        scratch_shapes=[pltpu.VMEM((MLA_HEADS * lq, KV_LORA), BF16),
                        pltpu.VMEM((MLA_HEADS * lq, LANES), BF16)],
        compiler_params=_params(("parallel",)),
        name="attn_cached",
    )(qn, qr, cache_lat, cache_kr_pad, new_lat, new_kr_pad, wuk_h, wuv_h)


def _top_values(s, top_ref, want_rank, one_per_round=False):
    rank = jnp.full(s.shape, float(PEER_TOPK), F32) if want_rank else None
    rows = lax.broadcasted_iota(jnp.int32, s.shape, 0).astype(F32) if one_per_round else None
    for r in range(PEER_TOPK):
        m = jnp.max(s, axis=0, keepdims=True)
        top_ref[r:r + 1, :] = m
        hit = s == m
        if one_per_round:
            first = jnp.min(jnp.where(hit, rows, float(s.shape[0])), axis=0, keepdims=True)
            hit = rows == first
        if want_rank:
            rank = jnp.where(hit, float(r), rank)
        s = jnp.where(hit, -jnp.inf, s)
    return rank, m


_CAND_LIMITS = tuple(PEER_TOPK // (b + 1) for b in range(1, 8))
_HALF_TOPK = PEER_TOPK // 2


def _peer_select_kernel(h_ref, wq_ref, keys_ref, r1_ref, e1_ref, c0_ref, e0_ref, s_ref, a0_ref, a1_ref,
                        n_ref, *, heads_per_step):
    hx = h_ref[...]
    for hh in range(heads_per_step):
        q_t = _bdot_nt(wq_ref[hh * 2 * N_KEYS:(hh + 1) * 2 * N_KEYS, :], hx)
        s_ref[2 * hh] = _bdot(keys_ref[hh, 0], q_t[:N_KEYS])
        s_ref[2 * hh + 1] = _bdot(keys_ref[hh, 1], q_t[N_KEYS:])
    tied = [_peer_select_head(s_ref[2 * hh], s_ref[2 * hh + 1], r1_ref.at[hh], e1_ref.at[hh],
                              c0_ref.at[hh], e0_ref.at[hh], a0_ref.at[hh], a1_ref.at[hh])
            for hh in range(heads_per_step)]

    def redo(hh):
        _peer_select_head_ties(s_ref[2 * hh], s_ref[2 * hh + 1], r1_ref.at[hh], e1_ref.at[hh],
                               c0_ref.at[hh], e0_ref.at[hh], a0_ref.at[hh], a1_ref.at[hh], n_ref)

    for hh in range(heads_per_step):
        pl.when(jnp.max(tied[hh].astype(F32)) > 0.0)(functools.partial(redo, hh))


def _peer_select_head(s0, s1, r1_ref, e1_ref, c0_ref, e0_ref, a0_ref, a1_ref):
    _, last0 = _top_values(s0, a0_ref, False)
    rank1, _ = _top_values(s1, a1_ref, True)
    removed0 = jnp.sum((s0 >= last0).astype(F32), axis=0, keepdims=True)
    removed1 = jnp.sum((rank1 < float(PEER_TOPK)).astype(F32), axis=0, keepdims=True)
    a0 = a0_ref[...]
    a1 = a1_ref[...]
    a0_lo = a0[:_HALF_TOPK, :]
    row = lax.broadcasted_iota(jnp.int32, a0_lo.shape, 0)
    pieces = [a1[0:1, :] + a0]
    for b, limit in enumerate(_CAND_LIMITS, start=1):
        pieces.append(jnp.where(row < limit, a1[b:b + 1, :] + a0_lo, -jnp.inf))
    pieces.append(a1[_HALF_TOPK:, :] + a0[0:1, :])
    cand = jnp.concatenate(pieces, axis=0)
    c = cand
    for r in range(PEER_TOPK):
        tau = jnp.max(c, axis=0, keepdims=True)
        if r + 1 < PEER_TOPK:
            c = jnp.where(c == tau, -jnp.inf, c)
    sel = cand >= tau
    cmax = a0[0:1, :] + a1[0:1, :]
    z = jnp.sum(jnp.where(sel, jnp.exp(cand - cmax), 0.0), axis=0, keepdims=True)
    self32 = sel.astype(F32)
    n_hi = self32[_HALF_TOPK:PEER_TOPK, :]
    n_lo = self32[:_HALF_TOPK, :]
    for b in range(1, 8):
        lo = PEER_TOPK + (b - 1) * _HALF_TOPK
        n_lo = n_lo + self32[lo:lo + _HALF_TOPK, :]
    tail = jnp.sum(self32[PEER_TOPK + 7 * _HALF_TOPK:, :], axis=0, keepdims=True)
    n_lo = n_lo + jnp.where(row == 0, tail, 0.0)
    count0 = jnp.zeros(s0.shape, F32)
    for a in range(PEER_TOPK):
        n_a = n_lo[a:a + 1, :] if a < _HALF_TOPK else n_hi[a - _HALF_TOPK:a - _HALF_TOPK + 1, :]
        count0 = jnp.where(s0 == a0[a:a + 1, :], n_a, count0)
    r1_ref[...] = rank1.astype(r1_ref.dtype)
    c0_ref[...] = count0
    e0_ref[...] = jnp.exp(s0 - a0[0:1, :]) / z
    e1_ref[...] = jnp.exp(s1 - a1[0:1, :]).astype(e1_ref.dtype)

    n_sel = jnp.sum(self32, axis=0, keepdims=True)
    k = float(PEER_TOPK)
    return (removed0 != k) | (removed1 != k) | (n_sel != k)


def _peer_select_head_ties(s0, s1, r1_ref, e1_ref, c0_ref, e0_ref, a0_ref, a1_ref, n_ref):
    rank0, _ = _top_values(s0, a0_ref, True, one_per_round=True)
    rank1, _ = _top_values(s1, a1_ref, True, one_per_round=True)
    a0 = a0_ref[...]
    a1 = a1_ref[...]
    cand = jnp.concatenate([a0[a:a + 1, :] + a1 for a in range(PEER_TOPK)], axis=0)
    rows = lax.broadcasted_iota(jnp.int32, cand.shape, 0).astype(F32)
    c = cand
    sel = jnp.zeros(cand.shape, F32)
    for _ in range(PEER_TOPK):
        m = jnp.max(c, axis=0, keepdims=True)
        first = jnp.min(jnp.where(c == m, rows, float(cand.shape[0])), axis=0, keepdims=True)
        hit = rows == first
        sel = jnp.where(hit, 1.0, sel)
        c = jnp.where(hit, -jnp.inf, c)
    cmax = a0[0:1, :] + a1[0:1, :]
    z = jnp.sum(sel * jnp.exp(cand - cmax), axis=0, keepdims=True)
    for a in range(PEER_TOPK):
        n_ref[a:a + 1, :] = jnp.sum(sel[a * PEER_TOPK:(a + 1) * PEER_TOPK, :], axis=0, keepdims=True)
    n = n_ref[...]
    count0 = jnp.zeros(s0.shape, F32)
    for a in range(PEER_TOPK):
        count0 = jnp.where(rank0 == float(a), n[a:a + 1, :], count0)
    r1_ref[...] = rank1.astype(r1_ref.dtype)
    c0_ref[...] = count0
    e0_ref[...] = jnp.exp(s0 - a0[0:1, :]) / z
    e1_ref[...] = jnp.exp(s1 - a1[0:1, :]).astype(e1_ref.dtype)


def _peer_select(hx, wq_t, keys, layer, tt):
    t, d = hx.shape
    hps = SELECT_HEADS_PER_STEP
    o_spec = pl.BlockSpec((hps, N_KEYS, tt), lambda i, g: (g, 0, i))
    return pl.pallas_call(
        functools.partial(_peer_select_kernel, heads_per_step=hps),
        grid=(t // tt, PEER_HEADS // hps),
        in_specs=[pl.BlockSpec((tt, d), lambda i, g: (i, 0)),
                  pl.BlockSpec((hps * 2 * N_KEYS, d), lambda i, g: (g, 0)),
                  pl.BlockSpec((None, hps, 2, N_KEYS, N_KEYS), lambda i, g: (layer, g, 0, 0, 0))],
        out_specs=[o_spec] * 4,
        out_shape=[jax.ShapeDtypeStruct((PEER_HEADS, N_KEYS, t), dt) for dt in (BF16, BF16, F32, F32)],
        scratch_shapes=[pltpu.VMEM((2 * hps, N_KEYS, tt), F32),
                        pltpu.VMEM((hps, PEER_TOPK, tt), F32), pltpu.VMEM((hps, PEER_TOPK, tt), F32),
                        pltpu.VMEM((PEER_TOPK, tt), F32)],
        compiler_params=_params(("parallel", "arbitrary")),
        name="peer_select",
    )(hx, wq_t, keys)


def _sublane_bcast_bf16(row, rows):
    tile_rows = 16
    packed = jnp.broadcast_to(row, (tile_rows, row.shape[1])).astype(BF16)
    return jnp.tile(packed, (rows // tile_rows, 1))


def _peer_mix_kernel(x_ref, u_ref, vt_ref, r1_ref, e1_ref, c0_ref, e0_ref, res_ref, gate_ref, o_ref,
                     acc_ref, a_ref, g_ref, p_ref, *, n_i):
    j = pl.program_id(1)

    @pl.when(j == 0)
    def _():
        acc_ref[...] = jnp.zeros_like(acc_ref)

    x = x_ref[...]
    rows_per_chunk = 2 * N_KEYS
    for c in range(n_i // 2):
        lo = c * rows_per_chunk
        a_ref[c] = _bdot_nt(u_ref[lo:lo + rows_per_chunk, :], x)
    zero = jnp.zeros((), BF16)
    for ii in range(n_i):
        w = None
        for h in range(PEER_HEADS):
            count = _sublane_bcast_bf16(c0_ref[h, ii:ii + 1, :], N_KEYS)
            e0 = _sublane_bcast_bf16(e0_ref[h, ii:ii + 1, :], N_KEYS)
            term = jnp.where(r1_ref[h] < count, e1_ref[h] * e0, zero)
            w = term if w is None else w + term
        g_ref[ii] = w
    for c in range(n_i // 2):
        lo = c * rows_per_chunk
        gates = jnp.concatenate([g_ref[2 * c], g_ref[2 * c + 1]], axis=0)
        p_ref[lo:lo + rows_per_chunk, :] = gates * _gelu(a_ref[c]).astype(BF16)
    acc_ref[...] += jnp.dot(vt_ref[...], p_ref[...], preferred_element_type=F32)

    @pl.when(j == pl.num_programs(1) - 1)
    def _():
        o_ref[...] = res_ref[...] + (1.0 + gate_ref[...]) * acc_ref[...].T


def _peer_mix(hx, u_b, vt_b, layer, r1, e1, c0, e0, x_res, gate, l, tt, et):
    t, d = hx.shape
    e = u_b.shape[1]
    n_i = et // N_KEYS
    tok_all = pl.BlockSpec((PEER_HEADS, N_KEYS, tt), lambda i, j: (0, 0, i))
    tok_i = pl.BlockSpec((PEER_HEADS, n_i, tt), lambda i, j: (0, j, i))
    gate_arr, gate_spec = _gate_operand(gate, l, tt, d)
    return pl.pallas_call(
        functools.partial(_peer_mix_kernel, n_i=n_i),
        grid=(t // tt, e // et),
        in_specs=[pl.BlockSpec((tt, d), lambda i, j: (i, 0)),
                  pl.BlockSpec((None, et, d), lambda i, j: (layer, j, 0)),
                  pl.BlockSpec((None, d, et), lambda i, j: (layer, 0, j)),
                  tok_all, tok_all, tok_i, tok_i,
                  pl.BlockSpec((tt, d), lambda i, j: (i, 0)), gate_spec],
        out_specs=pl.BlockSpec((tt, d), lambda i, j: (i, 0)),
        out_shape=jax.ShapeDtypeStruct((t, d), F32),
        scratch_shapes=[pltpu.VMEM((d, tt), F32),
                        pltpu.VMEM((n_i // 2, 2 * N_KEYS, tt), F32),
                        pltpu.VMEM((n_i, N_KEYS, tt), BF16),
                        pltpu.VMEM((et, tt), BF16)],
        compiler_params=_params(("parallel", "arbitrary"), vmem=PEER_MIX_VMEM_LIMIT),
        name="peer_mix",
    )(hx, u_b, vt_b, r1, e1, c0, e0, x_res, gate_arr)


def _peer(hx, wq_t, keys, layer, u_b, vt_b, x_res, gate, l):
    t = hx.shape[0]
    tt = min(t, 512)
    r1, e1, c0, e0 = _peer_select(hx, wq_t, keys, layer, tt)
    return _peer_mix(hx, u_b, vt_b, layer, r1, e1, c0, e0, x_res, gate, l, tt, 1024)


def _rope_tables(pos, half, reps):
    inv = ROPE_THETA ** (-jnp.arange(half, dtype=F32) / half)
    ang = pos.astype(F32)[:, None] * inv[None, :]
    return jnp.tile(jnp.cos(ang), (1, reps)), jnp.tile(jnp.sin(ang), (1, reps))


def _rotate_half_cols(w):
    half = w.shape[-1] // 2
    return jnp.concatenate([-w[..., half:], w[..., :half]], axis=-1)


def _pad_last(w, width):
    return jnp.pad(w, [(0, 0)] * (w.ndim - 1) + [(0, width - w.shape[-1])])


def _shared_weights(p):
    out = {}
    w_rope = p["mla_w_dkv"][:, KV_LORA:]
    out["w_dkv_ext"] = jnp.concatenate(
        [p["mla_w_dkv"][:, :KV_LORA], _pad_last(w_rope, LANES), _pad_last(_rotate_half_cols(w_rope), LANES)],
        axis=1)
    w_uq = p["mla_w_uq"][0].reshape(-1, MLA_HEADS, MLA_NOPE + MLA_ROPE)
    q_lora = w_uq.shape[0]
    out["w_uq_nope"] = w_uq[:, :, :MLA_NOPE].reshape(q_lora, MLA_HEADS * MLA_NOPE)
    w_qr = w_uq[:, :, MLA_NOPE:]
    out["w_uq_rope"] = _pad_last(w_qr, LANES).reshape(q_lora, MLA_HEADS * LANES)
    out["w_uq_rot"] = _pad_last(_rotate_half_cols(w_qr), LANES).reshape(q_lora, MLA_HEADS * LANES)
    out["w_ukv_flat"] = jnp.concatenate([p["mla_w_uk"].reshape(KV_LORA, MLA_HEADS * MLA_NOPE),
                                         p["mla_w_uv"].reshape(KV_LORA, MLA_HEADS * MLA_VDIM)], axis=1)
    out["w_uk_h"] = jnp.transpose(p["mla_w_uk"], (1, 0, 2))
    out["w_uv_h"] = jnp.transpose(p["mla_w_uv"], (1, 0, 2))
    out["peer_wq_t"] = [p["peer_w_q"][l].T for l in range(2)]
    out["peer_u_b"] = p["peer_u"].astype(BF16)
    out["peer_vt_b"] = jnp.swapaxes(p["peer_v"].astype(BF16), 1, 2)
    return out


def _adaln(c_all, w, b, lead, n_out):
    m, d = c_all.shape
    tn = 2048
    if w.ndim == 2:
        w = w[None]
        b = b[None]
    b3 = b.reshape(b.shape[0], 1, n_out)
    return _linear(
        c_all, [w], n_cols=n_out, tm=m, tn=tn, epilogue=_ep_bias, prologue=_silu, w_lead=lead,
        extras=[b3], extra_specs=[pl.BlockSpec((None, 1, tn), lambda i, j: (lead, 0, j))],
        out_shapes=[jax.ShapeDtypeStruct((m, n_out), F32)],
        out_specs=[pl.BlockSpec((m, tn), lambda i, j: (0, j))])[0]


def _trunk(x, pos, mods, kv_mod, p, sw, ret_state, cache_lat, cache_kr):
    nb, l, d = x.shape
    t = nb * l
    pos_rows = jnp.tile(pos, nb) if l < 256 else pos
    table_rows = pos_rows.shape[0]
    assert table_rows % _row_tile(t) == 0, "a row tile must cover whole sequences or divide one"

    def vec(a):
        return a.reshape(nb, 1, d)

    def flat(a):
        return a.reshape(t, a.shape[-1])

    sh1, sc1, gt1, sh2, sc2, gt2 = [vec(m) for m in mods[0]]
    (h,) = _norm_mod(x, mods=[(p["norm_g"][0, 0][None], sh1, sc1)])
    h = flat(h)
    tm = _row_tile(t)
    cos_r, sin_r = _rope_tables(pos_rows, RET_DK // 2, 1)
    tspec = _table_spec(tm, table_rows)
    w_in = p["ret_w_in"]
    qk_cols = RET_HEADS * RET_DK
    v_cols = RET_HEADS * RET_DV
    tn = _col_tile(d, qk_cols)

    def rope_proj(col0, scale):
        return _linear(
            h, [w_in], n_cols=qk_cols, tm=tm, tn=tn, w_lead=0, col_block0=col0 // tn,
            epilogue=functools.partial(_ep_rope_half128, scale=scale),
            extras=[cos_r, sin_r], extra_specs=[tspec, tspec],
            out_shapes=[jax.ShapeDtypeStruct((t, qk_cols), BF16)],
            out_specs=[pl.BlockSpec((tm, tn), lambda i, j: (i, j))])[0]

    def plain_proj(x_in, w, n_cols, col0=0, lead=None, dtype=BF16):
        tm_ = _row_tile(x_in.shape[0])
        tn_ = _col_tile(x_in.shape[1], n_cols)
        return _linear(
            x_in, [w], n_cols=n_cols, tm=tm_, tn=tn_, w_lead=lead, col_block0=col0 // tn_,
            epilogue=_ep_plain,
            out_shapes=[jax.ShapeDtypeStruct((x_in.shape[0], n_cols), dtype)],
            out_specs=[pl.BlockSpec((tm_, tn_), lambda i, j: (i, j))])[0]

    def resid_proj(x_in, w, x_res, gate):
        tm_ = _row_tile(x_in.shape[0])
        tn_ = _col_tile(x_in.shape[1], d)
        gate_arr, gate_spec = _gate_operand(gate, l, tm_, tn_)
        tile = pl.BlockSpec((tm_, tn_), lambda i, j: (i, j))
        return _linear(
            x_in, [w], n_cols=d, tm=tm_, tn=tn_, w_lead=0, epilogue=_ep_resid,
            extras=[x_res, gate_arr], extra_specs=[tile, gate_spec],
            out_shapes=[jax.ShapeDtypeStruct((t, d), F32)], out_specs=[tile])[0]

    def heads_proj(x_in, ws, n_cols, epilogue=_ep_heads, extras=(), extra_specs=()):
        tm_ = _row_tile(x_in.shape[0])
        tn_ = _col_tile(x_in.shape[1], n_cols, len(ws))
        return _linear(
            x_in, ws, n_cols=n_cols, tm=tm_, tn=tn_, epilogue=epilogue, extras=extras, extra_specs=extra_specs,
            out_shapes=[jax.ShapeDtypeStruct((n_cols // LANES, x_in.shape[0], LANES), BF16)],
            out_specs=[pl.BlockSpec((tn_ // LANES, tm_, LANES), lambda i, j: (j, i, 0))])[0]

    q_r = rope_proj(0, 1.0)
    k_r = rope_proj(qk_cols, RET_DK ** -0.5)
    v_r = plain_proj(h, w_in, v_cols, col0=2 * qk_cols, lead=0)
    g_r = plain_proj(h, w_in, v_cols, col0=2 * qk_cols + v_cols, lead=0)
    s0 = None if ret_state is None else ret_state[0]
    y_r, s_new = _retention(q_r, k_r, v_r, g_r, p["ret_gn_g"], s0, nb, l)
    xf = resid_proj(y_r, p["ret_w_out"], flat(x), gt1)
    (h,) = _norm_mod(xf.reshape(nb, l, d), mods=[(p["norm_g"][0, 1][None], sh2, sc2)])
    xf = _peer(flat(h), sw["peer_wq_t"][0], p["peer_keys"], 0, sw["peer_u_b"], sw["peer_vt_b"],
               xf, gt2, l)

    sh1, sc1, gt1b, sh2, sc2, gt2b = [vec(m) for m in mods[1]]
    kv_sh, kv_sc = [vec(m) for m in kv_mod]
    h, h_kv = _norm_mod(xf.reshape(nb, l, d),
                         mods=[(p["norm_g"][1, 0][None], sh1, sc1), (p["kv_norm_g"][None], kv_sh, kv_sc)])
    h = flat(h)
    h_kv = flat(h_kv)
    cos_m, sin_m = _rope_tables(pos_rows, MLA_ROPE // 2, LANES // (MLA_ROPE // 2))
    n_ext = sw["w_dkv_ext"].shape[1]
    lat, kr, kr_pad = _linear(
        h_kv, [sw["w_dkv_ext"]], n_cols=n_ext, tm=tm, tn=n_ext, epilogue=_ep_kv,
        extras=[p["mla_kv_norm_g"][None], cos_m, sin_m],
        extra_specs=[pl.BlockSpec((1, KV_LORA), lambda i, j: (0, 0)), tspec, tspec],
        out_shapes=[jax.ShapeDtypeStruct((t, KV_LORA), F32), jax.ShapeDtypeStruct((t, MLA_ROPE), F32),
                    jax.ShapeDtypeStruct((t, LANES), BF16)],
        out_specs=[pl.BlockSpec((tm, KV_LORA), lambda i, j: (i, 0)),
                   pl.BlockSpec((tm, MLA_ROPE), lambda i, j: (i, 0)),
                   pl.BlockSpec((tm, LANES), lambda i, j: (i, 0))])
    q_lora = p["mla_w_dq"].shape[-1]
    cq = _linear(
        h, [p["mla_w_dq"]], n_cols=q_lora, tm=tm, tn=q_lora, w_lead=0, epilogue=_ep_rms,
        extras=[p["mla_q_norm_g"]], extra_specs=[pl.BlockSpec((1, q_lora), lambda i, j: (0, 0))],
        out_shapes=[jax.ShapeDtypeStruct((t, q_lora), BF16)],
        out_specs=[pl.BlockSpec((tm, q_lora), lambda i, j: (i, 0))])[0]
    hl = MLA_HEADS * LANES
    tn_q = _col_tile(q_lora, hl, 3)
    q_heads = jax.ShapeDtypeStruct((MLA_HEADS, t, LANES), BF16)
    q_spec = pl.BlockSpec((tn_q // LANES, tm, LANES), lambda i, j: (j, i, 0))
    qn, qr = _linear(
        cq, [sw["w_uq_nope"], sw["w_uq_rope"], sw["w_uq_rot"]], n_cols=hl, tm=tm, tn=tn_q,
        epilogue=_ep_query_heads, extras=[cos_m, sin_m], extra_specs=[tspec, tspec],
        out_shapes=[q_heads, q_heads], out_specs=[q_spec, q_spec])
    if cache_lat is None:
        kv_h = heads_proj(lat, [sw["w_ukv_flat"]], 2 * hl)
        o = _attn_prompt(qn, qr, kv_h, kr_pad, nb, l)
    else:
        ckr = _pad_last(cache_kr, LANES).astype(BF16)
        o = _attn_cached(qn, qr, cache_lat, ckr, lat, kr_pad, sw["w_uk_h"], sw["w_uv_h"], nb, l)
    xf = resid_proj(o, p["mla_w_o"], xf, gt1b)
    (h,) = _norm_mod(xf.reshape(nb, l, d), mods=[(p["norm_g"][1, 1][None], sh2, sc2)])
    xf = _peer(flat(h), sw["peer_wq_t"][1], p["peer_keys"], 1, sw["peer_u_b"], sw["peer_vt_b"],
               xf, gt2b, l)
    (y,) = _norm_mod(xf.reshape(nb, l, d), final_g=p["final_g"][None])
    return y, s_new[None], lat.reshape(nb, l, KV_LORA), kr.reshape(nb, l, MLA_ROPE)


def kernel(x_prompt, x_sample, c_prompt, c_sample, state_retention, cache_mla_latent, cache_mla_krope,
           ada_w, ada_b, norm_g, ret_w_in, ret_gn_g, ret_w_out,
           kv_ada_w, kv_ada_b, kv_norm_g, mla_w_dkv, mla_kv_norm_g, mla_w_uk, mla_w_uv,
           mla_w_dq, mla_q_norm_g, mla_w_uq, mla_w_o,
           peer_w_q, peer_keys, peer_u, peer_v, final_g):
    p = dict(norm_g=norm_g, ret_w_in=ret_w_in, ret_gn_g=ret_gn_g[0][None], ret_w_out=ret_w_out,
             kv_norm_g=kv_norm_g, mla_w_dkv=mla_w_dkv, mla_kv_norm_g=mla_kv_norm_g,
             mla_w_uk=mla_w_uk, mla_w_uv=mla_w_uv, mla_w_dq=mla_w_dq, mla_q_norm_g=mla_q_norm_g,
             mla_w_uq=mla_w_uq, mla_w_o=mla_w_o, peer_w_q=peer_w_q, peer_keys=peer_keys,
             peer_u=peer_u, peer_v=peer_v, final_g=final_g)
    sw = _shared_weights(p)
    nbp, lp, d = x_prompt.shape
    nbs, ls, _ = x_sample.shape
    past = cache_mla_latent.shape[1]

    n_c = nbp + nbs
    c_all = jnp.pad(jnp.concatenate([c_prompt, c_sample], axis=0), ((0, (-n_c) % 8), (0, 0)))
    layer_mods = [_adaln(c_all, ada_w, ada_b, layer, 6 * d) for layer in range(2)]
    kv_mods = _adaln(c_all, kv_ada_w, kv_ada_b, 0, 2 * d)

    def stream_mods(lo, hi):
        return ([jnp.split(m[lo:hi], 6, axis=-1) for m in layer_mods], jnp.split(kv_mods[lo:hi], 2, axis=-1))

    mods_p, kv_p = stream_mods(0, nbp)
    mods_s, kv_s = stream_mods(nbp, n_c)
    pos_p = jnp.arange(lp, dtype=jnp.int32)
    pos_s = past + jnp.arange(ls, dtype=jnp.int32)
    y_p, ret_p, lat_p, kr_p = _trunk(x_prompt, pos_p, mods_p, kv_p, p, sw, None, None, None)
    y_s, ret_s, lat_s, kr_s = _trunk(x_sample, pos_s, mods_s, kv_s, p, sw, state_retention,
                                     cache_mla_latent, cache_mla_krope)
    return (y_p, y_s, ret_p, ret_s, lat_p, kr_p, lat_s, kr_s)
```

```python
import functools
import math

import jax
import jax.numpy as jnp
from jax import lax
from jax.experimental import pallas as pl
from jax.experimental.pallas import tpu as pltpu

F32 = jnp.float32
BF16 = jnp.bfloat16

EPS = 1e-6
ROPE_THETA = 10000.0
NEG_INF = -1e30
CHUNK = 64

RET_HEADS = 8
RET_DK = 256
RET_DV = 512

MLA_HEADS = 16
MLA_NOPE = 128
MLA_ROPE = 64
MLA_VDIM = 128
KV_LORA = 512
MLA_SCALE = (MLA_NOPE + MLA_ROPE) ** -0.5

PEER_HEADS = 8
N_KEYS = 128
PEER_TOPK = 16

LANES = 128
VMEM_LIMIT = 52 * 1024 * 1024
PEER_MIX_VMEM_LIMIT = 58 * 1024 * 1024
SELECT_HEADS_PER_STEP = 4


def _params(sem, vmem=VMEM_LIMIT):
    return pltpu.CompilerParams(dimension_semantics=sem, vmem_limit_bytes=vmem)


def _bdot(a, b):
    return jnp.dot(a.astype(BF16), b.astype(BF16), preferred_element_type=F32)


def _bdot_nt(a, b):
    return lax.dot_general(a.astype(BF16), b.astype(BF16), (((1,), (1,)), ((), ())),
                           preferred_element_type=F32)


def _bdot_tn(a, b):
    return lax.dot_general(a.astype(BF16), b.astype(BF16), (((0,), (0,)), ((), ())),
                           preferred_element_type=F32)


def _silu(x):
    return x * (1.0 / (1.0 + jnp.exp(-x)))


def _gelu(x):
    return 0.5 * x * (1.0 + lax.erf(x * (0.5 ** 0.5)))


def _linear_kernel(*refs, n_w, n_extra, prologue, epilogue):
    x_ref = refs[0]
    w_refs = refs[1:1 + n_w]
    extra_refs = refs[1 + n_w:1 + n_w + n_extra]
    out_refs = refs[1 + n_w + n_extra:]
    xv = x_ref[...]
    if prologue is not None:
        xv = prologue(xv)
    xb = xv.astype(BF16)
    accs = [jnp.dot(xb, w[...].astype(BF16), preferred_element_type=F32) for w in w_refs]
    epilogue(accs, extra_refs, out_refs)


def _linear(x, ws, *, n_cols, tm, tn, epilogue, out_shapes, out_specs, w_lead=None, col_block0=0,
            extras=(), extra_specs=(), prologue=None, name=None):
    m, k = x.shape
    if name is None:
        name = "linear" + getattr(epilogue, "func", epilogue).__name__
    assert m % tm == 0 and n_cols % tn == 0
    if w_lead is None:
        w_spec = pl.BlockSpec((k, tn), lambda i, j: (0, j + col_block0))
    else:
        w_spec = pl.BlockSpec((None, k, tn), lambda i, j: (w_lead, 0, j + col_block0))
    kern = functools.partial(_linear_kernel, n_w=len(ws), n_extra=len(extras), prologue=prologue,
                             epilogue=epilogue)
    return pl.pallas_call(
        kern,
        grid=(m // tm, n_cols // tn),
        in_specs=[pl.BlockSpec((tm, k), lambda i, j: (i, 0))] + [w_spec] * len(ws) + list(extra_specs),
        out_specs=out_specs,
        out_shape=out_shapes,
        compiler_params=_params(("parallel", "arbitrary")),
        name=name,
    )(x, *ws, *extras)


def _ep_plain(accs, extras, outs):
    outs[0][...] = accs[0].astype(outs[0].dtype)


def _ep_heads(accs, extras, outs):
    acc = accs[0]
    for jj in range(acc.shape[1] // LANES):
        outs[0][jj] = acc[:, jj * LANES:(jj + 1) * LANES].astype(outs[0].dtype)


def _ep_bias(accs, extras, outs):
    outs[0][...] = accs[0] + extras[0][...]


def _ep_resid(accs, extras, outs):
    outs[0][...] = extras[0][...] + (1.0 + extras[1][...]) * accs[0]


def _gate_operand(gate, l, rows_per_tile, cols_per_tile):
    nb, _, d = gate.shape
    col = (lambda j: j) if cols_per_tile < d else (lambda j: 0)
    if l % rows_per_tile == 0:
        tiles_per_batch = l // rows_per_tile
        return gate, pl.BlockSpec((None, 1, cols_per_tile), lambda i, j: (i // tiles_per_batch, 0, col(j)))
    rows = jnp.broadcast_to(gate, (nb, l, d)).reshape(nb * l, d)
    return rows, pl.BlockSpec((rows_per_tile, cols_per_tile), lambda i, j: (i, col(j)))


def _ep_rope_half128(accs, extras, outs, *, scale):
    acc = accs[0]
    cos = extras[0][...]
    sin = extras[1][...]
    for g in range(acc.shape[1] // 256):
        x1 = acc[:, g * 256:g * 256 + 128]
        x2 = acc[:, g * 256 + 128:(g + 1) * 256]
        outs[0][:, g * 256:g * 256 + 128] = ((x1 * cos - x2 * sin) * scale).astype(outs[0].dtype)
        outs[0][:, g * 256 + 128:(g + 1) * 256] = ((x1 * sin + x2 * cos) * scale).astype(outs[0].dtype)


def _ep_query_heads(accs, extras, outs):
    cos = extras[0][...]
    sin = extras[1][...]
    plain, a, b = accs
    for jj in range(a.shape[1] // LANES):
        sl = slice(jj * LANES, (jj + 1) * LANES)
        outs[0][jj] = plain[:, sl].astype(outs[0].dtype)
        outs[1][jj] = (a[:, sl] * cos + b[:, sl] * sin).astype(outs[1].dtype)


def _ep_rms(accs, extras, outs):
    acc = accs[0]
    g = extras[0][...]
    y = acc * lax.rsqrt(jnp.mean(acc * acc, axis=-1, keepdims=True) + EPS) * g
    outs[0][...] = y.astype(outs[0].dtype)


def _ep_kv(accs, extras, outs):
    acc = accs[0]
    g = extras[0][...]
    cos = extras[1][...]
    sin = extras[2][...]
    c = acc[:, :KV_LORA]
    outs[0][...] = c * lax.rsqrt(jnp.mean(c * c, axis=-1, keepdims=True) + EPS) * g
    kr = acc[:, KV_LORA:KV_LORA + LANES] * cos + acc[:, KV_LORA + LANES:KV_LORA + 2 * LANES] * sin
    outs[1][...] = kr[:, :MLA_ROPE]
    outs[2][...] = kr.astype(BF16)


def _row_tile(m):
    return min(m, 1024)


def _col_tile(k, n, n_weights=1):
    for tn in (1024, 512):
        if n % tn == 0 and 2 * n_weights * k * tn * 4 <= 16 * 1024 * 1024:
            return tn
    return min(n, 512)


def _table_spec(tm, table_rows):
    nblk = table_rows // tm
    return pl.BlockSpec((tm, LANES), lambda i, j: (i % nblk, 0))


def _norm_mod_kernel(*refs, n_mod):
    x = refs[0][...]
    xn = x * lax.rsqrt(jnp.mean(x * x, axis=-1, keepdims=True) + EPS)
    if n_mod == 0:
        refs[2][...] = xn * refs[1][...]
        return
    outs = refs[1 + 3 * n_mod:]
    for m in range(n_mod):
        g, sh, sc = (refs[1 + 3 * m + k][...] for k in range(3))
        outs[m][...] = ((xn * g) * (1.0 + sc) + sh).astype(outs[m].dtype)


def _norm_mod(x, mods=(), final_g=None):
    nb, l, d = x.shape
    tl = min(l, 512)
    tok = pl.BlockSpec((None, tl, d), lambda b, i: (b, i, 0))
    per_b = pl.BlockSpec((None, 1, d), lambda b, i: (b, 0, 0))
    gain = pl.BlockSpec((1, d), lambda b, i: (0, 0))
    args, specs = [x], [tok]
    if final_g is not None:
        assert not mods
        args.append(final_g)
        specs.append(gain)
        out_shapes = [jax.ShapeDtypeStruct((nb, l, d), F32)]
    else:
        for g, sh, sc in mods:
            args += [g, sh, sc]
            specs += [gain, per_b, per_b]
        out_shapes = [jax.ShapeDtypeStruct((nb, l, d), BF16) for _ in mods]
    return pl.pallas_call(
        functools.partial(_norm_mod_kernel, n_mod=len(mods)),
        grid=(nb, l // tl), in_specs=specs, out_specs=[tok] * len(out_shapes), out_shape=out_shapes,
        compiler_params=_params(("parallel", "parallel")),
        name="norm_mod",
    )(*args)


def _ret_log_decay():
    return jnp.log1p(-jnp.exp2(-5.0 - jnp.arange(RET_HEADS, dtype=F32)))


def _retention_tables(c):
    log_g = _ret_log_decay()[:, None, None]
    n = jnp.arange(c, dtype=F32)
    dist = n[:, None] - n[None, :]
    same = (jnp.arange(c)[:, None] // CHUNK) == (jnp.arange(c)[None, :] // CHUNK)
    earlier = (jnp.arange(c)[None, :] // CHUNK) < (jnp.arange(c)[:, None] // CHUNK)
    mask = jnp.where(same[None], jnp.exp(jnp.abs(dist)[None] * log_g),
                     jnp.where(earlier[None], jnp.exp(dist[None] * log_g), 0.0))
    q_decay = jnp.exp((n[None, :, None] + 1.0) * log_g)
    k_decay = jnp.exp((c - 1.0 - n)[None, :, None] * log_g)
    blk_decay = jnp.exp(c * log_g)
    return mask, q_decay, k_decay, blk_decay


def _retention_kernel(*refs, has_s0, hps):
    if has_s0:
        (q_ref, k_ref, v_ref, g_ref, mask_ref, qd_ref, kd_ref, bd_ref, gn_ref, s0_ref,
         y_ref, s_out_ref, s_ref) = refs
    else:
        (q_ref, k_ref, v_ref, g_ref, mask_ref, qd_ref, kd_ref, bd_ref, gn_ref,
         y_ref, s_out_ref, s_ref) = refs
    c = pl.program_id(2)

    @pl.when(c == 0)
    def _():
        if has_s0:
            s_ref[...] = s0_ref[...]
        else:
            s_ref[...] = jnp.zeros_like(s_ref)

    for hh in range(hps):
        qk_cols = slice(hh * RET_DK, (hh + 1) * RET_DK)
        v_cols = slice(hh * RET_DV, (hh + 1) * RET_DV)
        q = q_ref[:, qk_cols]
        k = k_ref[:, qk_cols]
        v = v_ref[:, v_cols]
        s_prev = s_ref[hh]
        scores = _bdot_nt(q, k) * mask_ref[hh]
        y = _bdot(scores, v) + _bdot(q, s_prev) * qd_ref[hh]
        k_scaled = k.astype(F32) * kd_ref[hh]
        s_ref[hh] = bd_ref[hh] * s_prev + _bdot_tn(k_scaled, v)

        mu = jnp.mean(y, axis=-1, keepdims=True)
        yc = y - mu
        var = jnp.mean(yc * yc, axis=-1, keepdims=True)
        yn = yc * lax.rsqrt(var + EPS) * gn_ref[:, v_cols]
        y_ref[:, v_cols] = (_silu(g_ref[:, v_cols].astype(F32)) * yn).astype(y_ref.dtype)

    @pl.when(c == pl.num_programs(2) - 1)
    def _():
        s_out_ref[...] = s_ref[...]


def _retention(q, k, v, g, gn_g, s0, nb, l):
    cb = min(l, 256)
    nc = l // cb
    hps = 4 if nc > 1 else RET_HEADS
    mask, qd, kd, bd = _retention_tables(cb)
    row = lambda b, hg, c: (b * nc + c, hg)
    per_h3 = lambda b, hg, c: (hg, 0, 0)
    in_specs = [
        pl.BlockSpec((cb, hps * RET_DK), row), pl.BlockSpec((cb, hps * RET_DK), row),
        pl.BlockSpec((cb, hps * RET_DV), row), pl.BlockSpec((cb, hps * RET_DV), row),
        pl.BlockSpec((hps, cb, cb), per_h3), pl.BlockSpec((hps, cb, 1), per_h3),
        pl.BlockSpec((hps, cb, 1), per_h3), pl.BlockSpec((hps, 1, 1), per_h3),
        pl.BlockSpec((1, hps * RET_DV), lambda b, hg, c: (0, hg)),
    ]
    args = [q, k, v, g, mask, qd, kd, bd, gn_g]
    state_spec = pl.BlockSpec((None, hps, RET_DK, RET_DV), lambda b, hg, c: (b, hg, 0, 0))
    if s0 is not None:
        in_specs.append(state_spec)
        args.append(s0)
    y, s_new = pl.pallas_call(
        functools.partial(_retention_kernel, has_s0=s0 is not None, hps=hps),
        grid=(nb, RET_HEADS // hps, nc),
        in_specs=in_specs,
        out_specs=[pl.BlockSpec((cb, hps * RET_DV), row), state_spec],
        out_shape=[jax.ShapeDtypeStruct((nb * l, RET_HEADS * RET_DV), BF16),
                   jax.ShapeDtypeStruct((nb, RET_HEADS, RET_DK, RET_DV), F32)],
        scratch_shapes=[pltpu.VMEM((hps, RET_DK, RET_DV), F32)],
        compiler_params=_params(("parallel", "parallel", "arbitrary")),
        name="retention",
    )(*args)
    return y, s_new


def _attn_prompt_kernel(qi_ref, ki_ref, qn_ref, qr_ref, kn_ref, v_ref, kr_ref, o_ref, m_ref, acc_ref, *, tq, tk):
    qi = qi_ref[pl.program_id(1)]
    ki = ki_ref[pl.program_id(1)]
    exp2_scale = MLA_SCALE * math.log2(math.e)

    @pl.when(ki == 0)
    def _():
        m_ref[...] = jnp.full_like(m_ref, NEG_INF)
        acc_ref[...] = jnp.zeros_like(acc_ref)

    def block(masked):
        if masked:
            q_chunk = (qi * tq + lax.broadcasted_iota(jnp.int32, (tq, tk), 0)) // CHUNK
            k_chunk = (ki * tk + lax.broadcasted_iota(jnp.int32, (tq, tk), 1)) // CHUNK
            visible = k_chunk <= q_chunk
        kr = kr_ref[...]
        ones = jnp.ones((tk, LANES), BF16)

        def head(h, carry):
            q = jnp.concatenate([qn_ref[h], qr_ref[h]], axis=1)
            k = jnp.concatenate([kn_ref[h], kr], axis=1)
            s = _bdot_nt(q, k)
            if masked:
                s = jnp.where(visible, s, NEG_INF)
            m_prev = m_ref[h]
            m_new = jnp.maximum(m_prev, jnp.max(s, axis=-1, keepdims=True))
            alpha = jnp.exp2((m_prev - m_new) * exp2_scale)
            p = jnp.exp2((s - jnp.tile(m_new, (1, tk // LANES))) * exp2_scale)
            v_ext = jnp.concatenate([v_ref[h], ones], axis=1)
            acc_ref[h] = jnp.tile(alpha, (1, 2)) * acc_ref[h] + _bdot(p, v_ext)
            m_ref[h] = m_new
            return carry

        lax.fori_loop(0, MLA_HEADS, head, 0, unroll=True)

    @pl.when(ki < qi)
    def _():
        block(False)

    @pl.when(ki == qi)
    def _():
        block(True)
        for h in range(MLA_HEADS):
            acc = acc_ref[h]
            o_ref[:, h * MLA_VDIM:(h + 1) * MLA_VDIM] = (
                acc[:, :MLA_VDIM] / acc[:, MLA_VDIM:]).astype(o_ref.dtype)


def _attn_prompt(qn, qr, kv, kr, nb, l):
    tq = tk = min(l, 512)
    nq = l // tq
    pairs = [(qi, ki) for qi in range(nq) for ki in range(qi + 1)]
    qi_of = jnp.asarray([qk[0] for qk in pairs], jnp.int32)
    ki_of = jnp.asarray([qk[1] for qk in pairs], jnp.int32)
    q_spec = pl.BlockSpec((MLA_HEADS, tq, LANES), lambda b, s, qi, ki: (0, b * nq + qi[s], 0))
    k_spec = pl.BlockSpec((MLA_HEADS, tk, LANES), lambda b, s, qi, ki: (0, b * nq + ki[s], 0))
    v_spec = pl.BlockSpec((MLA_HEADS, tk, LANES), lambda b, s, qi, ki: (1, b * nq + ki[s], 0))
    return pl.pallas_call(
        functools.partial(_attn_prompt_kernel, tq=tq, tk=tk),
        grid_spec=pltpu.PrefetchScalarGridSpec(
            num_scalar_prefetch=2,
            grid=(nb, len(pairs)),
            in_specs=[q_spec, q_spec, k_spec, v_spec,
                      pl.BlockSpec((tk, LANES), lambda b, s, qi, ki: (b * nq + ki[s], 0))],
            out_specs=pl.BlockSpec((tq, MLA_HEADS * MLA_VDIM), lambda b, s, qi, ki: (b * nq + qi[s], 0)),
            scratch_shapes=[pltpu.VMEM((MLA_HEADS, tq, LANES), F32),
                            pltpu.VMEM((MLA_HEADS, tq, MLA_VDIM + LANES), F32)]),
        out_shape=jax.ShapeDtypeStruct((nb * l, MLA_HEADS * MLA_VDIM), BF16),
        compiler_params=_params(("parallel", "arbitrary")),
        name="attn_prompt",
    )(qi_of, ki_of, qn, qr, kv, kv, kr)


def _attn_cached_kernel(qn_ref, qr_ref, clat_ref, ckr_ref, nlat_ref, nkr_ref, wuk_ref, wuv_ref,
                        o_ref, ql_ref, qrs_ref, *, lq):
    for h in range(MLA_HEADS):
        ql_ref[h * lq:(h + 1) * lq, :] = _bdot_nt(qn_ref[h], wuk_ref[h]).astype(BF16)
        qrs_ref[h * lq:(h + 1) * lq, :] = qr_ref[h]
    ql = ql_ref[...]
    qr = qrs_ref[...]
    clat = clat_ref[...].astype(BF16)
    nlat = nlat_ref[...].astype(BF16)
    s_c = (_bdot_nt(ql, clat) + _bdot_nt(qr, ckr_ref[...])) * MLA_SCALE
    s_n = (_bdot_nt(ql, nlat) + _bdot_nt(qr, nkr_ref[...])) * MLA_SCALE
    m = jnp.maximum(jnp.max(s_c, axis=-1, keepdims=True), jnp.max(s_n, axis=-1, keepdims=True))
    p_c = jnp.exp(s_c - m)
    p_n = jnp.exp(s_n - m)
    denom = jnp.sum(p_c, axis=-1, keepdims=True) + jnp.sum(p_n, axis=-1, keepdims=True)
    o_lat = (_bdot(p_c, clat) + _bdot(p_n, nlat)) / denom
    for h in range(MLA_HEADS):
        o_ref[:, h * MLA_VDIM:(h + 1) * MLA_VDIM] = _bdot(
            o_lat[h * lq:(h + 1) * lq, :], wuv_ref[h]).astype(o_ref.dtype)


def _attn_cached(qn, qr, cache_lat, cache_kr_pad, new_lat, new_kr_pad, wuk_h, wuv_h, nb, lq):
    past = cache_lat.shape[1]
    whole = lambda b: (0, 0, 0)
    return pl.pallas_call(
        functools.partial(_attn_cached_kernel, lq=lq),
        grid=(nb,),
        in_specs=[
            pl.BlockSpec((MLA_HEADS, lq, LANES), lambda b: (0, b, 0)),
            pl.BlockSpec((MLA_HEADS, lq, LANES), lambda b: (0, b, 0)),
            pl.BlockSpec((None, past, KV_LORA), lambda b: (b, 0, 0)),
            pl.BlockSpec((None, past, LANES), lambda b: (b, 0, 0)),
            pl.BlockSpec((lq, KV_LORA), lambda b: (b, 0)),
            pl.BlockSpec((lq, LANES), lambda b: (b, 0)),
            pl.BlockSpec((MLA_HEADS, KV_LORA, MLA_NOPE), whole),
            pl.BlockSpec((MLA_HEADS, KV_LORA, MLA_VDIM), whole),
        ],
        out_specs=pl.BlockSpec((lq, MLA_HEADS * MLA_VDIM), lambda b: (b, 0)),
        out_shape=jax.ShapeDtypeStruct((nb * lq, MLA_HEADS * MLA_VDIM), BF16),
        scratch_shapes=[pltpu.VMEM((MLA_HEADS * lq, KV_LORA), BF16),
                        pltpu.VMEM((MLA_HEADS * lq, LANES), BF16)],
        compiler_params=_params(("parallel",)),
        name="attn_cached",
    )(qn, qr, cache_lat, cache_kr_pad, new_lat, new_kr_pad, wuk_h, wuv_h)


def _top_values(s, top_ref, want_rank, one_per_round=False):
    rank = jnp.full(s.shape, float(PEER_TOPK), F32) if want_rank else None
    rows = lax.broadcasted_iota(jnp.int32, s.shape, 0).astype(F32) if one_per_round else None
    for r in range(PEER_TOPK):
        m = jnp.max(s, axis=0, keepdims=True)
        top_ref[r:r + 1, :] = m
        hit = s == m
        if one_per_round:
            first = jnp.min(jnp.where(hit, rows, float(s.shape[0])), axis=0, keepdims=True)
            hit = rows == first
        if want_rank:
            rank = jnp.where(hit, float(r), rank)
        s = jnp.where(hit, -jnp.inf, s)
    return rank, m


_CAND_LIMITS = tuple(PEER_TOPK // (b + 1) for b in range(1, 8))
_HALF_TOPK = PEER_TOPK // 2


def _peer_select_kernel(h_ref, wq_ref, keys_ref, r1_ref, e1_ref, c0_ref, e0_ref, s_ref, a0_ref, a1_ref,
                        n_ref, *, heads_per_step):
    hx = h_ref[...]
    for hh in range(heads_per_step):
        q_t = _bdot_nt(wq_ref[hh * 2 * N_KEYS:(hh + 1) * 2 * N_KEYS, :], hx)
        s_ref[2 * hh] = _bdot(keys_ref[hh, 0], q_t[:N_KEYS])
        s_ref[2 * hh + 1] = _bdot(keys_ref[hh, 1], q_t[N_KEYS:])
    tied = [_peer_select_head(s_ref[2 * hh], s_ref[2 * hh + 1], r1_ref.at[hh], e1_ref.at[hh],
                              c0_ref.at[hh], e0_ref.at[hh], a0_ref.at[hh], a1_ref.at[hh])
            for hh in range(heads_per_step)]

    def redo(hh):
        _peer_select_head_ties(s_ref[2 * hh], s_ref[2 * hh + 1], r1_ref.at[hh], e1_ref.at[hh],
                               c0_ref.at[hh], e0_ref.at[hh], a0_ref.at[hh], a1_ref.at[hh], n_ref)

    for hh in range(heads_per_step):
        pl.when(jnp.max(tied[hh].astype(F32)) > 0.0)(functools.partial(redo, hh))


def _peer_select_head(s0, s1, r1_ref, e1_ref, c0_ref, e0_ref, a0_ref, a1_ref):
    _, last0 = _top_values(s0, a0_ref, False)
    rank1, _ = _top_values(s1, a1_ref, True)
    removed0 = jnp.sum((s0 >= last0).astype(F32), axis=0, keepdims=True)
    removed1 = jnp.sum((rank1 < float(PEER_TOPK)).astype(F32), axis=0, keepdims=True)
    a0 = a0_ref[...]
    a1 = a1_ref[...]
    a0_lo = a0[:_HALF_TOPK, :]
    row = lax.broadcasted_iota(jnp.int32, a0_lo.shape, 0)
    pieces = [a1[0:1, :] + a0]
    for b, limit in enumerate(_CAND_LIMITS, start=1):
        pieces.append(jnp.where(row < limit, a1[b:b + 1, :] + a0_lo, -jnp.inf))
    pieces.append(a1[_HALF_TOPK:, :] + a0[0:1, :])
    cand = jnp.concatenate(pieces, axis=0)
    c = cand
    for r in range(PEER_TOPK):
        tau = jnp.max(c, axis=0, keepdims=True)
        if r + 1 < PEER_TOPK:
            c = jnp.where(c == tau, -jnp.inf, c)
    sel = cand >= tau
    cmax = a0[0:1, :] + a1[0:1, :]
    z = jnp.sum(jnp.where(sel, jnp.exp(cand - cmax), 0.0), axis=0, keepdims=True)
    self32 = sel.astype(F32)
    n_hi = self32[_HALF_TOPK:PEER_TOPK, :]
    n_lo = self32[:_HALF_TOPK, :]
    for b in range(1, 8):
        lo = PEER_TOPK + (b - 1) * _HALF_TOPK
        n_lo = n_lo + self32[lo:lo + _HALF_TOPK, :]
    tail = jnp.sum(self32[PEER_TOPK + 7 * _HALF_TOPK:, :], axis=0, keepdims=True)
    n_lo = n_lo + jnp.where(row == 0, tail, 0.0)
    count0 = jnp.zeros(s0.shape, F32)
    for a in range(PEER_TOPK):
        n_a = n_lo[a:a + 1, :] if a < _HALF_TOPK else n_hi[a - _HALF_TOPK:a - _HALF_TOPK + 1, :]
        count0 = jnp.where(s0 == a0[a:a + 1, :], n_a, count0)
    r1_ref[...] = rank1.astype(r1_ref.dtype)
    c0_ref[...] = count0
    e0_ref[...] = jnp.exp(s0 - a0[0:1, :]) / z
    e1_ref[...] = jnp.exp(s1 - a1[0:1, :]).astype(e1_ref.dtype)

    n_sel = jnp.sum(self32, axis=0, keepdims=True)
    k = float(PEER_TOPK)
    return (removed0 != k) | (removed1 != k) | (n_sel != k)


def _peer_select_head_ties(s0, s1, r1_ref, e1_ref, c0_ref, e0_ref, a0_ref, a1_ref, n_ref):
    rank0, _ = _top_values(s0, a0_ref, True, one_per_round=True)
    rank1, _ = _top_values(s1, a1_ref, True, one_per_round=True)
    a0 = a0_ref[...]
    a1 = a1_ref[...]
    cand = jnp.concatenate([a0[a:a + 1, :] + a1 for a in range(PEER_TOPK)], axis=0)
    rows = lax.broadcasted_iota(jnp.int32, cand.shape, 0).astype(F32)
    c = cand
    sel = jnp.zeros(cand.shape, F32)
    for _ in range(PEER_TOPK):
        m = jnp.max(c, axis=0, keepdims=True)
        first = jnp.min(jnp.where(c == m, rows, float(cand.shape[0])), axis=0, keepdims=True)
        hit = rows == first
        sel = jnp.where(hit, 1.0, sel)
        c = jnp.where(hit, -jnp.inf, c)
    cmax = a0[0:1, :] + a1[0:1, :]
    z = jnp.sum(sel * jnp.exp(cand - cmax), axis=0, keepdims=True)
    for a in range(PEER_TOPK):
        n_ref[a:a + 1, :] = jnp.sum(sel[a * PEER_TOPK:(a + 1) * PEER_TOPK, :], axis=0, keepdims=True)
    n = n_ref[...]
    count0 = jnp.zeros(s0.shape, F32)
    for a in range(PEER_TOPK):
        count0 = jnp.where(rank0 == float(a), n[a:a + 1, :], count0)
    r1_ref[...] = rank1.astype(r1_ref.dtype)
    c0_ref[...] = count0
    e0_ref[...] = jnp.exp(s0 - a0[0:1, :]) / z
    e1_ref[...] = jnp.exp(s1 - a1[0:1, :]).astype(e1_ref.dtype)


def _peer_select(hx, wq_t, keys, layer, tt):
    t, d = hx.shape
    hps = SELECT_HEADS_PER_STEP
    o_spec = pl.BlockSpec((hps, N_KEYS, tt), lambda i, g: (g, 0, i))
    return pl.pallas_call(
        functools.partial(_peer_select_kernel, heads_per_step=hps),
        grid=(t // tt, PEER_HEADS // hps),
        in_specs=[pl.BlockSpec((tt, d), lambda i, g: (i, 0)),
                  pl.BlockSpec((hps * 2 * N_KEYS, d), lambda i, g: (g, 0)),
                  pl.BlockSpec((None, hps, 2, N_KEYS, N_KEYS), lambda i, g: (layer, g, 0, 0, 0))],
        out_specs=[o_spec] * 4,
        out_shape=[jax.ShapeDtypeStruct((PEER_HEADS, N_KEYS, t), dt) for dt in (BF16, BF16, F32, F32)],
        scratch_shapes=[pltpu.VMEM((2 * hps, N_KEYS, tt), F32),
                        pltpu.VMEM((hps, PEER_TOPK, tt), F32), pltpu.VMEM((hps, PEER_TOPK, tt), F32),
                        pltpu.VMEM((PEER_TOPK, tt), F32)],
        compiler_params=_params(("parallel", "arbitrary")),
        name="peer_select",
    )(hx, wq_t, keys)


def _sublane_bcast_bf16(row, rows):
    tile_rows = 16
    packed = jnp.broadcast_to(row, (tile_rows, row.shape[1])).astype(BF16)
    return jnp.tile(packed, (rows // tile_rows, 1))


def _peer_mix_kernel(x_ref, u_ref, vt_ref, r1_ref, e1_ref, c0_ref, e0_ref, res_ref, gate_ref, o_ref,
                     acc_ref, a_ref, g_ref, p_ref, *, n_i):
    j = pl.program_id(1)

    @pl.when(j == 0)
    def _():
        acc_ref[...] = jnp.zeros_like(acc_ref)

    x = x_ref[...]
    rows_per_chunk = 2 * N_KEYS
    for c in range(n_i // 2):
        lo = c * rows_per_chunk
        a_ref[c] = _bdot_nt(u_ref[lo:lo + rows_per_chunk, :], x)
    zero = jnp.zeros((), BF16)
    for ii in range(n_i):
        w = None
        for h in range(PEER_HEADS):
            count = _sublane_bcast_bf16(c0_ref[h, ii:ii + 1, :], N_KEYS)
            e0 = _sublane_bcast_bf16(e0_ref[h, ii:ii + 1, :], N_KEYS)
            term = jnp.where(r1_ref[h] < count, e1_ref[h] * e0, zero)
            w = term if w is None else w + term
        g_ref[ii] = w
    for c in range(n_i // 2):
        lo = c * rows_per_chunk
        gates = jnp.concatenate([g_ref[2 * c], g_ref[2 * c + 1]], axis=0)
        p_ref[lo:lo + rows_per_chunk, :] = gates * _gelu(a_ref[c]).astype(BF16)
    acc_ref[...] += jnp.dot(vt_ref[...], p_ref[...], preferred_element_type=F32)

    @pl.when(j == pl.num_programs(1) - 1)
    def _():
        o_ref[...] = res_ref[...] + (1.0 + gate_ref[...]) * acc_ref[...].T


def _peer_mix(hx, u_b, vt_b, layer, r1, e1, c0, e0, x_res, gate, l, tt, et):
    t, d = hx.shape
    e = u_b.shape[1]
    n_i = et // N_KEYS
    tok_all = pl.BlockSpec((PEER_HEADS, N_KEYS, tt), lambda i, j: (0, 0, i))
    tok_i = pl.BlockSpec((PEER_HEADS, n_i, tt), lambda i, j: (0, j, i))
    gate_arr, gate_spec = _gate_operand(gate, l, tt, d)
    return pl.pallas_call(
        functools.partial(_peer_mix_kernel, n_i=n_i),
        grid=(t // tt, e // et),
        in_specs=[pl.BlockSpec((tt, d), lambda i, j: (i, 0)),
                  pl.BlockSpec((None, et, d), lambda i, j: (layer, j, 0)),
                  pl.BlockSpec((None, d, et), lambda i, j: (layer, 0, j)),
                  tok_all, tok_all, tok_i, tok_i,
                  pl.BlockSpec((tt, d), lambda i, j: (i, 0)), gate_spec],
        out_specs=pl.BlockSpec((tt, d), lambda i, j: (i, 0)),
        out_shape=jax.ShapeDtypeStruct((t, d), F32),
        scratch_shapes=[pltpu.VMEM((d, tt), F32),
                        pltpu.VMEM((n_i // 2, 2 * N_KEYS, tt), F32),
                        pltpu.VMEM((n_i, N_KEYS, tt), BF16),
                        pltpu.VMEM((et, tt), BF16)],
        compiler_params=_params(("parallel", "arbitrary"), vmem=PEER_MIX_VMEM_LIMIT),
        name="peer_mix",
    )(hx, u_b, vt_b, r1, e1, c0, e0, x_res, gate_arr)


def _peer(hx, wq_t, keys, layer, u_b, vt_b, x_res, gate, l):
    t = hx.shape[0]
    tt = min(t, 512)
    r1, e1, c0, e0 = _peer_select(hx, wq_t, keys, layer, tt)
    return _peer_mix(hx, u_b, vt_b, layer, r1, e1, c0, e0, x_res, gate, l, tt, 1024)


def _rope_tables(pos, half, reps):
    inv = ROPE_THETA ** (-jnp.arange(half, dtype=F32) / half)
    ang = pos.astype(F32)[:, None] * inv[None, :]
    return jnp.tile(jnp.cos(ang), (1, reps)), jnp.tile(jnp.sin(ang), (1, reps))


def _rotate_half_cols(w):
    half = w.shape[-1] // 2
    return jnp.concatenate([-w[..., half:], w[..., :half]], axis=-1)


def _pad_last(w, width):
    return jnp.pad(w, [(0, 0)] * (w.ndim - 1) + [(0, width - w.shape[-1])])


def _shared_weights(p):
    out = {}
    w_rope = p["mla_w_dkv"][:, KV_LORA:]
    out["w_dkv_ext"] = jnp.concatenate(
        [p["mla_w_dkv"][:, :KV_LORA], _pad_last(w_rope, LANES), _pad_last(_rotate_half_cols(w_rope), LANES)],
        axis=1)
    w_uq = p["mla_w_uq"][0].reshape(-1, MLA_HEADS, MLA_NOPE + MLA_ROPE)
    q_lora = w_uq.shape[0]
    out["w_uq_nope"] = w_uq[:, :, :MLA_NOPE].reshape(q_lora, MLA_HEADS * MLA_NOPE)
    w_qr = w_uq[:, :, MLA_NOPE:]
    out["w_uq_rope"] = _pad_last(w_qr, LANES).reshape(q_lora, MLA_HEADS * LANES)
    out["w_uq_rot"] = _pad_last(_rotate_half_cols(w_qr), LANES).reshape(q_lora, MLA_HEADS * LANES)
    out["w_ukv_flat"] = jnp.concatenate([p["mla_w_uk"].reshape(KV_LORA, MLA_HEADS * MLA_NOPE),
                                         p["mla_w_uv"].reshape(KV_LORA, MLA_HEADS * MLA_VDIM)], axis=1)
    out["w_uk_h"] = jnp.transpose(p["mla_w_uk"], (1, 0, 2))
    out["w_uv_h"] = jnp.transpose(p["mla_w_uv"], (1, 0, 2))
    out["peer_wq_t"] = [p["peer_w_q"][l].T for l in range(2)]
    out["peer_u_b"] = p["peer_u"].astype(BF16)
    out["peer_vt_b"] = jnp.swapaxes(p["peer_v"].astype(BF16), 1, 2)
    return out


def _adaln(c_all, w, b, lead, n_out):
    m, d = c_all.shape
    tn = 2048
    if w.ndim == 2:
        w = w[None]
        b = b[None]
    b3 = b.reshape(b.shape[0], 1, n_out)
    return _linear(
        c_all, [w], n_cols=n_out, tm=m, tn=tn, epilogue=_ep_bias, prologue=_silu, w_lead=lead,
        extras=[b3], extra_specs=[pl.BlockSpec((None, 1, tn), lambda i, j: (lead, 0, j))],
        out_shapes=[jax.ShapeDtypeStruct((m, n_out), F32)],
        out_specs=[pl.BlockSpec((m, tn), lambda i, j: (0, j))])[0]


def _trunk(x, pos, mods, kv_mod, p, sw, ret_state, cache_lat, cache_kr):
    nb, l, d = x.shape
    t = nb * l
    pos_rows = jnp.tile(pos, nb) if l < 256 else pos
    table_rows = pos_rows.shape[0]
    assert table_rows % _row_tile(t) == 0, "a row tile must cover whole sequences or divide one"

    def vec(a):
        return a.reshape(nb, 1, d)

    def flat(a):
        return a.reshape(t, a.shape[-1])

    sh1, sc1, gt1, sh2, sc2, gt2 = [vec(m) for m in mods[0]]
    (h,) = _norm_mod(x, mods=[(p["norm_g"][0, 0][None], sh1, sc1)])
    h = flat(h)
    tm = _row_tile(t)
    cos_r, sin_r = _rope_tables(pos_rows, RET_DK // 2, 1)
    tspec = _table_spec(tm, table_rows)
    w_in = p["ret_w_in"]
    qk_cols = RET_HEADS * RET_DK
    v_cols = RET_HEADS * RET_DV
    tn = _col_tile(d, qk_cols)

    def rope_proj(col0, scale):
        return _linear(
            h, [w_in], n_cols=qk_cols, tm=tm, tn=tn, w_lead=0, col_block0=col0 // tn,
            epilogue=functools.partial(_ep_rope_half128, scale=scale),
            extras=[cos_r, sin_r], extra_specs=[tspec, tspec],
            out_shapes=[jax.ShapeDtypeStruct((t, qk_cols), BF16)],
            out_specs=[pl.BlockSpec((tm, tn), lambda i, j: (i, j))])[0]

    def plain_proj(x_in, w, n_cols, col0=0, lead=None, dtype=BF16):
        tm_ = _row_tile(x_in.shape[0])
        tn_ = _col_tile(x_in.shape[1], n_cols)
        return _linear(
            x_in, [w], n_cols=n_cols, tm=tm_, tn=tn_, w_lead=lead, col_block0=col0 // tn_,
            epilogue=_ep_plain,
            out_shapes=[jax.ShapeDtypeStruct((x_in.shape[0], n_cols), dtype)],
            out_specs=[pl.BlockSpec((tm_, tn_), lambda i, j: (i, j))])[0]

    def resid_proj(x_in, w, x_res, gate):
        tm_ = _row_tile(x_in.shape[0])
        tn_ = _col_tile(x_in.shape[1], d)
        gate_arr, gate_spec = _gate_operand(gate, l, tm_, tn_)
        tile = pl.BlockSpec((tm_, tn_), lambda i, j: (i, j))
        return _linear(
            x_in, [w], n_cols=d, tm=tm_, tn=tn_, w_lead=0, epilogue=_ep_resid,
            extras=[x_res, gate_arr], extra_specs=[tile, gate_spec],
            out_shapes=[jax.ShapeDtypeStruct((t, d), F32)], out_specs=[tile])[0]

    def heads_proj(x_in, ws, n_cols, epilogue=_ep_heads, extras=(), extra_specs=()):
        tm_ = _row_tile(x_in.shape[0])
        tn_ = _col_tile(x_in.shape[1], n_cols, len(ws))
        return _linear(
            x_in, ws, n_cols=n_cols, tm=tm_, tn=tn_, epilogue=epilogue, extras=extras, extra_specs=extra_specs,
            out_shapes=[jax.ShapeDtypeStruct((n_cols // LANES, x_in.shape[0], LANES), BF16)],
            out_specs=[pl.BlockSpec((tn_ // LANES, tm_, LANES), lambda i, j: (j, i, 0))])[0]

    q_r = rope_proj(0, 1.0)
    k_r = rope_proj(qk_cols, RET_DK ** -0.5)
    v_r = plain_proj(h, w_in, v_cols, col0=2 * qk_cols, lead=0)
    g_r = plain_proj(h, w_in, v_cols, col0=2 * qk_cols + v_cols, lead=0)
    s0 = None if ret_state is None else ret_state[0]
    y_r, s_new = _retention(q_r, k_r, v_r, g_r, p["ret_gn_g"], s0, nb, l)
    xf = resid_proj(y_r, p["ret_w_out"], flat(x), gt1)
    (h,) = _norm_mod(xf.reshape(nb, l, d), mods=[(p["norm_g"][0, 1][None], sh2, sc2)])
    xf = _peer(flat(h), sw["peer_wq_t"][0], p["peer_keys"], 0, sw["peer_u_b"], sw["peer_vt_b"],
               xf, gt2, l)

    sh1, sc1, gt1b, sh2, sc2, gt2b = [vec(m) for m in mods[1]]
    kv_sh, kv_sc = [vec(m) for m in kv_mod]
    h, h_kv = _norm_mod(xf.reshape(nb, l, d),
                         mods=[(p["norm_g"][1, 0][None], sh1, sc1), (p["kv_norm_g"][None], kv_sh, kv_sc)])
    h = flat(h)
    h_kv = flat(h_kv)
    cos_m, sin_m = _rope_tables(pos_rows, MLA_ROPE // 2, LANES // (MLA_ROPE // 2))
    n_ext = sw["w_dkv_ext"].shape[1]
    lat, kr, kr_pad = _linear(
        h_kv, [sw["w_dkv_ext"]], n_cols=n_ext, tm=tm, tn=n_ext, epilogue=_ep_kv,
        extras=[p["mla_kv_norm_g"][None], cos_m, sin_m],
        extra_specs=[pl.BlockSpec((1, KV_LORA), lambda i, j: (0, 0)), tspec, tspec],
        out_shapes=[jax.ShapeDtypeStruct((t, KV_LORA), F32), jax.ShapeDtypeStruct((t, MLA_ROPE), F32),
                    jax.ShapeDtypeStruct((t, LANES), BF16)],
        out_specs=[pl.BlockSpec((tm, KV_LORA), lambda i, j: (i, 0)),
                   pl.BlockSpec((tm, MLA_ROPE), lambda i, j: (i, 0)),
                   pl.BlockSpec((tm, LANES), lambda i, j: (i, 0))])
    q_lora = p["mla_w_dq"].shape[-1]
    cq = _linear(
        h, [p["mla_w_dq"]], n_cols=q_lora, tm=tm, tn=q_lora, w_lead=0, epilogue=_ep_rms,
        extras=[p["mla_q_norm_g"]], extra_specs=[pl.BlockSpec((1, q_lora), lambda i, j: (0, 0))],
        out_shapes=[jax.ShapeDtypeStruct((t, q_lora), BF16)],
        out_specs=[pl.BlockSpec((tm, q_lora), lambda i, j: (i, 0))])[0]
    hl = MLA_HEADS * LANES
    tn_q = _col_tile(q_lora, hl, 3)
    q_heads = jax.ShapeDtypeStruct((MLA_HEADS, t, LANES), BF16)
    q_spec = pl.BlockSpec((tn_q // LANES, tm, LANES), lambda i, j: (j, i, 0))
    qn, qr = _linear(
        cq, [sw["w_uq_nope"], sw["w_uq_rope"], sw["w_uq_rot"]], n_cols=hl, tm=tm, tn=tn_q,
        epilogue=_ep_query_heads, extras=[cos_m, sin_m], extra_specs=[tspec, tspec],
        out_shapes=[q_heads, q_heads], out_specs=[q_spec, q_spec])
    if cache_lat is None:
        kv_h = heads_proj(lat, [sw["w_ukv_flat"]], 2 * hl)
        o = _attn_prompt(qn, qr, kv_h, kr_pad, nb, l)
    else:
        ckr = _pad_last(cache_kr, LANES).astype(BF16)
        o = _attn_cached(qn, qr, cache_lat, ckr, lat, kr_pad, sw["w_uk_h"], sw["w_uv_h"], nb, l)
    xf = resid_proj(o, p["mla_w_o"], xf, gt1b)
    (h,) = _norm_mod(xf.reshape(nb, l, d), mods=[(p["norm_g"][1, 1][None], sh2, sc2)])
    xf = _peer(flat(h), sw["peer_wq_t"][1], p["peer_keys"], 1, sw["peer_u_b"], sw["peer_vt_b"],
               xf, gt2b, l)
    (y,) = _norm_mod(xf.reshape(nb, l, d), final_g=p["final_g"][None])
    return y, s_new[None], lat.reshape(nb, l, KV_LORA), kr.reshape(nb, l, MLA_ROPE)


def kernel(x_prompt, x_sample, c_prompt, c_sample, state_retention, cache_mla_latent, cache_mla_krope,
           ada_w, ada_b, norm_g, ret_w_in, ret_gn_g, ret_w_out,
           kv_ada_w, kv_ada_b, kv_norm_g, mla_w_dkv, mla_kv_norm_g, mla_w_uk, mla_w_uv,
           mla_w_dq, mla_q_norm_g, mla_w_uq, mla_w_o,
           peer_w_q, peer_keys, peer_u, peer_v, final_g):
    p = dict(norm_g=norm_g, ret_w_in=ret_w_in, ret_gn_g=ret_gn_g[0][None], ret_w_out=ret_w_out,
             kv_norm_g=kv_norm_g, mla_w_dkv=mla_w_dkv, mla_kv_norm_g=mla_kv_norm_g,
             mla_w_uk=mla_w_uk, mla_w_uv=mla_w_uv, mla_w_dq=mla_w_dq, mla_q_norm_g=mla_q_norm_g,
             mla_w_uq=mla_w_uq, mla_w_o=mla_w_o, peer_w_q=peer_w_q, peer_keys=peer_keys,
             peer_u=peer_u, peer_v=peer_v, final_g=final_g)
    sw = _shared_weights(p)
    nbp, lp, d = x_prompt.shape
    nbs, ls, _ = x_sample.shape
    past = cache_mla_latent.shape[1]

    n_c = nbp + nbs
    c_all = jnp.pad(jnp.concatenate([c_prompt, c_sample], axis=0), ((0, (-n_c) % 8), (0, 0)))
    layer_mods = [_adaln(c_all, ada_w, ada_b, layer, 6 * d) for layer in range(2)]
    kv_mods = _adaln(c_all, kv_ada_w, kv_ada_b, 0, 2 * d)

    def stream_mods(lo, hi):
        return ([jnp.split(m[lo:hi], 6, axis=-1) for m in layer_mods], jnp.split(kv_mods[lo:hi], 2, axis=-1))

    mods_p, kv_p = stream_mods(0, nbp)
    mods_s, kv_s = stream_mods(nbp, n_c)
    pos_p = jnp.arange(lp, dtype=jnp.int32)
    pos_s = past + jnp.arange(ls, dtype=jnp.int32)
    y_p, ret_p, lat_p, kr_p = _trunk(x_prompt, pos_p, mods_p, kv_p, p, sw, None, None, None)
    y_s, ret_s, lat_s, kr_s = _trunk(x_sample, pos_s, mods_s, kv_s, p, sw, state_retention,
                                     cache_mla_latent, cache_mla_krope)
    return (y_p, y_s, ret_p, ret_s, lat_p, kr_p, lat_s, kr_s)
```

```python
import functools
import math

import jax
import jax.numpy as jnp
from jax import lax
from jax.experimental import pallas as pl
from jax.experimental.pallas import tpu as pltpu

F32 = jnp.float32
BF16 = jnp.bfloat16

EPS = 1e-6
ROPE_THETA = 10000.0
NEG_INF = -1e30
CHUNK = 64

RET_HEADS = 8
RET_DK = 256
RET_DV = 512

MLA_HEADS = 16
MLA_NOPE = 128
MLA_ROPE = 64
MLA_VDIM = 128
KV_LORA = 512
MLA_SCALE = (MLA_NOPE + MLA_ROPE) ** -0.5

PEER_HEADS = 8
N_KEYS = 128
PEER_TOPK = 16

LANES = 128
VMEM_LIMIT = 52 * 1024 * 1024
PEER_MIX_VMEM_LIMIT = 58 * 1024 * 1024
SELECT_HEADS_PER_STEP = 4


def _params(sem, vmem=VMEM_LIMIT):
    return pltpu.CompilerParams(dimension_semantics=sem, vmem_limit_bytes=vmem)


def _bdot(a, b):
    return jnp.dot(a.astype(BF16), b.astype(BF16), preferred_element_type=F32)


def _bdot_nt(a, b):
    return lax.dot_general(a.astype(BF16), b.astype(BF16), (((1,), (1,)), ((), ())),
                           preferred_element_type=F32)


def _bdot_tn(a, b):
    return lax.dot_general(a.astype(BF16), b.astype(BF16), (((0,), (0,)), ((), ())),
                           preferred_element_type=F32)


def _silu(x):
    return x * (1.0 / (1.0 + jnp.exp(-x)))


def _gelu(x):
    return 0.5 * x * (1.0 + lax.erf(x * (0.5 ** 0.5)))


def _linear_kernel(*refs, n_w, n_extra, prologue, epilogue):
    x_ref = refs[0]
    w_refs = refs[1:1 + n_w]
    extra_refs = refs[1 + n_w:1 + n_w + n_extra]
    out_refs = refs[1 + n_w + n_extra:]
    xv = x_ref[...]
    if prologue is not None:
        xv = prologue(xv)
    xb = xv.astype(BF16)
    accs = [jnp.dot(xb, w[...].astype(BF16), preferred_element_type=F32) for w in w_refs]
    epilogue(accs, extra_refs, out_refs)


def _linear(x, ws, *, n_cols, tm, tn, epilogue, out_shapes, out_specs, w_lead=None, col_block0=0,
            extras=(), extra_specs=(), prologue=None, name=None):
    m, k = x.shape
    if name is None:
        name = "linear" + getattr(epilogue, "func", epilogue).__name__
    assert m % tm == 0 and n_cols % tn == 0
    if w_lead is None:
        w_spec = pl.BlockSpec((k, tn), lambda i, j: (0, j + col_block0))
    else:
        w_spec = pl.BlockSpec((None, k, tn), lambda i, j: (w_lead, 0, j + col_block0))
    kern = functools.partial(_linear_kernel, n_w=len(ws), n_extra=len(extras), prologue=prologue,
                             epilogue=epilogue)
    return pl.pallas_call(
        kern,
        grid=(m // tm, n_cols // tn),
        in_specs=[pl.BlockSpec((tm, k), lambda i, j: (i, 0))] + [w_spec] * len(ws) + list(extra_specs),
        out_specs=out_specs,
        out_shape=out_shapes,
        compiler_params=_params(("parallel", "arbitrary")),
        name=name,
    )(x, *ws, *extras)


def _ep_plain(accs, extras, outs):
    outs[0][...] = accs[0].astype(outs[0].dtype)


def _ep_heads(accs, extras, outs):
    acc = accs[0]
    for jj in range(acc.shape[1] // LANES):
        outs[0][jj] = acc[:, jj * LANES:(jj + 1) * LANES].astype(outs[0].dtype)


def _ep_bias(accs, extras, outs):
    outs[0][...] = accs[0] + extras[0][...]


def _ep_resid(accs, extras, outs):
    outs[0][...] = extras[0][...] + (1.0 + extras[1][...]) * accs[0]


def _gate_operand(gate, l, rows_per_tile, cols_per_tile):
    nb, _, d = gate.shape
    col = (lambda j: j) if cols_per_tile < d else (lambda j: 0)
    if l % rows_per_tile == 0:
        tiles_per_batch = l // rows_per_tile
        return gate, pl.BlockSpec((None, 1, cols_per_tile), lambda i, j: (i // tiles_per_batch, 0, col(j)))
    rows = jnp.broadcast_to(gate, (nb, l, d)).reshape(nb * l, d)
    return rows, pl.BlockSpec((rows_per_tile, cols_per_tile), lambda i, j: (i, col(j)))


def _ep_rope_half128(accs, extras, outs, *, scale):
    acc = accs[0]
    cos = extras[0][...]
    sin = extras[1][...]
    for g in range(acc.shape[1] // 256):
        x1 = acc[:, g * 256:g * 256 + 128]
        x2 = acc[:, g * 256 + 128:(g + 1) * 256]
        outs[0][:, g * 256:g * 256 + 128] = ((x1 * cos - x2 * sin) * scale).astype(outs[0].dtype)
        outs[0][:, g * 256 + 128:(g + 1) * 256] = ((x1 * sin + x2 * cos) * scale).astype(outs[0].dtype)


def _ep_query_heads(accs, extras, outs):
    cos = extras[0][...]
    sin = extras[1][...]
    plain, a, b = accs
    for jj in range(a.shape[1] // LANES):
        sl = slice(jj * LANES, (jj + 1) * LANES)
        outs[0][jj] = plain[:, sl].astype(outs[0].dtype)
        outs[1][jj] = (a[:, sl] * cos + b[:, sl] * sin).astype(outs[1].dtype)


def _ep_rms(accs, extras, outs):
    acc = accs[0]
    g = extras[0][...]
    y = acc * lax.rsqrt(jnp.mean(acc * acc, axis=-1, keepdims=True) + EPS) * g
    outs[0][...] = y.astype(outs[0].dtype)


def _ep_kv(accs, extras, outs):
    acc = accs[0]
    g = extras[0][...]
    cos = extras[1][...]
    sin = extras[2][...]
    c = acc[:, :KV_LORA]
    outs[0][...] = c * lax.rsqrt(jnp.mean(c * c, axis=-1, keepdims=True) + EPS) * g
    kr = acc[:, KV_LORA:KV_LORA + LANES] * cos + acc[:, KV_LORA + LANES:KV_LORA + 2 * LANES] * sin
    outs[1][...] = kr[:, :MLA_ROPE]
    outs[2][...] = kr.astype(BF16)


def _row_tile(m):
    return min(m, 1024)


def _col_tile(k, n, n_weights=1):
    for tn in (1024, 512):
        if n % tn == 0 and 2 * n_weights * k * tn * 4 <= 16 * 1024 * 1024:
            return tn
    return min(n, 512)


def _table_spec(tm, table_rows):
    nblk = table_rows // tm
    return pl.BlockSpec((tm, LANES), lambda i, j: (i % nblk, 0))


def _norm_mod_kernel(*refs, n_mod):
    x = refs[0][...]
    xn = x * lax.rsqrt(jnp.mean(x * x, axis=-1, keepdims=True) + EPS)
    if n_mod == 0:
        refs[2][...] = xn * refs[1][...]
        return
    outs = refs[1 + 3 * n_mod:]
    for m in range(n_mod):
        g, sh, sc = (refs[1 + 3 * m + k][...] for k in range(3))
        outs[m][...] = ((xn * g) * (1.0 + sc) + sh).astype(outs[m].dtype)


def _norm_mod(x, mods=(), final_g=None):
    nb, l, d = x.shape
    tl = min(l, 512)
    tok = pl.BlockSpec((None, tl, d), lambda b, i: (b, i, 0))
    per_b = pl.BlockSpec((None, 1, d), lambda b, i: (b, 0, 0))
    gain = pl.BlockSpec((1, d), lambda b, i: (0, 0))
    args, specs = [x], [tok]
    if final_g is not None:
        assert not mods
        args.append(final_g)
        specs.append(gain)
        out_shapes = [jax.ShapeDtypeStruct((nb, l, d), F32)]
    else:
        for g, sh, sc in mods:
            args += [g, sh, sc]
            specs += [gain, per_b, per_b]
        out_shapes = [jax.ShapeDtypeStruct((nb, l, d), BF16) for _ in mods]
    return pl.pallas_call(
        functools.partial(_norm_mod_kernel, n_mod=len(mods)),
        grid=(nb, l // tl), in_specs=specs, out_specs=[tok] * len(out_shapes), out_shape=out_shapes,
        compiler_params=_params(("parallel", "parallel")),
        name="norm_mod",
    )(*args)


def _ret_log_decay():
    return jnp.log1p(-jnp.exp2(-5.0 - jnp.arange(RET_HEADS, dtype=F32)))


def _retention_tables(c):
    log_g = _ret_log_decay()[:, None, None]
    n = jnp.arange(c, dtype=F32)
    dist = n[:, None] - n[None, :]
    same = (jnp.arange(c)[:, None] // CHUNK) == (jnp.arange(c)[None, :] // CHUNK)
    earlier = (jnp.arange(c)[None, :] // CHUNK) < (jnp.arange(c)[:, None] // CHUNK)
    mask = jnp.where(same[None], jnp.exp(jnp.abs(dist)[None] * log_g),
                     jnp.where(earlier[None], jnp.exp(dist[None] * log_g), 0.0))
    q_decay = jnp.exp((n[None, :, None] + 1.0) * log_g)
    k_decay = jnp.exp((c - 1.0 - n)[None, :, None] * log_g)
    blk_decay = jnp.exp(c * log_g)
    return mask, q_decay, k_decay, blk_decay


def _retention_kernel(*refs, has_s0, hps):
    if has_s0:
        (q_ref, k_ref, v_ref, g_ref, mask_ref, qd_ref, kd_ref, bd_ref, gn_ref, s0_ref,
         y_ref, s_out_ref, s_ref) = refs
    else:
        (q_ref, k_ref, v_ref, g_ref, mask_ref, qd_ref, kd_ref, bd_ref, gn_ref,
         y_ref, s_out_ref, s_ref) = refs
    c = pl.program_id(2)

    @pl.when(c == 0)
    def _():
        if has_s0:
            s_ref[...] = s0_ref[...]
        else:
            s_ref[...] = jnp.zeros_like(s_ref)

    for hh in range(hps):
        qk_cols = slice(hh * RET_DK, (hh + 1) * RET_DK)
        v_cols = slice(hh * RET_DV, (hh + 1) * RET_DV)
        q = q_ref[:, qk_cols]
        k = k_ref[:, qk_cols]
        v = v_ref[:, v_cols]
        s_prev = s_ref[hh]
        scores = _bdot_nt(q, k) * mask_ref[hh]
        y = _bdot(scores, v) + _bdot(q, s_prev) * qd_ref[hh]
        k_scaled = k.astype(F32) * kd_ref[hh]
        s_ref[hh] = bd_ref[hh] * s_prev + _bdot_tn(k_scaled, v)

        mu = jnp.mean(y, axis=-1, keepdims=True)
        yc = y - mu
        var = jnp.mean(yc * yc, axis=-1, keepdims=True)
        yn = yc * lax.rsqrt(var + EPS) * gn_ref[:, v_cols]
        y_ref[:, v_cols] = (_silu(g_ref[:, v_cols].astype(F32)) * yn).astype(y_ref.dtype)

    @pl.when(c == pl.num_programs(2) - 1)
    def _():
        s_out_ref[...] = s_ref[...]


def _retention(q, k, v, g, gn_g, s0, nb, l):
    cb = min(l, 256)
    nc = l // cb
    hps = 4 if nc > 1 else RET_HEADS
    mask, qd, kd, bd = _retention_tables(cb)
    row = lambda b, hg, c: (b * nc + c, hg)
    per_h3 = lambda b, hg, c: (hg, 0, 0)
    in_specs = [
        pl.BlockSpec((cb, hps * RET_DK), row), pl.BlockSpec((cb, hps * RET_DK), row),
        pl.BlockSpec((cb, hps * RET_DV), row), pl.BlockSpec((cb, hps * RET_DV), row),
        pl.BlockSpec((hps, cb, cb), per_h3), pl.BlockSpec((hps, cb, 1), per_h3),
        pl.BlockSpec((hps, cb, 1), per_h3), pl.BlockSpec((hps, 1, 1), per_h3),
        pl.BlockSpec((1, hps * RET_DV), lambda b, hg, c: (0, hg)),
    ]
    args = [q, k, v, g, mask, qd, kd, bd, gn_g]
    state_spec = pl.BlockSpec((None, hps, RET_DK, RET_DV), lambda b, hg, c: (b, hg, 0, 0))
    if s0 is not None:
        in_specs.append(state_spec)
        args.append(s0)
    y, s_new = pl.pallas_call(
        functools.partial(_retention_kernel, has_s0=s0 is not None, hps=hps),
        grid=(nb, RET_HEADS // hps, nc),
        in_specs=in_specs,
        out_specs=[pl.BlockSpec((cb, hps * RET_DV), row), state_spec],
        out_shape=[jax.ShapeDtypeStruct((nb * l, RET_HEADS * RET_DV), BF16),
                   jax.ShapeDtypeStruct((nb, RET_HEADS, RET_DK, RET_DV), F32)],
        scratch_shapes=[pltpu.VMEM((hps, RET_DK, RET_DV), F32)],
        compiler_params=_params(("parallel", "parallel", "arbitrary")),
        name="retention",
    )(*args)
    return y, s_new


def _attn_prompt_kernel(qi_ref, ki_ref, qn_ref, qr_ref, kn_ref, v_ref, kr_ref, o_ref, m_ref, acc_ref, *, tq, tk):
    qi = qi_ref[pl.program_id(1)]
    ki = ki_ref[pl.program_id(1)]
    exp2_scale = MLA_SCALE * math.log2(math.e)

    @pl.when(ki == 0)
    def _():
        m_ref[...] = jnp.full_like(m_ref, NEG_INF)
        acc_ref[...] = jnp.zeros_like(acc_ref)

    def block(masked):
        if masked:
            q_chunk = (qi * tq + lax.broadcasted_iota(jnp.int32, (tq, tk), 0)) // CHUNK
            k_chunk = (ki * tk + lax.broadcasted_iota(jnp.int32, (tq, tk), 1)) // CHUNK
            visible = k_chunk <= q_chunk
        kr = kr_ref[...]
        ones = jnp.ones((tk, LANES), BF16)

        def head(h, carry):
            q = jnp.concatenate([qn_ref[h], qr_ref[h]], axis=1)
            k = jnp.concatenate([kn_ref[h], kr], axis=1)
            s = _bdot_nt(q, k)
            if masked:
                s = jnp.where(visible, s, NEG_INF)
            m_prev = m_ref[h]
            m_new = jnp.maximum(m_prev, jnp.max(s, axis=-1, keepdims=True))
            alpha = jnp.exp2((m_prev - m_new) * exp2_scale)
            p = jnp.exp2((s - jnp.tile(m_new, (1, tk // LANES))) * exp2_scale)
            v_ext = jnp.concatenate([v_ref[h], ones], axis=1)
            acc_ref[h] = jnp.tile(alpha, (1, 2)) * acc_ref[h] + _bdot(p, v_ext)
            m_ref[h] = m_new
            return carry

        lax.fori_loop(0, MLA_HEADS, head, 0, unroll=True)

    @pl.when(ki < qi)
    def _():
        block(False)

    @pl.when(ki == qi)
    def _():
        block(True)
        for h in range(MLA_HEADS):
            acc = acc_ref[h]
            o_ref[:, h * MLA_VDIM:(h + 1) * MLA_VDIM] = (
                acc[:, :MLA_VDIM] / acc[:, MLA_VDIM:]).astype(o_ref.dtype)


def _attn_prompt(qn, qr, kv, kr, nb, l):
    tq = tk = min(l, 512)
    nq = l // tq
    pairs = [(qi, ki) for qi in range(nq) for ki in range(qi + 1)]
    qi_of = jnp.asarray([qk[0] for qk in pairs], jnp.int32)
    ki_of = jnp.asarray([qk[1] for qk in pairs], jnp.int32)
    q_spec = pl.BlockSpec((MLA_HEADS, tq, LANES), lambda b, s, qi, ki: (0, b * nq + qi[s], 0))
    k_spec = pl.BlockSpec((MLA_HEADS, tk, LANES), lambda b, s, qi, ki: (0, b * nq + ki[s], 0))
    v_spec = pl.BlockSpec((MLA_HEADS, tk, LANES), lambda b, s, qi, ki: (1, b * nq + ki[s], 0))
    return pl.pallas_call(
        functools.partial(_attn_prompt_kernel, tq=tq, tk=tk),
        grid_spec=pltpu.PrefetchScalarGridSpec(
            num_scalar_prefetch=2,
            grid=(nb, len(pairs)),
            in_specs=[q_spec, q_spec, k_spec, v_spec,
                      pl.BlockSpec((tk, LANES), lambda b, s, qi, ki: (b * nq + ki[s], 0))],
            out_specs=pl.BlockSpec((tq, MLA_HEADS * MLA_VDIM), lambda b, s, qi, ki: (b * nq + qi[s], 0)),
            scratch_shapes=[pltpu.VMEM((MLA_HEADS, tq, LANES), F32),
                            pltpu.VMEM((MLA_HEADS, tq, MLA_VDIM + LANES), F32)]),
        out_shape=jax.ShapeDtypeStruct((nb * l, MLA_HEADS * MLA_VDIM), BF16),
        compiler_params=_params(("parallel", "arbitrary")),
        name="attn_prompt",
    )(qi_of, ki_of, qn, qr, kv, kv, kr)


def _attn_cached_kernel(qn_ref, qr_ref, clat_ref, ckr_ref, nlat_ref, nkr_ref, wuk_ref, wuv_ref,
                        o_ref, ql_ref, qrs_ref, *, lq):
    for h in range(MLA_HEADS):
        ql_ref[h * lq:(h + 1) * lq, :] = _bdot_nt(qn_ref[h], wuk_ref[h]).astype(BF16)
        qrs_ref[h * lq:(h + 1) * lq, :] = qr_ref[h]
    ql = ql_ref[...]
    qr = qrs_ref[...]
    clat = clat_ref[...].astype(BF16)
    nlat = nlat_ref[...].astype(BF16)
    s_c = (_bdot_nt(ql, clat) + _bdot_nt(qr, ckr_ref[...])) * MLA_SCALE
    s_n = (_bdot_nt(ql, nlat) + _bdot_nt(qr, nkr_ref[...])) * MLA_SCALE
    m = jnp.maximum(jnp.max(s_c, axis=-1, keepdims=True), jnp.max(s_n, axis=-1, keepdims=True))
    p_c = jnp.exp(s_c - m)
    p_n = jnp.exp(s_n - m)
    denom = jnp.sum(p_c, axis=-1, keepdims=True) + jnp.sum(p_n, axis=-1, keepdims=True)
    o_lat = (_bdot(p_c, clat) + _bdot(p_n, nlat)) / denom
    for h in range(MLA_HEADS):
        o_ref[:, h * MLA_VDIM:(h + 1) * MLA_VDIM] = _bdot(
            o_lat[h * lq:(h + 1) * lq, :], wuv_ref[h]).astype(o_ref.dtype)


def _attn_cached(qn, qr, cache_lat, cache_kr_pad, new_lat, new_kr_pad, wuk_h, wuv_h, nb, lq):
    past = cache_lat.shape[1]
    whole = lambda b: (0, 0, 0)
    return pl.pallas_call(
        functools.partial(_attn_cached_kernel, lq=lq),
        grid=(nb,),
        in_specs=[
            pl.BlockSpec((MLA_HEADS, lq, LANES), lambda b: (0, b, 0)),
            pl.BlockSpec((MLA_HEADS, lq, LANES), lambda b: (0, b, 0)),
            pl.BlockSpec((None, past, KV_LORA), lambda b: (b, 0, 0)),
            pl.BlockSpec((None, past, LANES), lambda b: (b, 0, 0)),
            pl.BlockSpec((lq, KV_LORA), lambda b: (b, 0)),
            pl.BlockSpec((lq, LANES), lambda b: (b, 0)),
            pl.BlockSpec((MLA_HEADS, KV_LORA, MLA_NOPE), whole),
            pl.BlockSpec((MLA_HEADS, KV_LORA, MLA_VDIM), whole),
        ],
        out_specs=pl.BlockSpec((lq, MLA_HEADS * MLA_VDIM), lambda b: (b, 0)),
        out_shape=jax.ShapeDtypeStruct((nb * lq, MLA_HEADS * MLA_VDIM), BF16),
        scratch_shapes=[pltpu.VMEM((MLA_HEADS * lq, KV_LORA), BF16),
                        pltpu.VMEM((MLA_HEADS * lq, LANES), BF16)],
        compiler_params=_params(("parallel",)),
        name="attn_cached",
    )(qn, qr, cache_lat, cache_kr_pad, new_lat, new_kr_pad, wuk_h, wuv_h)


def _top_values(s, top_ref, want_rank, one_per_round=False):
    rank = jnp.full(s.shape, float(PEER_TOPK), F32) if want_rank else None
    rows = lax.broadcasted_iota(jnp.int32, s.shape, 0).astype(F32) if one_per_round else None
    for r in range(PEER_TOPK):
        m = jnp.max(s, axis=0, keepdims=True)
        top_ref[r:r + 1, :] = m
        hit = s == m
        if one_per_round:
            first = jnp.min(jnp.where(hit, rows, float(s.shape[0])), axis=0, keepdims=True)
            hit = rows == first
        if want_rank:
            rank = jnp.where(hit, float(r), rank)
        s = jnp.where(hit, -jnp.inf, s)
    return rank, m


_CAND_LIMITS = tuple(PEER_TOPK // (b + 1) for b in range(1, 8))
_HALF_TOPK = PEER_TOPK // 2


def _peer_select_kernel(h_ref, wq_ref, keys_ref, r1_ref, e1_ref, c0_ref, e0_ref, s_ref, a0_ref, a1_ref,
                        n_ref, *, heads_per_step):
    hx = h_ref[...]
    for hh in range(heads_per_step):
        q_t = _bdot_nt(wq_ref[hh * 2 * N_KEYS:(hh + 1) * 2 * N_KEYS, :], hx)
        s_ref[2 * hh] = _bdot(keys_ref[hh, 0], q_t[:N_KEYS])
        s_ref[2 * hh + 1] = _bdot(keys_ref[hh, 1], q_t[N_KEYS:])
    tied = [_peer_select_head(s_ref[2 * hh], s_ref[2 * hh + 1], r1_ref.at[hh], e1_ref.at[hh],
                              c0_ref.at[hh], e0_ref.at[hh], a0_ref.at[hh], a1_ref.at[hh])
            for hh in range(heads_per_step)]

    def redo(hh):
        _peer_select_head_ties(s_ref[2 * hh], s_ref[2 * hh + 1], r1_ref.at[hh], e1_ref.at[hh],
                               c0_ref.at[hh], e0_ref.at[hh], a0_ref.at[hh], a1_ref.at[hh], n_ref)

    for hh in range(heads_per_step):
        pl.when(jnp.max(tied[hh].astype(F32)) > 0.0)(functools.partial(redo, hh))


def _peer_select_head(s0, s1, r1_ref, e1_ref, c0_ref, e0_ref, a0_ref, a1_ref):
    _, last0 = _top_values(s0, a0_ref, False)
    rank1, _ = _top_values(s1, a1_ref, True)
    removed0 = jnp.sum((s0 >= last0).astype(F32), axis=0, keepdims=True)
    removed1 = jnp.sum((rank1 < float(PEER_TOPK)).astype(F32), axis=0, keepdims=True)
    a0 = a0_ref[...]
    a1 = a1_ref[...]
    a0_lo = a0[:_HALF_TOPK, :]
    row = lax.broadcasted_iota(jnp.int32, a0_lo.shape, 0)
    pieces = [a1[0:1, :] + a0]
    for b, limit in enumerate(_CAND_LIMITS, start=1):
        pieces.append(jnp.where(row < limit, a1[b:b + 1, :] + a0_lo, -jnp.inf))
    pieces.append(a1[_HALF_TOPK:, :] + a0[0:1, :])
    cand = jnp.concatenate(pieces, axis=0)
    c = cand
    for r in range(PEER_TOPK):
        tau = jnp.max(c, axis=0, keepdims=True)
        if r + 1 < PEER_TOPK:
            c = jnp.where(c == tau, -jnp.inf, c)
    sel = cand >= tau
    cmax = a0[0:1, :] + a1[0:1, :]
    z = jnp.sum(jnp.where(sel, jnp.exp(cand - cmax), 0.0), axis=0, keepdims=True)
    self32 = sel.astype(F32)
    n_hi = self32[_HALF_TOPK:PEER_TOPK, :]
    n_lo = self32[:_HALF_TOPK, :]
    for b in range(1, 8):
        lo = PEER_TOPK + (b - 1) * _HALF_TOPK
        n_lo = n_lo + self32[lo:lo + _HALF_TOPK, :]
    tail = jnp.sum(self32[PEER_TOPK + 7 * _HALF_TOPK:, :], axis=0, keepdims=True)
    n_lo = n_lo + jnp.where(row == 0, tail, 0.0)
    count0 = jnp.zeros(s0.shape, F32)
    for a in range(PEER_TOPK):
        n_a = n_lo[a:a + 1, :] if a < _HALF_TOPK else n_hi[a - _HALF_TOPK:a - _HALF_TOPK + 1, :]
        count0 = jnp.where(s0 == a0[a:a + 1, :], n_a, count0)
    r1_ref[...] = rank1.astype(r1_ref.dtype)
    c0_ref[...] = count0
    e0_ref[...] = jnp.exp(s0 - a0[0:1, :]) / z
    e1_ref[...] = jnp.exp(s1 - a1[0:1, :]).astype(e1_ref.dtype)

    n_sel = jnp.sum(self32, axis=0, keepdims=True)
    k = float(PEER_TOPK)
    return (removed0 != k) | (removed1 != k) | (n_sel != k)


def _peer_select_head_ties(s0, s1, r1_ref, e1_ref, c0_ref, e0_ref, a0_ref, a1_ref, n_ref):
    rank0, _ = _top_values(s0, a0_ref, True, one_per_round=True)
    rank1, _ = _top_values(s1, a1_ref, True, one_per_round=True)
    a0 = a0_ref[...]
    a1 = a1_ref[...]
    cand = jnp.concatenate([a0[a:a + 1, :] + a1 for a in range(PEER_TOPK)], axis=0)
    rows = lax.broadcasted_iota(jnp.int32, cand.shape, 0).astype(F32)
    c = cand
    sel = jnp.zeros(cand.shape, F32)
    for _ in range(PEER_TOPK):
        m = jnp.max(c, axis=0, keepdims=True)
        first = jnp.min(jnp.where(c == m, rows, float(cand.shape[0])), axis=0, keepdims=True)
        hit = rows == first
        sel = jnp.where(hit, 1.0, sel)
        c = jnp.where(hit, -jnp.inf, c)
    cmax = a0[0:1, :] + a1[0:1, :]
    z = jnp.sum(sel * jnp.exp(cand - cmax), axis=0, keepdims=True)
    for a in range(PEER_TOPK):
        n_ref[a:a + 1, :] = jnp.sum(sel[a * PEER_TOPK:(a + 1) * PEER_TOPK, :], axis=0, keepdims=True)
    n = n_ref[...]
    count0 = jnp.zeros(s0.shape, F32)
    for a in range(PEER_TOPK):
        count0 = jnp.where(rank0 == float(a), n[a:a + 1, :], count0)
    r1_ref[...] = rank1.astype(r1_ref.dtype)
    c0_ref[...] = count0
    e0_ref[...] = jnp.exp(s0 - a0[0:1, :]) / z
    e1_ref[...] = jnp.exp(s1 - a1[0:1, :]).astype(e1_ref.dtype)


def _peer_select(hx, wq_t, keys, layer, tt):
    t, d = hx.shape
    hps = SELECT_HEADS_PER_STEP
    o_spec = pl.BlockSpec((hps, N_KEYS, tt), lambda i, g: (g, 0, i))
    return pl.pallas_call(
        functools.partial(_peer_select_kernel, heads_per_step=hps),
        grid=(t // tt, PEER_HEADS // hps),
        in_specs=[pl.BlockSpec((tt, d), lambda i, g: (i, 0)),
                  pl.BlockSpec((hps * 2 * N_KEYS, d), lambda i, g: (g, 0)),
                  pl.BlockSpec((None, hps, 2, N_KEYS, N_KEYS), lambda i, g: (layer, g, 0, 0, 0))],
        out_specs=[o_spec] * 4,
        out_shape=[jax.ShapeDtypeStruct((PEER_HEADS, N_KEYS, t), dt) for dt in (BF16, BF16, F32, F32)],
        scratch_shapes=[pltpu.VMEM((2 * hps, N_KEYS, tt), F32),
                        pltpu.VMEM((hps, PEER_TOPK, tt), F32), pltpu.VMEM((hps, PEER_TOPK, tt), F32),
                        pltpu.VMEM((PEER_TOPK, tt), F32)],
        compiler_params=_params(("parallel", "arbitrary")),
        name="peer_select",
    )(hx, wq_t, keys)


def _sublane_bcast_bf16(row, rows):
    tile_rows = 16
    packed = jnp.broadcast_to(row, (tile_rows, row.shape[1])).astype(BF16)
    return jnp.tile(packed, (rows // tile_rows, 1))


def _peer_mix_kernel(x_ref, u_ref, vt_ref, r1_ref, e1_ref, c0_ref, e0_ref, res_ref, gate_ref, o_ref,
                     acc_ref, a_ref, g_ref, p_ref, *, n_i):
    j = pl.program_id(1)

    @pl.when(j == 0)
    def _():
        acc_ref[...] = jnp.zeros_like(acc_ref)

    x = x_ref[...]
    rows_per_chunk = 2 * N_KEYS
    for c in range(n_i // 2):
        lo = c * rows_per_chunk
        a_ref[c] = _bdot_nt(u_ref[lo:lo + rows_per_chunk, :], x)
    zero = jnp.zeros((), BF16)
    for ii in range(n_i):
        w = None
        for h in range(PEER_HEADS):
            count = _sublane_bcast_bf16(c0_ref[h, ii:ii + 1, :], N_KEYS)
            e0 = _sublane_bcast_bf16(e0_ref[h, ii:ii + 1, :], N_KEYS)
            term = jnp.where(r1_ref[h] < count, e1_ref[h] * e0, zero)
            w = term if w is None else w + term
        g_ref[ii] = w
    for c in range(n_i // 2):
        lo = c * rows_per_chunk
        gates = jnp.concatenate([g_ref[2 * c], g_ref[2 * c + 1]], axis=0)
        p_ref[lo:lo + rows_per_chunk, :] = gates * _gelu(a_ref[c]).astype(BF16)
    acc_ref[...] += jnp.dot(vt_ref[...], p_ref[...], preferred_element_type=F32)

    @pl.when(j == pl.num_programs(1) - 1)
    def _():
        o_ref[...] = res_ref[...] + (1.0 + gate_ref[...]) * acc_ref[...].T


def _peer_mix(hx, u_b, vt_b, layer, r1, e1, c0, e0, x_res, gate, l, tt, et):
    t, d = hx.shape
    e = u_b.shape[1]
    n_i = et // N_KEYS
    tok_all = pl.BlockSpec((PEER_HEADS, N_KEYS, tt), lambda i, j: (0, 0, i))
    tok_i = pl.BlockSpec((PEER_HEADS, n_i, tt), lambda i, j: (0, j, i))
    gate_arr, gate_spec = _gate_operand(gate, l, tt, d)
    return pl.pallas_call(
        functools.partial(_peer_mix_kernel, n_i=n_i),
        grid=(t // tt, e // et),
        in_specs=[pl.BlockSpec((tt, d), lambda i, j: (i, 0)),
                  pl.BlockSpec((None, et, d), lambda i, j: (layer, j, 0)),
                  pl.BlockSpec((None, d, et), lambda i, j: (layer, 0, j)),
                  tok_all, tok_all, tok_i, tok_i,
                  pl.BlockSpec((tt, d), lambda i, j: (i, 0)), gate_spec],
        out_specs=pl.BlockSpec((tt, d), lambda i, j: (i, 0)),
        out_shape=jax.ShapeDtypeStruct((t, d), F32),
        scratch_shapes=[pltpu.VMEM((d, tt), F32),
                        pltpu.VMEM((n_i // 2, 2 * N_KEYS, tt), F32),
                        pltpu.VMEM((n_i, N_KEYS, tt), BF16),
                        pltpu.VMEM((et, tt), BF16)],
        compiler_params=_params(("parallel", "arbitrary"), vmem=PEER_MIX_VMEM_LIMIT),
        name="peer_mix",
    )(hx, u_b, vt_b, r1, e1, c0, e0, x_res, gate_arr)


def _peer(hx, wq_t, keys, layer, u_b, vt_b, x_res, gate, l):
    t = hx.shape[0]
    tt = min(t, 512)
    r1, e1, c0, e0 = _peer_select(hx, wq_t, keys, layer, tt)
    return _peer_mix(hx, u_b, vt_b, layer, r1, e1, c0, e0, x_res, gate, l, tt, 1024)


def _rope_tables(pos, half, reps):
    inv = ROPE_THETA ** (-jnp.arange(half, dtype=F32) / half)
    ang = pos.astype(F32)[:, None] * inv[None, :]
    return jnp.tile(jnp.cos(ang), (1, reps)), jnp.tile(jnp.sin(ang), (1, reps))


def _rotate_half_cols(w):
    half = w.shape[-1] // 2
    return jnp.concatenate([-w[..., half:], w[..., :half]], axis=-1)


def _pad_last(w, width):
    return jnp.pad(w, [(0, 0)] * (w.ndim - 1) + [(0, width - w.shape[-1])])


def _shared_weights(p):
    out = {}
    w_rope = p["mla_w_dkv"][:, KV_LORA:]
    out["w_dkv_ext"] = jnp.concatenate(
        [p["mla_w_dkv"][:, :KV_LORA], _pad_last(w_rope, LANES), _pad_last(_rotate_half_cols(w_rope), LANES)],
        axis=1)
    w_uq = p["mla_w_uq"][0].reshape(-1, MLA_HEADS, MLA_NOPE + MLA_ROPE)
    q_lora = w_uq.shape[0]
    out["w_uq_nope"] = w_uq[:, :, :MLA_NOPE].reshape(q_lora, MLA_HEADS * MLA_NOPE)
    w_qr = w_uq[:, :, MLA_NOPE:]
    out["w_uq_rope"] = _pad_last(w_qr, LANES).reshape(q_lora, MLA_HEADS * LANES)
    out["w_uq_rot"] = _pad_last(_rotate_half_cols(w_qr), LANES).reshape(q_lora, MLA_HEADS * LANES)
    out["w_ukv_flat"] = jnp.concatenate([p["mla_w_uk"].reshape(KV_LORA, MLA_HEADS * MLA_NOPE),
                                         p["mla_w_uv"].reshape(KV_LORA, MLA_HEADS * MLA_VDIM)], axis=1)
    out["w_uk_h"] = jnp.transpose(p["mla_w_uk"], (1, 0, 2))
    out["w_uv_h"] = jnp.transpose(p["mla_w_uv"], (1, 0, 2))
    out["peer_wq_t"] = [p["peer_w_q"][l].T for l in range(2)]
    out["peer_u_b"] = p["peer_u"].astype(BF16)
    out["peer_vt_b"] = jnp.swapaxes(p["peer_v"].astype(BF16), 1, 2)
    return out


def _adaln(c_all, w, b, lead, n_out):
    m, d = c_all.shape
    tn = 2048
    if w.ndim == 2:
        w = w[None]
        b = b[None]
    b3 = b.reshape(b.shape[0], 1, n_out)
    return _linear(
        c_all, [w], n_cols=n_out, tm=m, tn=tn, epilogue=_ep_bias, prologue=_silu, w_lead=lead,
        extras=[b3], extra_specs=[pl.BlockSpec((None, 1, tn), lambda i, j: (lead, 0, j))],
        out_shapes=[jax.ShapeDtypeStruct((m, n_out), F32)],
        out_specs=[pl.BlockSpec((m, tn), lambda i, j: (0, j))])[0]


def _trunk(x, pos, mods, kv_mod, p, sw, ret_state, cache_lat, cache_kr):
    nb, l, d = x.shape
    t = nb * l
    pos_rows = jnp.tile(pos, nb) if l < 256 else pos
    table_rows = pos_rows.shape[0]
    assert table_rows % _row_tile(t) == 0, "a row tile must cover whole sequences or divide one"

    def vec(a):
        return a.reshape(nb, 1, d)

    def flat(a):
        return a.reshape(t, a.shape[-1])

    sh1, sc1, gt1, sh2, sc2, gt2 = [vec(m) for m in mods[0]]
    (h,) = _norm_mod(x, mods=[(p["norm_g"][0, 0][None], sh1, sc1)])
    h = flat(h)
    tm = _row_tile(t)
    cos_r, sin_r = _rope_tables(pos_rows, RET_DK // 2, 1)
    tspec = _table_spec(tm, table_rows)
    w_in = p["ret_w_in"]
    qk_cols = RET_HEADS * RET_DK
    v_cols = RET_HEADS * RET_DV
    tn = _col_tile(d, qk_cols)

    def rope_proj(col0, scale):
        return _linear(
            h, [w_in], n_cols=qk_cols, tm=tm, tn=tn, w_lead=0, col_block0=col0 // tn,
            epilogue=functools.partial(_ep_rope_half128, scale=scale),
            extras=[cos_r, sin_r], extra_specs=[tspec, tspec],
            out_shapes=[jax.ShapeDtypeStruct((t, qk_cols), BF16)],
            out_specs=[pl.BlockSpec((tm, tn), lambda i, j: (i, j))])[0]

    def plain_proj(x_in, w, n_cols, col0=0, lead=None, dtype=BF16):
        tm_ = _row_tile(x_in.shape[0])
        tn_ = _col_tile(x_in.shape[1], n_cols)
        return _linear(
            x_in, [w], n_cols=n_cols, tm=tm_, tn=tn_, w_lead=lead, col_block0=col0 // tn_,
            epilogue=_ep_plain,
            out_shapes=[jax.ShapeDtypeStruct((x_in.shape[0], n_cols), dtype)],
            out_specs=[pl.BlockSpec((tm_, tn_), lambda i, j: (i, j))])[0]

    def resid_proj(x_in, w, x_res, gate):
        tm_ = _row_tile(x_in.shape[0])
        tn_ = _col_tile(x_in.shape[1], d)
        gate_arr, gate_spec = _gate_operand(gate, l, tm_, tn_)
        tile = pl.BlockSpec((tm_, tn_), lambda i, j: (i, j))
        return _linear(
            x_in, [w], n_cols=d, tm=tm_, tn=tn_, w_lead=0, epilogue=_ep_resid,
            extras=[x_res, gate_arr], extra_specs=[tile, gate_spec],
            out_shapes=[jax.ShapeDtypeStruct((t, d), F32)], out_specs=[tile])[0]

    def heads_proj(x_in, ws, n_cols, epilogue=_ep_heads, extras=(), extra_specs=()):
        tm_ = _row_tile(x_in.shape[0])
        tn_ = _col_tile(x_in.shape[1], n_cols, len(ws))
        return _linear(
            x_in, ws, n_cols=n_cols, tm=tm_, tn=tn_, epilogue=epilogue, extras=extras, extra_specs=extra_specs,
            out_shapes=[jax.ShapeDtypeStruct((n_cols // LANES, x_in.shape[0], LANES), BF16)],
            out_specs=[pl.BlockSpec((tn_ // LANES, tm_, LANES), lambda i, j: (j, i, 0))])[0]

    q_r = rope_proj(0, 1.0)
    k_r = rope_proj(qk_cols, RET_DK ** -0.5)
    v_r = plain_proj(h, w_in, v_cols, col0=2 * qk_cols, lead=0)
    g_r = plain_proj(h, w_in, v_cols, col0=2 * qk_cols + v_cols, lead=0)
    s0 = None if ret_state is None else ret_state[0]
    y_r, s_new = _retention(q_r, k_r, v_r, g_r, p["ret_gn_g"], s0, nb, l)
    xf = resid_proj(y_r, p["ret_w_out"], flat(x), gt1)
    (h,) = _norm_mod(xf.reshape(nb, l, d), mods=[(p["norm_g"][0, 1][None], sh2, sc2)])
    xf = _peer(flat(h), sw["peer_wq_t"][0], p["peer_keys"], 0, sw["peer_u_b"], sw["peer_vt_b"],
               xf, gt2, l)

    sh1, sc1, gt1b, sh2, sc2, gt2b = [vec(m) for m in mods[1]]
    kv_sh, kv_sc = [vec(m) for m in kv_mod]
    h, h_kv = _norm_mod(xf.reshape(nb, l, d),
                         mods=[(p["norm_g"][1, 0][None], sh1, sc1), (p["kv_norm_g"][None], kv_sh, kv_sc)])
    h = flat(h)
    h_kv = flat(h_kv)
    cos_m, sin_m = _rope_tables(pos_rows, MLA_ROPE // 2, LANES // (MLA_ROPE // 2))
    n_ext = sw["w_dkv_ext"].shape[1]
    lat, kr, kr_pad = _linear(
        h_kv, [sw["w_dkv_ext"]], n_cols=n_ext, tm=tm, tn=n_ext, epilogue=_ep_kv,
        extras=[p["mla_kv_norm_g"][None], cos_m, sin_m],
        extra_specs=[pl.BlockSpec((1, KV_LORA), lambda i, j: (0, 0)), tspec, tspec],
        out_shapes=[jax.ShapeDtypeStruct((t, KV_LORA), F32), jax.ShapeDtypeStruct((t, MLA_ROPE), F32),
                    jax.ShapeDtypeStruct((t, LANES), BF16)],
        out_specs=[pl.BlockSpec((tm, KV_LORA), lambda i, j: (i, 0)),
                   pl.BlockSpec((tm, MLA_ROPE), lambda i, j: (i, 0)),
                   pl.BlockSpec((tm, LANES), lambda i, j: (i, 0))])
    q_lora = p["mla_w_dq"].shape[-1]
    cq = _linear(
        h, [p["mla_w_dq"]], n_cols=q_lora, tm=tm, tn=q_lora, w_lead=0, epilogue=_ep_rms,
        extras=[p["mla_q_norm_g"]], extra_specs=[pl.BlockSpec((1, q_lora), lambda i, j: (0, 0))],
        out_shapes=[jax.ShapeDtypeStruct((t, q_lora), BF16)],
        out_specs=[pl.BlockSpec((tm, q_lora), lambda i, j: (i, 0))])[0]
    hl = MLA_HEADS * LANES
    tn_q = _col_tile(q_lora, hl, 3)
    q_heads = jax.ShapeDtypeStruct((MLA_HEADS, t, LANES), BF16)
    q_spec = pl.BlockSpec((tn_q // LANES, tm, LANES), lambda i, j: (j, i, 0))
    qn, qr = _linear(
        cq, [sw["w_uq_nope"], sw["w_uq_rope"], sw["w_uq_rot"]], n_cols=hl, tm=tm, tn=tn_q,
        epilogue=_ep_query_heads, extras=[cos_m, sin_m], extra_specs=[tspec, tspec],
        out_shapes=[q_heads, q_heads], out_specs=[q_spec, q_spec])
    if cache_lat is None:
        kv_h = heads_proj(lat, [sw["w_ukv_flat"]], 2 * hl)
        o = _attn_prompt(qn, qr, kv_h, kr_pad, nb, l)
    else:
        ckr = _pad_last(cache_kr, LANES).astype(BF16)
        o = _attn_cached(qn, qr, cache_lat, ckr, lat, kr_pad, sw["w_uk_h"], sw["w_uv_h"], nb, l)
    xf = resid_proj(o, p["mla_w_o"], xf, gt1b)
    (h,) = _norm_mod(xf.reshape(nb, l, d), mods=[(p["norm_g"][1, 1][None], sh2, sc2)])
    xf = _peer(flat(h), sw["peer_wq_t"][1], p["peer_keys"], 1, sw["peer_u_b"], sw["peer_vt_b"],
               xf, gt2b, l)
    (y,) = _norm_mod(xf.reshape(nb, l, d), final_g=p["final_g"][None])
    return y, s_new[None], lat.reshape(nb, l, KV_LORA), kr.reshape(nb, l, MLA_ROPE)


def kernel(x_prompt, x_sample, c_prompt, c_sample, state_retention, cache_mla_latent, cache_mla_krope,
           ada_w, ada_b, norm_g, ret_w_in, ret_gn_g, ret_w_out,
           kv_ada_w, kv_ada_b, kv_norm_g, mla_w_dkv, mla_kv_norm_g, mla_w_uk, mla_w_uv,
           mla_w_dq, mla_q_norm_g, mla_w_uq, mla_w_o,
           peer_w_q, peer_keys, peer_u, peer_v, final_g):
    p = dict(norm_g=norm_g, ret_w_in=ret_w_in.astype(BF16), ret_gn_g=ret_gn_g[0][None],
             ret_w_out=ret_w_out.astype(BF16),
             kv_norm_g=kv_norm_g, mla_w_dkv=mla_w_dkv, mla_kv_norm_g=mla_kv_norm_g,
             mla_w_uk=mla_w_uk, mla_w_uv=mla_w_uv, mla_w_dq=mla_w_dq, mla_q_norm_g=mla_q_norm_g,
             mla_w_uq=mla_w_uq, mla_w_o=mla_w_o.astype(BF16), peer_w_q=peer_w_q, peer_keys=peer_keys,
             peer_u=peer_u, peer_v=peer_v, final_g=final_g)
    sw = _shared_weights(p)
    nbp, lp, d = x_prompt.shape
    nbs, ls, _ = x_sample.shape
    past = cache_mla_latent.shape[1]

    n_c = nbp + nbs
    c_all = jnp.pad(jnp.concatenate([c_prompt, c_sample], axis=0), ((0, (-n_c) % 8), (0, 0)))
    layer_mods = [_adaln(c_all, ada_w, ada_b, layer, 6 * d) for layer in range(2)]
    kv_mods = _adaln(c_all, kv_ada_w, kv_ada_b, 0, 2 * d)

    def stream_mods(lo, hi):
        return ([jnp.split(m[lo:hi], 6, axis=-1) for m in layer_mods], jnp.split(kv_mods[lo:hi], 2, axis=-1))

    mods_p, kv_p = stream_mods(0, nbp)
    mods_s, kv_s = stream_mods(nbp, n_c)
    pos_p = jnp.arange(lp, dtype=jnp.int32)
    pos_s = past + jnp.arange(ls, dtype=jnp.int32)
    y_p, ret_p, lat_p, kr_p = _trunk(x_prompt, pos_p, mods_p, kv_p, p, sw, None, None, None)
    y_s, ret_s, lat_s, kr_s = _trunk(x_sample, pos_s, mods_s, kv_s, p, sw, state_retention,
                                     cache_mla_latent, cache_mla_krope)
    return (y_p, y_s, ret_p, ret_s, lat_p, kr_p, lat_s, kr_s)
```

```python
import functools
import math

import jax
import jax.numpy as jnp
from jax import lax
from jax.experimental import pallas as pl
from jax.experimental.pallas import tpu as pltpu

F32 = jnp.float32
BF16 = jnp.bfloat16

EPS = 1e-6
ROPE_THETA = 10000.0
NEG_INF = -1e30
CHUNK = 64

RET_HEADS = 8
RET_DK = 256
RET_DV = 512

MLA_HEADS = 16
MLA_NOPE = 128
MLA_ROPE = 64
MLA_VDIM = 128
KV_LORA = 512
MLA_SCALE = (MLA_NOPE + MLA_ROPE) ** -0.5

PEER_HEADS = 8
N_KEYS = 128
PEER_TOPK = 16

LANES = 128
VMEM_LIMIT = 52 * 1024 * 1024
PEER_MIX_VMEM_LIMIT = 58 * 1024 * 1024
SELECT_HEADS_PER_STEP = 4


def _params(sem, vmem=VMEM_LIMIT):
    return pltpu.CompilerParams(dimension_semantics=sem, vmem_limit_bytes=vmem)


def _bdot(a, b):
    return jnp.dot(a.astype(BF16), b.astype(BF16), preferred_element_type=F32)


def _bdot_nt(a, b):
    return lax.dot_general(a.astype(BF16), b.astype(BF16), (((1,), (1,)), ((), ())),
                           preferred_element_type=F32)


def _bdot_tn(a, b):
    return lax.dot_general(a.astype(BF16), b.astype(BF16), (((0,), (0,)), ((), ())),
                           preferred_element_type=F32)


def _silu(x):
    return x * (1.0 / (1.0 + jnp.exp(-x)))


def _gelu(x):
    return 0.5 * x * (1.0 + lax.erf(x * (0.5 ** 0.5)))


def _linear_kernel(*refs, n_w, n_extra, prologue, epilogue):
    x_ref = refs[0]
    w_refs = refs[1:1 + n_w]
    extra_refs = refs[1 + n_w:1 + n_w + n_extra]
    out_refs = refs[1 + n_w + n_extra:]
    xv = x_ref[...]
    if prologue is not None:
        xv = prologue(xv)
    xb = xv.astype(BF16)
    accs = [jnp.dot(xb, w[...].astype(BF16), preferred_element_type=F32) for w in w_refs]
    epilogue(accs, extra_refs, out_refs)


def _linear(x, ws, *, n_cols, tm, tn, epilogue, out_shapes, out_specs, w_lead=None, col_block0=0,
            extras=(), extra_specs=(), prologue=None, name=None):
    m, k = x.shape
    if name is None:
        name = "linear" + getattr(epilogue, "func", epilogue).__name__
    assert m % tm == 0 and n_cols % tn == 0
    if w_lead is None:
        w_spec = pl.BlockSpec((k, tn), lambda i, j: (0, j + col_block0))
    else:
        w_spec = pl.BlockSpec((None, k, tn), lambda i, j: (w_lead, 0, j + col_block0))
    kern = functools.partial(_linear_kernel, n_w=len(ws), n_extra=len(extras), prologue=prologue,
                             epilogue=epilogue)
    return pl.pallas_call(
        kern,
        grid=(m // tm, n_cols // tn),
        in_specs=[pl.BlockSpec((tm, k), lambda i, j: (i, 0))] + [w_spec] * len(ws) + list(extra_specs),
        out_specs=out_specs,
        out_shape=out_shapes,
        compiler_params=_params(("parallel", "arbitrary")),
        name=name,
    )(x, *ws, *extras)


def _ep_plain(accs, extras, outs):
    outs[0][...] = accs[0].astype(outs[0].dtype)


def _ep_heads(accs, extras, outs):
    acc = accs[0]
    for jj in range(acc.shape[1] // LANES):
        outs[0][jj] = acc[:, jj * LANES:(jj + 1) * LANES].astype(outs[0].dtype)


def _ep_bias(accs, extras, outs):
    outs[0][...] = accs[0] + extras[0][...]


def _ep_resid(accs, extras, outs):
    outs[0][...] = extras[0][...] + (1.0 + extras[1][...]) * accs[0]


def _gate_operand(gate, l, rows_per_tile, cols_per_tile):
    nb, _, d = gate.shape
    col = (lambda j: j) if cols_per_tile < d else (lambda j: 0)
    if l % rows_per_tile == 0:
        tiles_per_batch = l // rows_per_tile
        return gate, pl.BlockSpec((None, 1, cols_per_tile), lambda i, j: (i // tiles_per_batch, 0, col(j)))
    rows = jnp.broadcast_to(gate, (nb, l, d)).reshape(nb * l, d)
    return rows, pl.BlockSpec((rows_per_tile, cols_per_tile), lambda i, j: (i, col(j)))


def _ep_rope_half128(accs, extras, outs, *, scale):
    acc = accs[0]
    cos = extras[0][...]
    sin = extras[1][...]
    for g in range(acc.shape[1] // 256):
        x1 = acc[:, g * 256:g * 256 + 128]
        x2 = acc[:, g * 256 + 128:(g + 1) * 256]
        outs[0][:, g * 256:g * 256 + 128] = ((x1 * cos - x2 * sin) * scale).astype(outs[0].dtype)
        outs[0][:, g * 256 + 128:(g + 1) * 256] = ((x1 * sin + x2 * cos) * scale).astype(outs[0].dtype)


def _ep_query_heads(accs, extras, outs):
    cos = extras[0][...]
    sin = extras[1][...]
    plain, a, b = accs
    for jj in range(a.shape[1] // LANES):
        sl = slice(jj * LANES, (jj + 1) * LANES)
        outs[0][jj] = plain[:, sl].astype(outs[0].dtype)
        outs[1][jj] = (a[:, sl] * cos + b[:, sl] * sin).astype(outs[1].dtype)


def _ep_rms(accs, extras, outs):
    acc = accs[0]
    g = extras[0][...]
    y = acc * lax.rsqrt(jnp.mean(acc * acc, axis=-1, keepdims=True) + EPS) * g
    outs[0][...] = y.astype(outs[0].dtype)


def _ep_kv(accs, extras, outs):
    acc = accs[0]
    g = extras[0][...]
    cos = extras[1][...]
    sin = extras[2][...]
    c = acc[:, :KV_LORA]
    outs[0][...] = c * lax.rsqrt(jnp.mean(c * c, axis=-1, keepdims=True) + EPS) * g
    kr = acc[:, KV_LORA:KV_LORA + LANES] * cos + acc[:, KV_LORA + LANES:KV_LORA + 2 * LANES] * sin
    outs[1][...] = kr[:, :MLA_ROPE]
    outs[2][...] = kr.astype(BF16)


def _row_tile(m):
    return min(m, 1024)


def _col_tile(k, n, n_weights=1):
    for tn in (1024, 512):
        if n % tn == 0 and 2 * n_weights * k * tn * 4 <= 16 * 1024 * 1024:
            return tn
    return min(n, 512)


def _table_spec(tm, table_rows):
    nblk = table_rows // tm
    return pl.BlockSpec((tm, LANES), lambda i, j: (i % nblk, 0))


def _norm_mod_kernel(*refs, n_mod):
    x = refs[0][...]
    xn = x * lax.rsqrt(jnp.mean(x * x, axis=-1, keepdims=True) + EPS)
    if n_mod == 0:
        refs[2][...] = xn * refs[1][...]
        return
    outs = refs[1 + 3 * n_mod:]
    for m in range(n_mod):
        g, sh, sc = (refs[1 + 3 * m + k][...] for k in range(3))
        outs[m][...] = ((xn * g) * (1.0 + sc) + sh).astype(outs[m].dtype)


def _norm_mod(x, mods=(), final_g=None):
    nb, l, d = x.shape
    tl = min(l, 512)
    tok = pl.BlockSpec((None, tl, d), lambda b, i: (b, i, 0))
    per_b = pl.BlockSpec((None, 1, d), lambda b, i: (b, 0, 0))
    gain = pl.BlockSpec((1, d), lambda b, i: (0, 0))
    args, specs = [x], [tok]
    if final_g is not None:
        assert not mods
        args.append(final_g)
        specs.append(gain)
        out_shapes = [jax.ShapeDtypeStruct((nb, l, d), F32)]
    else:
        for g, sh, sc in mods:
            args += [g, sh, sc]
            specs += [gain, per_b, per_b]
        out_shapes = [jax.ShapeDtypeStruct((nb, l, d), BF16) for _ in mods]
    return pl.pallas_call(
        functools.partial(_norm_mod_kernel, n_mod=len(mods)),
        grid=(nb, l // tl), in_specs=specs, out_specs=[tok] * len(out_shapes), out_shape=out_shapes,
        compiler_params=_params(("parallel", "parallel")),
        name="norm_mod",
    )(*args)


def _ret_log_decay():
    return jnp.log1p(-jnp.exp2(-5.0 - jnp.arange(RET_HEADS, dtype=F32)))


def _retention_tables(c):
    log_g = _ret_log_decay()[:, None, None]
    n = jnp.arange(c, dtype=F32)
    dist = n[:, None] - n[None, :]
    same = (jnp.arange(c)[:, None] // CHUNK) == (jnp.arange(c)[None, :] // CHUNK)
    earlier = (jnp.arange(c)[None, :] // CHUNK) < (jnp.arange(c)[:, None] // CHUNK)
    mask = jnp.where(same[None], jnp.exp(jnp.abs(dist)[None] * log_g),
                     jnp.where(earlier[None], jnp.exp(dist[None] * log_g), 0.0))
    q_decay = jnp.exp((n[None, :, None] + 1.0) * log_g)
    k_decay = jnp.exp((c - 1.0 - n)[None, :, None] * log_g)
    blk_decay = jnp.exp(c * log_g)
    return mask, q_decay, k_decay, blk_decay


def _retention_kernel(*refs, has_s0, hps):
    if has_s0:
        (q_ref, k_ref, v_ref, g_ref, mask_ref, qd_ref, kd_ref, bd_ref, gn_ref, s0_ref,
         y_ref, s_out_ref, s_ref) = refs
    else:
        (q_ref, k_ref, v_ref, g_ref, mask_ref, qd_ref, kd_ref, bd_ref, gn_ref,
         y_ref, s_out_ref, s_ref) = refs
    c = pl.program_id(2)

    @pl.when(c == 0)
    def _():
        if has_s0:
            s_ref[...] = s0_ref[...]
        else:
            s_ref[...] = jnp.zeros_like(s_ref)

    for hh in range(hps):
        qk_cols = slice(hh * RET_DK, (hh + 1) * RET_DK)
        v_cols = slice(hh * RET_DV, (hh + 1) * RET_DV)
        q = q_ref[:, qk_cols]
        k = k_ref[:, qk_cols]
        v = v_ref[:, v_cols]
        s_prev = s_ref[hh]
        scores = _bdot_nt(q, k) * mask_ref[hh]
        y = _bdot(scores, v) + _bdot(q, s_prev) * qd_ref[hh]
        k_scaled = k.astype(F32) * kd_ref[hh]
        s_ref[hh] = bd_ref[hh] * s_prev + _bdot_tn(k_scaled, v)

        mu = jnp.mean(y, axis=-1, keepdims=True)
        yc = y - mu
        var = jnp.mean(yc * yc, axis=-1, keepdims=True)
        yn = yc * lax.rsqrt(var + EPS) * gn_ref[:, v_cols]
        y_ref[:, v_cols] = (_silu(g_ref[:, v_cols].astype(F32)) * yn).astype(y_ref.dtype)

    @pl.when(c == pl.num_programs(2) - 1)
    def _():
        s_out_ref[...] = s_ref[...]


def _retention(q, k, v, g, gn_g, s0, nb, l):
    cb = min(l, 256)
    nc = l // cb
    hps = 4 if nc > 1 else RET_HEADS
    mask, qd, kd, bd = _retention_tables(cb)
    row = lambda b, hg, c: (b * nc + c, hg)
    per_h3 = lambda b, hg, c: (hg, 0, 0)
    in_specs = [
        pl.BlockSpec((cb, hps * RET_DK), row), pl.BlockSpec((cb, hps * RET_DK), row),
        pl.BlockSpec((cb, hps * RET_DV), row), pl.BlockSpec((cb, hps * RET_DV), row),
        pl.BlockSpec((hps, cb, cb), per_h3), pl.BlockSpec((hps, cb, 1), per_h3),
        pl.BlockSpec((hps, cb, 1), per_h3), pl.BlockSpec((hps, 1, 1), per_h3),
        pl.BlockSpec((1, hps * RET_DV), lambda b, hg, c: (0, hg)),
    ]
    args = [q, k, v, g, mask, qd, kd, bd, gn_g]
    state_spec = pl.BlockSpec((None, hps, RET_DK, RET_DV), lambda b, hg, c: (b, hg, 0, 0))
    if s0 is not None:
        in_specs.append(state_spec)
        args.append(s0)
    y, s_new = pl.pallas_call(
        functools.partial(_retention_kernel, has_s0=s0 is not None, hps=hps),
        grid=(nb, RET_HEADS // hps, nc),
        in_specs=in_specs,
        out_specs=[pl.BlockSpec((cb, hps * RET_DV), row), state_spec],
        out_shape=[jax.ShapeDtypeStruct((nb * l, RET_HEADS * RET_DV), BF16),
                   jax.ShapeDtypeStruct((nb, RET_HEADS, RET_DK, RET_DV), F32)],
        scratch_shapes=[pltpu.VMEM((hps, RET_DK, RET_DV), F32)],
        compiler_params=_params(("parallel", "parallel", "arbitrary")),
        name="retention",
    )(*args)
    return y, s_new


def _attn_prompt_kernel(qi_ref, ki_ref, qn_ref, qr_ref, kn_ref, v_ref, kr_ref, o_ref, m_ref, acc_ref, *, tq, tk):
    qi = qi_ref[pl.program_id(1)]
    ki = ki_ref[pl.program_id(1)]
    exp2_scale = MLA_SCALE * math.log2(math.e)

    @pl.when(ki == 0)
    def _():
        m_ref[...] = jnp.full_like(m_ref, NEG_INF)
        acc_ref[...] = jnp.zeros_like(acc_ref)

    def block(masked):
        if masked:
            q_chunk = (qi * tq + lax.broadcasted_iota(jnp.int32, (tq, tk), 0)) // CHUNK
            k_chunk = (ki * tk + lax.broadcasted_iota(jnp.int32, (tq, tk), 1)) // CHUNK
            visible = k_chunk <= q_chunk
        kr = kr_ref[...]
        ones = jnp.ones((tk, LANES), BF16)

        def head(h, carry):
            q = jnp.concatenate([qn_ref[h], qr_ref[h]], axis=1)
            k = jnp.concatenate([kn_ref[h], kr], axis=1)
            s = _bdot_nt(q, k)
            if masked:
                s = jnp.where(visible, s, NEG_INF)
            m_prev = m_ref[h]
            m_new = jnp.maximum(m_prev, jnp.max(s, axis=-1, keepdims=True))
            alpha = jnp.exp2((m_prev - m_new) * exp2_scale)
            p = jnp.exp2((s - jnp.tile(m_new, (1, tk // LANES))) * exp2_scale)
            v_ext = jnp.concatenate([v_ref[h], ones], axis=1)
            acc_ref[h] = jnp.tile(alpha, (1, 2)) * acc_ref[h] + _bdot(p, v_ext)
            m_ref[h] = m_new
            return carry

        lax.fori_loop(0, MLA_HEADS, head, 0, unroll=True)

    @pl.when(ki < qi)
    def _():
        block(False)

    @pl.when(ki == qi)
    def _():
        block(True)
        for h in range(MLA_HEADS):
            acc = acc_ref[h]
            o_ref[:, h * MLA_VDIM:(h + 1) * MLA_VDIM] = (
                acc[:, :MLA_VDIM] / acc[:, MLA_VDIM:]).astype(o_ref.dtype)


def _attn_prompt(qn, qr, kv, kr, nb, l):
    tq = tk = min(l, 512)
    nq = l // tq
    pairs = [(qi, ki) for qi in range(nq) for ki in range(qi + 1)]
    qi_of = jnp.asarray([qk[0] for qk in pairs], jnp.int32)
    ki_of = jnp.asarray([qk[1] for qk in pairs], jnp.int32)
    q_spec = pl.BlockSpec((MLA_HEADS, tq, LANES), lambda b, s, qi, ki: (0, b * nq + qi[s], 0))
    k_spec = pl.BlockSpec((MLA_HEADS, tk, LANES), lambda b, s, qi, ki: (0, b * nq + ki[s], 0))
    v_spec = pl.BlockSpec((MLA_HEADS, tk, LANES), lambda b, s, qi, ki: (1, b * nq + ki[s], 0))
    return pl.pallas_call(
        functools.partial(_attn_prompt_kernel, tq=tq, tk=tk),
        grid_spec=pltpu.PrefetchScalarGridSpec(
            num_scalar_prefetch=2,
            grid=(nb, len(pairs)),
            in_specs=[q_spec, q_spec, k_spec, v_spec,
                      pl.BlockSpec((tk, LANES), lambda b, s, qi, ki: (b * nq + ki[s], 0))],
            out_specs=pl.BlockSpec((tq, MLA_HEADS * MLA_VDIM), lambda b, s, qi, ki: (b * nq + qi[s], 0)),
            scratch_shapes=[pltpu.VMEM((MLA_HEADS, tq, LANES), F32),
                            pltpu.VMEM((MLA_HEADS, tq, MLA_VDIM + LANES), F32)]),
        out_shape=jax.ShapeDtypeStruct((nb * l, MLA_HEADS * MLA_VDIM), BF16),
        compiler_params=_params(("parallel", "arbitrary")),
        name="attn_prompt",
    )(qi_of, ki_of, qn, qr, kv, kv, kr)


def _attn_cached_kernel(qn_ref, qr_ref, clat_ref, ckr_ref, nlat_ref, nkr_ref, wuk_ref, wuv_ref,
                        o_ref, ql_ref, qrs_ref, *, lq):
    for h in range(MLA_HEADS):
        ql_ref[h * lq:(h + 1) * lq, :] = _bdot_nt(qn_ref[h], wuk_ref[h]).astype(BF16)
        qrs_ref[h * lq:(h + 1) * lq, :] = qr_ref[h]
    ql = ql_ref[...]
    qr = qrs_ref[...]
    clat = clat_ref[...].astype(BF16)
    nlat = nlat_ref[...].astype(BF16)
    s_c = (_bdot_nt(ql, clat) + _bdot_nt(qr, ckr_ref[...])) * MLA_SCALE
    s_n = (_bdot_nt(ql, nlat) + _bdot_nt(qr, nkr_ref[...])) * MLA_SCALE
    m = jnp.maximum(jnp.max(s_c, axis=-1, keepdims=True), jnp.max(s_n, axis=-1, keepdims=True))
    p_c = jnp.exp(s_c - m)
    p_n = jnp.exp(s_n - m)
    denom = jnp.sum(p_c, axis=-1, keepdims=True) + jnp.sum(p_n, axis=-1, keepdims=True)
    o_lat = (_bdot(p_c, clat) + _bdot(p_n, nlat)) / denom
    for h in range(MLA_HEADS):
        o_ref[:, h * MLA_VDIM:(h + 1) * MLA_VDIM] = _bdot(
            o_lat[h * lq:(h + 1) * lq, :], wuv_ref[h]).astype(o_ref.dtype)


def _attn_cached(qn, qr, cache_lat, cache_kr_pad, new_lat, new_kr_pad, wuk_h, wuv_h, nb, lq):
    past = cache_lat.shape[1]
    whole = lambda b: (0, 0, 0)
    return pl.pallas_call(
        functools.partial(_attn_cached_kernel, lq=lq),
        grid=(nb,),
        in_specs=[
            pl.BlockSpec((MLA_HEADS, lq, LANES), lambda b: (0, b, 0)),
            pl.BlockSpec((MLA_HEADS, lq, LANES), lambda b: (0, b, 0)),
            pl.BlockSpec((None, past, KV_LORA), lambda b: (b, 0, 0)),
            pl.BlockSpec((None, past, LANES), lambda b: (b, 0, 0)),
            pl.BlockSpec((lq, KV_LORA), lambda b: (b, 0)),
            pl.BlockSpec((lq, LANES), lambda b: (b, 0)),
            pl.BlockSpec((MLA_HEADS, KV_LORA, MLA_NOPE), whole),
            pl.BlockSpec((MLA_HEADS, KV_LORA, MLA_VDIM), whole),
        ],
        out_specs=pl.BlockSpec((lq, MLA_HEADS * MLA_VDIM), lambda b: (b, 0)),
        out_shape=jax.ShapeDtypeStruct((nb * lq, MLA_HEADS * MLA_VDIM), BF16),
        scratch_shapes=[pltpu.VMEM((MLA_HEADS * lq, KV_LORA), BF16),
                        pltpu.VMEM((MLA_HEADS * lq, LANES), BF16)],
        compiler_params=_params(("parallel",)),
        name="attn_cached",
    )(qn, qr, cache_lat, cache_kr_pad, new_lat, new_kr_pad, wuk_h, wuv_h)


def _top_values(s, top_ref, want_rank, one_per_round=False):
    rank = jnp.full(s.shape, float(PEER_TOPK), F32) if want_rank else None
    rows = lax.broadcasted_iota(jnp.int32, s.shape, 0).astype(F32) if one_per_round else None
    for r in range(PEER_TOPK):
        m = jnp.max(s, axis=0, keepdims=True)
        top_ref[r:r + 1, :] = m
        hit = s == m
        if one_per_round:
            first = jnp.min(jnp.where(hit, rows, float(s.shape[0])), axis=0, keepdims=True)
            hit = rows == first
        if want_rank:
            rank = jnp.where(hit, float(r), rank)
        s = jnp.where(hit, -jnp.inf, s)
    return rank, m


_CAND_LIMITS = tuple(PEER_TOPK // (b + 1) for b in range(1, 8))
_HALF_TOPK = PEER_TOPK // 2


def _peer_select_kernel(h_ref, wq_ref, keys_ref, r1_ref, e1_ref, c0_ref, e0_ref, s_ref, a0_ref, a1_ref,
                        n_ref, *, heads_per_step):
    hx = h_ref[...]
    for hh in range(heads_per_step):
        q_t = _bdot_nt(wq_ref[hh * 2 * N_KEYS:(hh + 1) * 2 * N_KEYS, :], hx)
        s_ref[2 * hh] = _bdot(keys_ref[hh, 0], q_t[:N_KEYS])
        s_ref[2 * hh + 1] = _bdot(keys_ref[hh, 1], q_t[N_KEYS:])
    tied = [_peer_select_head(s_ref[2 * hh], s_ref[2 * hh + 1], r1_ref.at[hh], e1_ref.at[hh],
                              c0_ref.at[hh], e0_ref.at[hh], a0_ref.at[hh], a1_ref.at[hh])
            for hh in range(heads_per_step)]

    def redo(hh):
        _peer_select_head_ties(s_ref[2 * hh], s_ref[2 * hh + 1], r1_ref.at[hh], e1_ref.at[hh],
                               c0_ref.at[hh], e0_ref.at[hh], a0_ref.at[hh], a1_ref.at[hh], n_ref)

    for hh in range(heads_per_step):
        pl.when(jnp.max(tied[hh].astype(F32)) > 0.0)(functools.partial(redo, hh))


def _peer_select_head(s0, s1, r1_ref, e1_ref, c0_ref, e0_ref, a0_ref, a1_ref):
    _, last0 = _top_values(s0, a0_ref, False)
    rank1, _ = _top_values(s1, a1_ref, True)
    removed0 = jnp.sum((s0 >= last0).astype(F32), axis=0, keepdims=True)
    removed1 = jnp.sum((rank1 < float(PEER_TOPK)).astype(F32), axis=0, keepdims=True)
    a0 = a0_ref[...]
    a1 = a1_ref[...]
    a0_lo = a0[:_HALF_TOPK, :]
    row = lax.broadcasted_iota(jnp.int32, a0_lo.shape, 0)
    pieces = [a1[0:1, :] + a0]
    for b, limit in enumerate(_CAND_LIMITS, start=1):
        pieces.append(jnp.where(row < limit, a1[b:b + 1, :] + a0_lo, -jnp.inf))
    pieces.append(a1[_HALF_TOPK:, :] + a0[0:1, :])
    cand = jnp.concatenate(pieces, axis=0)
    c = cand
    for r in range(PEER_TOPK):
        tau = jnp.max(c, axis=0, keepdims=True)
        if r + 1 < PEER_TOPK:
            c = jnp.where(c == tau, -jnp.inf, c)
    sel = cand >= tau
    cmax = a0[0:1, :] + a1[0:1, :]
    z = jnp.sum(jnp.where(sel, jnp.exp(cand - cmax), 0.0), axis=0, keepdims=True)
    self32 = sel.astype(F32)
    n_hi = self32[_HALF_TOPK:PEER_TOPK, :]
    n_lo = self32[:_HALF_TOPK, :]
    for b in range(1, 8):
        lo = PEER_TOPK + (b - 1) * _HALF_TOPK
        n_lo = n_lo + self32[lo:lo + _HALF_TOPK, :]
    tail = jnp.sum(self32[PEER_TOPK + 7 * _HALF_TOPK:, :], axis=0, keepdims=True)
    n_lo = n_lo + jnp.where(row == 0, tail, 0.0)
    count0 = jnp.zeros(s0.shape, F32)
    for a in range(PEER_TOPK):
        n_a = n_lo[a:a + 1, :] if a < _HALF_TOPK else n_hi[a - _HALF_TOPK:a - _HALF_TOPK + 1, :]
        count0 = jnp.where(s0 == a0[a:a + 1, :], n_a, count0)
    r1_ref[...] = rank1.astype(r1_ref.dtype)
    c0_ref[...] = count0
    e0_ref[...] = jnp.exp(s0 - a0[0:1, :]) / z
    e1_ref[...] = jnp.exp(s1 - a1[0:1, :]).astype(e1_ref.dtype)

    n_sel = jnp.sum(self32, axis=0, keepdims=True)
    k = float(PEER_TOPK)
    return (removed0 != k) | (removed1 != k) | (n_sel != k)


def _peer_select_head_ties(s0, s1, r1_ref, e1_ref, c0_ref, e0_ref, a0_ref, a1_ref, n_ref):
    rank0, _ = _top_values(s0, a0_ref, True, one_per_round=True)
    rank1, _ = _top_values(s1, a1_ref, True, one_per_round=True)
    a0 = a0_ref[...]
    a1 = a1_ref[...]
    cand = jnp.concatenate([a0[a:a + 1, :] + a1 for a in range(PEER_TOPK)], axis=0)
    rows = lax.broadcasted_iota(jnp.int32, cand.shape, 0).astype(F32)
    c = cand
    sel = jnp.zeros(cand.shape, F32)
    for _ in range(PEER_TOPK):
        m = jnp.max(c, axis=0, keepdims=True)
        first = jnp.min(jnp.where(c == m, rows, float(cand.shape[0])), axis=0, keepdims=True)
        hit = rows == first
        sel = jnp.where(hit, 1.0, sel)
        c = jnp.where(hit, -jnp.inf, c)
    cmax = a0[0:1, :] + a1[0:1, :]
    z = jnp.sum(sel * jnp.exp(cand - cmax), axis=0, keepdims=True)
    for a in range(PEER_TOPK):
        n_ref[a:a + 1, :] = jnp.sum(sel[a * PEER_TOPK:(a + 1) * PEER_TOPK, :], axis=0, keepdims=True)
    n = n_ref[...]
    count0 = jnp.zeros(s0.shape, F32)
    for a in range(PEER_TOPK):
        count0 = jnp.where(rank0 == float(a), n[a:a + 1, :], count0)
    r1_ref[...] = rank1.astype(r1_ref.dtype)
    c0_ref[...] = count0
    e0_ref[...] = jnp.exp(s0 - a0[0:1, :]) / z
    e1_ref[...] = jnp.exp(s1 - a1[0:1, :]).astype(e1_ref.dtype)


def _peer_select(hx, wq_t, keys, layer, tt):
    t, d = hx.shape
    hps = SELECT_HEADS_PER_STEP
    o_spec = pl.BlockSpec((hps, N_KEYS, tt), lambda i, g: (g, 0, i))
    return pl.pallas_call(
        functools.partial(_peer_select_kernel, heads_per_step=hps),
        grid=(t // tt, PEER_HEADS // hps),
        in_specs=[pl.BlockSpec((tt, d), lambda i, g: (i, 0)),
                  pl.BlockSpec((hps * 2 * N_KEYS, d), lambda i, g: (g, 0)),
                  pl.BlockSpec((None, hps, 2, N_KEYS, N_KEYS), lambda i, g: (layer, g, 0, 0, 0))],
        out_specs=[o_spec] * 4,
        out_shape=[jax.ShapeDtypeStruct((PEER_HEADS, N_KEYS, t), dt) for dt in (BF16, BF16, F32, F32)],
        scratch_shapes=[pltpu.VMEM((2 * hps, N_KEYS, tt), F32),
                        pltpu.VMEM((hps, PEER_TOPK, tt), F32), pltpu.VMEM((hps, PEER_TOPK, tt), F32),
                        pltpu.VMEM((PEER_TOPK, tt), F32)],
        compiler_params=_params(("parallel", "arbitrary")),
        name="peer_select",
    )(hx, wq_t, keys)


def _sublane_bcast_bf16(row, rows):
    tile_rows = 16
    packed = jnp.broadcast_to(row, (tile_rows, row.shape[1])).astype(BF16)
    return jnp.tile(packed, (rows // tile_rows, 1))


def _peer_mix_kernel(x_ref, u_ref, vt_ref, r1_ref, e1_ref, c0_ref, e0_ref, res_ref, gate_ref, o_ref,
                     acc_ref, a_ref, g_ref, p_ref, *, n_i):
    j = pl.program_id(1)

    @pl.when(j == 0)
    def _():
        acc_ref[...] = jnp.zeros_like(acc_ref)

    x = x_ref[...]
    rows_per_chunk = 2 * N_KEYS
    for c in range(n_i // 2):
        lo = c * rows_per_chunk
        a_ref[c] = _bdot_nt(u_ref[lo:lo + rows_per_chunk, :], x)
    zero = jnp.zeros((), BF16)
    for ii in range(n_i):
        w = None
        for h in range(PEER_HEADS):
            count = _sublane_bcast_bf16(c0_ref[h, ii:ii + 1, :], N_KEYS)
            e0 = _sublane_bcast_bf16(e0_ref[h, ii:ii + 1, :], N_KEYS)
            term = jnp.where(r1_ref[h] < count, e1_ref[h] * e0, zero)
            w = term if w is None else w + term
        g_ref[ii] = w
    for c in range(n_i // 2):
        lo = c * rows_per_chunk
        gates = jnp.concatenate([g_ref[2 * c], g_ref[2 * c + 1]], axis=0)
        p_ref[lo:lo + rows_per_chunk, :] = gates * _gelu(a_ref[c]).astype(BF16)
    acc_ref[...] += jnp.dot(vt_ref[...], p_ref[...], preferred_element_type=F32)

    @pl.when(j == pl.num_programs(1) - 1)
    def _():
        o_ref[...] = res_ref[...] + (1.0 + gate_ref[...]) * acc_ref[...].T


def _peer_mix(hx, u_b, vt_b, layer, r1, e1, c0, e0, x_res, gate, l, tt, et):
    t, d = hx.shape
    e = u_b.shape[1]
    n_i = et // N_KEYS
    tok_all = pl.BlockSpec((PEER_HEADS, N_KEYS, tt), lambda i, j: (0, 0, i))
    tok_i = pl.BlockSpec((PEER_HEADS, n_i, tt), lambda i, j: (0, j, i))
    gate_arr, gate_spec = _gate_operand(gate, l, tt, d)
    return pl.pallas_call(
        functools.partial(_peer_mix_kernel, n_i=n_i),
        grid=(t // tt, e // et),
        in_specs=[pl.BlockSpec((tt, d), lambda i, j: (i, 0)),
                  pl.BlockSpec((None, et, d), lambda i, j: (layer, j, 0)),
                  pl.BlockSpec((None, d, et), lambda i, j: (layer, 0, j)),
                  tok_all, tok_all, tok_i, tok_i,
                  pl.BlockSpec((tt, d), lambda i, j: (i, 0)), gate_spec],
        out_specs=pl.BlockSpec((tt, d), lambda i, j: (i, 0)),
        out_shape=jax.ShapeDtypeStruct((t, d), F32),
        scratch_shapes=[pltpu.VMEM((d, tt), F32),
                        pltpu.VMEM((n_i // 2, 2 * N_KEYS, tt), F32),
                        pltpu.VMEM((n_i, N_KEYS, tt), BF16),
                        pltpu.VMEM((et, tt), BF16)],
        compiler_params=_params(("parallel", "arbitrary"), vmem=PEER_MIX_VMEM_LIMIT),
        name="peer_mix",
    )(hx, u_b, vt_b, r1, e1, c0, e0, x_res, gate_arr)


def _peer(hx, wq_t, keys, layer, u_b, vt_b, x_res, gate, l):
    t = hx.shape[0]
    tt = min(t, 512)
    r1, e1, c0, e0 = _peer_select(hx, wq_t, keys, layer, tt)
    return _peer_mix(hx, u_b, vt_b, layer, r1, e1, c0, e0, x_res, gate, l, tt, 1024)


def _rope_tables(pos, half, reps):
    inv = ROPE_THETA ** (-jnp.arange(half, dtype=F32) / half)
    ang = pos.astype(F32)[:, None] * inv[None, :]
    return jnp.tile(jnp.cos(ang), (1, reps)), jnp.tile(jnp.sin(ang), (1, reps))


def _rotate_half_cols(w):
    half = w.shape[-1] // 2
    return jnp.concatenate([-w[..., half:], w[..., :half]], axis=-1)


def _pad_last(w, width):
    return jnp.pad(w, [(0, 0)] * (w.ndim - 1) + [(0, width - w.shape[-1])])


def _shared_weights(p):
    out = {}
    w_rope = p["mla_w_dkv"][:, KV_LORA:]
    out["w_dkv_ext"] = jnp.concatenate(
        [p["mla_w_dkv"][:, :KV_LORA], _pad_last(w_rope, LANES), _pad_last(_rotate_half_cols(w_rope), LANES)],
        axis=1)
    w_uq = p["mla_w_uq"][0].reshape(-1, MLA_HEADS, MLA_NOPE + MLA_ROPE)
    q_lora = w_uq.shape[0]
    out["w_uq_nope"] = w_uq[:, :, :MLA_NOPE].reshape(q_lora, MLA_HEADS * MLA_NOPE)
    w_qr = w_uq[:, :, MLA_NOPE:]
    out["w_uq_rope"] = _pad_last(w_qr, LANES).reshape(q_lora, MLA_HEADS * LANES)
    out["w_uq_rot"] = _pad_last(_rotate_half_cols(w_qr), LANES).reshape(q_lora, MLA_HEADS * LANES)
    out["w_ukv_flat"] = jnp.concatenate([p["mla_w_uk"].reshape(KV_LORA, MLA_HEADS * MLA_NOPE),
                                         p["mla_w_uv"].reshape(KV_LORA, MLA_HEADS * MLA_VDIM)], axis=1)
    out["w_uk_h"] = jnp.transpose(p["mla_w_uk"], (1, 0, 2))
    out["w_uv_h"] = jnp.transpose(p["mla_w_uv"], (1, 0, 2))
    out["peer_wq_t"] = [p["peer_w_q"][l].T.astype(BF16) for l in range(2)]
    for name in ("w_dkv_ext", "w_uq_nope", "w_uq_rope", "w_uq_rot", "w_ukv_flat"):
        out[name] = out[name].astype(BF16)
    out["peer_u_b"] = p["peer_u"].astype(BF16)
    out["peer_vt_b"] = jnp.swapaxes(p["peer_v"].astype(BF16), 1, 2)
    return out


def _adaln(c_all, w, b, lead, n_out):
    m, d = c_all.shape
    tn = 2048
    if w.ndim == 2:
        w = w[None]
        b = b[None]
    b3 = b.reshape(b.shape[0], 1, n_out)
    return _linear(
        c_all, [w], n_cols=n_out, tm=m, tn=tn, epilogue=_ep_bias, prologue=_silu, w_lead=lead,
        extras=[b3], extra_specs=[pl.BlockSpec((None, 1, tn), lambda i, j: (lead, 0, j))],
        out_shapes=[jax.ShapeDtypeStruct((m, n_out), F32)],
        out_specs=[pl.BlockSpec((m, tn), lambda i, j: (0, j))])[0]


def _trunk(x, pos, mods, kv_mod, p, sw, ret_state, cache_lat, cache_kr):
    nb, l, d = x.shape
    t = nb * l
    pos_rows = jnp.tile(pos, nb) if l < 256 else pos
    table_rows = pos_rows.shape[0]
    assert table_rows % _row_tile(t) == 0, "a row tile must cover whole sequences or divide one"

    def vec(a):
        return a.reshape(nb, 1, d)

    def flat(a):
        return a.reshape(t, a.shape[-1])

    sh1, sc1, gt1, sh2, sc2, gt2 = [vec(m) for m in mods[0]]
    (h,) = _norm_mod(x, mods=[(p["norm_g"][0, 0][None], sh1, sc1)])
    h = flat(h)
    tm = _row_tile(t)
    cos_r, sin_r = _rope_tables(pos_rows, RET_DK // 2, 1)
    tspec = _table_spec(tm, table_rows)
    w_in = p["ret_w_in"]
    qk_cols = RET_HEADS * RET_DK
    v_cols = RET_HEADS * RET_DV
    tn = _col_tile(d, qk_cols)

    def rope_proj(col0, scale):
        return _linear(
            h, [w_in], n_cols=qk_cols, tm=tm, tn=tn, w_lead=0, col_block0=col0 // tn,
            epilogue=functools.partial(_ep_rope_half128, scale=scale),
            extras=[cos_r, sin_r], extra_specs=[tspec, tspec],
            out_shapes=[jax.ShapeDtypeStruct((t, qk_cols), BF16)],
            out_specs=[pl.BlockSpec((tm, tn), lambda i, j: (i, j))])[0]

    def plain_proj(x_in, w, n_cols, col0=0, lead=None, dtype=BF16):
        tm_ = _row_tile(x_in.shape[0])
        tn_ = _col_tile(x_in.shape[1], n_cols)
        return _linear(
            x_in, [w], n_cols=n_cols, tm=tm_, tn=tn_, w_lead=lead, col_block0=col0 // tn_,
            epilogue=_ep_plain,
            out_shapes=[jax.ShapeDtypeStruct((x_in.shape[0], n_cols), dtype)],
            out_specs=[pl.BlockSpec((tm_, tn_), lambda i, j: (i, j))])[0]

    def resid_proj(x_in, w, x_res, gate):
        tm_ = _row_tile(x_in.shape[0])
        tn_ = _col_tile(x_in.shape[1], d)
        gate_arr, gate_spec = _gate_operand(gate, l, tm_, tn_)
        tile = pl.BlockSpec((tm_, tn_), lambda i, j: (i, j))
        return _linear(
            x_in, [w], n_cols=d, tm=tm_, tn=tn_, w_lead=0, epilogue=_ep_resid,
            extras=[x_res, gate_arr], extra_specs=[tile, gate_spec],
            out_shapes=[jax.ShapeDtypeStruct((t, d), F32)], out_specs=[tile])[0]

    def heads_proj(x_in, ws, n_cols, epilogue=_ep_heads, extras=(), extra_specs=()):
        tm_ = _row_tile(x_in.shape[0])
        tn_ = _col_tile(x_in.shape[1], n_cols, len(ws))
        return _linear(
            x_in, ws, n_cols=n_cols, tm=tm_, tn=tn_, epilogue=epilogue, extras=extras, extra_specs=extra_specs,
            out_shapes=[jax.ShapeDtypeStruct((n_cols // LANES, x_in.shape[0], LANES), BF16)],
            out_specs=[pl.BlockSpec((tn_ // LANES, tm_, LANES), lambda i, j: (j, i, 0))])[0]

    q_r = rope_proj(0, 1.0)
    k_r = rope_proj(qk_cols, RET_DK ** -0.5)
    v_r = plain_proj(h, w_in, v_cols, col0=2 * qk_cols, lead=0)
    g_r = plain_proj(h, w_in, v_cols, col0=2 * qk_cols + v_cols, lead=0)
    s0 = None if ret_state is None else ret_state[0]
    y_r, s_new = _retention(q_r, k_r, v_r, g_r, p["ret_gn_g"], s0, nb, l)
    xf = resid_proj(y_r, p["ret_w_out"], flat(x), gt1)
    (h,) = _norm_mod(xf.reshape(nb, l, d), mods=[(p["norm_g"][0, 1][None], sh2, sc2)])
    xf = _peer(flat(h), sw["peer_wq_t"][0], p["peer_keys"], 0, sw["peer_u_b"], sw["peer_vt_b"],
               xf, gt2, l)

    sh1, sc1, gt1b, sh2, sc2, gt2b = [vec(m) for m in mods[1]]
    kv_sh, kv_sc = [vec(m) for m in kv_mod]
    h, h_kv = _norm_mod(xf.reshape(nb, l, d),
                         mods=[(p["norm_g"][1, 0][None], sh1, sc1), (p["kv_norm_g"][None], kv_sh, kv_sc)])
    h = flat(h)
    h_kv = flat(h_kv)
    cos_m, sin_m = _rope_tables(pos_rows, MLA_ROPE // 2, LANES // (MLA_ROPE // 2))
    n_ext = sw["w_dkv_ext"].shape[1]
    lat, kr, kr_pad = _linear(
        h_kv, [sw["w_dkv_ext"]], n_cols=n_ext, tm=tm, tn=n_ext, epilogue=_ep_kv,
        extras=[p["mla_kv_norm_g"][None], cos_m, sin_m],
        extra_specs=[pl.BlockSpec((1, KV_LORA), lambda i, j: (0, 0)), tspec, tspec],
        out_shapes=[jax.ShapeDtypeStruct((t, KV_LORA), F32), jax.ShapeDtypeStruct((t, MLA_ROPE), F32),
                    jax.ShapeDtypeStruct((t, LANES), BF16)],
        out_specs=[pl.BlockSpec((tm, KV_LORA), lambda i, j: (i, 0)),
                   pl.BlockSpec((tm, MLA_ROPE), lambda i, j: (i, 0)),
                   pl.BlockSpec((tm, LANES), lambda i, j: (i, 0))])
    q_lora = p["mla_w_dq"].shape[-1]
    cq = _linear(
        h, [p["mla_w_dq"]], n_cols=q_lora, tm=tm, tn=q_lora, w_lead=0, epilogue=_ep_rms,
        extras=[p["mla_q_norm_g"]], extra_specs=[pl.BlockSpec((1, q_lora), lambda i, j: (0, 0))],
        out_shapes=[jax.ShapeDtypeStruct((t, q_lora), BF16)],
        out_specs=[pl.BlockSpec((tm, q_lora), lambda i, j: (i, 0))])[0]
    hl = MLA_HEADS * LANES
    tn_q = _col_tile(q_lora, hl, 3)
    q_heads = jax.ShapeDtypeStruct((MLA_HEADS, t, LANES), BF16)
    q_spec = pl.BlockSpec((tn_q // LANES, tm, LANES), lambda i, j: (j, i, 0))
    qn, qr = _linear(
        cq, [sw["w_uq_nope"], sw["w_uq_rope"], sw["w_uq_rot"]], n_cols=hl, tm=tm, tn=tn_q,
        epilogue=_ep_query_heads, extras=[cos_m, sin_m], extra_specs=[tspec, tspec],
        out_shapes=[q_heads, q_heads], out_specs=[q_spec, q_spec])
    if cache_lat is None:
        kv_h = heads_proj(lat, [sw["w_ukv_flat"]], 2 * hl)
        o = _attn_prompt(qn, qr, kv_h, kr_pad, nb, l)
    else:
        ckr = _pad_last(cache_kr, LANES).astype(BF16)
        o = _attn_cached(qn, qr, cache_lat, ckr, lat, kr_pad, sw["w_uk_h"], sw["w_uv_h"], nb, l)
    xf = resid_proj(o, p["mla_w_o"], xf, gt1b)
    (h,) = _norm_mod(xf.reshape(nb, l, d), mods=[(p["norm_g"][1, 1][None], sh2, sc2)])
    xf = _peer(flat(h), sw["peer_wq_t"][1], p["peer_keys"], 1, sw["peer_u_b"], sw["peer_vt_b"],
               xf, gt2b, l)
    (y,) = _norm_mod(xf.reshape(nb, l, d), final_g=p["final_g"][None])
    return y, s_new[None], lat.reshape(nb, l, KV_LORA), kr.reshape(nb, l, MLA_ROPE)


def kernel(x_prompt, x_sample, c_prompt, c_sample, state_retention, cache_mla_latent, cache_mla_krope,
           ada_w, ada_b, norm_g, ret_w_in, ret_gn_g, ret_w_out,
           kv_ada_w, kv_ada_b, kv_norm_g, mla_w_dkv, mla_kv_norm_g, mla_w_uk, mla_w_uv,
           mla_w_dq, mla_q_norm_g, mla_w_uq, mla_w_o,
           peer_w_q, peer_keys, peer_u, peer_v, final_g):
    p = dict(norm_g=norm_g, ret_w_in=ret_w_in.astype(BF16), ret_gn_g=ret_gn_g[0][None],
             ret_w_out=ret_w_out.astype(BF16),
             kv_norm_g=kv_norm_g, mla_w_dkv=mla_w_dkv, mla_kv_norm_g=mla_kv_norm_g,
             mla_w_uk=mla_w_uk, mla_w_uv=mla_w_uv, mla_w_dq=mla_w_dq, mla_q_norm_g=mla_q_norm_g,
             mla_w_uq=mla_w_uq, mla_w_o=mla_w_o.astype(BF16), peer_w_q=peer_w_q, peer_keys=peer_keys,
             peer_u=peer_u, peer_v=peer_v, final_g=final_g)
    sw = _shared_weights(p)
    nbp, lp, d = x_prompt.shape
    nbs, ls, _ = x_sample.shape
    past = cache_mla_latent.shape[1]

    n_c = nbp + nbs
    c_all = jnp.pad(jnp.concatenate([c_prompt, c_sample], axis=0), ((0, (-n_c) % 8), (0, 0)))
    layer_mods = [_adaln(c_all, ada_w, ada_b, layer, 6 * d) for layer in range(2)]
    kv_mods = _adaln(c_all, kv_ada_w, kv_ada_b, 0, 2 * d)

    def stream_mods(lo, hi):
        return ([jnp.split(m[lo:hi], 6, axis=-1) for m in layer_mods], jnp.split(kv_mods[lo:hi], 2, axis=-1))

    mods_p, kv_p = stream_mods(0, nbp)
    mods_s, kv_s = stream_mods(nbp, n_c)
    pos_p = jnp.arange(lp, dtype=jnp.int32)
    pos_s = past + jnp.arange(ls, dtype=jnp.int32)
    y_p, ret_p, lat_p, kr_p = _trunk(x_prompt, pos_p, mods_p, kv_p, p, sw, None, None, None)
    y_s, ret_s, lat_s, kr_s = _trunk(x_sample, pos_s, mods_s, kv_s, p, sw, state_retention,
                                     cache_mla_latent, cache_mla_krope)
    return (y_p, y_s, ret_p, ret_s, lat_p, kr_p, lat_s, kr_s)
```
